```python
import jax, jax.numpy as jnp
from jax import lax
import numpy as np

D_MODEL = 1024
BATCH = 32
SEQ = 2048
DEPTH = 1

HEAD_DIM = 64
N_HEADS = D_MODEL // HEAD_DIM
N_HEADS_NA = N_HEADS // 2
N_HEADS_DIL = N_HEADS - N_HEADS_NA
GRID_W = 64
NA_ROWS_MAX = 8
NA_COLS = 16
DIL_PAIRS = ((128, 1), (512, 4), (2048, 16))
ROPE_THETA = 10000.0
N_GROUPS = 4
EXPERTS_PER_GROUP = 8
N_EXPERTS = N_GROUPS * EXPERTS_PER_GROUP
TOP_K_IN_GROUP = 2
D_EXPERT = 512
PLE_DIM = 256
EPS = 1e-6
NEG = -1e30

kernel_name = "hybrid_na_dilated_hiermoe_encoder"


def rms_norm(x, gain):
    xf = x.astype(jnp.float32)
    y = xf * lax.rsqrt(jnp.mean(xf * xf, axis=-1, keepdims=True) + EPS)
    return (y * gain.astype(jnp.float32)).astype(x.dtype)


def rope(x, pos):
    half = x.shape[-1] // 2
    inv = ROPE_THETA ** (-jnp.arange(half, dtype=jnp.float32) / half)
    ang = pos.astype(jnp.float32)[:, None] * inv[None, :]
    cos = jnp.cos(ang).astype(x.dtype)
    sin = jnp.sin(ang).astype(x.dtype)
    x1, x2 = x[..., :half], x[..., half:]
    return jnp.concatenate([x1 * cos - x2 * sin, x1 * sin + x2 * cos], axis=-1)


def neighbourhood_attention(q, k, v, rpb):
    b, h, s, dh = q.shape
    rows = s // GRID_W
    kr = min(NA_ROWS_MAX, rows)
    qg = q.reshape(b, h, rows, GRID_W, dh)
    kg = k.reshape(b, h, rows, GRID_W, dh)
    vg = v.reshape(b, h, rows, GRID_W, dh)
    row_start = jnp.clip(jnp.arange(rows) - kr // 2, 0, rows - kr)
    col_start = jnp.clip(jnp.arange(GRID_W) - NA_COLS // 2, 0, GRID_W - NA_COLS)
    col_idx = col_start[:, None] + jnp.arange(NA_COLS)[None, :]
    col_off = col_idx - jnp.arange(GRID_W)[:, None] + (NA_COLS - 1)

    def one_row(r):
        rs = row_start[r]
        q_r = lax.dynamic_index_in_dim(qg, r, axis=2, keepdims=False)
        k_rows = lax.dynamic_slice_in_dim(kg, rs, kr, axis=2)
        v_rows = lax.dynamic_slice_in_dim(vg, rs, kr, axis=2)
        k_win = k_rows[:, :, :, col_idx]
        v_win = v_rows[:, :, :, col_idx]
        row_off = rs + jnp.arange(kr) - r + (NA_ROWS_MAX - 1)
        bias = rpb[:, row_off][:, :, col_off]
        bias = bias.transpose(0, 2, 1, 3).astype(jnp.float32)
        sc = jnp.einsum('bhwd,bhrwkd->bhwrk', q_r, k_win).astype(jnp.float32) + bias[None]
        pr = jax.nn.softmax(sc.reshape(b, h, GRID_W, kr * NA_COLS), axis=-1)
        pr = pr.astype(v.dtype).reshape(b, h, GRID_W, kr, NA_COLS)
        return jnp.einsum('bhwrk,bhrwkd->bhwd', pr, v_win)

    out = lax.map(one_row, jnp.arange(rows))
    return out.transpose(1, 2, 0, 3, 4).reshape(b, h, s, dh)


def dilated_window_attention(q, k, v, window, dilation):
    b, h, s, dh = q.shape
    radius = window // (2 * dilation)
    blk = radius
    unit = dilation * blk
    s_pad = -(-s // unit) * unit
    pad = s_pad - s
    l = s_pad // dilation
    nb = l // blk

    def split(t):
        t = jnp.pad(t, ((0, 0), (0, 0), (0, pad), (0, 0)))
        t = t.reshape(b, h, l, dilation, dh).transpose(0, 1, 3, 2, 4)
        return t.reshape(b, h, dilation, nb, blk, dh)

    def band(t):
        tp = jnp.pad(t, ((0, 0), (0, 0), (0, 0), (1, 1), (0, 0), (0, 0)))
        return jnp.concatenate([tp[:, :, :, 0:nb], tp[:, :, :, 1:nb + 1], tp[:, :, :, 2:nb + 2]], axis=4)

    qs = split(q)
    kb = band(split(k))
    vb = band(split(v))
    sc = jnp.einsum('bhrnqd,bhrnkd->bhrnqk', qs, kb).astype(jnp.float32)
    lq = jnp.arange(nb)[:, None, None] * blk + jnp.arange(blk)[None, :, None]
    lk = (jnp.arange(nb)[:, None, None] - 1) * blk + jnp.arange(3 * blk)[None, None, :]
    pos_k = lk[None] * dilation + jnp.arange(dilation)[:, None, None, None]
    valid = (jnp.abs(lk - lq) <= radius)[None] & (lk >= 0)[None] & (pos_k < s)
    sc = jnp.where(valid, sc, NEG)
    m = jnp.max(sc, axis=-1, keepdims=True)
    e = jnp.exp(sc - m)
    den = jnp.sum(e, axis=-1, keepdims=True)
    o = jnp.einsum('bhrnqk,bhrnkd->bhrnqd', (e / den).astype(v.dtype), vb)
    lse = (m + jnp.log(den))[..., 0]
    o = o.reshape(b, h, dilation, l, dh).transpose(0, 1, 3, 2, 4).reshape(b, h, s_pad, dh)[:, :, :s]
    lse = lse.reshape(b, h, dilation, l).transpose(0, 1, 3, 2).reshape(b, h, s_pad)[:, :, :s]
    return o, lse


def hierarchical_moe(h, w_rg, b_rg, w_re, b_re, w_gate, w_up, w_down):
    b, s, d = h.shape
    hf = h.reshape(b * s, d)
    h32 = hf.astype(jnp.float32)
    g_logits = h32 @ w_rg.astype(jnp.float32) + b_rg.astype(jnp.float32)
    g_probs = jax.nn.softmax(g_logits, axis=-1)
    g_sel = jnp.argmax(g_logits, axis=-1)
    g_w = jnp.take_along_axis(g_probs, g_sel[:, None], axis=-1)
    e_logits = jnp.einsum('nd,gde->nge', h32, w_re.astype(jnp.float32)) + b_re.astype(jnp.float32)
    e_logits = jnp.take_along_axis(e_logits, g_sel[:, None, None], axis=1)[:, 0]
    top_v, top_i = lax.top_k(e_logits, TOP_K_IN_GROUP)
    top_w = jax.nn.softmax(top_v, axis=-1) * g_w
    expert_id = g_sel[:, None] * EXPERTS_PER_GROUP + top_i
    gates = jnp.sum(jax.nn.one_hot(expert_id, N_EXPERTS, dtype=jnp.float32) * top_w[..., None], axis=1)
    gates = gates.astype(h.dtype)
    y = jnp.zeros_like(hf)
    for e in range(N_EXPERTS):
        a = jax.nn.silu(hf @ w_gate[e]) * (hf @ w_up[e])
        y = y + gates[:, e:e + 1] * (a @ w_down[e])
    return y.reshape(b, s, d)


def hybrid_layer(x, p_l, g_attn, w_qkv, q_norm_na, k_norm_na, rpb_na, q_norm_dil, k_norm_dil,
                 g_out_na, g_out_dil, w_o, g_ffn, w_router_group, b_router_group,
                 w_router_expert, b_router_expert, w_exp_gate, w_exp_up, w_exp_down,
                 g_ple, w_ple_gate, w_ple_proj):
    b, s, d = x.shape
    scale = HEAD_DIM ** -0.5
    h = rms_norm(x, g_attn)
    qkv = jnp.einsum('bsd,de->bse', h, w_qkv).reshape(b, s, 3, N_HEADS, HEAD_DIM)
    q = qkv[:, :, 0].transpose(0, 2, 1, 3)
    k = qkv[:, :, 1].transpose(0, 2, 1, 3)
    v = qkv[:, :, 2].transpose(0, 2, 1, 3)
    qa = rms_norm(q[:, :N_HEADS_NA], q_norm_na) * scale
    ka = rms_norm(k[:, :N_HEADS_NA], k_norm_na)
    out_a = neighbourhood_attention(qa, ka, v[:, :N_HEADS_NA], rpb_na)
    pos = jnp.arange(s)
    qb = rope(rms_norm(q[:, N_HEADS_NA:], q_norm_dil), pos) * scale
    kb = rope(rms_norm(k[:, N_HEADS_NA:], k_norm_dil), pos)
    vb = v[:, N_HEADS_NA:]
    outs = []
    lses = []
    for window, dilation in DIL_PAIRS:
        o_i, lse_i = dilated_window_attention(qb, kb, vb, window, dilation)
        outs.append(o_i)
        lses.append(lse_i)
    wts = jax.nn.softmax(jnp.stack(lses, axis=0), axis=0).astype(vb.dtype)
    out_b = jnp.einsum('pbhs,pbhsd->bhsd', wts, jnp.stack(outs, axis=0))
    ya = rms_norm(out_a.transpose(0, 2, 1, 3).reshape(b, s, N_HEADS_NA * HEAD_DIM), g_out_na)
    yb = rms_norm(out_b.transpose(0, 2, 1, 3).reshape(b, s, N_HEADS_DIL * HEAD_DIM), g_out_dil)
    x = x + jnp.einsum('bse,ed->bsd', jnp.concatenate([ya, yb], axis=-1), w_o)
    x = x + hierarchical_moe(rms_norm(x, g_ffn), w_router_group, b_router_group, w_router_expert,
                             b_router_expert, w_exp_gate, w_exp_up, w_exp_down)
    gate = jax.nn.sigmoid(jnp.einsum('bsd,de->bse', rms_norm(x, g_ple), w_ple_gate))
    return x + gate * jnp.einsum('bsp,pd->bsd', p_l, w_ple_proj)


def setup_inputs(seed: int = 0) -> dict:
    key = jax.random.key(seed)
    ks = jax.random.split(key, 23)
    f32 = jnp.float32
    D = D_MODEL
    na_w = N_HEADS_NA * HEAD_DIM
    dil_w = N_HEADS_DIL * HEAD_DIM

    def nrm(k, shape, sc):
        return sc * jax.random.normal(k, shape, f32)

    def gain(k, shape):
        return 1.0 + 0.05 * jax.random.normal(k, shape, f32)

    return {
        "x": nrm(ks[0], (BATCH, SEQ, D), 1.0),
        "p": nrm(ks[1], (DEPTH, BATCH, SEQ, PLE_DIM), 1.0),
        "g_attn": gain(ks[2], (DEPTH, D)),
        "w_qkv": nrm(ks[3], (DEPTH, D, 3 * D), D ** -0.5),
        "q_norm_na": gain(ks[4], (DEPTH, HEAD_DIM)),
        "k_norm_na": gain(ks[5], (DEPTH, HEAD_DIM)),
        "rpb_na": nrm(ks[6], (DEPTH, N_HEADS_NA, 2 * NA_ROWS_MAX - 1, 2 * NA_COLS - 1), 0.5),
        "q_norm_dil": gain(ks[7], (DEPTH, HEAD_DIM)),
        "k_norm_dil": gain(ks[8], (DEPTH, HEAD_DIM)),
        "g_out_na": gain(ks[9], (DEPTH, na_w)),
        "g_out_dil": gain(ks[10], (DEPTH, dil_w)),
        "w_o": nrm(ks[11], (DEPTH, D, D), D ** -0.5),
        "g_ffn": gain(ks[12], (DEPTH, D)),
        "w_router_group": nrm(ks[13], (DEPTH, D, N_GROUPS), D ** -0.5),
        "b_router_group": nrm(ks[14], (DEPTH, N_GROUPS), 0.01),
        "w_router_expert": nrm(ks[15], (DEPTH, N_GROUPS, D, EXPERTS_PER_GROUP), D ** -0.5),
        "b_router_expert": nrm(ks[16], (DEPTH, N_GROUPS, EXPERTS_PER_GROUP), 0.01),
        "w_exp_gate": nrm(ks[17], (DEPTH, N_EXPERTS, D, D_EXPERT), D ** -0.5),
        "w_exp_up": nrm(ks[18], (DEPTH, N_EXPERTS, D, D_EXPERT), D ** -0.5),
        "w_exp_down": nrm(ks[19], (DEPTH, N_EXPERTS, D_EXPERT, D), D_EXPERT ** -0.5),
        "g_ple": gain(ks[20], (DEPTH, D)),
        "w_ple_gate": nrm(ks[21], (DEPTH, D, D), D ** -0.5),
        "w_ple_proj": nrm(ks[22], (DEPTH, PLE_DIM, D), PLE_DIM ** -0.5),
    }


def reference(x, p, g_attn, w_qkv, q_norm_na, k_norm_na, rpb_na, q_norm_dil, k_norm_dil,
              g_out_na, g_out_dil, w_o, g_ffn, w_router_group, b_router_group,
              w_router_expert, b_router_expert, w_exp_gate, w_exp_up, w_exp_down,
              g_ple, w_ple_gate, w_ple_proj):
    for i in range(DEPTH):
        x = hybrid_layer(x, p[i], g_attn[i], w_qkv[i], q_norm_na[i], k_norm_na[i], rpb_na[i],
                         q_norm_dil[i], k_norm_dil[i], g_out_na[i], g_out_dil[i], w_o[i],
                         g_ffn[i], w_router_group[i], b_router_group[i], w_router_expert[i],
                         b_router_expert[i], w_exp_gate[i], w_exp_up[i], w_exp_down[i],
                         g_ple[i], w_ple_gate[i], w_ple_proj[i])
    return x
```

```python
import functools

import numpy as np
import jax
import jax.numpy as jnp
from jax import lax
from jax.experimental import pallas as pl
from jax.experimental.pallas import tpu as pltpu

HEAD_DIM = 64
N_HEADS = 16
N_HEADS_NA = 8
GRID_W = 64
NA_ROWS = 8
NA_COLS = 16
DIL_PAIRS = ((128, 1), (512, 4), (2048, 16))
ROPE_THETA = 10000.0
N_GROUPS = 4
EXPERTS_PER_GROUP = 8
N_EXPERTS = N_GROUPS * EXPERTS_PER_GROUP
EPS = 1e-6
NEG = -1e30

LANES = 128
PAIR_W = 2 * HEAD_DIM
DIL_BLK = 128
VMEM_LIMIT = 56 * 1024 * 1024

F32 = jnp.float32
BF16 = jnp.bfloat16


def _dot(a, b):
    return jnp.dot(a, b, preferred_element_type=F32)


def _dot_nt(a, b):
    return lax.dot_general(a, b, (((1,), (1,)), ((), ())), preferred_element_type=F32)


def _rms(x, gain):
    return x * lax.rsqrt(jnp.mean(x * x, axis=-1, keepdims=True) + EPS) * gain


def _lane_first_half(shape):
    return lax.broadcasted_iota(jnp.int32, shape, len(shape) - 1) < HEAD_DIM


def _qkv_kernel(x_ref, g_ref, w_ref, gq_na_ref, gk_na_ref, gq_dil_ref, gk_dil_ref, cos_ref, sin_ref,
                hsum_ref, qa_ref, ka_ref, va_ref, qb_ref, kb_ref, vb_ref):
    d = x_ref.shape[1]
    half = d // 2
    scale = HEAD_DIM ** -0.5
    h = _rms(x_ref[...], g_ref[...]).astype(BF16)

    def proj(col):
        return _dot(h, w_ref[:, col:col + half])

    def head_norm(y, gain):
        sq = (y * y).astype(BF16)
        w = hsum_ref.shape[0]
        ms = jnp.concatenate([_dot(sq[:, c:c + w], hsum_ref[...]) for c in range(0, half, w)], axis=1)
        return y * lax.rsqrt(ms + EPS) * gain

    def rope(y):
        lane = lax.broadcasted_iota(jnp.int32, (y.shape[0], LANES), 1)
        lower = (lane % HEAD_DIM) < HEAD_DIM // 2
        cos = cos_ref[...]
        sin = sin_ref[...]
        outs = []
        for c in range(0, half, LANES):
            yc = y[:, c:c + LANES]
            up = pltpu.roll(yc, LANES - HEAD_DIM // 2, axis=1)
            down = pltpu.roll(yc, HEAD_DIM // 2, axis=1)
            outs.append(yc * cos + jnp.where(lower, up, down) * sin)
        return jnp.concatenate(outs, axis=1)

    qa_ref[...] = (head_norm(proj(0), gq_na_ref[...]) * scale).astype(BF16)
    qb_ref[...] = (rope(head_norm(proj(half), gq_dil_ref[...])) * scale).astype(BF16)
    ka_ref[...] = head_norm(proj(d), gk_na_ref[...]).astype(BF16)
    kb_ref[...] = rope(head_norm(proj(d + half), gk_dil_ref[...])).astype(BF16)
    va_ref[...] = proj(2 * d).astype(BF16)
    vb_ref[...] = proj(2 * d + half).astype(BF16)


def _qkv_proj(x2, g_attn, w_qkv, gq_na, gk_na, gq_dil, gk_dil, seq, tm):
    n, d = x2.shape
    half = d // 2
    pos = jnp.arange(seq, dtype=F32)
    inv = ROPE_THETA ** (-jnp.arange(HEAD_DIM // 2, dtype=F32) / (HEAD_DIM // 2))
    ang = pos[:, None] * inv[None, :]
    cos = jnp.tile(jnp.cos(ang), (1, LANES // (HEAD_DIM // 2)))
    sin = jnp.tile(jnp.concatenate([-jnp.sin(ang), jnp.sin(ang)], axis=1), (1, LANES // HEAD_DIM))
    hs_w = 2 * LANES
    blk = np.arange(hs_w) // HEAD_DIM
    hsum = jnp.asarray((blk[:, None] == blk[None, :]).astype(np.float32) / HEAD_DIM, BF16)
    tile_gain = lambda g: jnp.tile(g.astype(F32), half // HEAD_DIM)[None, :]
    steps_per_seq = seq // tm
    full = lambda shape: pl.BlockSpec(shape, lambda i: (0,) * len(shape))
    out = jax.ShapeDtypeStruct((n, half), BF16)
    return pl.pallas_call(
        _qkv_kernel,
        grid=(n // tm,),
        in_specs=[
            pl.BlockSpec((tm, d), lambda i: (i, 0)),
            full((1, d)),
            full((d, 3 * d)),
            full((1, half)), full((1, half)), full((1, half)), full((1, half)),
            pl.BlockSpec((tm, LANES), lambda i: (i % steps_per_seq, 0)),
            pl.BlockSpec((tm, LANES), lambda i: (i % steps_per_seq, 0)),
            full((hs_w, hs_w)),
        ],
        out_specs=[pl.BlockSpec((tm, half), lambda i: (i, 0))] * 6,
        out_shape=[out] * 6,
        compiler_params=pltpu.CompilerParams(dimension_semantics=("parallel",), vmem_limit_bytes=VMEM_LIMIT),
        name="qkv_proj",
    )(x2, g_attn[None, :].astype(F32), w_qkv.astype(BF16), tile_gain(gq_na), tile_gain(gk_na),
      tile_gain(gq_dil), tile_gain(gk_dil), cos, sin, hsum)


def _pair_attention(q, kwin, va_win, vb_win, bias_a, bias_b, first):
    zero = jnp.zeros_like(q)
    sa = _dot_nt(jnp.where(first, q, zero), kwin) + bias_a
    sb = _dot_nt(jnp.where(first, zero, q), kwin) + bias_b
    ma = jnp.max(sa, axis=-1, keepdims=True)
    mb = jnp.max(sb, axis=-1, keepdims=True)
    ra = _dot(jnp.exp(sa - ma).astype(BF16), va_win)
    rb = _dot(jnp.exp(sb - mb).astype(BF16), vb_win)
    num = jnp.where(first, ra, rb)
    den = jnp.where(first, pltpu.roll(ra, HEAD_DIM, axis=1), pltpu.roll(rb, HEAD_DIM, axis=1))
    return num, den, ma, mb


def _fill_v_aug(v, va_ref, vb_ref):
    first = _lane_first_half(v.shape)
    one = jnp.ones_like(v)
    va_ref[...] = jnp.where(first, v, one)
    vb_ref[...] = jnp.where(first, one, v)


def _na_kernel(q_ref, k_ref, v_ref, bias_ref, o_ref, va_ref, vb_ref):
    rows = q_ref.shape[0] // GRID_W
    win = NA_ROWS * GRID_W
    _fill_v_aug(v_ref[...], va_ref, vb_ref)
    first = _lane_first_half((GRID_W, PAIR_W))

    def row(r, carry):
        rs = jnp.clip(r - NA_ROWS // 2, 0, rows - NA_ROWS)
        delta = r - rs
        qs = pl.ds(pl.multiple_of(r * GRID_W, GRID_W), GRID_W)
        ks = pl.ds(pl.multiple_of(rs * GRID_W, GRID_W), win)
        num, den, _, _ = _pair_attention(q_ref[qs, :], k_ref[ks, :], va_ref[ks, :], vb_ref[ks, :],
                                         bias_ref[0, delta], bias_ref[1, delta], first)
        o_ref[qs, :] = num / den
        return carry

    lax.fori_loop(0, rows, row, 0)


def _na_bias_table(rpb):
    w = np.arange(GRID_W)
    cs = np.clip(w - NA_COLS // 2, 0, GRID_W - NA_COLS)
    kc = np.arange(GRID_W)
    valid = (kc[None, :] >= cs[:, None]) & (kc[None, :] < cs[:, None] + NA_COLS)
    coff = np.clip(kc[None, :] - w[:, None] + NA_COLS - 1, 0, 2 * NA_COLS - 2)
    roff = np.arange(NA_ROWS)[None, :] - np.arange(NA_ROWS)[:, None] + NA_ROWS - 1
    tab = rpb.astype(F32)[:, roff][:, :, :, coff]
    tab = jnp.where(valid[None, None, None], tab, NEG)
    tab = tab.transpose(0, 1, 3, 2, 4)
    return tab.reshape(rpb.shape[0], NA_ROWS, GRID_W, NA_ROWS * GRID_W)


def _na_attention(q, k, v, rpb):
    b, s, width = q.shape
    pairs = width // PAIR_W
    assert s % GRID_W == 0 and s // GRID_W >= NA_ROWS
    bias = _na_bias_table(rpb).reshape(pairs, 2, NA_ROWS, GRID_W, NA_ROWS * GRID_W)
    qkv_spec = pl.BlockSpec((None, s, PAIR_W), lambda bi, j: (bi, 0, j))
    return pl.pallas_call(
        _na_kernel,
        grid=(b, pairs),
        in_specs=[qkv_spec, qkv_spec, qkv_spec,
                  pl.BlockSpec((None, 2, NA_ROWS, GRID_W, NA_ROWS * GRID_W), lambda bi, j: (j, 0, 0, 0, 0))],
        out_specs=pl.BlockSpec((None, s, PAIR_W), lambda bi, j: (bi, 0, j)),
        out_shape=jax.ShapeDtypeStruct((b, s, width), F32),
        scratch_shapes=[pltpu.VMEM((s, PAIR_W), BF16), pltpu.VMEM((s, PAIR_W), BF16)],
        compiler_params=pltpu.CompilerParams(dimension_semantics=("parallel", "parallel"),
                                             vmem_limit_bytes=VMEM_LIMIT),
        name="na_attn",
    )(q, k, v, bias)


def _dil_kernel(q_ref, k_ref, v_ref, mwide_ref, mfull_ref, o_ref,
                qf_ref, kf_ref, vf_ref, qc_ref, kc_ref, va_ref, vb_ref, acc_ref, den_ref, max_ref):
    s = q_ref.shape[0]
    qf_ref[...] = q_ref[...].astype(F32)
    kf_ref[...] = k_ref[...].astype(F32)
    vf_ref[...] = v_ref[...].astype(F32)
    first = _lane_first_half((DIL_BLK, PAIR_W))

    for p, (window, dil) in enumerate(DIL_PAIRS):
        radius = window // (2 * dil)
        cls_len = s // dil
        nblk = cls_len // DIL_BLK
        wide = cls_len >= 2 * DIL_BLK
        win = 2 * DIL_BLK if wide else cls_len
        assert radius == DIL_BLK // 2 and cls_len % DIL_BLK == 0

        def to_classes(c, carry):
            src = pl.ds(c, cls_len, stride=dil)
            dst = pl.ds(pl.multiple_of(c * cls_len, DIL_BLK), cls_len)
            qc_ref[dst, :] = qf_ref[src, :].astype(BF16)
            kc_ref[dst, :] = kf_ref[src, :].astype(BF16)
            v = vf_ref[src, :].astype(BF16)
            fh = _lane_first_half(v.shape)
            one = jnp.ones_like(v)
            va_ref[dst, :] = jnp.where(fh, v, one)
            vb_ref[dst, :] = jnp.where(fh, one, v)
            return carry

        lax.fori_loop(0, dil, to_classes, 0)

        def block(n, carry):
            c = n // nblk
            i = n % nblk
            base = c * cls_len
            qs = pl.ds(pl.multiple_of(base + i * DIL_BLK, DIL_BLK), DIL_BLK)
            if wide:
                ws = jnp.clip(i * DIL_BLK - radius, 0, cls_len - win)
                kind = jnp.where(i == 0, 0, jnp.where(i == nblk - 1, 2, 1))
                mask = mwide_ref[kind]
            else:
                ws = 0
                mask = mfull_ref[...]
            ks = pl.ds(pl.multiple_of(base + ws, radius), win)
            num, den, ma, mb = _pair_attention(qc_ref[qs, :], kc_ref[ks, :], va_ref[ks, :], vb_ref[ks, :],
                                               mask, mask, first)
            tok = pl.ds(c + dil * DIL_BLK * i, DIL_BLK, stride=dil)
            acc_ref[p, tok, :] = num
            den_ref[p, tok, :] = den
            max_ref[p, tok, :] = jnp.where(first, ma, mb)
            return carry

        lax.fori_loop(0, dil * nblk, block, 0)

    m = jnp.maximum(jnp.maximum(max_ref[0], max_ref[1]), max_ref[2])
    num = jnp.zeros_like(m)
    den = jnp.zeros_like(m)
    for p in range(len(DIL_PAIRS)):
        w = jnp.exp(max_ref[p] - m)
        num = num + w * acc_ref[p]
        den = den + w * den_ref[p]
    o_ref[...] = num / den


def _band_mask(kind):
    radius = DIL_BLK // 2
    qq = np.arange(DIL_BLK)[:, None]
    if kind == "full":
        kk = np.arange(DIL_BLK)[None, :]
        shift = 0
    else:
        kk = np.arange(2 * DIL_BLK)[None, :]
        shift = {"first": 0, "inner": radius, "last": DIL_BLK}[kind]
    return np.where(np.abs(kk - qq - shift) <= radius, 0.0, NEG).astype(np.float32)


def _dil_attention(q, k, v):
    b, s, width = q.shape
    pairs = width // PAIR_W
    for window, dil in DIL_PAIRS:
        assert s % (window // 2) == 0 and (s // dil) % DIL_BLK == 0
    mwide = jnp.asarray(np.stack([_band_mask("first"), _band_mask("inner"), _band_mask("last")]))
    mfull = jnp.asarray(_band_mask("full"))
    qkv_spec = pl.BlockSpec((None, s, PAIR_W), lambda bi, j: (bi, 0, j))
    npat = len(DIL_PAIRS)
    return pl.pallas_call(
        _dil_kernel,
        grid=(b, pairs),
        in_specs=[qkv_spec, qkv_spec, qkv_spec,
                  pl.BlockSpec(mwide.shape, lambda bi, j: (0, 0, 0)),
                  pl.BlockSpec(mfull.shape, lambda bi, j: (0, 0))],
        out_specs=pl.BlockSpec((None, s, PAIR_W), lambda bi, j: (bi, 0, j)),
        out_shape=jax.ShapeDtypeStruct((b, s, width), F32),
        scratch_shapes=[pltpu.VMEM((s, PAIR_W), F32)] * 3 + [pltpu.VMEM((s, PAIR_W), BF16)] * 4
        + [pltpu.VMEM((npat, s, PAIR_W), F32)] * 3,
        compiler_params=pltpu.CompilerParams(dimension_semantics=("parallel", "parallel"),
                                             vmem_limit_bytes=VMEM_LIMIT),
        name="dil_attn",
    )(q, k, v, mwide, mfull)


def _split_bf16(x):
    hi = x.astype(BF16)
    return hi, (x - hi.astype(F32)).astype(BF16)


def _out_router_kernel(oa_ref, ob_ref, x_ref, ga_ref, gb_ref, wo_ref, gf_ref, wr_hi_ref, wr_lo_ref, br_ref,
                       x1_ref, h_ref, rt_ref):
    half = oa_ref.shape[1]
    ya = _rms(oa_ref[...], ga_ref[...]).astype(BF16)
    yb = _rms(ob_ref[...], gb_ref[...]).astype(BF16)
    x1 = x_ref[...] + _dot(ya, wo_ref[:half, :]) + _dot(yb, wo_ref[half:, :])
    x1_ref[...] = x1
    h = _rms(x1, gf_ref[...])
    h_ref[...] = h.astype(BF16)

    h_hi, h_lo = _split_bf16(h)
    logits = _dot(h_hi, wr_hi_ref[...]) + _dot(h_lo, wr_hi_ref[...]) + _dot(h_hi, wr_lo_ref[...]) + br_ref[...]
    lane = lax.broadcasted_iota(jnp.int32, logits.shape, 1)
    ninf = jnp.float32(-jnp.inf)

    def first_argmax(vals, vmax):
        return jnp.min(jnp.where(vals == vmax, lane, LANES), axis=-1, keepdims=True)

    gl = jnp.where(lane < N_GROUPS, logits, ninf)
    gmax = jnp.max(gl, axis=-1, keepdims=True)
    gsel = first_argmax(gl, gmax)
    gw = 1.0 / jnp.sum(jnp.exp(gl - gmax), axis=-1, keepdims=True)
    lo = N_GROUPS + EXPERTS_PER_GROUP * gsel
    el = jnp.where((lane >= lo) & (lane < lo + EXPERTS_PER_GROUP), logits, ninf)
    v0 = jnp.max(el, axis=-1, keepdims=True)
    i0 = first_argmax(el, v0)
    el = jnp.where(lane == i0, ninf, el)
    v1 = jnp.max(el, axis=-1, keepdims=True)
    i1 = first_argmax(el, v1)
    t = jnp.exp(v1 - v0)
    w0 = gw / (1.0 + t)
    w1 = gw * t / (1.0 + t)
    e0 = (i0 - N_GROUPS).astype(F32)
    e1 = (i1 - N_GROUPS).astype(F32)
    rt_ref[...] = jnp.where(lane == 0, e0, jnp.where(lane == 1, e1, jnp.where(lane == 2, w0, jnp.where(lane == 3, w1, 0.0))))


def _out_router(oa, ob, x2, g_na, g_dil, w_o, g_ffn, w_rg, b_rg, w_re, b_re, tm):
    n, d = x2.shape
    half = d // 2
    wr = jnp.concatenate([w_rg.astype(F32), w_re.astype(F32).transpose(1, 0, 2).reshape(d, N_EXPERTS)], axis=1)
    wr = jnp.pad(wr, ((0, 0), (0, LANES - wr.shape[1])))
    wr_hi = wr.astype(BF16)
    wr_lo = (wr - wr_hi.astype(F32)).astype(BF16)
    br = jnp.pad(jnp.concatenate([b_rg.astype(F32), b_re.astype(F32).reshape(-1)]), (0, LANES - N_GROUPS - N_EXPERTS))
    full = lambda shape: pl.BlockSpec(shape, lambda i: (0,) * len(shape))
    row = lambda w: pl.BlockSpec((tm, w), lambda i: (i, 0))
    return pl.pallas_call(
        _out_router_kernel,
        grid=(n // tm,),
        in_specs=[row(half), row(half), row(d), full((1, half)), full((1, half)), full((d, d)), full((1, d)),
                  full((d, LANES)), full((d, LANES)), full((1, LANES))],
        out_specs=[row(d), row(d), row(LANES)],
        out_shape=[jax.ShapeDtypeStruct((n, d), F32), jax.ShapeDtypeStruct((n, d), BF16),
                   jax.ShapeDtypeStruct((n, LANES), F32)],
        compiler_params=pltpu.CompilerParams(dimension_semantics=("parallel",), vmem_limit_bytes=VMEM_LIMIT),
        name="out_router",
    )(oa, ob, x2, g_na[None, :].astype(F32), g_dil[None, :].astype(F32), w_o.astype(BF16),
      g_ffn[None, :].astype(F32), wr_hi, wr_lo, br[None, :])


def _moe_kernel(h_ref, rt_ref, wg_ref, wu_ref, wd_ref, y_ref):
    e = pl.program_id(1)

    @pl.when(e == 0)
    def _():
        y_ref[...] = jnp.zeros_like(y_ref)

    rt = rt_ref[...]
    ef = e.astype(F32)
    gate = jnp.where(rt[:, 0:1] == ef, rt[:, 2:3], 0.0) + jnp.where(rt[:, 1:2] == ef, rt[:, 3:4], 0.0)
    h = h_ref[...]
    a = _dot(h, wg_ref[...])
    a = a * jax.nn.sigmoid(a) * _dot(h, wu_ref[...])
    y_ref[...] += gate * _dot(a.astype(BF16), wd_ref[...])


def _moe(h, rt, w_gate, w_up, w_down, tm):
    n, d = h.shape
    ne, _, de = w_gate.shape
    return pl.pallas_call(
        _moe_kernel,
        grid=(n // tm, ne),
        in_specs=[pl.BlockSpec((tm, d), lambda i, e: (i, 0)),
                  pl.BlockSpec((tm, LANES), lambda i, e: (i, 0)),
                  pl.BlockSpec((None, d, de), lambda i, e: (e, 0, 0)),
                  pl.BlockSpec((None, d, de), lambda i, e: (e, 0, 0)),
                  pl.BlockSpec((None, de, d), lambda i, e: (e, 0, 0))],
        out_specs=pl.BlockSpec((tm, d), lambda i, e: (i, 0)),
        out_shape=jax.ShapeDtypeStruct((n, d), F32),
        compiler_params=pltpu.CompilerParams(dimension_semantics=("parallel", "arbitrary"),
                                             vmem_limit_bytes=VMEM_LIMIT),
        name="moe",
    )(h, rt, w_gate.astype(BF16), w_up.astype(BF16), w_down.astype(BF16))


def _ple_kernel(x1_ref, y_ref, p_ref, g_ref, wg_ref, wp_ref, o_ref):
    x2 = x1_ref[...] + y_ref[...]
    gate = jax.nn.sigmoid(_dot(_rms(x2, g_ref[...]).astype(BF16), wg_ref[...]))
    o_ref[...] = x2 + gate * _dot(p_ref[...].astype(BF16), wp_ref[...])


def _ple(x1, y, p2, g_ple, w_gate, w_proj, tm):
    n, d = x1.shape
    dp = p2.shape[1]
    full = lambda shape: pl.BlockSpec(shape, lambda i: (0,) * len(shape))
    row = lambda w: pl.BlockSpec((tm, w), lambda i: (i, 0))
    return pl.pallas_call(
        _ple_kernel,
        grid=(n // tm,),
        in_specs=[row(d), row(d), row(dp), full((1, d)), full((d, d)), full((dp, d))],
        out_specs=row(d),
        out_shape=jax.ShapeDtypeStruct((n, d), F32),
        compiler_params=pltpu.CompilerParams(dimension_semantics=("parallel",), vmem_limit_bytes=VMEM_LIMIT),
        name="ple",
    )(x1, y, p2, g_ple[None, :].astype(F32), w_gate.astype(BF16), w_proj.astype(BF16))


def _layer(x, p_l, g_attn, w_qkv, q_norm_na, k_norm_na, rpb_na, q_norm_dil, k_norm_dil, g_out_na, g_out_dil,
           w_o, g_ffn, w_rg, b_rg, w_re, b_re, w_exp_gate, w_exp_up, w_exp_down, g_ple, w_ple_gate, w_ple_proj):
    b, s, d = x.shape
    n = b * s
    half = d // 2
    assert d == N_HEADS * HEAD_DIM and half == N_HEADS_NA * HEAD_DIM
    tm = 512
    assert s % tm == 0
    x2 = x.reshape(n, d)
    qa, ka, va, qb, kb, vb = _qkv_proj(x2, g_attn, w_qkv, q_norm_na, k_norm_na, q_norm_dil, k_norm_dil, s, tm)
    seq = lambda t: t.reshape(b, s, half)
    oa = _na_attention(seq(qa), seq(ka), seq(va), rpb_na).reshape(n, half)
    ob = _dil_attention(seq(qb), seq(kb), seq(vb)).reshape(n, half)
    x1, h, rt = _out_router(oa, ob, x2, g_out_na, g_out_dil, w_o, g_ffn, w_rg, b_rg, w_re, b_re, tm)
    y = _moe(h, rt, w_exp_gate, w_exp_up, w_exp_down, 2 * tm)
    out = _ple(x1, y, p_l.reshape(n, -1), g_ple, w_ple_gate, w_ple_proj, tm)
    return out.reshape(b, s, d)


def kernel(x, p, g_attn, w_qkv, q_norm_na, k_norm_na, rpb_na, q_norm_dil, k_norm_dil, g_out_na, g_out_dil, w_o,
           g_ffn, w_router_group, b_router_group, w_router_expert, b_router_expert, w_exp_gate, w_exp_up,
           w_exp_down, g_ple, w_ple_gate, w_ple_proj):
    for i in range(p.shape[0]):
        x = _layer(x, p[i], g_attn[i], w_qkv[i], q_norm_na[i], k_norm_na[i], rpb_na[i], q_norm_dil[i],
                   k_norm_dil[i], g_out_na[i], g_out_dil[i], w_o[i], g_ffn[i], w_router_group[i],
                   b_router_group[i], w_router_expert[i], b_router_expert[i], w_exp_gate[i], w_exp_up[i],
                   w_exp_down[i], g_ple[i], w_ple_gate[i], w_ple_proj[i])
    return x
```

```python
import functools

import numpy as np
import jax
import jax.numpy as jnp
from jax import lax
from jax.experimental import pallas as pl
from jax.experimental.pallas import tpu as pltpu
from jax.experimental.pallas import tpu_sc as plsc

HEAD_DIM = 64
N_HEADS = 16
N_HEADS_NA = 8
GRID_W = 64
NA_ROWS = 8
NA_COLS = 16
DIL_PAIRS = ((128, 1), (512, 4), (2048, 16))
ROPE_THETA = 10000.0
N_GROUPS = 4
EXPERTS_PER_GROUP = 8
N_EXPERTS = N_GROUPS * EXPERTS_PER_GROUP
EPS = 1e-6
NEG = -1e30

LANES = 128
PAIR_W = 2 * HEAD_DIM
DIL_BLK = 128
VMEM_LIMIT = 56 * 1024 * 1024
SC_CORES = 2
SC_SUBCORES = 16
SC_CHUNK = 128

F32 = jnp.float32
BF16 = jnp.bfloat16


def _dot(a, b):
    return jnp.dot(a, b, preferred_element_type=F32)


def _dot_nt(a, b):
    return lax.dot_general(a, b, (((1,), (1,)), ((), ())), preferred_element_type=F32)


def _rms(x, gain):
    return x * lax.rsqrt(jnp.mean(x * x, axis=-1, keepdims=True) + EPS) * gain


def _lane_first_half(shape):
    return lax.broadcasted_iota(jnp.int32, shape, len(shape) - 1) < HEAD_DIM


def _qkv_kernel(x_ref, g_ref, w_ref, gq_na_ref, gk_na_ref, gq_dil_ref, gk_dil_ref, cos_ref, sin_ref,
                hsum_ref, qa_ref, ka_ref, va_ref, qb_ref, kb_ref, vb_ref):
    d = x_ref.shape[1]
    half = d // 2
    scale = HEAD_DIM ** -0.5
    h = _rms(x_ref[...], g_ref[...]).astype(BF16)

    def proj(col):
        return _dot(h, w_ref[:, col:col + half])

    def head_norm(y, gain):
        sq = (y * y).astype(BF16)
        w = hsum_ref.shape[0]
        ms = jnp.concatenate([_dot(sq[:, c:c + w], hsum_ref[...]) for c in range(0, half, w)], axis=1)
        return y * lax.rsqrt(ms + EPS) * gain

    def rope(y):
        lane = lax.broadcasted_iota(jnp.int32, (y.shape[0], LANES), 1)
        lower = (lane % HEAD_DIM) < HEAD_DIM // 2
        cos = cos_ref[...]
        sin = sin_ref[...]
        outs = []
        for c in range(0, half, LANES):
            yc = y[:, c:c + LANES]
            up = pltpu.roll(yc, LANES - HEAD_DIM // 2, axis=1)
            down = pltpu.roll(yc, HEAD_DIM // 2, axis=1)
            outs.append(yc * cos + jnp.where(lower, up, down) * sin)
        return jnp.concatenate(outs, axis=1)

    qa_ref[...] = (head_norm(proj(0), gq_na_ref[...]) * scale).astype(BF16)
    qb_ref[...] = (rope(head_norm(proj(half), gq_dil_ref[...])) * scale).astype(BF16)
    ka_ref[...] = head_norm(proj(d), gk_na_ref[...]).astype(BF16)
    kb_ref[...] = rope(head_norm(proj(d + half), gk_dil_ref[...])).astype(BF16)
    va_ref[...] = proj(2 * d).astype(BF16)
    vb_ref[...] = proj(2 * d + half).astype(BF16)


def _qkv_proj(x2, g_attn, w_qkv, gq_na, gk_na, gq_dil, gk_dil, seq, tm):
    n, d = x2.shape
    half = d // 2
    pos = jnp.arange(seq, dtype=F32)
    inv = ROPE_THETA ** (-jnp.arange(HEAD_DIM // 2, dtype=F32) / (HEAD_DIM // 2))
    ang = pos[:, None] * inv[None, :]
    cos = jnp.tile(jnp.cos(ang), (1, LANES // (HEAD_DIM // 2)))
    sin = jnp.tile(jnp.concatenate([-jnp.sin(ang), jnp.sin(ang)], axis=1), (1, LANES // HEAD_DIM))
    hs_w = 2 * LANES
    blk = np.arange(hs_w) // HEAD_DIM
    hsum = jnp.asarray((blk[:, None] == blk[None, :]).astype(np.float32) / HEAD_DIM, BF16)
    tile_gain = lambda g: jnp.tile(g.astype(F32), half // HEAD_DIM)[None, :]
    steps_per_seq = seq // tm
    full = lambda shape: pl.BlockSpec(shape, lambda i: (0,) * len(shape))
    out = jax.ShapeDtypeStruct((n, half), BF16)
    return pl.pallas_call(
        _qkv_kernel,
        grid=(n // tm,),
        in_specs=[
            pl.BlockSpec((tm, d), lambda i: (i, 0)),
            full((1, d)),
            full((d, 3 * d)),
            full((1, half)), full((1, half)), full((1, half)), full((1, half)),
            pl.BlockSpec((tm, LANES), lambda i: (i % steps_per_seq, 0)),
            pl.BlockSpec((tm, LANES), lambda i: (i % steps_per_seq, 0)),
            full((hs_w, hs_w)),
        ],
        out_specs=[pl.BlockSpec((tm, half), lambda i: (i, 0))] * 6,
        out_shape=[out] * 6,
        compiler_params=pltpu.CompilerParams(dimension_semantics=("parallel",), vmem_limit_bytes=VMEM_LIMIT),
        name="qkv_proj",
    )(x2, g_attn[None, :].astype(F32), w_qkv.astype(BF16), tile_gain(gq_na), tile_gain(gk_na),
      tile_gain(gq_dil), tile_gain(gk_dil), cos, sin, hsum)


def _pair_attention(q, kwin, va_win, vb_win, bias_a, bias_b, first):
    zero = jnp.zeros_like(q)
    sa = _dot_nt(jnp.where(first, q, zero), kwin) + bias_a
    sb = _dot_nt(jnp.where(first, zero, q), kwin) + bias_b
    ma = jnp.max(sa, axis=-1, keepdims=True)
    mb = jnp.max(sb, axis=-1, keepdims=True)
    ra = _dot(jnp.exp(sa - ma).astype(BF16), va_win)
    rb = _dot(jnp.exp(sb - mb).astype(BF16), vb_win)
    num = jnp.where(first, ra, rb)
    den = jnp.where(first, pltpu.roll(ra, HEAD_DIM, axis=1), pltpu.roll(rb, HEAD_DIM, axis=1))
    return num, den, ma, mb


def _fill_v_aug(v, va_ref, vb_ref):
    first = _lane_first_half(v.shape)
    one = jnp.ones_like(v)
    va_ref[...] = jnp.where(first, v, one)
    vb_ref[...] = jnp.where(first, one, v)


def _na_kernel(q_ref, k_ref, v_ref, bias_ref, o_ref, va_ref, vb_ref):
    rows = q_ref.shape[0] // GRID_W
    win = NA_ROWS * GRID_W
    _fill_v_aug(v_ref[...], va_ref, vb_ref)
    first = _lane_first_half((GRID_W, PAIR_W))

    def row(r, carry):
        rs = jnp.clip(r - NA_ROWS // 2, 0, rows - NA_ROWS)
        delta = r - rs
        qs = pl.ds(pl.multiple_of(r * GRID_W, GRID_W), GRID_W)
        ks = pl.ds(pl.multiple_of(rs * GRID_W, GRID_W), win)
        num, den, _, _ = _pair_attention(q_ref[qs, :], k_ref[ks, :], va_ref[ks, :], vb_ref[ks, :],
                                         bias_ref[0, delta], bias_ref[1, delta], first)
        o_ref[qs, :] = num / den
        return carry

    lax.fori_loop(0, rows, row, 0)


def _na_bias_table(rpb):
    w = np.arange(GRID_W)
    cs = np.clip(w - NA_COLS // 2, 0, GRID_W - NA_COLS)
    kc = np.arange(GRID_W)
    valid = (kc[None, :] >= cs[:, None]) & (kc[None, :] < cs[:, None] + NA_COLS)
    coff = np.clip(kc[None, :] - w[:, None] + NA_COLS - 1, 0, 2 * NA_COLS - 2)
    roff = np.arange(NA_ROWS)[None, :] - np.arange(NA_ROWS)[:, None] + NA_ROWS - 1
    tab = rpb.astype(F32)[:, roff][:, :, :, coff]
    tab = jnp.where(valid[None, None, None], tab, NEG)
    tab = tab.transpose(0, 1, 3, 2, 4)
    return tab.reshape(rpb.shape[0], NA_ROWS, GRID_W, NA_ROWS * GRID_W)


def _na_attention(q, k, v, rpb):
    b, s, width = q.shape
    pairs = width // PAIR_W
    assert s % GRID_W == 0 and s // GRID_W >= NA_ROWS
    bias = _na_bias_table(rpb).reshape(pairs, 2, NA_ROWS, GRID_W, NA_ROWS * GRID_W)
    qkv_spec = pl.BlockSpec((None, s, PAIR_W), lambda bi, j: (bi, 0, j))
    return pl.pallas_call(
        _na_kernel,
        grid=(b, pairs),
        in_specs=[qkv_spec, qkv_spec, qkv_spec,
                  pl.BlockSpec((None, 2, NA_ROWS, GRID_W, NA_ROWS * GRID_W), lambda bi, j: (j, 0, 0, 0, 0))],
        out_specs=pl.BlockSpec((None, s, PAIR_W), lambda bi, j: (bi, 0, j)),
        out_shape=jax.ShapeDtypeStruct((b, s, width), F32),
        scratch_shapes=[pltpu.VMEM((s, PAIR_W), BF16), pltpu.VMEM((s, PAIR_W), BF16)],
        compiler_params=pltpu.CompilerParams(dimension_semantics=("parallel", "parallel"),
                                             vmem_limit_bytes=VMEM_LIMIT),
        name="na_attn",
    )(q, k, v, bias)


def _dil_kernel(q_ref, k_ref, v_ref, mwide_ref, mfull_ref, o_ref,
                qf_ref, kf_ref, vf_ref, qc_ref, kc_ref, va_ref, vb_ref, acc_ref, den_ref, max_ref):
    s = q_ref.shape[0]
    qf_ref[...] = q_ref[...].astype(F32)
    kf_ref[...] = k_ref[...].astype(F32)
    vf_ref[...] = v_ref[...].astype(F32)
    first = _lane_first_half((DIL_BLK, PAIR_W))

    for p, (window, dil) in enumerate(DIL_PAIRS):
        radius = window // (2 * dil)
        cls_len = s // dil
        nblk = cls_len // DIL_BLK
        wide = cls_len >= 2 * DIL_BLK
        win = 2 * DIL_BLK if wide else cls_len
        assert radius == DIL_BLK // 2 and cls_len % DIL_BLK == 0

        def to_classes(c, carry):
            src = pl.ds(c, cls_len, stride=dil)
            dst = pl.ds(pl.multiple_of(c * cls_len, DIL_BLK), cls_len)
            qc_ref[dst, :] = qf_ref[src, :].astype(BF16)
            kc_ref[dst, :] = kf_ref[src, :].astype(BF16)
            v = vf_ref[src, :].astype(BF16)
            fh = _lane_first_half(v.shape)
            one = jnp.ones_like(v)
            va_ref[dst, :] = jnp.where(fh, v, one)
            vb_ref[dst, :] = jnp.where(fh, one, v)
            return carry

        lax.fori_loop(0, dil, to_classes, 0)

        def block(n, carry):
            c = n // nblk
            i = n % nblk
            base = c * cls_len
            qs = pl.ds(pl.multiple_of(base + i * DIL_BLK, DIL_BLK), DIL_BLK)
            if wide:
                ws = jnp.clip(i * DIL_BLK - radius, 0, cls_len - win)
                kind = jnp.where(i == 0, 0, jnp.where(i == nblk - 1, 2, 1))
                mask = mwide_ref[kind]
            else:
                ws = 0
                mask = mfull_ref[...]
            ks = pl.ds(pl.multiple_of(base + ws, radius), win)
            num, den, ma, mb = _pair_attention(qc_ref[qs, :], kc_ref[ks, :], va_ref[ks, :], vb_ref[ks, :],
                                               mask, mask, first)
            tok = pl.ds(c + dil * DIL_BLK * i, DIL_BLK, stride=dil)
            acc_ref[p, tok, :] = num
            den_ref[p, tok, :] = den
            max_ref[p, tok, :] = jnp.where(first, ma, mb)
            return carry

        lax.fori_loop(0, dil * nblk, block, 0)

    m = jnp.maximum(jnp.maximum(max_ref[0], max_ref[1]), max_ref[2])
    num = jnp.zeros_like(m)
    den = jnp.zeros_like(m)
    for p in range(len(DIL_PAIRS)):
        w = jnp.exp(max_ref[p] - m)
        num = num + w * acc_ref[p]
        den = den + w * den_ref[p]
    o_ref[...] = num / den


def _band_mask(kind):
    radius = DIL_BLK // 2
    qq = np.arange(DIL_BLK)[:, None]
    if kind == "full":
        kk = np.arange(DIL_BLK)[None, :]
        shift = 0
    else:
        kk = np.arange(2 * DIL_BLK)[None, :]
        shift = {"first": 0, "inner": radius, "last": DIL_BLK}[kind]
    return np.where(np.abs(kk - qq - shift) <= radius, 0.0, NEG).astype(np.float32)


def _dil_attention(q, k, v):
    b, s, width = q.shape
    pairs = width // PAIR_W
    for window, dil in DIL_PAIRS:
        assert s % (window // 2) == 0 and (s // dil) % DIL_BLK == 0
    mwide = jnp.asarray(np.stack([_band_mask("first"), _band_mask("inner"), _band_mask("last")]))
    mfull = jnp.asarray(_band_mask("full"))
    qkv_spec = pl.BlockSpec((None, s, PAIR_W), lambda bi, j: (bi, 0, j))
    npat = len(DIL_PAIRS)
    return pl.pallas_call(
        _dil_kernel,
        grid=(b, pairs),
        in_specs=[qkv_spec, qkv_spec, qkv_spec,
                  pl.BlockSpec(mwide.shape, lambda bi, j: (0, 0, 0)),
                  pl.BlockSpec(mfull.shape, lambda bi, j: (0, 0))],
        out_specs=pl.BlockSpec((None, s, PAIR_W), lambda bi, j: (bi, 0, j)),
        out_shape=jax.ShapeDtypeStruct((b, s, width), F32),
        scratch_shapes=[pltpu.VMEM((s, PAIR_W), F32)] * 3 + [pltpu.VMEM((s, PAIR_W), BF16)] * 4
        + [pltpu.VMEM((npat, s, PAIR_W), F32)] * 3,
        compiler_params=pltpu.CompilerParams(dimension_semantics=("parallel", "parallel"),
                                             vmem_limit_bytes=VMEM_LIMIT),
        name="dil_attn",
    )(q, k, v, mwide, mfull)


def _split_bf16(x):
    hi = x.astype(BF16)
    return hi, (x - hi.astype(F32)).astype(BF16)


def _out_router_kernel(oa_ref, ob_ref, x_ref, ga_ref, gb_ref, wo_ref, gf_ref, wr_hi_ref, wr_lo_ref, br_ref,
                       x1_ref, h_ref, rt_ref):
    half = oa_ref.shape[1]
    ya = _rms(oa_ref[...], ga_ref[...]).astype(BF16)
    yb = _rms(ob_ref[...], gb_ref[...]).astype(BF16)
    x1 = x_ref[...] + _dot(ya, wo_ref[:half, :]) + _dot(yb, wo_ref[half:, :])
    x1_ref[...] = x1
    h = _rms(x1, gf_ref[...])
    h_ref[...] = _pack_bf16_pairs(h)

    h_hi, h_lo = _split_bf16(h)
    logits = _dot(h_hi, wr_hi_ref[...]) + _dot(h_lo, wr_hi_ref[...]) + _dot(h_hi, wr_lo_ref[...]) + br_ref[...]
    lane = lax.broadcasted_iota(jnp.int32, logits.shape, 1)
    ninf = jnp.float32(-jnp.inf)

    def first_argmax(vals, vmax):
        return jnp.min(jnp.where(vals == vmax, lane, LANES), axis=-1, keepdims=True)

    gl = jnp.where(lane < N_GROUPS, logits, ninf)
    gmax = jnp.max(gl, axis=-1, keepdims=True)
    gsel = first_argmax(gl, gmax)
    gw = 1.0 / jnp.sum(jnp.exp(gl - gmax), axis=-1, keepdims=True)
    lo = N_GROUPS + EXPERTS_PER_GROUP * gsel
    el = jnp.where((lane >= lo) & (lane < lo + EXPERTS_PER_GROUP), logits, ninf)
    v0 = jnp.max(el, axis=-1, keepdims=True)
    i0 = first_argmax(el, v0)
    el = jnp.where(lane == i0, ninf, el)
    v1 = jnp.max(el, axis=-1, keepdims=True)
    i1 = first_argmax(el, v1)
    t = jnp.exp(v1 - v0)
    w0 = gw / (1.0 + t)
    w1 = gw * t / (1.0 + t)
    e0 = (i0 - N_GROUPS).astype(F32)
    e1 = (i1 - N_GROUPS).astype(F32)
    rt_ref[...] = jnp.where(lane == 0, e0, jnp.where(lane == 1, e1, jnp.where(lane == 2, w0, jnp.where(lane == 3, w1, 0.0))))


def _out_router(oa, ob, x2, g_na, g_dil, w_o, g_ffn, w_rg, b_rg, w_re, b_re, tm):
    n, d = x2.shape
    half = d // 2
    wr = jnp.concatenate([w_rg.astype(F32), w_re.astype(F32).transpose(1, 0, 2).reshape(d, N_EXPERTS)], axis=1)
    wr = jnp.pad(wr, ((0, 0), (0, LANES - wr.shape[1])))
    wr_hi = wr.astype(BF16)
    wr_lo = (wr - wr_hi.astype(F32)).astype(BF16)
    br = jnp.pad(jnp.concatenate([b_rg.astype(F32), b_re.astype(F32).reshape(-1)]), (0, LANES - N_GROUPS - N_EXPERTS))
    full = lambda shape: pl.BlockSpec(shape, lambda i: (0,) * len(shape))
    row = lambda w: pl.BlockSpec((tm, w), lambda i: (i, 0))
    return pl.pallas_call(
        _out_router_kernel,
        grid=(n // tm,),
        in_specs=[row(half), row(half), row(d), full((1, half)), full((1, half)), full((d, d)), full((1, d)),
                  full((d, LANES)), full((d, LANES)), full((1, LANES))],
        out_specs=[row(d), row(half), row(LANES)],
        out_shape=[jax.ShapeDtypeStruct((n, d), F32), jax.ShapeDtypeStruct((n, half), jnp.uint32),
                   jax.ShapeDtypeStruct((n, LANES), F32)],
        compiler_params=pltpu.CompilerParams(dimension_semantics=("parallel",), vmem_limit_bytes=VMEM_LIMIT),
        name="out_router",
    )(oa, ob, x2, g_na[None, :].astype(F32), g_dil[None, :].astype(F32), w_o.astype(BF16),
      g_ffn[None, :].astype(F32), wr_hi, wr_lo, br[None, :])


def _pack_bf16_pairs(x):
    w = x.shape[1] // 2
    bits = lax.bitcast_convert_type(x.astype(BF16).astype(F32), jnp.uint32)
    return bits[:, :w] | (bits[:, w:] >> 16)


def _unpack_bf16_pairs(u):
    hi = lax.bitcast_convert_type(u & jnp.uint32(0xFFFF0000), F32)
    lo = lax.bitcast_convert_type(u << 16, F32)
    return hi, lo


def _rank_kernel(rt_ref, rank_ref, cnt_ref, base_ref):
    tm = rt_ref.shape[0]

    @pl.when(pl.program_id(0) == 0)
    def _():
        base_ref[...] = jnp.zeros_like(base_ref)

    rt = rt_ref[...]
    lane = lax.broadcasted_iota(jnp.int32, rt.shape, 1)
    lane_f = lane.astype(F32)
    oh0 = (lane_f == rt[:, 0:1]).astype(F32)
    oh1 = (lane_f == rt[:, 1:2]).astype(F32)
    oh = oh0 + oh1
    earlier = (lax.broadcasted_iota(jnp.int32, (tm, tm), 0) > lax.broadcasted_iota(jnp.int32, (tm, tm), 1))
    before = _dot(earlier.astype(BF16), oh.astype(BF16)) + base_ref[...]
    r0 = jnp.sum(before * oh0, axis=-1, keepdims=True)
    r1 = jnp.sum(before * oh1, axis=-1, keepdims=True)
    rank_ref[...] = jnp.where(lane == 0, r0, jnp.where(lane == 1, r1, 0.0))
    base_ref[...] += jnp.sum(oh, axis=0, keepdims=True)
    cnt_ref[...] = jnp.broadcast_to(base_ref[...], cnt_ref.shape)


def _expert_ranks(rt, tm):
    n = rt.shape[0]
    return pl.pallas_call(
        _rank_kernel,
        grid=(n // tm,),
        in_specs=[pl.BlockSpec((tm, LANES), lambda i: (i, 0))],
        out_specs=[pl.BlockSpec((tm, LANES), lambda i: (i, 0)), pl.BlockSpec((8, LANES), lambda i: (0, 0))],
        out_shape=[jax.ShapeDtypeStruct((n, LANES), F32), jax.ShapeDtypeStruct((8, LANES), F32)],
        scratch_shapes=[pltpu.VMEM((1, LANES), F32)],
        compiler_params=pltpu.CompilerParams(dimension_semantics=("arbitrary",)),
        name="expert_ranks",
    )(rt)


def _sc_mesh():
    return plsc.VectorSubcoreMesh(core_axis_name="c", subcore_axis_name="s",
                                  num_cores=SC_CORES, num_subcores=SC_SUBCORES)


def _sc_dispatch(hp, pos0, pos1, n_out):
    n, w = hp.shape
    workers = SC_CORES * SC_SUBCORES
    per = n // workers
    chunks = per // SC_CHUNK
    assert n % (workers * SC_CHUNK) == 0

    @functools.partial(
        pl.kernel, out_type=jax.ShapeDtypeStruct((n_out, w), hp.dtype), mesh=_sc_mesh(),
        scratch_types=[pltpu.VMEM((chunks, SC_CHUNK), jnp.int32), pltpu.VMEM((chunks, SC_CHUNK), jnp.int32),
                       pltpu.VMEM((SC_CHUNK, w), hp.dtype)],
        name="moe_dispatch")
    def body(h_hbm, p0_hbm, p1_hbm, xs_hbm, i0_v, i1_v, rows_v):
        wid = lax.axis_index("s") * SC_CORES + lax.axis_index("c")
        pltpu.sync_copy(p0_hbm.at[wid], i0_v)
        pltpu.sync_copy(p1_hbm.at[wid], i1_v)

        @pl.loop(0, chunks)
        def _(j):
            pltpu.sync_copy(h_hbm.at[pl.ds(wid * per + j * SC_CHUNK, SC_CHUNK)], rows_v)
            pltpu.sync_copy(rows_v, xs_hbm.at[i0_v.at[j]])
            pltpu.sync_copy(rows_v, xs_hbm.at[i1_v.at[j]])

    return body(hp, pos0.reshape(workers, chunks, SC_CHUNK), pos1.reshape(workers, chunks, SC_CHUNK))


def _sc_collect(ys, pos0, pos1):
    n = pos0.shape[0]
    w = ys.shape[1]
    workers = SC_CORES * SC_SUBCORES
    per = n // workers
    chunks = per // SC_CHUNK
    out = jax.ShapeDtypeStruct((n, w), ys.dtype)

    @functools.partial(
        pl.kernel, out_type=(out, out), mesh=_sc_mesh(),
        scratch_types=[pltpu.VMEM((chunks, SC_CHUNK), jnp.int32), pltpu.VMEM((chunks, SC_CHUNK), jnp.int32),
                       pltpu.VMEM((SC_CHUNK, w), ys.dtype)],
        name="moe_collect")
    def body(ys_hbm, p0_hbm, p1_hbm, y0_hbm, y1_hbm, i0_v, i1_v, rows_v):
        wid = lax.axis_index("s") * SC_CORES + lax.axis_index("c")
        pltpu.sync_copy(p0_hbm.at[wid], i0_v)
        pltpu.sync_copy(p1_hbm.at[wid], i1_v)

        @pl.loop(0, chunks)
        def _(j):
            dst = pl.ds(wid * per + j * SC_CHUNK, SC_CHUNK)
            pltpu.sync_copy(ys_hbm.at[i0_v.at[j]], rows_v)
            pltpu.sync_copy(rows_v, y0_hbm.at[dst])
            pltpu.sync_copy(ys_hbm.at[i1_v.at[j]], rows_v)
            pltpu.sync_copy(rows_v, y1_hbm.at[dst])

    return body(ys, pos0.reshape(workers, chunks, SC_CHUNK), pos1.reshape(workers, chunks, SC_CHUNK))


def _experts_kernel(te_ref, nt_ref, xs_ref, wg_ref, wu_ref, wd_ref, ys_ref):
    @pl.when(pl.program_id(0) < nt_ref[0])
    def _():
        half = wg_ref.shape[0] // 2
        hi, lo = _unpack_bf16_pairs(xs_ref[...])
        hi = hi.astype(BF16)
        lo = lo.astype(BF16)
        a = _dot(hi, wg_ref[:half, :]) + _dot(lo, wg_ref[half:, :])
        u = _dot(hi, wu_ref[:half, :]) + _dot(lo, wu_ref[half:, :])
        act = (a * jax.nn.sigmoid(a) * u).astype(BF16)
        ys_ref[...] = _pack_bf16_pairs(_dot(act, wd_ref[...]))


def _experts(xs, tile_expert, n_tiles, w_gate, w_up, w_down, tmg):
    rows, w = xs.shape
    ne, d, de = w_gate.shape
    live = lambda t, nt: jnp.minimum(t, nt[0] - 1)
    return pl.pallas_call(
        _experts_kernel,
        grid_spec=pltpu.PrefetchScalarGridSpec(
            num_scalar_prefetch=2,
            grid=(rows // tmg,),
            in_specs=[pl.BlockSpec((tmg, w), lambda t, te, nt: (live(t, nt), 0)),
                      pl.BlockSpec((None, d, de), lambda t, te, nt: (te[live(t, nt)], 0, 0)),
                      pl.BlockSpec((None, d, de), lambda t, te, nt: (te[live(t, nt)], 0, 0)),
                      pl.BlockSpec((None, de, d), lambda t, te, nt: (te[live(t, nt)], 0, 0))],
            out_specs=pl.BlockSpec((tmg, w), lambda t, te, nt: (live(t, nt), 0)),
        ),
        out_shape=jax.ShapeDtypeStruct((rows, w), jnp.uint32),
        compiler_params=pltpu.CompilerParams(dimension_semantics=("arbitrary",), vmem_limit_bytes=VMEM_LIMIT),
        name="experts",
    )(tile_expert, n_tiles, xs, w_gate.astype(BF16), w_up.astype(BF16), w_down.astype(BF16))


def _moe(hp, rt, w_gate, w_up, w_down, tm, tmg):
    n = hp.shape[0]
    ne = w_gate.shape[0]
    rank, cnt = _expert_ranks(rt, tm)
    counts = cnt[0, :ne].astype(jnp.int32)
    padded = (counts + tmg - 1) // tmg * tmg
    ends = jnp.cumsum(padded)
    starts = ends - padded
    e0 = rt[:, 0].astype(jnp.int32)
    e1 = rt[:, 1].astype(jnp.int32)
    pos0 = starts[e0] + rank[:, 0].astype(jnp.int32)
    pos1 = starts[e1] + rank[:, 1].astype(jnp.int32)
    rows = 2 * n + ne * tmg
    tile_expert = jnp.minimum(jnp.searchsorted(ends, jnp.arange(rows // tmg, dtype=jnp.int32) * tmg, side="right"),
                              ne - 1).astype(jnp.int32)
    n_tiles = (ends[-1:] // tmg).astype(jnp.int32)
    xs = _sc_dispatch(hp, pos0, pos1, rows)
    ys = _experts(xs, tile_expert, n_tiles, w_gate, w_up, w_down, tmg)
    return _sc_collect(ys, pos0, pos1)


def _ple_kernel(x1_ref, y0_ref, y1_ref, rt_ref, p_ref, g_ref, wg_ref, wp_ref, o_ref):
    rt = rt_ref[...]
    y0 = jnp.concatenate(_unpack_bf16_pairs(y0_ref[...]), axis=1)
    y1 = jnp.concatenate(_unpack_bf16_pairs(y1_ref[...]), axis=1)
    x2 = x1_ref[...] + rt[:, 2:3] * y0 + rt[:, 3:4] * y1
    gate = jax.nn.sigmoid(_dot(_rms(x2, g_ref[...]).astype(BF16), wg_ref[...]))
    o_ref[...] = x2 + gate * _dot(p_ref[...].astype(BF16), wp_ref[...])


def _ple(x1, y0, y1, rt, p2, g_ple, w_gate, w_proj, tm):
    n, d = x1.shape
    dp = p2.shape[1]
    full = lambda shape: pl.BlockSpec(shape, lambda i: (0,) * len(shape))
    row = lambda w: pl.BlockSpec((tm, w), lambda i: (i, 0))
    return pl.pallas_call(
        _ple_kernel,
        grid=(n // tm,),
        in_specs=[row(d), row(d // 2), row(d // 2), row(LANES), row(dp), full((1, d)), full((d, d)), full((dp, d))],
        out_specs=row(d),
        out_shape=jax.ShapeDtypeStruct((n, d), F32),
        compiler_params=pltpu.CompilerParams(dimension_semantics=("parallel",), vmem_limit_bytes=VMEM_LIMIT),
        name="ple",
    )(x1, y0, y1, rt, p2, g_ple[None, :].astype(F32), w_gate.astype(BF16), w_proj.astype(BF16))


def _layer(x, p_l, g_attn, w_qkv, q_norm_na, k_norm_na, rpb_na, q_norm_dil, k_norm_dil, g_out_na, g_out_dil,
           w_o, g_ffn, w_rg, b_rg, w_re, b_re, w_exp_gate, w_exp_up, w_exp_down, g_ple, w_ple_gate, w_ple_proj):
    b, s, d = x.shape
    n = b * s
    half = d // 2
    assert d == N_HEADS * HEAD_DIM and half == N_HEADS_NA * HEAD_DIM
    tm = 512
    assert s % tm == 0
    x2 = x.reshape(n, d)
    qa, ka, va, qb, kb, vb = _qkv_proj(x2, g_attn, w_qkv, q_norm_na, k_norm_na, q_norm_dil, k_norm_dil, s, tm)
    seq = lambda t: t.reshape(b, s, half)
    oa = _na_attention(seq(qa), seq(ka), seq(va), rpb_na).reshape(n, half)
    ob = _dil_attention(seq(qb), seq(kb), seq(vb)).reshape(n, half)
    x1, h, rt = _out_router(oa, ob, x2, g_out_na, g_out_dil, w_o, g_ffn, w_rg, b_rg, w_re, b_re, tm)
    y0, y1 = _moe(h, rt, w_exp_gate, w_exp_up, w_exp_down, tm, tm)
    out = _ple(x1, y0, y1, rt, p_l.reshape(n, -1), g_ple, w_ple_gate, w_ple_proj, tm)
    return out.reshape(b, s, d)


def kernel(x, p, g_attn, w_qkv, q_norm_na, k_norm_na, rpb_na, q_norm_dil, k_norm_dil, g_out_na, g_out_dil, w_o,
           g_ffn, w_router_group, b_router_group, w_router_expert, b_router_expert, w_exp_gate, w_exp_up,
           w_exp_down, g_ple, w_ple_gate, w_ple_proj):
    for i in range(p.shape[0]):
        x = _layer(x, p[i], g_attn[i], w_qkv[i], q_norm_na[i], k_norm_na[i], rpb_na[i], q_norm_dil[i],
                   k_norm_dil[i], g_out_na[i], g_out_dil[i], w_o[i], g_ffn[i], w_router_group[i],
                   b_router_group[i], w_router_expert[i], b_router_expert[i], w_exp_gate[i], w_exp_up[i],
                   w_exp_down[i], g_ple[i], w_ple_gate[i], w_ple_proj[i])
    return x
```

```python
import functools

import numpy as np
import jax
import jax.numpy as jnp
from jax import lax
from jax.experimental import pallas as pl
from jax.experimental.pallas import tpu as pltpu
from jax.experimental.pallas import tpu_sc as plsc

HEAD_DIM = 64
N_HEADS = 16
N_HEADS_NA = 8
GRID_W = 64
NA_ROWS = 8
NA_COLS = 16
DIL_PAIRS = ((128, 1), (512, 4), (2048, 16))
ROPE_THETA = 10000.0
N_GROUPS = 4
EXPERTS_PER_GROUP = 8
N_EXPERTS = N_GROUPS * EXPERTS_PER_GROUP
EPS = 1e-6
NEG = -1e30

LANES = 128
PAIR_W = 2 * HEAD_DIM
DIL_BLK = 128
VMEM_LIMIT = 56 * 1024 * 1024
SC_CORES = 2
SC_SUBCORES = 16
SC_CHUNK = 128

F32 = jnp.float32
BF16 = jnp.bfloat16


def _dot(a, b):
    return jnp.dot(a, b, preferred_element_type=F32)


def _dot_nt(a, b):
    return lax.dot_general(a, b, (((1,), (1,)), ((), ())), preferred_element_type=F32)


def _rms(x, gain):
    return x * lax.rsqrt(jnp.mean(x * x, axis=-1, keepdims=True) + EPS) * gain


def _lane_first_half(shape):
    return lax.broadcasted_iota(jnp.int32, shape, len(shape) - 1) < HEAD_DIM


def _qkv_kernel(x_ref, g_ref, w_ref, gq_na_ref, gk_na_ref, gq_dil_ref, gk_dil_ref, cos_ref, sin_ref,
                hsum_ref, qa_ref, ka_ref, va_ref, qb_ref, kb_ref, vb_ref):
    d = x_ref.shape[1]
    half = d // 2
    scale = HEAD_DIM ** -0.5
    h = _rms(x_ref[...], g_ref[...]).astype(BF16)

    def proj(col):
        return _dot(h, w_ref[:, col:col + half])

    def head_norm(y, gain):
        sq = (y * y).astype(BF16)
        w = hsum_ref.shape[0]
        ms = jnp.concatenate([_dot(sq[:, c:c + w], hsum_ref[...]) for c in range(0, half, w)], axis=1)
        return y * lax.rsqrt(ms + EPS) * gain

    def rope(y):
        lane = lax.broadcasted_iota(jnp.int32, (y.shape[0], LANES), 1)
        lower = (lane % HEAD_DIM) < HEAD_DIM // 2
        cos = cos_ref[...]
        sin = sin_ref[...]
        outs = []
        for c in range(0, half, LANES):
            yc = y[:, c:c + LANES]
            up = pltpu.roll(yc, LANES - HEAD_DIM // 2, axis=1)
            down = pltpu.roll(yc, HEAD_DIM // 2, axis=1)
            outs.append(yc * cos + jnp.where(lower, up, down) * sin)
        return jnp.concatenate(outs, axis=1)

    qa_ref[...] = (head_norm(proj(0), gq_na_ref[...]) * scale).astype(BF16)
    qb_ref[...] = (rope(head_norm(proj(half), gq_dil_ref[...])) * scale).astype(BF16)
    ka_ref[...] = head_norm(proj(d), gk_na_ref[...]).astype(BF16)
    kb_ref[...] = rope(head_norm(proj(d + half), gk_dil_ref[...])).astype(BF16)
    va_ref[...] = proj(2 * d).astype(BF16)
    vb_ref[...] = proj(2 * d + half).astype(BF16)


def _qkv_proj(x2, g_attn, w_qkv, gq_na, gk_na, gq_dil, gk_dil, seq, tm):
    n, d = x2.shape
    half = d // 2
    pos = jnp.arange(seq, dtype=F32)
    inv = ROPE_THETA ** (-jnp.arange(HEAD_DIM // 2, dtype=F32) / (HEAD_DIM // 2))
    ang = pos[:, None] * inv[None, :]
    cos = jnp.tile(jnp.cos(ang), (1, LANES // (HEAD_DIM // 2)))
    sin = jnp.tile(jnp.concatenate([-jnp.sin(ang), jnp.sin(ang)], axis=1), (1, LANES // HEAD_DIM))
    hs_w = 2 * LANES
    blk = np.arange(hs_w) // HEAD_DIM
    hsum = jnp.asarray((blk[:, None] == blk[None, :]).astype(np.float32) / HEAD_DIM, BF16)
    tile_gain = lambda g: jnp.tile(g.astype(F32), half // HEAD_DIM)[None, :]
    steps_per_seq = seq // tm
    full = lambda shape: pl.BlockSpec(shape, lambda i: (0,) * len(shape))
    out = jax.ShapeDtypeStruct((n, half), BF16)
    return pl.pallas_call(
        _qkv_kernel,
        grid=(n // tm,),
        in_specs=[
            pl.BlockSpec((tm, d), lambda i: (i, 0)),
            full((1, d)),
            full((d, 3 * d)),
            full((1, half)), full((1, half)), full((1, half)), full((1, half)),
            pl.BlockSpec((tm, LANES), lambda i: (i % steps_per_seq, 0)),
            pl.BlockSpec((tm, LANES), lambda i: (i % steps_per_seq, 0)),
            full((hs_w, hs_w)),
        ],
        out_specs=[pl.BlockSpec((tm, half), lambda i: (i, 0))] * 6,
        out_shape=[out] * 6,
        compiler_params=pltpu.CompilerParams(dimension_semantics=("parallel",), vmem_limit_bytes=VMEM_LIMIT),
        name="qkv_proj",
    )(x2, g_attn[None, :].astype(F32), w_qkv.astype(BF16), tile_gain(gq_na), tile_gain(gk_na),
      tile_gain(gq_dil), tile_gain(gk_dil), cos, sin, hsum)


def _pair_attention(q, kwin, va_win, vb_win, bias_a, bias_b, first):
    zero = jnp.zeros_like(q)
    sa = _dot_nt(jnp.where(first, q, zero), kwin) + bias_a
    sb = _dot_nt(jnp.where(first, zero, q), kwin) + bias_b
    ma = jnp.max(sa, axis=-1, keepdims=True)
    mb = jnp.max(sb, axis=-1, keepdims=True)
    ra = _dot(jnp.exp(sa - ma).astype(BF16), va_win)
    rb = _dot(jnp.exp(sb - mb).astype(BF16), vb_win)
    num = jnp.where(first, ra, rb)
    den = jnp.where(first, pltpu.roll(ra, HEAD_DIM, axis=1), pltpu.roll(rb, HEAD_DIM, axis=1))
    return num, den, ma, mb


def _fill_v_aug(v, va_ref, vb_ref):
    first = _lane_first_half(v.shape)
    one = jnp.ones_like(v)
    va_ref[...] = jnp.where(first, v, one)
    vb_ref[...] = jnp.where(first, one, v)


def _na_kernel(q_ref, k_ref, v_ref, bias_ref, o_ref, va_ref, vb_ref):
    rows = q_ref.shape[0] // GRID_W
    win = NA_ROWS * GRID_W
    _fill_v_aug(v_ref[...], va_ref, vb_ref)
    first = _lane_first_half((GRID_W, PAIR_W))

    def row(r, carry):
        rs = jnp.clip(r - NA_ROWS // 2, 0, rows - NA_ROWS)
        delta = r - rs
        qs = pl.ds(pl.multiple_of(r * GRID_W, GRID_W), GRID_W)
        ks = pl.ds(pl.multiple_of(rs * GRID_W, GRID_W), win)
        num, den, _, _ = _pair_attention(q_ref[qs, :], k_ref[ks, :], va_ref[ks, :], vb_ref[ks, :],
                                         bias_ref[0, delta], bias_ref[1, delta], first)
        o_ref[qs, :] = num / den
        return carry

    lax.fori_loop(0, rows, row, 0)


def _na_bias_table(rpb):
    w = np.arange(GRID_W)
    cs = np.clip(w - NA_COLS // 2, 0, GRID_W - NA_COLS)
    kc = np.arange(GRID_W)
    valid = (kc[None, :] >= cs[:, None]) & (kc[None, :] < cs[:, None] + NA_COLS)
    coff = np.clip(kc[None, :] - w[:, None] + NA_COLS - 1, 0, 2 * NA_COLS - 2)
    roff = np.arange(NA_ROWS)[None, :] - np.arange(NA_ROWS)[:, None] + NA_ROWS - 1
    tab = rpb.astype(F32)[:, roff][:, :, :, coff]
    tab = jnp.where(valid[None, None, None], tab, NEG)
    tab = tab.transpose(0, 1, 3, 2, 4)
    return tab.reshape(rpb.shape[0], NA_ROWS, GRID_W, NA_ROWS * GRID_W)


def _na_attention(q, k, v, rpb):
    b, s, width = q.shape
    pairs = width // PAIR_W
    assert s % GRID_W == 0 and s // GRID_W >= NA_ROWS
    bias = _na_bias_table(rpb).reshape(pairs, 2, NA_ROWS, GRID_W, NA_ROWS * GRID_W)
    qkv_spec = pl.BlockSpec((None, s, PAIR_W), lambda bi, j: (bi, 0, j))
    return pl.pallas_call(
        _na_kernel,
        grid=(b, pairs),
        in_specs=[qkv_spec, qkv_spec, qkv_spec,
                  pl.BlockSpec((None, 2, NA_ROWS, GRID_W, NA_ROWS * GRID_W), lambda bi, j: (j, 0, 0, 0, 0))],
        out_specs=pl.BlockSpec((None, s, PAIR_W), lambda bi, j: (bi, 0, j)),
        out_shape=jax.ShapeDtypeStruct((b, s, width), F32),
        scratch_shapes=[pltpu.VMEM((s, PAIR_W), BF16), pltpu.VMEM((s, PAIR_W), BF16)],
        compiler_params=pltpu.CompilerParams(dimension_semantics=("parallel", "parallel"),
                                             vmem_limit_bytes=VMEM_LIMIT),
        name="na_attn",
    )(q, k, v, bias)


def _dil_kernel(q_ref, k_ref, v_ref, mwide_ref, mfull_ref, o_ref,
                qf_ref, kf_ref, vf_ref, qc_ref, kc_ref, va_ref, vb_ref, acc_ref, den_ref, max_ref):
    s = q_ref.shape[0]
    qf_ref[...] = q_ref[...].astype(F32)
    kf_ref[...] = k_ref[...].astype(F32)
    vf_ref[...] = v_ref[...].astype(F32)
    first = _lane_first_half((DIL_BLK, PAIR_W))

    for p, (window, dil) in enumerate(DIL_PAIRS):
        radius = window // (2 * dil)
        cls_len = s // dil
        nblk = cls_len // DIL_BLK
        wide = cls_len >= 2 * DIL_BLK
        win = 2 * DIL_BLK if wide else cls_len
        assert radius == DIL_BLK // 2 and cls_len % DIL_BLK == 0

        def to_classes(c, carry):
            src = pl.ds(c, cls_len, stride=dil)
            dst = pl.ds(pl.multiple_of(c * cls_len, DIL_BLK), cls_len)
            qc_ref[dst, :] = qf_ref[src, :].astype(BF16)
            kc_ref[dst, :] = kf_ref[src, :].astype(BF16)
            v = vf_ref[src, :].astype(BF16)
            fh = _lane_first_half(v.shape)
            one = jnp.ones_like(v)
            va_ref[dst, :] = jnp.where(fh, v, one)
            vb_ref[dst, :] = jnp.where(fh, one, v)
            return carry

        lax.fori_loop(0, dil, to_classes, 0)

        def block(n, carry):
            c = n // nblk
            i = n % nblk
            base = c * cls_len
            qs = pl.ds(pl.multiple_of(base + i * DIL_BLK, DIL_BLK), DIL_BLK)
            if wide:
                ws = jnp.clip(i * DIL_BLK - radius, 0, cls_len - win)
                kind = jnp.where(i == 0, 0, jnp.where(i == nblk - 1, 2, 1))
                mask = mwide_ref[kind]
            else:
                ws = 0
                mask = mfull_ref[...]
            ks = pl.ds(pl.multiple_of(base + ws, radius), win)
            num, den, ma, mb = _pair_attention(qc_ref[qs, :], kc_ref[ks, :], va_ref[ks, :], vb_ref[ks, :],
                                               mask, mask, first)
            tok = pl.ds(c + dil * DIL_BLK * i, DIL_BLK, stride=dil)
            acc_ref[p, tok, :] = num
            den_ref[p, tok, :] = den
            max_ref[p, tok, :] = jnp.where(first, ma, mb)
            return carry

        lax.fori_loop(0, dil * nblk, block, 0)

    m = jnp.maximum(jnp.maximum(max_ref[0], max_ref[1]), max_ref[2])
    num = jnp.zeros_like(m)
    den = jnp.zeros_like(m)
    for p in range(len(DIL_PAIRS)):
        w = jnp.exp(max_ref[p] - m)
        num = num + w * acc_ref[p]
        den = den + w * den_ref[p]
    o_ref[...] = num / den


def _band_mask(kind):
    radius = DIL_BLK // 2
    qq = np.arange(DIL_BLK)[:, None]
    if kind == "full":
        kk = np.arange(DIL_BLK)[None, :]
        shift = 0
    else:
        kk = np.arange(2 * DIL_BLK)[None, :]
        shift = {"first": 0, "inner": radius, "last": DIL_BLK}[kind]
    return np.where(np.abs(kk - qq - shift) <= radius, 0.0, NEG).astype(np.float32)


def _dil_attention(q, k, v):
    b, s, width = q.shape
    pairs = width // PAIR_W
    for window, dil in DIL_PAIRS:
        assert s % (window // 2) == 0 and (s // dil) % DIL_BLK == 0
    mwide = jnp.asarray(np.stack([_band_mask("first"), _band_mask("inner"), _band_mask("last")]))
    mfull = jnp.asarray(_band_mask("full"))
    qkv_spec = pl.BlockSpec((None, s, PAIR_W), lambda bi, j: (bi, 0, j))
    npat = len(DIL_PAIRS)
    return pl.pallas_call(
        _dil_kernel,
        grid=(b, pairs),
        in_specs=[qkv_spec, qkv_spec, qkv_spec,
                  pl.BlockSpec(mwide.shape, lambda bi, j: (0, 0, 0)),
                  pl.BlockSpec(mfull.shape, lambda bi, j: (0, 0))],
        out_specs=pl.BlockSpec((None, s, PAIR_W), lambda bi, j: (bi, 0, j)),
        out_shape=jax.ShapeDtypeStruct((b, s, width), F32),
        scratch_shapes=[pltpu.VMEM((s, PAIR_W), F32)] * 3 + [pltpu.VMEM((s, PAIR_W), BF16)] * 4
        + [pltpu.VMEM((npat, s, PAIR_W), F32)] * 3,
        compiler_params=pltpu.CompilerParams(dimension_semantics=("parallel", "parallel"),
                                             vmem_limit_bytes=VMEM_LIMIT),
        name="dil_attn",
    )(q, k, v, mwide, mfull)


def _split_bf16(x):
    hi = x.astype(BF16)
    return hi, (x - hi.astype(F32)).astype(BF16)


def _out_router_kernel(oa_ref, ob_ref, x_ref, ga_ref, gb_ref, wo_ref, gf_ref, wr_hi_ref, wr_lo_ref, br_ref,
                       x1_ref, h_ref, rt_ref, rtt_ref):
    half = oa_ref.shape[1]
    ya = _rms(oa_ref[...], ga_ref[...]).astype(BF16)
    yb = _rms(ob_ref[...], gb_ref[...]).astype(BF16)
    x1 = x_ref[...] + _dot(ya, wo_ref[:half, :]) + _dot(yb, wo_ref[half:, :])
    x1_ref[...] = x1
    h = _rms(x1, gf_ref[...])
    h_ref[...] = _pack_bf16_pairs(h)

    h_hi, h_lo = _split_bf16(h)
    logits = _dot(h_hi, wr_hi_ref[...]) + _dot(h_lo, wr_hi_ref[...]) + _dot(h_hi, wr_lo_ref[...]) + br_ref[...]
    lane = lax.broadcasted_iota(jnp.int32, logits.shape, 1)
    ninf = jnp.float32(-jnp.inf)

    def first_argmax(vals, vmax):
        return jnp.min(jnp.where(vals == vmax, lane, LANES), axis=-1, keepdims=True)

    gl = jnp.where(lane < N_GROUPS, logits, ninf)
    gmax = jnp.max(gl, axis=-1, keepdims=True)
    gsel = first_argmax(gl, gmax)
    gw = 1.0 / jnp.sum(jnp.exp(gl - gmax), axis=-1, keepdims=True)
    lo = N_GROUPS + EXPERTS_PER_GROUP * gsel
    el = jnp.where((lane >= lo) & (lane < lo + EXPERTS_PER_GROUP), logits, ninf)
    v0 = jnp.max(el, axis=-1, keepdims=True)
    i0 = first_argmax(el, v0)
    el = jnp.where(lane == i0, ninf, el)
    v1 = jnp.max(el, axis=-1, keepdims=True)
    i1 = first_argmax(el, v1)
    t = jnp.exp(v1 - v0)
    w0 = gw / (1.0 + t)
    w1 = gw * t / (1.0 + t)
    e0 = (i0 - N_GROUPS).astype(F32)
    e1 = (i1 - N_GROUPS).astype(F32)
    rt = jnp.where(lane == 0, e0, jnp.where(lane == 1, e1, jnp.where(lane == 2, w0, jnp.where(lane == 3, w1, 0.0))))
    rt_ref[...] = rt
    rtt_ref[...] = rt.T[:rtt_ref.shape[0], :]


def _out_router(oa, ob, x2, g_na, g_dil, w_o, g_ffn, w_rg, b_rg, w_re, b_re, tm):
    n, d = x2.shape
    half = d // 2
    wr = jnp.concatenate([w_rg.astype(F32), w_re.astype(F32).transpose(1, 0, 2).reshape(d, N_EXPERTS)], axis=1)
    wr = jnp.pad(wr, ((0, 0), (0, LANES - wr.shape[1])))
    wr_hi = wr.astype(BF16)
    wr_lo = (wr - wr_hi.astype(F32)).astype(BF16)
    br = jnp.pad(jnp.concatenate([b_rg.astype(F32), b_re.astype(F32).reshape(-1)]), (0, LANES - N_GROUPS - N_EXPERTS))
    full = lambda shape: pl.BlockSpec(shape, lambda i: (0,) * len(shape))
    row = lambda w: pl.BlockSpec((tm, w), lambda i: (i, 0))
    return pl.pallas_call(
        _out_router_kernel,
        grid=(n // tm,),
        in_specs=[row(half), row(half), row(d), full((1, half)), full((1, half)), full((d, d)), full((1, d)),
                  full((d, LANES)), full((d, LANES)), full((1, LANES))],
        out_specs=[row(d), row(half), row(LANES), pl.BlockSpec((8, tm), lambda i: (0, i))],
        out_shape=[jax.ShapeDtypeStruct((n, d), F32), jax.ShapeDtypeStruct((n, half), jnp.uint32),
                   jax.ShapeDtypeStruct((n, LANES), F32), jax.ShapeDtypeStruct((8, n), F32)],
        compiler_params=pltpu.CompilerParams(dimension_semantics=("parallel",), vmem_limit_bytes=VMEM_LIMIT),
        name="out_router",
    )(oa, ob, x2, g_na[None, :].astype(F32), g_dil[None, :].astype(F32), w_o.astype(BF16),
      g_ffn[None, :].astype(F32), wr_hi, wr_lo, br[None, :])


def _pack_bf16_pairs(x):
    w = x.shape[1] // 2
    bits = lax.bitcast_convert_type(x.astype(BF16).astype(F32), jnp.uint32)
    return bits[:, :w] | (bits[:, w:] >> 16)


def _unpack_bf16_pairs(u):
    hi = lax.bitcast_convert_type(u & jnp.uint32(0xFFFF0000), F32)
    lo = lax.bitcast_convert_type(u << 16, F32)
    return hi, lo


def _slot_one_hots(rtt):
    sub = lax.broadcasted_iota(jnp.int32, (LANES, rtt.shape[1]), 0).astype(F32)
    return (sub == rtt[0:1, :]).astype(F32), (sub == rtt[1:2, :]).astype(F32)


def _count_kernel(rtt_ref, cnt_ref):
    @pl.when(pl.program_id(0) == 0)
    def _():
        cnt_ref[...] = jnp.zeros_like(cnt_ref)

    oh0, oh1 = _slot_one_hots(rtt_ref[...])
    cnt_ref[...] += jnp.sum(oh0 + oh1, axis=1, keepdims=True)


def _position_kernel(rtt_ref, start_ref, pos_ref, base_ref):
    tm = rtt_ref.shape[1]

    @pl.when(pl.program_id(0) == 0)
    def _():
        base_ref[...] = start_ref[...]

    oh0, oh1 = _slot_one_hots(rtt_ref[...])
    oh = oh0 + oh1
    earlier = lax.broadcasted_iota(jnp.int32, (tm, tm), 0) < lax.broadcasted_iota(jnp.int32, (tm, tm), 1)
    before = _dot(oh.astype(BF16), earlier.astype(BF16)) + base_ref[:, 0:1]
    p0 = jnp.sum(before * oh0, axis=0, keepdims=True)
    p1 = jnp.sum(before * oh1, axis=0, keepdims=True)
    row = lax.broadcasted_iota(jnp.int32, pos_ref.shape, 0)
    pos_ref[...] = jnp.where(row == 0, p0, jnp.where(row == 1, p1, 0.0)).astype(jnp.int32)
    base_ref[...] += jnp.sum(oh, axis=1, keepdims=True)


def _expert_counts(rtt, tm):
    n = rtt.shape[1]
    return pl.pallas_call(
        _count_kernel,
        grid=(n // tm,),
        in_specs=[pl.BlockSpec((8, tm), lambda i: (0, i))],
        out_specs=pl.BlockSpec((LANES, LANES), lambda i: (0, 0)),
        out_shape=jax.ShapeDtypeStruct((LANES, LANES), F32),
        compiler_params=pltpu.CompilerParams(dimension_semantics=("arbitrary",)),
        name="expert_counts",
    )(rtt)


def _expert_positions(rtt, starts, tm):
    n = rtt.shape[1]
    return pl.pallas_call(
        _position_kernel,
        grid=(n // tm,),
        in_specs=[pl.BlockSpec((8, tm), lambda i: (0, i)), pl.BlockSpec((LANES, LANES), lambda i: (0, 0))],
        out_specs=pl.BlockSpec((8, tm), lambda i: (0, i)),
        out_shape=jax.ShapeDtypeStruct((8, n), jnp.int32),
        scratch_shapes=[pltpu.VMEM((LANES, LANES), F32)],
        compiler_params=pltpu.CompilerParams(dimension_semantics=("arbitrary",)),
        name="expert_positions",
    )(rtt, starts)


def _sc_mesh():
    return plsc.VectorSubcoreMesh(core_axis_name="c", subcore_axis_name="s",
                                  num_cores=SC_CORES, num_subcores=SC_SUBCORES)


def _sc_dispatch(hp, pos0, pos1, n_out):
    n, w = hp.shape
    workers = SC_CORES * SC_SUBCORES
    per = n // workers
    chunks = per // SC_CHUNK
    assert n % (workers * SC_CHUNK) == 0

    @functools.partial(
        pl.kernel, out_type=jax.ShapeDtypeStruct((n_out, w), hp.dtype), mesh=_sc_mesh(),
        scratch_types=[pltpu.VMEM((chunks, SC_CHUNK), jnp.int32), pltpu.VMEM((chunks, SC_CHUNK), jnp.int32),
                       pltpu.VMEM((SC_CHUNK, w), hp.dtype)],
        name="moe_dispatch")
    def body(h_hbm, p0_hbm, p1_hbm, xs_hbm, i0_v, i1_v, rows_v):
        wid = lax.axis_index("s") * SC_CORES + lax.axis_index("c")
        pltpu.sync_copy(p0_hbm.at[wid], i0_v)
        pltpu.sync_copy(p1_hbm.at[wid], i1_v)

        @pl.loop(0, chunks)
        def _(j):
            pltpu.sync_copy(h_hbm.at[pl.ds(wid * per + j * SC_CHUNK, SC_CHUNK)], rows_v)
            pltpu.sync_copy(rows_v, xs_hbm.at[i0_v.at[j]])
            pltpu.sync_copy(rows_v, xs_hbm.at[i1_v.at[j]])

    return body(hp, pos0.reshape(workers, chunks, SC_CHUNK), pos1.reshape(workers, chunks, SC_CHUNK))


def _sc_collect(ys, pos0, pos1):
    n = pos0.shape[0]
    w = ys.shape[1]
    workers = SC_CORES * SC_SUBCORES
    per = n // workers
    chunks = per // SC_CHUNK
    out = jax.ShapeDtypeStruct((n, w), ys.dtype)

    @functools.partial(
        pl.kernel, out_type=(out, out), mesh=_sc_mesh(),
        scratch_types=[pltpu.VMEM((chunks, SC_CHUNK), jnp.int32), pltpu.VMEM((chunks, SC_CHUNK), jnp.int32),
                       pltpu.VMEM((SC_CHUNK, w), ys.dtype)],
        name="moe_collect")
    def body(ys_hbm, p0_hbm, p1_hbm, y0_hbm, y1_hbm, i0_v, i1_v, rows_v):
        wid = lax.axis_index("s") * SC_CORES + lax.axis_index("c")
        pltpu.sync_copy(p0_hbm.at[wid], i0_v)
        pltpu.sync_copy(p1_hbm.at[wid], i1_v)

        @pl.loop(0, chunks)
        def _(j):
            dst = pl.ds(wid * per + j * SC_CHUNK, SC_CHUNK)
            pltpu.sync_copy(ys_hbm.at[i0_v.at[j]], rows_v)
            pltpu.sync_copy(rows_v, y0_hbm.at[dst])
            pltpu.sync_copy(ys_hbm.at[i1_v.at[j]], rows_v)
            pltpu.sync_copy(rows_v, y1_hbm.at[dst])

    return body(ys, pos0.reshape(workers, chunks, SC_CHUNK), pos1.reshape(workers, chunks, SC_CHUNK))


def _experts_kernel(te_ref, nt_ref, xs_ref, wg_ref, wu_ref, wd_ref, ys_ref):
    @pl.when(pl.program_id(0) < nt_ref[0])
    def _():
        half = wg_ref.shape[0] // 2
        hi, lo = _unpack_bf16_pairs(xs_ref[...])
        hi = hi.astype(BF16)
        lo = lo.astype(BF16)
        a = _dot(hi, wg_ref[:half, :]) + _dot(lo, wg_ref[half:, :])
        u = _dot(hi, wu_ref[:half, :]) + _dot(lo, wu_ref[half:, :])
        act = (a * jax.nn.sigmoid(a) * u).astype(BF16)
        ys_ref[...] = _pack_bf16_pairs(_dot(act, wd_ref[...]))


def _experts(xs, tile_expert, n_tiles, w_gate, w_up, w_down, tmg):
    rows, w = xs.shape
    ne, d, de = w_gate.shape
    live = lambda t, nt: jnp.minimum(t, nt[0] - 1)
    return pl.pallas_call(
        _experts_kernel,
        grid_spec=pltpu.PrefetchScalarGridSpec(
            num_scalar_prefetch=2,
            grid=(rows // tmg,),
            in_specs=[pl.BlockSpec((tmg, w), lambda t, te, nt: (live(t, nt), 0)),
                      pl.BlockSpec((None, d, de), lambda t, te, nt: (te[live(t, nt)], 0, 0)),
                      pl.BlockSpec((None, d, de), lambda t, te, nt: (te[live(t, nt)], 0, 0)),
                      pl.BlockSpec((None, de, d), lambda t, te, nt: (te[live(t, nt)], 0, 0))],
            out_specs=pl.BlockSpec((tmg, w), lambda t, te, nt: (live(t, nt), 0)),
        ),
        out_shape=jax.ShapeDtypeStruct((rows, w), jnp.uint32),
        compiler_params=pltpu.CompilerParams(dimension_semantics=("arbitrary",), vmem_limit_bytes=VMEM_LIMIT),
        name="experts",
    )(tile_expert, n_tiles, xs, w_gate.astype(BF16), w_up.astype(BF16), w_down.astype(BF16))


def _moe(hp, rtt, w_gate, w_up, w_down, tm, tmg):
    n = hp.shape[0]
    ne = w_gate.shape[0]
    counts = _expert_counts(rtt, tm)[:ne, 0].astype(jnp.int32)
    padded = (counts + tmg - 1) // tmg * tmg
    ends = jnp.cumsum(padded)
    starts = jnp.pad((ends - padded).astype(F32), (0, LANES - ne))
    pos = _expert_positions(rtt, jnp.broadcast_to(starts[:, None], (LANES, LANES)), tm)
    pos0, pos1 = pos[0], pos[1]
    rows = 2 * n + ne * tmg
    tile_start = jnp.arange(rows // tmg, dtype=jnp.int32) * tmg
    tile_expert = jnp.minimum(jnp.sum(tile_start[:, None] >= ends[None, :], axis=1), ne - 1).astype(jnp.int32)
    n_tiles = (ends[-1:] // tmg).astype(jnp.int32)
    xs = _sc_dispatch(hp, pos0, pos1, rows)
    ys = _experts(xs, tile_expert, n_tiles, w_gate, w_up, w_down, tmg)
    return _sc_collect(ys, pos0, pos1)


def _ple_kernel(x1_ref, y0_ref, y1_ref, rt_ref, p_ref, g_ref, wg_ref, wp_ref, o_ref):
    rt = rt_ref[...]
    y0 = jnp.concatenate(_unpack_bf16_pairs(y0_ref[...]), axis=1)
    y1 = jnp.concatenate(_unpack_bf16_pairs(y1_ref[...]), axis=1)
    x2 = x1_ref[...] + rt[:, 2:3] * y0 + rt[:, 3:4] * y1
    gate = jax.nn.sigmoid(_dot(_rms(x2, g_ref[...]).astype(BF16), wg_ref[...]))
    o_ref[...] = x2 + gate * _dot(p_ref[...].astype(BF16), wp_ref[...])


def _ple(x1, y0, y1, rt, p2, g_ple, w_gate, w_proj, tm):
    n, d = x1.shape
    dp = p2.shape[1]
    full = lambda shape: pl.BlockSpec(shape, lambda i: (0,) * len(shape))
    row = lambda w: pl.BlockSpec((tm, w), lambda i: (i, 0))
    return pl.pallas_call(
        _ple_kernel,
        grid=(n // tm,),
        in_specs=[row(d), row(d // 2), row(d // 2), row(LANES), row(dp), full((1, d)), full((d, d)), full((dp, d))],
        out_specs=row(d),
        out_shape=jax.ShapeDtypeStruct((n, d), F32),
        compiler_params=pltpu.CompilerParams(dimension_semantics=("parallel",), vmem_limit_bytes=VMEM_LIMIT),
        name="ple",
    )(x1, y0, y1, rt, p2, g_ple[None, :].astype(F32), w_gate.astype(BF16), w_proj.astype(BF16))


def _layer(x, p_l, g_attn, w_qkv, q_norm_na, k_norm_na, rpb_na, q_norm_dil, k_norm_dil, g_out_na, g_out_dil,
           w_o, g_ffn, w_rg, b_rg, w_re, b_re, w_exp_gate, w_exp_up, w_exp_down, g_ple, w_ple_gate, w_ple_proj):
    b, s, d = x.shape
    n = b * s
    half = d // 2
    assert d == N_HEADS * HEAD_DIM and half == N_HEADS_NA * HEAD_DIM
    tm = 512
    assert s % tm == 0
    x2 = x.reshape(n, d)
    qa, ka, va, qb, kb, vb = _qkv_proj(x2, g_attn, w_qkv, q_norm_na, k_norm_na, q_norm_dil, k_norm_dil, s, tm)
    seq = lambda t: t.reshape(b, s, half)
    oa = _na_attention(seq(qa), seq(ka), seq(va), rpb_na).reshape(n, half)
    ob = _dil_attention(seq(qb), seq(kb), seq(vb)).reshape(n, half)
    x1, h, rt, rtt = _out_router(oa, ob, x2, g_out_na, g_out_dil, w_o, g_ffn, w_rg, b_rg, w_re, b_re, tm)
    y0, y1 = _moe(h, rtt, w_exp_gate, w_exp_up, w_exp_down, tm, tm)
    out = _ple(x1, y0, y1, rt, p_l.reshape(n, -1), g_ple, w_ple_gate, w_ple_proj, tm)
    return out.reshape(b, s, d)


def kernel(x, p, g_attn, w_qkv, q_norm_na, k_norm_na, rpb_na, q_norm_dil, k_norm_dil, g_out_na, g_out_dil, w_o,
           g_ffn, w_router_group, b_router_group, w_router_expert, b_router_expert, w_exp_gate, w_exp_up,
           w_exp_down, g_ple, w_ple_gate, w_ple_proj):
    for i in range(p.shape[0]):
        x = _layer(x, p[i], g_attn[i], w_qkv[i], q_norm_na[i], k_norm_na[i], rpb_na[i], q_norm_dil[i],
                   k_norm_dil[i], g_out_na[i], g_out_dil[i], w_o[i], g_ffn[i], w_router_group[i],
                   b_router_group[i], w_router_expert[i], b_router_expert[i], w_exp_gate[i], w_exp_up[i],
                   w_exp_down[i], g_ple[i], w_ple_gate[i], w_ple_proj[i])
    return x
```

```python
import functools

import numpy as np
import jax
import jax.numpy as jnp
from jax import lax
from jax.experimental import pallas as pl
from jax.experimental.pallas import tpu as pltpu
from jax.experimental.pallas import tpu_sc as plsc

HEAD_DIM = 64
N_HEADS = 16
N_HEADS_NA = 8
GRID_W = 64
NA_ROWS = 8
NA_COLS = 16
DIL_PAIRS = ((128, 1), (512, 4), (2048, 16))
ROPE_THETA = 10000.0
N_GROUPS = 4
EXPERTS_PER_GROUP = 8
N_EXPERTS = N_GROUPS * EXPERTS_PER_GROUP
EPS = 1e-6
NEG = -1e30

LANES = 128
PAIR_W = 2 * HEAD_DIM
DIL_BLK = 128
ATTN_STEPS_PER_TRIP = 64
VMEM_LIMIT = 56 * 1024 * 1024
SC_CORES = 2
SC_SUBCORES = 16
SC_CHUNK = 128

F32 = jnp.float32
BF16 = jnp.bfloat16


def _dot(a, b):
    return jnp.dot(a, b, preferred_element_type=F32)


def _dot_nt(a, b):
    return lax.dot_general(a, b, (((1,), (1,)), ((), ())), preferred_element_type=F32)


def _rms(x, gain):
    return x * lax.rsqrt(jnp.mean(x * x, axis=-1, keepdims=True) + EPS) * gain


def _lane_first_half(shape):
    return lax.broadcasted_iota(jnp.int32, shape, len(shape) - 1) < HEAD_DIM


def _qkv_kernel(x_ref, g_ref, w_ref, gq_na_ref, gk_na_ref, gq_dil_ref, gk_dil_ref, cos_ref, sin_ref,
                hsum_ref, qa_ref, ka_ref, va_ref, qb_ref, kb_ref, vb_ref):
    d = x_ref.shape[1]
    half = d // 2
    scale = HEAD_DIM ** -0.5
    h = _rms(x_ref[...], g_ref[...]).astype(BF16)

    def proj(col):
        return _dot(h, w_ref[:, col:col + half])

    def head_norm(y, gain):
        sq = (y * y).astype(BF16)
        w = hsum_ref.shape[0]
        ms = jnp.concatenate([_dot(sq[:, c:c + w], hsum_ref[...]) for c in range(0, half, w)], axis=1)
        return y * lax.rsqrt(ms + EPS) * gain

    def rope(y):
        lane = lax.broadcasted_iota(jnp.int32, (y.shape[0], LANES), 1)
        lower = (lane % HEAD_DIM) < HEAD_DIM // 2
        cos = cos_ref[...]
        sin = sin_ref[...]
        outs = []
        for c in range(0, half, LANES):
            yc = y[:, c:c + LANES]
            up = pltpu.roll(yc, LANES - HEAD_DIM // 2, axis=1)
            down = pltpu.roll(yc, HEAD_DIM // 2, axis=1)
            outs.append(yc * cos + jnp.where(lower, up, down) * sin)
        return jnp.concatenate(outs, axis=1)

    qa_ref[...] = (head_norm(proj(0), gq_na_ref[...]) * scale).astype(BF16)
    qb_ref[...] = (rope(head_norm(proj(half), gq_dil_ref[...])) * scale).astype(BF16)
    ka_ref[...] = head_norm(proj(d), gk_na_ref[...]).astype(BF16)
    kb_ref[...] = rope(head_norm(proj(d + half), gk_dil_ref[...])).astype(BF16)
    va_ref[...] = proj(2 * d).astype(BF16)
    vb_ref[...] = proj(2 * d + half).astype(BF16)


def _qkv_proj(x2, g_attn, w_qkv, gq_na, gk_na, gq_dil, gk_dil, seq, tm):
    n, d = x2.shape
    half = d // 2
    pos = jnp.arange(seq, dtype=F32)
    inv = ROPE_THETA ** (-jnp.arange(HEAD_DIM // 2, dtype=F32) / (HEAD_DIM // 2))
    ang = pos[:, None] * inv[None, :]
    cos = jnp.tile(jnp.cos(ang), (1, LANES // (HEAD_DIM // 2)))
    sin = jnp.tile(jnp.concatenate([-jnp.sin(ang), jnp.sin(ang)], axis=1), (1, LANES // HEAD_DIM))
    hs_w = 2 * LANES
    blk = np.arange(hs_w) // HEAD_DIM
    hsum = jnp.asarray((blk[:, None] == blk[None, :]).astype(np.float32) / HEAD_DIM, BF16)
    tile_gain = lambda g: jnp.tile(g.astype(F32), half // HEAD_DIM)[None, :]
    steps_per_seq = seq // tm
    full = lambda shape: pl.BlockSpec(shape, lambda i: (0,) * len(shape))
    out = jax.ShapeDtypeStruct((n, half), BF16)
    return pl.pallas_call(
        _qkv_kernel,
        grid=(n // tm,),
        in_specs=[
            pl.BlockSpec((tm, d), lambda i: (i, 0)),
            full((1, d)),
            full((d, 3 * d)),
            full((1, half)), full((1, half)), full((1, half)), full((1, half)),
            pl.BlockSpec((tm, LANES), lambda i: (i % steps_per_seq, 0)),
            pl.BlockSpec((tm, LANES), lambda i: (i % steps_per_seq, 0)),
            full((hs_w, hs_w)),
        ],
        out_specs=[pl.BlockSpec((tm, half), lambda i: (i, 0))] * 6,
        out_shape=[out] * 6,
        compiler_params=pltpu.CompilerParams(dimension_semantics=("parallel",), vmem_limit_bytes=VMEM_LIMIT),
        name="qkv_proj",
    )(x2, g_attn[None, :].astype(F32), w_qkv.astype(BF16), tile_gain(gq_na), tile_gain(gk_na),
      tile_gain(gq_dil), tile_gain(gk_dil), cos, sin, hsum)


def _is_static(x):
    return isinstance(x, int)


def _clip(x, lo, hi):
    return min(max(x, lo), hi) if _is_static(x) else jnp.clip(x, lo, hi)


def _aligned_ds(start, size, align):
    return pl.ds(start if _is_static(start) else pl.multiple_of(start, align), size)


def _software_pipeline(n_items, stages, steps_per_trip):
    depth = len(stages)
    assert steps_per_trip % 2 == 0 and depth % 2 == 1

    def step(t, parity, static):
        for k in reversed(range(depth)):
            if static and not 0 <= t - k < n_items:
                continue
            stages[k](t - k, (parity + k) % 2)

    first_full = depth - 1
    trips = max(n_items - first_full, 0) // steps_per_trip
    if trips < 2:
        trips = 0
    looped_end = first_full + trips * steps_per_trip
    for t in range(first_full):
        step(t, t % 2, True)
    if trips:
        def body(i, carry):
            for j in range(steps_per_trip):
                step(first_full + i * steps_per_trip + j, j % 2, False)
            return carry

        lax.fori_loop(0, trips, body, 0)
    for t in range(looped_end, n_items + depth - 1):
        step(t, t % 2, True)


def _qk_stage(q, kwin, bias_a, bias_b, s_ref, slot, first):
    w = kwin.shape[0]
    zero = jnp.zeros_like(q)
    s_ref[slot, 0, :, :w] = _dot_nt(jnp.where(first, q, zero), kwin) + bias_a
    s_ref[slot, 1, :, :w] = _dot_nt(jnp.where(first, zero, q), kwin) + bias_b


def _softmax_stage(s_ref, p_ref, slot, w):
    maxima = []
    for head in range(2):
        m = jnp.max(s_ref[slot, head, :, :w], axis=-1, keepdims=True)
        p_ref[slot, head, :, :w] = jnp.exp(s_ref[slot, head, :, :w] - m).astype(BF16)
        maxima.append(m)
    return maxima


def _pv_stage(p_ref, slot, va_win, vb_win, first):
    w = va_win.shape[0]
    ra = _dot(p_ref[slot, 0, :, :w], va_win)
    rb = _dot(p_ref[slot, 1, :, :w], vb_win)
    num = jnp.where(first, ra, rb)
    den = jnp.where(first, pltpu.roll(ra, HEAD_DIM, axis=1), pltpu.roll(rb, HEAD_DIM, axis=1))
    return num, den


def _fill_v_aug(v, va_ref, vb_ref):
    first = _lane_first_half(v.shape)
    one = jnp.ones_like(v)
    va_ref[...] = jnp.where(first, v, one)
    vb_ref[...] = jnp.where(first, one, v)


def _na_kernel(q_ref, k_ref, v_ref, bias_ref, o_ref, va_ref, vb_ref, s_ref, p_ref):
    rows = q_ref.shape[0] // GRID_W
    win = NA_ROWS * GRID_W
    _fill_v_aug(v_ref[...], va_ref, vb_ref)
    first = _lane_first_half((GRID_W, PAIR_W))

    def slices(r):
        rs = _clip(r - NA_ROWS // 2, 0, rows - NA_ROWS)
        return _aligned_ds(r * GRID_W, GRID_W, GRID_W), _aligned_ds(rs * GRID_W, win, GRID_W), r - rs

    def qk(r, slot):
        qs, ks, delta = slices(r)
        _qk_stage(q_ref[qs, :], k_ref[ks, :], bias_ref[0, delta], bias_ref[1, delta], s_ref, slot, first)

    def softmax(r, slot):
        _softmax_stage(s_ref, p_ref, slot, win)

    def pv(r, slot):
        qs, ks, _ = slices(r)
        num, den = _pv_stage(p_ref, slot, va_ref[ks, :], vb_ref[ks, :], first)
        o_ref[qs, :] = num / den

    _software_pipeline(rows, (qk, softmax, pv), ATTN_STEPS_PER_TRIP)


def _na_bias_table(rpb):
    w = np.arange(GRID_W)
    cs = np.clip(w - NA_COLS // 2, 0, GRID_W - NA_COLS)
    kc = np.arange(GRID_W)
    valid = (kc[None, :] >= cs[:, None]) & (kc[None, :] < cs[:, None] + NA_COLS)
    coff = np.clip(kc[None, :] - w[:, None] + NA_COLS - 1, 0, 2 * NA_COLS - 2)
    roff = np.arange(NA_ROWS)[None, :] - np.arange(NA_ROWS)[:, None] + NA_ROWS - 1
    tab = rpb.astype(F32)[:, roff][:, :, :, coff]
    tab = jnp.where(valid[None, None, None], tab, NEG)
    tab = tab.transpose(0, 1, 3, 2, 4)
    return tab.reshape(rpb.shape[0], NA_ROWS, GRID_W, NA_ROWS * GRID_W)


def _na_attention(q, k, v, rpb):
    b, s, width = q.shape
    pairs = width // PAIR_W
    assert s % GRID_W == 0 and s // GRID_W >= NA_ROWS
    bias = _na_bias_table(rpb).reshape(pairs, 2, NA_ROWS, GRID_W, NA_ROWS * GRID_W)
    qkv_spec = pl.BlockSpec((None, s, PAIR_W), lambda bi, j: (bi, 0, j))
    return pl.pallas_call(
        _na_kernel,
        grid=(b, pairs),
        in_specs=[qkv_spec, qkv_spec, qkv_spec,
                  pl.BlockSpec((None, 2, NA_ROWS, GRID_W, NA_ROWS * GRID_W), lambda bi, j: (j, 0, 0, 0, 0))],
        out_specs=pl.BlockSpec((None, s, PAIR_W), lambda bi, j: (bi, 0, j)),
        out_shape=jax.ShapeDtypeStruct((b, s, width), F32),
        scratch_shapes=[pltpu.VMEM((s, PAIR_W), BF16), pltpu.VMEM((s, PAIR_W), BF16),
                        pltpu.VMEM((2, 2, GRID_W, NA_ROWS * GRID_W), F32),
                        pltpu.VMEM((2, 2, GRID_W, NA_ROWS * GRID_W), BF16)],
        compiler_params=pltpu.CompilerParams(dimension_semantics=("parallel", "parallel"),
                                             vmem_limit_bytes=VMEM_LIMIT),
        name="na_attn",
    )(q, k, v, bias)


def _dil_kernel(q_ref, k_ref, v_ref, mwide_ref, mfull_ref, o_ref,
                qf_ref, kf_ref, vf_ref, qc_ref, kc_ref, va_ref, vb_ref, acc_ref, den_ref, max_ref, s_ref, p_ref):
    s = q_ref.shape[0]
    qf_ref[...] = q_ref[...].astype(F32)
    kf_ref[...] = k_ref[...].astype(F32)
    vf_ref[...] = v_ref[...].astype(F32)
    first = _lane_first_half((DIL_BLK, PAIR_W))

    for p, (window, dil) in enumerate(DIL_PAIRS):
        radius = window // (2 * dil)
        cls_len = s // dil
        nblk = cls_len // DIL_BLK
        wide = cls_len >= 2 * DIL_BLK
        win = 2 * DIL_BLK if wide else cls_len
        assert radius == DIL_BLK // 2 and cls_len % DIL_BLK == 0

        def to_classes(c, carry):
            src = pl.ds(c, cls_len, stride=dil)
            dst = pl.ds(pl.multiple_of(c * cls_len, DIL_BLK), cls_len)
            qc_ref[dst, :] = qf_ref[src, :].astype(BF16)
            kc_ref[dst, :] = kf_ref[src, :].astype(BF16)
            v = vf_ref[src, :].astype(BF16)
            fh = _lane_first_half(v.shape)
            one = jnp.ones_like(v)
            va_ref[dst, :] = jnp.where(fh, v, one)
            vb_ref[dst, :] = jnp.where(fh, one, v)
            return carry

        lax.fori_loop(0, dil, to_classes, 0)

        def slices(n):
            c, i = divmod(n, nblk) if _is_static(n) else (n // nblk, n % nblk)
            base = c * cls_len
            ws = _clip(i * DIL_BLK - radius, 0, cls_len - win)
            if _is_static(i):
                kind = 0 if i == 0 else (2 if i == nblk - 1 else 1)
            else:
                kind = jnp.where(i == 0, 0, jnp.where(i == nblk - 1, 2, 1))
            return (_aligned_ds(base + i * DIL_BLK, DIL_BLK, DIL_BLK), _aligned_ds(base + ws, win, radius), kind,
                    pl.ds(c + dil * DIL_BLK * i, DIL_BLK, stride=dil))

        def qk(n, slot):
            qs, ks, kind, _ = slices(n)
            mask = mwide_ref[kind] if wide else mfull_ref[...]
            _qk_stage(qc_ref[qs, :], kc_ref[ks, :], mask, mask, s_ref, slot, first)

        def softmax(n, slot):
            ma, mb = _softmax_stage(s_ref, p_ref, slot, win)
            max_ref[p, slices(n)[3], :] = jnp.where(first, ma, mb)

        def pv(n, slot):
            _, ks, _, tok = slices(n)
            num, den = _pv_stage(p_ref, slot, va_ref[ks, :], vb_ref[ks, :], first)
            acc_ref[p, tok, :] = num
            den_ref[p, tok, :] = den

        _software_pipeline(dil * nblk, (qk, softmax, pv), ATTN_STEPS_PER_TRIP)

    m = jnp.maximum(jnp.maximum(max_ref[0], max_ref[1]), max_ref[2])
    num = jnp.zeros_like(m)
    den = jnp.zeros_like(m)
    for p in range(len(DIL_PAIRS)):
        w = jnp.exp(max_ref[p] - m)
        num = num + w * acc_ref[p]
        den = den + w * den_ref[p]
    o_ref[...] = num / den


def _band_mask(kind):
    radius = DIL_BLK // 2
    qq = np.arange(DIL_BLK)[:, None]
    if kind == "full":
        kk = np.arange(DIL_BLK)[None, :]
        shift = 0
    else:
        kk = np.arange(2 * DIL_BLK)[None, :]
        shift = {"first": 0, "inner": radius, "last": DIL_BLK}[kind]
    return np.where(np.abs(kk - qq - shift) <= radius, 0.0, NEG).astype(np.float32)


def _dil_attention(q, k, v):
    b, s, width = q.shape
    pairs = width // PAIR_W
    for window, dil in DIL_PAIRS:
        assert s % (window // 2) == 0 and (s // dil) % DIL_BLK == 0
    mwide = jnp.asarray(np.stack([_band_mask("first"), _band_mask("inner"), _band_mask("last")]))
    mfull = jnp.asarray(_band_mask("full"))
    qkv_spec = pl.BlockSpec((None, s, PAIR_W), lambda bi, j: (bi, 0, j))
    npat = len(DIL_PAIRS)
    return pl.pallas_call(
        _dil_kernel,
        grid=(b, pairs),
        in_specs=[qkv_spec, qkv_spec, qkv_spec,
                  pl.BlockSpec(mwide.shape, lambda bi, j: (0, 0, 0)),
                  pl.BlockSpec(mfull.shape, lambda bi, j: (0, 0))],
        out_specs=pl.BlockSpec((None, s, PAIR_W), lambda bi, j: (bi, 0, j)),
        out_shape=jax.ShapeDtypeStruct((b, s, width), F32),
        scratch_shapes=[pltpu.VMEM((s, PAIR_W), F32)] * 3 + [pltpu.VMEM((s, PAIR_W), BF16)] * 4
        + [pltpu.VMEM((npat, s, PAIR_W), F32)] * 3
        + [pltpu.VMEM((2, 2, DIL_BLK, 2 * DIL_BLK), F32), pltpu.VMEM((2, 2, DIL_BLK, 2 * DIL_BLK), BF16)],
        compiler_params=pltpu.CompilerParams(dimension_semantics=("parallel", "parallel"),
                                             vmem_limit_bytes=VMEM_LIMIT),
        name="dil_attn",
    )(q, k, v, mwide, mfull)


def _split_bf16(x):
    hi = x.astype(BF16)
    return hi, (x - hi.astype(F32)).astype(BF16)


def _out_router_kernel(oa_ref, ob_ref, x_ref, ga_ref, gb_ref, wo_ref, gf_ref, wr_hi_ref, wr_lo_ref, br_ref,
                       x1_ref, h_ref, rt_ref, rtt_ref):
    half = oa_ref.shape[1]
    ya = _rms(oa_ref[...], ga_ref[...]).astype(BF16)
    yb = _rms(ob_ref[...], gb_ref[...]).astype(BF16)
    x1 = x_ref[...] + _dot(ya, wo_ref[:half, :]) + _dot(yb, wo_ref[half:, :])
    x1_ref[...] = x1
    h = _rms(x1, gf_ref[...])
    h_ref[...] = _pack_bf16_pairs(h)

    h_hi, h_lo = _split_bf16(h)
    logits = _dot(h_hi, wr_hi_ref[...]) + _dot(h_lo, wr_hi_ref[...]) + _dot(h_hi, wr_lo_ref[...]) + br_ref[...]
    lane = lax.broadcasted_iota(jnp.int32, logits.shape, 1)
    ninf = jnp.float32(-jnp.inf)

    def first_argmax(vals, vmax):
        return jnp.min(jnp.where(vals == vmax, lane, LANES), axis=-1, keepdims=True)

    gl = jnp.where(lane < N_GROUPS, logits, ninf)
    gmax = jnp.max(gl, axis=-1, keepdims=True)
    gsel = first_argmax(gl, gmax)
    gw = 1.0 / jnp.sum(jnp.exp(gl - gmax), axis=-1, keepdims=True)
    lo = N_GROUPS + EXPERTS_PER_GROUP * gsel
    el = jnp.where((lane >= lo) & (lane < lo + EXPERTS_PER_GROUP), logits, ninf)
    v0 = jnp.max(el, axis=-1, keepdims=True)
    i0 = first_argmax(el, v0)
    el = jnp.where(lane == i0, ninf, el)
    v1 = jnp.max(el, axis=-1, keepdims=True)
    i1 = first_argmax(el, v1)
    t = jnp.exp(v1 - v0)
    w0 = gw / (1.0 + t)
    w1 = gw * t / (1.0 + t)
    e0 = (i0 - N_GROUPS).astype(F32)
    e1 = (i1 - N_GROUPS).astype(F32)
    rt = jnp.where(lane == 0, e0, jnp.where(lane == 1, e1, jnp.where(lane == 2, w0, jnp.where(lane == 3, w1, 0.0))))
    rt_ref[...] = rt
    rtt_ref[...] = rt.T[:rtt_ref.shape[0], :]


def _out_router(oa, ob, x2, g_na, g_dil, w_o, g_ffn, w_rg, b_rg, w_re, b_re, tm):
    n, d = x2.shape
    half = d // 2
    wr = jnp.concatenate([w_rg.astype(F32), w_re.astype(F32).transpose(1, 0, 2).reshape(d, N_EXPERTS)], axis=1)
    wr = jnp.pad(wr, ((0, 0), (0, LANES - wr.shape[1])))
    wr_hi = wr.astype(BF16)
    wr_lo = (wr - wr_hi.astype(F32)).astype(BF16)
    br = jnp.pad(jnp.concatenate([b_rg.astype(F32), b_re.astype(F32).reshape(-1)]), (0, LANES - N_GROUPS - N_EXPERTS))
    full = lambda shape: pl.BlockSpec(shape, lambda i: (0,) * len(shape))
    row = lambda w: pl.BlockSpec((tm, w), lambda i: (i, 0))
    return pl.pallas_call(
        _out_router_kernel,
        grid=(n // tm,),
        in_specs=[row(half), row(half), row(d), full((1, half)), full((1, half)), full((d, d)), full((1, d)),
                  full((d, LANES)), full((d, LANES)), full((1, LANES))],
        out_specs=[row(d), row(half), row(LANES), pl.BlockSpec((8, tm), lambda i: (0, i))],
        out_shape=[jax.ShapeDtypeStruct((n, d), F32), jax.ShapeDtypeStruct((n, half), jnp.uint32),
                   jax.ShapeDtypeStruct((n, LANES), F32), jax.ShapeDtypeStruct((8, n), F32)],
        compiler_params=pltpu.CompilerParams(dimension_semantics=("parallel",), vmem_limit_bytes=VMEM_LIMIT),
        name="out_router",
    )(oa, ob, x2, g_na[None, :].astype(F32), g_dil[None, :].astype(F32), w_o.astype(BF16),
      g_ffn[None, :].astype(F32), wr_hi, wr_lo, br[None, :])


def _pack_bf16_pairs(x):
    w = x.shape[1] // 2
    bits = lax.bitcast_convert_type(x.astype(BF16).astype(F32), jnp.uint32)
    return bits[:, :w] | (bits[:, w:] >> 16)


def _unpack_bf16_pairs(u):
    hi = lax.bitcast_convert_type(u & jnp.uint32(0xFFFF0000), F32)
    lo = lax.bitcast_convert_type(u << 16, F32)
    return hi, lo


def _slot_one_hots(rtt):
    sub = lax.broadcasted_iota(jnp.int32, (LANES, rtt.shape[1]), 0).astype(F32)
    return (sub == rtt[0:1, :]).astype(F32), (sub == rtt[1:2, :]).astype(F32)


def _count_kernel(rtt_ref, cnt_ref):
    @pl.when(pl.program_id(0) == 0)
    def _():
        cnt_ref[...] = jnp.zeros_like(cnt_ref)

    oh0, oh1 = _slot_one_hots(rtt_ref[...])
    cnt_ref[...] += jnp.sum(oh0 + oh1, axis=1, keepdims=True)


def _position_kernel(rtt_ref, start_ref, pos_ref, base_ref):
    tm = rtt_ref.shape[1]

    @pl.when(pl.program_id(0) == 0)
    def _():
        base_ref[...] = start_ref[...]

    oh0, oh1 = _slot_one_hots(rtt_ref[...])
    oh = oh0 + oh1
    earlier = lax.broadcasted_iota(jnp.int32, (tm, tm), 0) < lax.broadcasted_iota(jnp.int32, (tm, tm), 1)
    before = _dot(oh.astype(BF16), earlier.astype(BF16)) + base_ref[:, 0:1]
    p0 = jnp.sum(before * oh0, axis=0, keepdims=True)
    p1 = jnp.sum(before * oh1, axis=0, keepdims=True)
    row = lax.broadcasted_iota(jnp.int32, pos_ref.shape, 0)
    pos_ref[...] = jnp.where(row == 0, p0, jnp.where(row == 1, p1, 0.0)).astype(jnp.int32)
    base_ref[...] += jnp.sum(oh, axis=1, keepdims=True)


def _expert_counts(rtt, tm):
    n = rtt.shape[1]
    return pl.pallas_call(
        _count_kernel,
        grid=(n // tm,),
        in_specs=[pl.BlockSpec((8, tm), lambda i: (0, i))],
        out_specs=pl.BlockSpec((LANES, LANES), lambda i: (0, 0)),
        out_shape=jax.ShapeDtypeStruct((LANES, LANES), F32),
        compiler_params=pltpu.CompilerParams(dimension_semantics=("arbitrary",)),
        name="expert_counts",
    )(rtt)


def _expert_positions(rtt, starts, tm):
    n = rtt.shape[1]
    return pl.pallas_call(
        _position_kernel,
        grid=(n // tm,),
        in_specs=[pl.BlockSpec((8, tm), lambda i: (0, i)), pl.BlockSpec((LANES, LANES), lambda i: (0, 0))],
        out_specs=pl.BlockSpec((8, tm), lambda i: (0, i)),
        out_shape=jax.ShapeDtypeStruct((8, n), jnp.int32),
        scratch_shapes=[pltpu.VMEM((LANES, LANES), F32)],
        compiler_params=pltpu.CompilerParams(dimension_semantics=("arbitrary",)),
        name="expert_positions",
    )(rtt, starts)


def _sc_mesh():
    return plsc.VectorSubcoreMesh(core_axis_name="c", subcore_axis_name="s",
                                  num_cores=SC_CORES, num_subcores=SC_SUBCORES)


def _sc_dispatch(hp, pos0, pos1, n_out):
    n, w = hp.shape
    workers = SC_CORES * SC_SUBCORES
    per = n // workers
    chunks = per // SC_CHUNK
    assert n % (workers * SC_CHUNK) == 0

    @functools.partial(
        pl.kernel, out_type=jax.ShapeDtypeStruct((n_out, w), hp.dtype), mesh=_sc_mesh(),
        scratch_types=[pltpu.VMEM((chunks, SC_CHUNK), jnp.int32), pltpu.VMEM((chunks, SC_CHUNK), jnp.int32),
                       pltpu.VMEM((SC_CHUNK, w), hp.dtype)],
        name="moe_dispatch")
    def body(h_hbm, p0_hbm, p1_hbm, xs_hbm, i0_v, i1_v, rows_v):
        wid = lax.axis_index("s") * SC_CORES + lax.axis_index("c")
        pltpu.sync_copy(p0_hbm.at[wid], i0_v)
        pltpu.sync_copy(p1_hbm.at[wid], i1_v)

        @pl.loop(0, chunks)
        def _(j):
            pltpu.sync_copy(h_hbm.at[pl.ds(wid * per + j * SC_CHUNK, SC_CHUNK)], rows_v)
            pltpu.sync_copy(rows_v, xs_hbm.at[i0_v.at[j]])
            pltpu.sync_copy(rows_v, xs_hbm.at[i1_v.at[j]])

    return body(hp, pos0.reshape(workers, chunks, SC_CHUNK), pos1.reshape(workers, chunks, SC_CHUNK))


def _sc_collect(ys, pos0, pos1):
    n = pos0.shape[0]
    w = ys.shape[1]
    workers = SC_CORES * SC_SUBCORES
    per = n // workers
    chunks = per // SC_CHUNK
    out = jax.ShapeDtypeStruct((n, w), ys.dtype)

    @functools.partial(
        pl.kernel, out_type=(out, out), mesh=_sc_mesh(),
        scratch_types=[pltpu.VMEM((chunks, SC_CHUNK), jnp.int32), pltpu.VMEM((chunks, SC_CHUNK), jnp.int32),
                       pltpu.VMEM((SC_CHUNK, w), ys.dtype)],
        name="moe_collect")
    def body(ys_hbm, p0_hbm, p1_hbm, y0_hbm, y1_hbm, i0_v, i1_v, rows_v):
        wid = lax.axis_index("s") * SC_CORES + lax.axis_index("c")
        pltpu.sync_copy(p0_hbm.at[wid], i0_v)
        pltpu.sync_copy(p1_hbm.at[wid], i1_v)

        @pl.loop(0, chunks)
        def _(j):
            dst = pl.ds(wid * per + j * SC_CHUNK, SC_CHUNK)
            pltpu.sync_copy(ys_hbm.at[i0_v.at[j]], rows_v)
            pltpu.sync_copy(rows_v, y0_hbm.at[dst])
            pltpu.sync_copy(ys_hbm.at[i1_v.at[j]], rows_v)
            pltpu.sync_copy(rows_v, y1_hbm.at[dst])

    return body(ys, pos0.reshape(workers, chunks, SC_CHUNK), pos1.reshape(workers, chunks, SC_CHUNK))


def _experts_kernel(te_ref, nt_ref, xs_ref, wg_ref, wu_ref, wd_ref, ys_ref):
    @pl.when(pl.program_id(0) < nt_ref[0])
    def _():
        half = wg_ref.shape[0] // 2
        hi, lo = _unpack_bf16_pairs(xs_ref[...])
        hi = hi.astype(BF16)
        lo = lo.astype(BF16)
        a = _dot(hi, wg_ref[:half, :]) + _dot(lo, wg_ref[half:, :])
        u = _dot(hi, wu_ref[:half, :]) + _dot(lo, wu_ref[half:, :])
        act = (a * jax.nn.sigmoid(a) * u).astype(BF16)
        ys_ref[...] = _pack_bf16_pairs(_dot(act, wd_ref[...]))


def _experts(xs, tile_expert, n_tiles, w_gate, w_up, w_down, tmg):
    rows, w = xs.shape
    ne, d, de = w_gate.shape
    live = lambda t, nt: jnp.minimum(t, nt[0] - 1)
    return pl.pallas_call(
        _experts_kernel,
        grid_spec=pltpu.PrefetchScalarGridSpec(
            num_scalar_prefetch=2,
            grid=(rows // tmg,),
            in_specs=[pl.BlockSpec((tmg, w), lambda t, te, nt: (live(t, nt), 0)),
                      pl.BlockSpec((None, d, de), lambda t, te, nt: (te[live(t, nt)], 0, 0)),
                      pl.BlockSpec((None, d, de), lambda t, te, nt: (te[live(t, nt)], 0, 0)),
                      pl.BlockSpec((None, de, d), lambda t, te, nt: (te[live(t, nt)], 0, 0))],
            out_specs=pl.BlockSpec((tmg, w), lambda t, te, nt: (live(t, nt), 0)),
        ),
        out_shape=jax.ShapeDtypeStruct((rows, w), jnp.uint32),
        compiler_params=pltpu.CompilerParams(dimension_semantics=("arbitrary",), vmem_limit_bytes=VMEM_LIMIT),
        name="experts",
    )(tile_expert, n_tiles, xs, w_gate.astype(BF16), w_up.astype(BF16), w_down.astype(BF16))


def _moe(hp, rtt, w_gate, w_up, w_down, tm, tmg):
    n = hp.shape[0]
    ne = w_gate.shape[0]
    counts = _expert_counts(rtt, tm)[:ne, 0].astype(jnp.int32)
    padded = (counts + tmg - 1) // tmg * tmg
    ends = jnp.cumsum(padded)
    starts = jnp.pad((ends - padded).astype(F32), (0, LANES - ne))
    pos = _expert_positions(rtt, jnp.broadcast_to(starts[:, None], (LANES, LANES)), tm)
    pos0, pos1 = pos[0], pos[1]
    rows = 2 * n + ne * tmg
    tile_start = jnp.arange(rows // tmg, dtype=jnp.int32) * tmg
    tile_expert = jnp.minimum(jnp.sum(tile_start[:, None] >= ends[None, :], axis=1), ne - 1).astype(jnp.int32)
    n_tiles = (ends[-1:] // tmg).astype(jnp.int32)
    xs = _sc_dispatch(hp, pos0, pos1, rows)
    ys = _experts(xs, tile_expert, n_tiles, w_gate, w_up, w_down, tmg)
    return _sc_collect(ys, pos0, pos1)


def _ple_kernel(x1_ref, y0_ref, y1_ref, rt_ref, p_ref, g_ref, wg_ref, wp_ref, o_ref):
    rt = rt_ref[...]
    y0 = jnp.concatenate(_unpack_bf16_pairs(y0_ref[...]), axis=1)
    y1 = jnp.concatenate(_unpack_bf16_pairs(y1_ref[...]), axis=1)
    x2 = x1_ref[...] + rt[:, 2:3] * y0 + rt[:, 3:4] * y1
    gate = jax.nn.sigmoid(_dot(_rms(x2, g_ref[...]).astype(BF16), wg_ref[...]))
    o_ref[...] = x2 + gate * _dot(p_ref[...].astype(BF16), wp_ref[...])


def _ple(x1, y0, y1, rt, p2, g_ple, w_gate, w_proj, tm):
    n, d = x1.shape
    dp = p2.shape[1]
    full = lambda shape: pl.BlockSpec(shape, lambda i: (0,) * len(shape))
    row = lambda w: pl.BlockSpec((tm, w), lambda i: (i, 0))
    return pl.pallas_call(
        _ple_kernel,
        grid=(n // tm,),
        in_specs=[row(d), row(d // 2), row(d // 2), row(LANES), row(dp), full((1, d)), full((d, d)), full((dp, d))],
        out_specs=row(d),
        out_shape=jax.ShapeDtypeStruct((n, d), F32),
        compiler_params=pltpu.CompilerParams(dimension_semantics=("parallel",), vmem_limit_bytes=VMEM_LIMIT),
        name="ple",
    )(x1, y0, y1, rt, p2, g_ple[None, :].astype(F32), w_gate.astype(BF16), w_proj.astype(BF16))


def _layer(x, p_l, g_attn, w_qkv, q_norm_na, k_norm_na, rpb_na, q_norm_dil, k_norm_dil, g_out_na, g_out_dil,
           w_o, g_ffn, w_rg, b_rg, w_re, b_re, w_exp_gate, w_exp_up, w_exp_down, g_ple, w_ple_gate, w_ple_proj):
    b, s, d = x.shape
    n = b * s
    half = d // 2
    assert d == N_HEADS * HEAD_DIM and half == N_HEADS_NA * HEAD_DIM
    tm = 512
    assert s % tm == 0
    x2 = x.reshape(n, d)
    qa, ka, va, qb, kb, vb = _qkv_proj(x2, g_attn, w_qkv, q_norm_na, k_norm_na, q_norm_dil, k_norm_dil, s, tm)
    seq = lambda t: t.reshape(b, s, half)
    oa = _na_attention(seq(qa), seq(ka), seq(va), rpb_na).reshape(n, half)
    ob = _dil_attention(seq(qb), seq(kb), seq(vb)).reshape(n, half)
    x1, h, rt, rtt = _out_router(oa, ob, x2, g_out_na, g_out_dil, w_o, g_ffn, w_rg, b_rg, w_re, b_re, tm)
    y0, y1 = _moe(h, rtt, w_exp_gate, w_exp_up, w_exp_down, tm, tm)
    out = _ple(x1, y0, y1, rt, p_l.reshape(n, -1), g_ple, w_ple_gate, w_ple_proj, tm)
    return out.reshape(b, s, d)


def kernel(x, p, g_attn, w_qkv, q_norm_na, k_norm_na, rpb_na, q_norm_dil, k_norm_dil, g_out_na, g_out_dil, w_o,
           g_ffn, w_router_group, b_router_group, w_router_expert, b_router_expert, w_exp_gate, w_exp_up,
           w_exp_down, g_ple, w_ple_gate, w_ple_proj):
    for i in range(p.shape[0]):
        x = _layer(x, p[i], g_attn[i], w_qkv[i], q_norm_na[i], k_norm_na[i], rpb_na[i], q_norm_dil[i],
                   k_norm_dil[i], g_out_na[i], g_out_dil[i], w_o[i], g_ffn[i], w_router_group[i],
                   b_router_group[i], w_router_expert[i], b_router_expert[i], w_exp_gate[i], w_exp_up[i],
                   w_exp_down[i], g_ple[i], w_ple_gate[i], w_ple_proj[i])
    return x
```

```python
import functools

import numpy as np
import jax
import jax.numpy as jnp
from jax import lax
from jax.experimental import pallas as pl
from jax.experimental.pallas import tpu as pltpu
from jax.experimental.pallas import tpu_sc as plsc

HEAD_DIM = 64
N_HEADS = 16
N_HEADS_NA = 8
GRID_W = 64
NA_ROWS = 8
NA_COLS = 16
DIL_PAIRS = ((128, 1), (512, 4), (2048, 16))
ROPE_THETA = 10000.0
N_GROUPS = 4
EXPERTS_PER_GROUP = 8
N_EXPERTS = N_GROUPS * EXPERTS_PER_GROUP
EPS = 1e-6
NEG = -1e30

LANES = 128
PAIR_W = 2 * HEAD_DIM
DIL_BLK = 128
ATTN_STEPS_PER_TRIP = 64
ATTN_SLOTS = 4
VMEM_LIMIT = 56 * 1024 * 1024
SC_CORES = 2
SC_SUBCORES = 16
SC_CHUNK = 128

F32 = jnp.float32
BF16 = jnp.bfloat16


def _dot(a, b):
    return jnp.dot(a, b, preferred_element_type=F32)


def _dot_nt(a, b):
    return lax.dot_general(a, b, (((1,), (1,)), ((), ())), preferred_element_type=F32)


def _rms(x, gain):
    return x * lax.rsqrt(jnp.mean(x * x, axis=-1, keepdims=True) + EPS) * gain


def _lane_first_half(shape):
    return lax.broadcasted_iota(jnp.int32, shape, len(shape) - 1) < HEAD_DIM


def _qkv_kernel(x_ref, g_ref, w_ref, gq_na_ref, gk_na_ref, gq_dil_ref, gk_dil_ref, cos_ref, sin_ref,
                hsum_ref, qa_ref, ka_ref, va_ref, qb_ref, kb_ref, vb_ref):
    d = x_ref.shape[1]
    half = d // 2
    scale = HEAD_DIM ** -0.5
    h = _rms(x_ref[...], g_ref[...]).astype(BF16)

    def proj(col):
        return _dot(h, w_ref[:, col:col + half])

    def head_norm(y, gain):
        sq = (y * y).astype(BF16)
        w = hsum_ref.shape[0]
        ms = jnp.concatenate([_dot(sq[:, c:c + w], hsum_ref[...]) for c in range(0, half, w)], axis=1)
        return y * lax.rsqrt(ms + EPS) * gain

    def rope(y):
        lane = lax.broadcasted_iota(jnp.int32, (y.shape[0], LANES), 1)
        lower = (lane % HEAD_DIM) < HEAD_DIM // 2
        cos = cos_ref[...]
        sin = sin_ref[...]
        outs = []
        for c in range(0, half, LANES):
            yc = y[:, c:c + LANES]
            up = pltpu.roll(yc, LANES - HEAD_DIM // 2, axis=1)
            down = pltpu.roll(yc, HEAD_DIM // 2, axis=1)
            outs.append(yc * cos + jnp.where(lower, up, down) * sin)
        return jnp.concatenate(outs, axis=1)

    qa_ref[...] = (head_norm(proj(0), gq_na_ref[...]) * scale).astype(BF16)
    qb_ref[...] = (rope(head_norm(proj(half), gq_dil_ref[...])) * scale).astype(BF16)
    ka_ref[...] = head_norm(proj(d), gk_na_ref[...]).astype(BF16)
    kb_ref[...] = rope(head_norm(proj(d + half), gk_dil_ref[...])).astype(BF16)
    va_ref[...] = proj(2 * d).astype(BF16)
    vb_ref[...] = proj(2 * d + half).astype(BF16)


def _qkv_proj(x2, g_attn, w_qkv, gq_na, gk_na, gq_dil, gk_dil, seq, tm):
    n, d = x2.shape
    half = d // 2
    pos = jnp.arange(seq, dtype=F32)
    inv = ROPE_THETA ** (-jnp.arange(HEAD_DIM // 2, dtype=F32) / (HEAD_DIM // 2))
    ang = pos[:, None] * inv[None, :]
    cos = jnp.tile(jnp.cos(ang), (1, LANES // (HEAD_DIM // 2)))
    sin = jnp.tile(jnp.concatenate([-jnp.sin(ang), jnp.sin(ang)], axis=1), (1, LANES // HEAD_DIM))
    hs_w = 2 * LANES
    blk = np.arange(hs_w) // HEAD_DIM
    hsum = jnp.asarray((blk[:, None] == blk[None, :]).astype(np.float32) / HEAD_DIM, BF16)
    tile_gain = lambda g: jnp.tile(g.astype(F32), half // HEAD_DIM)[None, :]
    steps_per_seq = seq // tm
    full = lambda shape: pl.BlockSpec(shape, lambda i: (0,) * len(shape))
    out = jax.ShapeDtypeStruct((n, half), BF16)
    return pl.pallas_call(
        _qkv_kernel,
        grid=(n // tm,),
        in_specs=[
            pl.BlockSpec((tm, d), lambda i: (i, 0)),
            full((1, d)),
            full((d, 3 * d)),
            full((1, half)), full((1, half)), full((1, half)), full((1, half)),
            pl.BlockSpec((tm, LANES), lambda i: (i % steps_per_seq, 0)),
            pl.BlockSpec((tm, LANES), lambda i: (i % steps_per_seq, 0)),
            full((hs_w, hs_w)),
        ],
        out_specs=[pl.BlockSpec((tm, half), lambda i: (i, 0))] * 6,
        out_shape=[out] * 6,
        compiler_params=pltpu.CompilerParams(dimension_semantics=("parallel",), vmem_limit_bytes=VMEM_LIMIT),
        name="qkv_proj",
    )(x2, g_attn[None, :].astype(F32), w_qkv.astype(BF16), tile_gain(gq_na), tile_gain(gk_na),
      tile_gain(gq_dil), tile_gain(gk_dil), cos, sin, hsum)


def _is_static(x):
    return isinstance(x, int)


def _clip(x, lo, hi):
    return min(max(x, lo), hi) if _is_static(x) else jnp.clip(x, lo, hi)


def _aligned_ds(start, size, align):
    return pl.ds(start if _is_static(start) else pl.multiple_of(start, align), size)


def _software_pipeline(n_items, stages, steps_per_trip):
    depth = len(stages)
    assert steps_per_trip % ATTN_SLOTS == 0

    def step(t, phase, static):
        for k in reversed(range(depth)):
            if static and not 0 <= t - k < n_items:
                continue
            stages[k](t - k, (phase - k) % ATTN_SLOTS)

    first_full = -(-(depth - 1) // ATTN_SLOTS) * ATTN_SLOTS
    trips = max(n_items - first_full, 0) // steps_per_trip
    if trips < 2:
        trips = 0
    looped_end = first_full + trips * steps_per_trip if trips else 0
    for t in range(first_full if trips else 0):
        step(t, t % ATTN_SLOTS, True)
    if trips:
        def body(i, carry):
            for j in range(steps_per_trip):
                step(first_full + i * steps_per_trip + j, j % ATTN_SLOTS, False)
            return carry

        lax.fori_loop(0, trips, body, 0)
    for t in range(looped_end, n_items + depth - 1):
        step(t, t % ATTN_SLOTS, True)


def _qk_stage(q, kwin, bias_a, bias_b, s_ref, slot, first):
    m = q.shape[0]
    w = kwin.shape[0]
    zero = jnp.zeros_like(q)
    s = _dot_nt(jnp.concatenate([jnp.where(first, q, zero), jnp.where(first, zero, q)], axis=0), kwin)
    s_ref[slot, :m, :w] = s[:m] + bias_a
    s_ref[slot, m:, :w] = s[m:] + bias_b


def _softmax_stage(s_ref, p_ref, slot, w):
    m = jnp.max(s_ref[slot, :, :w], axis=-1, keepdims=True)
    p_ref[slot, :, :w] = jnp.exp(s_ref[slot, :, :w] - m).astype(BF16)
    return m


def _pv_stage(p_ref, slot, v_win, first):
    w = v_win.shape[0]
    r = _dot(p_ref[slot, :, :w], v_win)
    m = r.shape[0] // 2
    return jnp.where(first, r[:m, :PAIR_W], r[m:, :PAIR_W]), jnp.where(first, r[:m, PAIR_W:], r[m:, PAIR_W:])


def _with_ones(v):
    return jnp.concatenate([v, jnp.ones_like(v)], axis=1)


def _na_kernel(q_ref, k_ref, v_ref, bias_ref, o_ref, v1_ref, s_ref, p_ref):
    rows = q_ref.shape[0] // GRID_W
    win = NA_ROWS * GRID_W
    v1_ref[...] = _with_ones(v_ref[...])
    first = _lane_first_half((GRID_W, PAIR_W))

    def slices(r):
        rs = _clip(r - NA_ROWS // 2, 0, rows - NA_ROWS)
        return _aligned_ds(r * GRID_W, GRID_W, GRID_W), _aligned_ds(rs * GRID_W, win, GRID_W), r - rs

    def qk(r, slot):
        qs, ks, delta = slices(r)
        _qk_stage(q_ref[qs, :], k_ref[ks, :], bias_ref[0, delta], bias_ref[1, delta], s_ref, slot, first)

    def softmax(r, slot):
        _softmax_stage(s_ref, p_ref, slot, win)

    def pv(r, slot):
        qs, ks, _ = slices(r)
        num, den = _pv_stage(p_ref, slot, v1_ref[ks, :], first)
        o_ref[qs, :] = num / den

    _software_pipeline(rows, (qk, softmax, pv), ATTN_STEPS_PER_TRIP)


def _na_bias_table(rpb):
    w = np.arange(GRID_W)
    cs = np.clip(w - NA_COLS // 2, 0, GRID_W - NA_COLS)
    kc = np.arange(GRID_W)
    valid = (kc[None, :] >= cs[:, None]) & (kc[None, :] < cs[:, None] + NA_COLS)
    coff = np.clip(kc[None, :] - w[:, None] + NA_COLS - 1, 0, 2 * NA_COLS - 2)
    roff = np.arange(NA_ROWS)[None, :] - np.arange(NA_ROWS)[:, None] + NA_ROWS - 1
    tab = rpb.astype(F32)[:, roff][:, :, :, coff]
    tab = jnp.where(valid[None, None, None], tab, NEG)
    tab = tab.transpose(0, 1, 3, 2, 4)
    return tab.reshape(rpb.shape[0], NA_ROWS, GRID_W, NA_ROWS * GRID_W)


def _na_attention(q, k, v, rpb):
    b, s, width = q.shape
    pairs = width // PAIR_W
    assert s % GRID_W == 0 and s // GRID_W >= NA_ROWS
    bias = _na_bias_table(rpb).reshape(pairs, 2, NA_ROWS, GRID_W, NA_ROWS * GRID_W)
    qkv_spec = pl.BlockSpec((None, s, PAIR_W), lambda bi, j: (bi, 0, j))
    return pl.pallas_call(
        _na_kernel,
        grid=(b, pairs),
        in_specs=[qkv_spec, qkv_spec, qkv_spec,
                  pl.BlockSpec((None, 2, NA_ROWS, GRID_W, NA_ROWS * GRID_W), lambda bi, j: (j, 0, 0, 0, 0))],
        out_specs=pl.BlockSpec((None, s, PAIR_W), lambda bi, j: (bi, 0, j)),
        out_shape=jax.ShapeDtypeStruct((b, s, width), F32),
        scratch_shapes=[pltpu.VMEM((s, 2 * PAIR_W), BF16),
                        pltpu.VMEM((ATTN_SLOTS, 2 * GRID_W, NA_ROWS * GRID_W), F32),
                        pltpu.VMEM((ATTN_SLOTS, 2 * GRID_W, NA_ROWS * GRID_W), BF16)],
        compiler_params=pltpu.CompilerParams(dimension_semantics=("parallel", "parallel"),
                                             vmem_limit_bytes=VMEM_LIMIT),
        name="na_attn",
    )(q, k, v, bias)


def _dil_kernel(q_ref, k_ref, v_ref, mwide_ref, mfull_ref, o_ref,
                qf_ref, kf_ref, vf_ref, qc_ref, kc_ref, vc_ref, acc_ref, den_ref, max_ref, s_ref, p_ref):
    s = q_ref.shape[0]
    qf_ref[...] = q_ref[...].astype(F32)
    kf_ref[...] = k_ref[...].astype(F32)
    vf_ref[...] = v_ref[...].astype(F32)
    first = _lane_first_half((DIL_BLK, PAIR_W))

    for p, (window, dil) in enumerate(DIL_PAIRS):
        radius = window // (2 * dil)
        cls_len = s // dil
        nblk = cls_len // DIL_BLK
        wide = cls_len >= 2 * DIL_BLK
        win = 2 * DIL_BLK if wide else cls_len
        assert radius == DIL_BLK // 2 and cls_len % DIL_BLK == 0

        for c in range(dil):
            src = pl.ds(c, cls_len, stride=dil)
            dst = pl.ds(c * cls_len, cls_len)
            qc_ref[dst, :] = qf_ref[src, :].astype(BF16)
            kc_ref[dst, :] = kf_ref[src, :].astype(BF16)
            vc_ref[dst, :] = _with_ones(vf_ref[src, :].astype(BF16))

        def slices(n):
            c, i = divmod(n, nblk) if _is_static(n) else (n // nblk, n % nblk)
            base = c * cls_len
            ws = _clip(i * DIL_BLK - radius, 0, cls_len - win)
            if _is_static(i):
                kind = 0 if i == 0 else (2 if i == nblk - 1 else 1)
            else:
                kind = jnp.where(i == 0, 0, jnp.where(i == nblk - 1, 2, 1))
            return (_aligned_ds(base + i * DIL_BLK, DIL_BLK, DIL_BLK), _aligned_ds(base + ws, win, radius), kind,
                    pl.ds(c + dil * DIL_BLK * i, DIL_BLK, stride=dil))

        def qk(n, slot):
            qs, ks, kind, _ = slices(n)
            mask = mwide_ref[kind] if wide else mfull_ref[...]
            _qk_stage(qc_ref[qs, :], kc_ref[ks, :], mask, mask, s_ref, slot, first)

        def softmax(n, slot):
            m = _softmax_stage(s_ref, p_ref, slot, win)
            max_ref[p, slices(n)[3], :] = jnp.where(first, m[:DIL_BLK], m[DIL_BLK:])

        def pv(n, slot):
            _, ks, _, tok = slices(n)
            num, den = _pv_stage(p_ref, slot, vc_ref[ks, :], first)
            acc_ref[p, tok, :] = num
            den_ref[p, tok, :] = den

        _software_pipeline(dil * nblk, (qk, softmax, pv), ATTN_STEPS_PER_TRIP)

    m = jnp.maximum(jnp.maximum(max_ref[0], max_ref[1]), max_ref[2])
    num = jnp.zeros_like(m)
    den = jnp.zeros_like(m)
    for p in range(len(DIL_PAIRS)):
        w = jnp.exp(max_ref[p] - m)
        num = num + w * acc_ref[p]
        den = den + w * den_ref[p]
    o_ref[...] = num / den


def _band_mask(kind):
    radius = DIL_BLK // 2
    qq = np.arange(DIL_BLK)[:, None]
    if kind == "full":
        kk = np.arange(DIL_BLK)[None, :]
        shift = 0
    else:
        kk = np.arange(2 * DIL_BLK)[None, :]
        shift = {"first": 0, "inner": radius, "last": DIL_BLK}[kind]
    return np.where(np.abs(kk - qq - shift) <= radius, 0.0, NEG).astype(np.float32)


def _dil_attention(q, k, v):
    b, s, width = q.shape
    pairs = width // PAIR_W
    for window, dil in DIL_PAIRS:
        assert s % (window // 2) == 0 and (s // dil) % DIL_BLK == 0
    mwide = jnp.asarray(np.stack([_band_mask("first"), _band_mask("inner"), _band_mask("last")]))
    mfull = jnp.asarray(_band_mask("full"))
    qkv_spec = pl.BlockSpec((None, s, PAIR_W), lambda bi, j: (bi, 0, j))
    npat = len(DIL_PAIRS)
    return pl.pallas_call(
        _dil_kernel,
        grid=(b, pairs),
        in_specs=[qkv_spec, qkv_spec, qkv_spec,
                  pl.BlockSpec(mwide.shape, lambda bi, j: (0, 0, 0)),
                  pl.BlockSpec(mfull.shape, lambda bi, j: (0, 0))],
        out_specs=pl.BlockSpec((None, s, PAIR_W), lambda bi, j: (bi, 0, j)),
        out_shape=jax.ShapeDtypeStruct((b, s, width), F32),
        scratch_shapes=[pltpu.VMEM((s, PAIR_W), F32)] * 3 + [pltpu.VMEM((s, PAIR_W), BF16)] * 2
        + [pltpu.VMEM((s, 2 * PAIR_W), BF16)] + [pltpu.VMEM((npat, s, PAIR_W), F32)] * 3
        + [pltpu.VMEM((ATTN_SLOTS, 2 * DIL_BLK, 2 * DIL_BLK), F32),
           pltpu.VMEM((ATTN_SLOTS, 2 * DIL_BLK, 2 * DIL_BLK), BF16)],
        compiler_params=pltpu.CompilerParams(dimension_semantics=("parallel", "parallel"),
                                             vmem_limit_bytes=VMEM_LIMIT),
        name="dil_attn",
    )(q, k, v, mwide, mfull)


def _split_bf16(x):
    hi = x.astype(BF16)
    return hi, (x - hi.astype(F32)).astype(BF16)


def _out_router_kernel(oa_ref, ob_ref, x_ref, ga_ref, gb_ref, wo_ref, gf_ref, wr_hi_ref, wr_lo_ref, br_ref,
                       x1_ref, h_ref, rt_ref, rtt_ref):
    half = oa_ref.shape[1]
    ya = _rms(oa_ref[...], ga_ref[...]).astype(BF16)
    yb = _rms(ob_ref[...], gb_ref[...]).astype(BF16)
    x1 = x_ref[...] + _dot(ya, wo_ref[:half, :]) + _dot(yb, wo_ref[half:, :])
    x1_ref[...] = x1
    h = _rms(x1, gf_ref[...])
    h_ref[...] = _pack_bf16_pairs(h)

    h_hi, h_lo = _split_bf16(h)
    logits = _dot(h_hi, wr_hi_ref[...]) + _dot(h_lo, wr_hi_ref[...]) + _dot(h_hi, wr_lo_ref[...]) + br_ref[...]
    lane = lax.broadcasted_iota(jnp.int32, logits.shape, 1)
    ninf = jnp.float32(-jnp.inf)

    def first_argmax(vals, vmax):
        return jnp.min(jnp.where(vals == vmax, lane, LANES), axis=-1, keepdims=True)

    gl = jnp.where(lane < N_GROUPS, logits, ninf)
    gmax = jnp.max(gl, axis=-1, keepdims=True)
    gsel = first_argmax(gl, gmax)
    gw = 1.0 / jnp.sum(jnp.exp(gl - gmax), axis=-1, keepdims=True)
    lo = N_GROUPS + EXPERTS_PER_GROUP * gsel
    el = jnp.where((lane >= lo) & (lane < lo + EXPERTS_PER_GROUP), logits, ninf)
    v0 = jnp.max(el, axis=-1, keepdims=True)
    i0 = first_argmax(el, v0)
    el = jnp.where(lane == i0, ninf, el)
    v1 = jnp.max(el, axis=-1, keepdims=True)
    i1 = first_argmax(el, v1)
    t = jnp.exp(v1 - v0)
    w0 = gw / (1.0 + t)
    w1 = gw * t / (1.0 + t)
    e0 = (i0 - N_GROUPS).astype(F32)
    e1 = (i1 - N_GROUPS).astype(F32)
    rt = jnp.where(lane == 0, e0, jnp.where(lane == 1, e1, jnp.where(lane == 2, w0, jnp.where(lane == 3, w1, 0.0))))
    rt_ref[...] = rt
    rtt_ref[...] = rt.T[:rtt_ref.shape[0], :]


def _out_router(oa, ob, x2, g_na, g_dil, w_o, g_ffn, w_rg, b_rg, w_re, b_re, tm):
    n, d = x2.shape
    half = d // 2
    wr = jnp.concatenate([w_rg.astype(F32), w_re.astype(F32).transpose(1, 0, 2).reshape(d, N_EXPERTS)], axis=1)
    wr = jnp.pad(wr, ((0, 0), (0, LANES - wr.shape[1])))
    wr_hi = wr.astype(BF16)
    wr_lo = (wr - wr_hi.astype(F32)).astype(BF16)
    br = jnp.pad(jnp.concatenate([b_rg.astype(F32), b_re.astype(F32).reshape(-1)]), (0, LANES - N_GROUPS - N_EXPERTS))
    full = lambda shape: pl.BlockSpec(shape, lambda i: (0,) * len(shape))
    row = lambda w: pl.BlockSpec((tm, w), lambda i: (i, 0))
    return pl.pallas_call(
        _out_router_kernel,
        grid=(n // tm,),
        in_specs=[row(half), row(half), row(d), full((1, half)), full((1, half)), full((d, d)), full((1, d)),
                  full((d, LANES)), full((d, LANES)), full((1, LANES))],
        out_specs=[row(d), row(half), row(LANES), pl.BlockSpec((8, tm), lambda i: (0, i))],
        out_shape=[jax.ShapeDtypeStruct((n, d), F32), jax.ShapeDtypeStruct((n, half), jnp.uint32),
                   jax.ShapeDtypeStruct((n, LANES), F32), jax.ShapeDtypeStruct((8, n), F32)],
        compiler_params=pltpu.CompilerParams(dimension_semantics=("parallel",), vmem_limit_bytes=VMEM_LIMIT),
        name="out_router",
    )(oa, ob, x2, g_na[None, :].astype(F32), g_dil[None, :].astype(F32), w_o.astype(BF16),
      g_ffn[None, :].astype(F32), wr_hi, wr_lo, br[None, :])


def _pack_bf16_pairs(x):
    w = x.shape[1] // 2
    bits = lax.bitcast_convert_type(x.astype(BF16).astype(F32), jnp.uint32)
    return bits[:, :w] | (bits[:, w:] >> 16)


def _unpack_bf16_pairs(u):
    hi = lax.bitcast_convert_type(u & jnp.uint32(0xFFFF0000), F32)
    lo = lax.bitcast_convert_type(u << 16, F32)
    return hi, lo


def _slot_one_hots(rtt):
    sub = lax.broadcasted_iota(jnp.int32, (LANES, rtt.shape[1]), 0).astype(F32)
    return (sub == rtt[0:1, :]).astype(F32), (sub == rtt[1:2, :]).astype(F32)


def _count_kernel(rtt_ref, cnt_ref):
    @pl.when(pl.program_id(0) == 0)
    def _():
        cnt_ref[...] = jnp.zeros_like(cnt_ref)

    oh0, oh1 = _slot_one_hots(rtt_ref[...])
    cnt_ref[...] += jnp.sum(oh0 + oh1, axis=1, keepdims=True)


def _position_kernel(rtt_ref, start_ref, pos_ref, base_ref):
    tm = rtt_ref.shape[1]

    @pl.when(pl.program_id(0) == 0)
    def _():
        base_ref[...] = start_ref[...]

    oh0, oh1 = _slot_one_hots(rtt_ref[...])
    oh = oh0 + oh1
    earlier = lax.broadcasted_iota(jnp.int32, (tm, tm), 0) < lax.broadcasted_iota(jnp.int32, (tm, tm), 1)
    before = _dot(oh.astype(BF16), earlier.astype(BF16)) + base_ref[:, 0:1]
    p0 = jnp.sum(before * oh0, axis=0, keepdims=True)
    p1 = jnp.sum(before * oh1, axis=0, keepdims=True)
    row = lax.broadcasted_iota(jnp.int32, pos_ref.shape, 0)
    pos_ref[...] = jnp.where(row == 0, p0, jnp.where(row == 1, p1, 0.0)).astype(jnp.int32)
    base_ref[...] += jnp.sum(oh, axis=1, keepdims=True)


def _expert_counts(rtt, tm):
    n = rtt.shape[1]
    return pl.pallas_call(
        _count_kernel,
        grid=(n // tm,),
        in_specs=[pl.BlockSpec((8, tm), lambda i: (0, i))],
        out_specs=pl.BlockSpec((LANES, LANES), lambda i: (0, 0)),
        out_shape=jax.ShapeDtypeStruct((LANES, LANES), F32),
        compiler_params=pltpu.CompilerParams(dimension_semantics=("arbitrary",)),
        name="expert_counts",
    )(rtt)


def _expert_positions(rtt, starts, tm):
    n = rtt.shape[1]
    return pl.pallas_call(
        _position_kernel,
        grid=(n // tm,),
        in_specs=[pl.BlockSpec((8, tm), lambda i: (0, i)), pl.BlockSpec((LANES, LANES), lambda i: (0, 0))],
        out_specs=pl.BlockSpec((8, tm), lambda i: (0, i)),
        out_shape=jax.ShapeDtypeStruct((8, n), jnp.int32),
        scratch_shapes=[pltpu.VMEM((LANES, LANES), F32)],
        compiler_params=pltpu.CompilerParams(dimension_semantics=("arbitrary",)),
        name="expert_positions",
    )(rtt, starts)


def _sc_mesh():
    return plsc.VectorSubcoreMesh(core_axis_name="c", subcore_axis_name="s",
                                  num_cores=SC_CORES, num_subcores=SC_SUBCORES)


def _sc_dispatch(hp, pos0, pos1, n_out):
    n, w = hp.shape
    workers = SC_CORES * SC_SUBCORES
    per = n // workers
    chunks = per // SC_CHUNK
    assert n % (workers * SC_CHUNK) == 0

    @functools.partial(
        pl.kernel, out_type=jax.ShapeDtypeStruct((n_out, w), hp.dtype), mesh=_sc_mesh(),
        scratch_types=[pltpu.VMEM((chunks, SC_CHUNK), jnp.int32), pltpu.VMEM((chunks, SC_CHUNK), jnp.int32),
                       pltpu.VMEM((SC_CHUNK, w), hp.dtype)],
        name="moe_dispatch")
    def body(h_hbm, p0_hbm, p1_hbm, xs_hbm, i0_v, i1_v, rows_v):
        wid = lax.axis_index("s") * SC_CORES + lax.axis_index("c")
        pltpu.sync_copy(p0_hbm.at[wid], i0_v)
        pltpu.sync_copy(p1_hbm.at[wid], i1_v)

        @pl.loop(0, chunks)
        def _(j):
            pltpu.sync_copy(h_hbm.at[pl.ds(wid * per + j * SC_CHUNK, SC_CHUNK)], rows_v)
            pltpu.sync_copy(rows_v, xs_hbm.at[i0_v.at[j]])
            pltpu.sync_copy(rows_v, xs_hbm.at[i1_v.at[j]])

    return body(hp, pos0.reshape(workers, chunks, SC_CHUNK), pos1.reshape(workers, chunks, SC_CHUNK))


def _sc_collect(ys, pos0, pos1):
    n = pos0.shape[0]
    w = ys.shape[1]
    workers = SC_CORES * SC_SUBCORES
    per = n // workers
    chunks = per // SC_CHUNK
    out = jax.ShapeDtypeStruct((n, w), ys.dtype)

    @functools.partial(
        pl.kernel, out_type=(out, out), mesh=_sc_mesh(),
        scratch_types=[pltpu.VMEM((chunks, SC_CHUNK), jnp.int32), pltpu.VMEM((chunks, SC_CHUNK), jnp.int32),
                       pltpu.VMEM((SC_CHUNK, w), ys.dtype)],
        name="moe_collect")
    def body(ys_hbm, p0_hbm, p1_hbm, y0_hbm, y1_hbm, i0_v, i1_v, rows_v):
        wid = lax.axis_index("s") * SC_CORES + lax.axis_index("c")
        pltpu.sync_copy(p0_hbm.at[wid], i0_v)
        pltpu.sync_copy(p1_hbm.at[wid], i1_v)

        @pl.loop(0, chunks)
        def _(j):
            dst = pl.ds(wid * per + j * SC_CHUNK, SC_CHUNK)
            pltpu.sync_copy(ys_hbm.at[i0_v.at[j]], rows_v)
            pltpu.sync_copy(rows_v, y0_hbm.at[dst])
            pltpu.sync_copy(ys_hbm.at[i1_v.at[j]], rows_v)
            pltpu.sync_copy(rows_v, y1_hbm.at[dst])

    return body(ys, pos0.reshape(workers, chunks, SC_CHUNK), pos1.reshape(workers, chunks, SC_CHUNK))


def _experts_kernel(te_ref, nt_ref, xs_ref, wg_ref, wu_ref, wd_ref, ys_ref):
    @pl.when(pl.program_id(0) < nt_ref[0])
    def _():
        half = wg_ref.shape[0] // 2
        hi, lo = _unpack_bf16_pairs(xs_ref[...])
        hi = hi.astype(BF16)
        lo = lo.astype(BF16)
        a = _dot(hi, wg_ref[:half, :]) + _dot(lo, wg_ref[half:, :])
        u = _dot(hi, wu_ref[:half, :]) + _dot(lo, wu_ref[half:, :])
        act = (a * jax.nn.sigmoid(a) * u).astype(BF16)
        ys_ref[...] = _pack_bf16_pairs(_dot(act, wd_ref[...]))


def _experts(xs, tile_expert, n_tiles, w_gate, w_up, w_down, tmg):
    rows, w = xs.shape
    ne, d, de = w_gate.shape
    live = lambda t, nt: jnp.minimum(t, nt[0] - 1)
    return pl.pallas_call(
        _experts_kernel,
        grid_spec=pltpu.PrefetchScalarGridSpec(
            num_scalar_prefetch=2,
            grid=(rows // tmg,),
            in_specs=[pl.BlockSpec((tmg, w), lambda t, te, nt: (live(t, nt), 0)),
                      pl.BlockSpec((None, d, de), lambda t, te, nt: (te[live(t, nt)], 0, 0)),
                      pl.BlockSpec((None, d, de), lambda t, te, nt: (te[live(t, nt)], 0, 0)),
                      pl.BlockSpec((None, de, d), lambda t, te, nt: (te[live(t, nt)], 0, 0))],
            out_specs=pl.BlockSpec((tmg, w), lambda t, te, nt: (live(t, nt), 0)),
        ),
        out_shape=jax.ShapeDtypeStruct((rows, w), jnp.uint32),
        compiler_params=pltpu.CompilerParams(dimension_semantics=("arbitrary",), vmem_limit_bytes=VMEM_LIMIT),
        name="experts",
    )(tile_expert, n_tiles, xs, w_gate.astype(BF16), w_up.astype(BF16), w_down.astype(BF16))


def _moe(hp, rtt, w_gate, w_up, w_down, tm, tmg):
    n = hp.shape[0]
    ne = w_gate.shape[0]
    counts = _expert_counts(rtt, tm)[:ne, 0].astype(jnp.int32)
    padded = (counts + tmg - 1) // tmg * tmg
    ends = jnp.cumsum(padded)
    starts = jnp.pad((ends - padded).astype(F32), (0, LANES - ne))
    pos = _expert_positions(rtt, jnp.broadcast_to(starts[:, None], (LANES, LANES)), tm)
    pos0, pos1 = pos[0], pos[1]
    rows = 2 * n + ne * tmg
    tile_start = jnp.arange(rows // tmg, dtype=jnp.int32) * tmg
    tile_expert = jnp.minimum(jnp.sum(tile_start[:, None] >= ends[None, :], axis=1), ne - 1).astype(jnp.int32)
    n_tiles = (ends[-1:] // tmg).astype(jnp.int32)
    xs = _sc_dispatch(hp, pos0, pos1, rows)
    ys = _experts(xs, tile_expert, n_tiles, w_gate, w_up, w_down, tmg)
    return _sc_collect(ys, pos0, pos1)


def _ple_kernel(x1_ref, y0_ref, y1_ref, rt_ref, p_ref, g_ref, wg_ref, wp_ref, o_ref):
    rt = rt_ref[...]
    y0 = jnp.concatenate(_unpack_bf16_pairs(y0_ref[...]), axis=1)
    y1 = jnp.concatenate(_unpack_bf16_pairs(y1_ref[...]), axis=1)
    x2 = x1_ref[...] + rt[:, 2:3] * y0 + rt[:, 3:4] * y1
    gate = jax.nn.sigmoid(_dot(_rms(x2, g_ref[...]).astype(BF16), wg_ref[...]))
    o_ref[...] = x2 + gate * _dot(p_ref[...].astype(BF16), wp_ref[...])


def _ple(x1, y0, y1, rt, p2, g_ple, w_gate, w_proj, tm):
    n, d = x1.shape
    dp = p2.shape[1]
    full = lambda shape: pl.BlockSpec(shape, lambda i: (0,) * len(shape))
    row = lambda w: pl.BlockSpec((tm, w), lambda i: (i, 0))
    return pl.pallas_call(
        _ple_kernel,
        grid=(n // tm,),
        in_specs=[row(d), row(d // 2), row(d // 2), row(LANES), row(dp), full((1, d)), full((d, d)), full((dp, d))],
        out_specs=row(d),
        out_shape=jax.ShapeDtypeStruct((n, d), F32),
        compiler_params=pltpu.CompilerParams(dimension_semantics=("parallel",), vmem_limit_bytes=VMEM_LIMIT),
        name="ple",
    )(x1, y0, y1, rt, p2, g_ple[None, :].astype(F32), w_gate.astype(BF16), w_proj.astype(BF16))


def _layer(x, p_l, g_attn, w_qkv, q_norm_na, k_norm_na, rpb_na, q_norm_dil, k_norm_dil, g_out_na, g_out_dil,
           w_o, g_ffn, w_rg, b_rg, w_re, b_re, w_exp_gate, w_exp_up, w_exp_down, g_ple, w_ple_gate, w_ple_proj):
    b, s, d = x.shape
    n = b * s
    half = d // 2
    assert d == N_HEADS * HEAD_DIM and half == N_HEADS_NA * HEAD_DIM
    tm = 512
    assert s % tm == 0
    x2 = x.reshape(n, d)
    qa, ka, va, qb, kb, vb = _qkv_proj(x2, g_attn, w_qkv, q_norm_na, k_norm_na, q_norm_dil, k_norm_dil, s, tm)
    seq = lambda t: t.reshape(b, s, half)
    oa = _na_attention(seq(qa), seq(ka), seq(va), rpb_na).reshape(n, half)
    ob = _dil_attention(seq(qb), seq(kb), seq(vb)).reshape(n, half)
    x1, h, rt, rtt = _out_router(oa, ob, x2, g_out_na, g_out_dil, w_o, g_ffn, w_rg, b_rg, w_re, b_re, tm)
    y0, y1 = _moe(h, rtt, w_exp_gate, w_exp_up, w_exp_down, tm, tm)
    out = _ple(x1, y0, y1, rt, p_l.reshape(n, -1), g_ple, w_ple_gate, w_ple_proj, tm)
    return out.reshape(b, s, d)


def kernel(x, p, g_attn, w_qkv, q_norm_na, k_norm_na, rpb_na, q_norm_dil, k_norm_dil, g_out_na, g_out_dil, w_o,
           g_ffn, w_router_group, b_router_group, w_router_expert, b_router_expert, w_exp_gate, w_exp_up,
           w_exp_down, g_ple, w_ple_gate, w_ple_proj):
    for i in range(p.shape[0]):
        x = _layer(x, p[i], g_attn[i], w_qkv[i], q_norm_na[i], k_norm_na[i], rpb_na[i], q_norm_dil[i],
                   k_norm_dil[i], g_out_na[i], g_out_dil[i], w_o[i], g_ffn[i], w_router_group[i],
                   b_router_group[i], w_router_expert[i], b_router_expert[i], w_exp_gate[i], w_exp_up[i],
                   w_exp_down[i], g_ple[i], w_ple_gate[i], w_ple_proj[i])
    return x
```

```python
import functools

import numpy as np
import jax
import jax.numpy as jnp
from jax import lax
from jax.experimental import pallas as pl
from jax.experimental.pallas import tpu as pltpu
from jax.experimental.pallas import tpu_sc as plsc

HEAD_DIM = 64
N_HEADS = 16
N_HEADS_NA = 8
GRID_W = 64
NA_ROWS = 8
NA_COLS = 16
DIL_PAIRS = ((128, 1), (512, 4), (2048, 16))
ROPE_THETA = 10000.0
N_GROUPS = 4
EXPERTS_PER_GROUP = 8
N_EXPERTS = N_GROUPS * EXPERTS_PER_GROUP
EPS = 1e-6
NEG = -1e30
LOG2_E = 1.4426950408889634

LANES = 128
PAIR_W = 2 * HEAD_DIM
DIL_BLK = 128
ATTN_STEPS_PER_TRIP = 64
ATTN_SLOTS = 4
ROUTER_SUB = 512
PLE_SUB = 256
ROW_TILE = 1024
VMEM_LIMIT = 56 * 1024 * 1024
SC_CORES = 2
SC_SUBCORES = 16
SC_CHUNK = 128

F32 = jnp.float32
BF16 = jnp.bfloat16


def _dot(a, b):
    return jnp.dot(a, b, preferred_element_type=F32)


def _dot_nt(a, b):
    return lax.dot_general(a, b, (((1,), (1,)), ((), ())), preferred_element_type=F32)


def _rms(x, gain):
    return x * lax.rsqrt(jnp.mean(x * x, axis=-1, keepdims=True) + EPS) * gain


def _lane_first_half(shape):
    return lax.broadcasted_iota(jnp.int32, shape, len(shape) - 1) < HEAD_DIM


def _qkv_kernel(x_ref, g_ref, w_ref, gq_na_ref, gk_na_ref, gq_dil_ref, gk_dil_ref, cos_ref, sin_ref,
                hsum_ref, qa_ref, ka_ref, va_ref, qb_ref, kb_ref, vb_ref):
    d = x_ref.shape[1]
    half = d // 2
    scale = HEAD_DIM ** -0.5 * LOG2_E
    h = _rms(x_ref[...], g_ref[...]).astype(BF16)

    def proj(col):
        return _dot(h, w_ref[:, col:col + half])

    def head_norm(y, gain):
        sq = (y * y).astype(BF16)
        w = hsum_ref.shape[0]
        ms = jnp.concatenate([_dot(sq[:, c:c + w], hsum_ref[...]) for c in range(0, half, w)], axis=1)
        return y * lax.rsqrt(ms + EPS) * gain

    def rope(y):
        lane = lax.broadcasted_iota(jnp.int32, (y.shape[0], LANES), 1)
        lower = (lane % HEAD_DIM) < HEAD_DIM // 2
        cos = cos_ref[...]
        sin = sin_ref[...]
        outs = []
        for c in range(0, half, LANES):
            yc = y[:, c:c + LANES]
            up = pltpu.roll(yc, LANES - HEAD_DIM // 2, axis=1)
            down = pltpu.roll(yc, HEAD_DIM // 2, axis=1)
            outs.append(yc * cos + jnp.where(lower, up, down) * sin)
        return jnp.concatenate(outs, axis=1)

    qa_ref[...] = (head_norm(proj(0), gq_na_ref[...]) * scale).astype(BF16)
    qb_ref[...] = (rope(head_norm(proj(half), gq_dil_ref[...])) * scale).astype(BF16)
    ka_ref[...] = head_norm(proj(d), gk_na_ref[...]).astype(BF16)
    kb_ref[...] = rope(head_norm(proj(d + half), gk_dil_ref[...])).astype(BF16)
    va_ref[...] = proj(2 * d).astype(BF16)
    vb_ref[...] = proj(2 * d + half).astype(BF16)


def _qkv_proj(x2, g_attn, w_qkv, gq_na, gk_na, gq_dil, gk_dil, seq, tm):
    n, d = x2.shape
    half = d // 2
    pos = jnp.arange(seq, dtype=F32)
    inv = ROPE_THETA ** (-jnp.arange(HEAD_DIM // 2, dtype=F32) / (HEAD_DIM // 2))
    ang = pos[:, None] * inv[None, :]
    cos = jnp.tile(jnp.cos(ang), (1, LANES // (HEAD_DIM // 2)))
    sin = jnp.tile(jnp.concatenate([-jnp.sin(ang), jnp.sin(ang)], axis=1), (1, LANES // HEAD_DIM))
    hs_w = 2 * LANES
    blk = np.arange(hs_w) // HEAD_DIM
    hsum = jnp.asarray((blk[:, None] == blk[None, :]).astype(np.float32) / HEAD_DIM, BF16)
    tile_gain = lambda g: jnp.tile(g.astype(F32), half // HEAD_DIM)[None, :]
    steps_per_seq = seq // tm
    full = lambda shape: pl.BlockSpec(shape, lambda i: (0,) * len(shape))
    out = jax.ShapeDtypeStruct((n, half), BF16)
    return pl.pallas_call(
        _qkv_kernel,
        grid=(n // tm,),
        in_specs=[
            pl.BlockSpec((tm, d), lambda i: (i, 0)),
            full((1, d)),
            full((d, 3 * d)),
            full((1, half)), full((1, half)), full((1, half)), full((1, half)),
            pl.BlockSpec((tm, LANES), lambda i: (i % steps_per_seq, 0)),
            pl.BlockSpec((tm, LANES), lambda i: (i % steps_per_seq, 0)),
            full((hs_w, hs_w)),
        ],
        out_specs=[pl.BlockSpec((tm, half), lambda i: (i, 0))] * 6,
        out_shape=[out] * 6,
        compiler_params=pltpu.CompilerParams(dimension_semantics=("parallel",), vmem_limit_bytes=VMEM_LIMIT),
        name="qkv_proj",
    )(x2, g_attn[None, :].astype(F32), w_qkv.astype(BF16), tile_gain(gq_na), tile_gain(gk_na),
      tile_gain(gq_dil), tile_gain(gk_dil), cos, sin, hsum)


def _is_static(x):
    return isinstance(x, int)


def _clip(x, lo, hi):
    return min(max(x, lo), hi) if _is_static(x) else jnp.clip(x, lo, hi)


def _aligned_ds(start, size, align):
    return pl.ds(start if _is_static(start) else pl.multiple_of(start, align), size)


def _software_pipeline(n_items, stages, steps_per_trip):
    depth = len(stages)
    assert steps_per_trip % ATTN_SLOTS == 0

    def step(t, phase, static):
        for k in reversed(range(depth)):
            if static and not 0 <= t - k < n_items:
                continue
            stages[k](t - k, (phase - k) % ATTN_SLOTS)

    first_full = -(-(depth - 1) // ATTN_SLOTS) * ATTN_SLOTS
    trips = max(n_items - first_full, 0) // steps_per_trip
    if trips < 2:
        trips = 0
    looped_end = first_full + trips * steps_per_trip if trips else 0
    for t in range(first_full if trips else 0):
        step(t, t % ATTN_SLOTS, True)
    if trips:
        def body(i, carry):
            for j in range(steps_per_trip):
                step(first_full + i * steps_per_trip + j, j % ATTN_SLOTS, False)
            return carry

        lax.fori_loop(0, trips, body, 0)
    for t in range(looped_end, n_items + depth - 1):
        step(t, t % ATTN_SLOTS, True)


def _qk_stage(q, kwin, bias_a, bias_b, s_ref, slot, first):
    m = q.shape[0]
    w = kwin.shape[0]
    zero = jnp.zeros_like(q)
    s = _dot_nt(jnp.concatenate([jnp.where(first, q, zero), jnp.where(first, zero, q)], axis=0), kwin)
    s_ref[slot, :m, :w] = s[:m] + bias_a
    s_ref[slot, m:, :w] = s[m:] + bias_b


def _softmax_stage(s_ref, p_ref, slot, w):
    m = jnp.max(s_ref[slot, :, :w], axis=-1, keepdims=True)
    p_ref[slot, :, :w] = jnp.exp2(s_ref[slot, :, :w] - m).astype(BF16)
    return m


def _pv_stage(p_ref, slot, v_win, first):
    w = v_win.shape[0]
    r = _dot(p_ref[slot, :, :w], v_win)
    m = r.shape[0] // 2
    return jnp.where(first, r[:m, :PAIR_W], r[m:, :PAIR_W]), jnp.where(first, r[:m, PAIR_W:], r[m:, PAIR_W:])


def _with_ones(v):
    return jnp.concatenate([v, jnp.ones_like(v)], axis=1)


def _na_kernel(q_ref, k_ref, v_ref, bias_ref, o_ref, v1_ref, s_ref, p_ref):
    rows = q_ref.shape[0] // GRID_W
    win = NA_ROWS * GRID_W
    v1_ref[...] = _with_ones(v_ref[...])
    first = _lane_first_half((GRID_W, PAIR_W))

    def slices(r):
        rs = _clip(r - NA_ROWS // 2, 0, rows - NA_ROWS)
        return _aligned_ds(r * GRID_W, GRID_W, GRID_W), _aligned_ds(rs * GRID_W, win, GRID_W), r - rs

    def qk(r, slot):
        qs, ks, delta = slices(r)
        _qk_stage(q_ref[qs, :], k_ref[ks, :], bias_ref[0, delta], bias_ref[1, delta], s_ref, slot, first)

    def softmax(r, slot):
        _softmax_stage(s_ref, p_ref, slot, win)

    def pv(r, slot):
        qs, ks, _ = slices(r)
        num, den = _pv_stage(p_ref, slot, v1_ref[ks, :], first)
        o_ref[qs, :] = num / den

    _software_pipeline(rows, (qk, softmax, pv), ATTN_STEPS_PER_TRIP)


def _na_bias_table(rpb):
    w = np.arange(GRID_W)
    cs = np.clip(w - NA_COLS // 2, 0, GRID_W - NA_COLS)
    kc = np.arange(GRID_W)
    valid = (kc[None, :] >= cs[:, None]) & (kc[None, :] < cs[:, None] + NA_COLS)
    coff = np.clip(kc[None, :] - w[:, None] + NA_COLS - 1, 0, 2 * NA_COLS - 2)
    roff = np.arange(NA_ROWS)[None, :] - np.arange(NA_ROWS)[:, None] + NA_ROWS - 1
    tab = rpb.astype(F32)[:, roff][:, :, :, coff]
    tab = jnp.where(valid[None, None, None], tab * LOG2_E, NEG)
    tab = tab.transpose(0, 1, 3, 2, 4)
    return tab.reshape(rpb.shape[0], NA_ROWS, GRID_W, NA_ROWS * GRID_W)


def _na_attention(q, k, v, rpb):
    b, s, width = q.shape
    pairs = width // PAIR_W
    assert s % GRID_W == 0 and s // GRID_W >= NA_ROWS
    bias = _na_bias_table(rpb).reshape(pairs, 2, NA_ROWS, GRID_W, NA_ROWS * GRID_W)
    qkv_spec = pl.BlockSpec((None, s, PAIR_W), lambda bi, j: (bi, 0, j))
    return pl.pallas_call(
        _na_kernel,
        grid=(b, pairs),
        in_specs=[qkv_spec, qkv_spec, qkv_spec,
                  pl.BlockSpec((None, 2, NA_ROWS, GRID_W, NA_ROWS * GRID_W), lambda bi, j: (j, 0, 0, 0, 0))],
        out_specs=pl.BlockSpec((None, s, PAIR_W), lambda bi, j: (bi, 0, j)),
        out_shape=jax.ShapeDtypeStruct((b, s, width), F32),
        scratch_shapes=[pltpu.VMEM((s, 2 * PAIR_W), BF16),
                        pltpu.VMEM((ATTN_SLOTS, 2 * GRID_W, NA_ROWS * GRID_W), F32),
                        pltpu.VMEM((ATTN_SLOTS, 2 * GRID_W, NA_ROWS * GRID_W), BF16)],
        compiler_params=pltpu.CompilerParams(dimension_semantics=("parallel", "parallel"),
                                             vmem_limit_bytes=VMEM_LIMIT),
        name="na_attn",
    )(q, k, v, bias)


def _dil_kernel(q_ref, k_ref, v_ref, mwide_ref, mfull_ref, o_ref,
                f32a_ref, f32b_ref, qc_ref, kc_ref, vc_ref, acc_ref, den_ref, max_ref, s_ref, p_ref):
    s = q_ref.shape[0]
    for i, ref in enumerate((q_ref, k_ref, v_ref)):
        f32a_ref[i] = ref[...].astype(F32)
    first = _lane_first_half((DIL_BLK, PAIR_W))
    prev_ref, next_ref, prev_dil = f32a_ref, f32b_ref, 1

    for p, (window, dil) in enumerate(DIL_PAIRS):
        radius = window // (2 * dil)
        cls_len = s // dil
        nblk = cls_len // DIL_BLK
        wide = cls_len >= 2 * DIL_BLK
        win = 2 * DIL_BLK if wide else cls_len
        assert radius == DIL_BLK // 2 and cls_len % DIL_BLK == 0 and dil % prev_dil == 0

        ratio = dil // prev_dil
        keep = ratio > 1 and p + 1 < len(DIL_PAIRS)
        for c in range(dil):
            src = pl.ds((c % prev_dil) * (s // prev_dil) + c // prev_dil, cls_len, stride=ratio)
            dst = pl.ds(c * cls_len, cls_len)
            vals = [prev_ref[i, src, :] for i in range(3)]
            if keep:
                for i in range(3):
                    next_ref[i, dst, :] = vals[i]
            qc_ref[dst, :] = vals[0].astype(BF16)
            kc_ref[dst, :] = vals[1].astype(BF16)
            vc_ref[dst, :] = _with_ones(vals[2].astype(BF16))
        if keep:
            prev_ref, next_ref, prev_dil = next_ref, prev_ref, dil

        def slices(n):
            c, i = divmod(n, nblk) if _is_static(n) else (n // nblk, n % nblk)
            base = c * cls_len
            ws = _clip(i * DIL_BLK - radius, 0, cls_len - win)
            if _is_static(i):
                kind = 0 if i == 0 else (2 if i == nblk - 1 else 1)
            else:
                kind = jnp.where(i == 0, 0, jnp.where(i == nblk - 1, 2, 1))
            return (_aligned_ds(base + i * DIL_BLK, DIL_BLK, DIL_BLK), _aligned_ds(base + ws, win, radius), kind,
                    pl.ds(c + dil * DIL_BLK * i, DIL_BLK, stride=dil))

        def qk(n, slot):
            qs, ks, kind, _ = slices(n)
            mask = mwide_ref[kind] if wide else mfull_ref[...]
            _qk_stage(qc_ref[qs, :], kc_ref[ks, :], mask, mask, s_ref, slot, first)

        def softmax(n, slot):
            m = _softmax_stage(s_ref, p_ref, slot, win)
            max_ref[p, slices(n)[3], :] = jnp.where(first, m[:DIL_BLK], m[DIL_BLK:])

        def pv(n, slot):
            _, ks, _, tok = slices(n)
            num, den = _pv_stage(p_ref, slot, vc_ref[ks, :], first)
            acc_ref[p, tok, :] = num
            den_ref[p, tok, :] = den

        _software_pipeline(dil * nblk, (qk, softmax, pv), ATTN_STEPS_PER_TRIP)

    m = jnp.maximum(jnp.maximum(max_ref[0], max_ref[1]), max_ref[2])
    num = jnp.zeros_like(m)
    den = jnp.zeros_like(m)
    for p in range(len(DIL_PAIRS)):
        w = jnp.exp2(max_ref[p] - m)
        num = num + w * acc_ref[p]
        den = den + w * den_ref[p]
    o_ref[...] = num / den


def _band_mask(kind):
    radius = DIL_BLK // 2
    qq = np.arange(DIL_BLK)[:, None]
    if kind == "full":
        kk = np.arange(DIL_BLK)[None, :]
        shift = 0
    else:
        kk = np.arange(2 * DIL_BLK)[None, :]
        shift = {"first": 0, "inner": radius, "last": DIL_BLK}[kind]
    return np.where(np.abs(kk - qq - shift) <= radius, 0.0, NEG).astype(np.float32)


def _dil_attention(q, k, v):
    b, s, width = q.shape
    pairs = width // PAIR_W
    for window, dil in DIL_PAIRS:
        assert s % (window // 2) == 0 and (s // dil) % DIL_BLK == 0
    mwide = jnp.asarray(np.stack([_band_mask("first"), _band_mask("inner"), _band_mask("last")]))
    mfull = jnp.asarray(_band_mask("full"))
    qkv_spec = pl.BlockSpec((None, s, PAIR_W), lambda bi, j: (bi, 0, j))
    npat = len(DIL_PAIRS)
    return pl.pallas_call(
        _dil_kernel,
        grid=(b, pairs),
        in_specs=[qkv_spec, qkv_spec, qkv_spec,
                  pl.BlockSpec(mwide.shape, lambda bi, j: (0, 0, 0)),
                  pl.BlockSpec(mfull.shape, lambda bi, j: (0, 0))],
        out_specs=pl.BlockSpec((None, s, PAIR_W), lambda bi, j: (bi, 0, j)),
        out_shape=jax.ShapeDtypeStruct((b, s, width), F32),
        scratch_shapes=[pltpu.VMEM((3, s, PAIR_W), F32)] * 2 + [pltpu.VMEM((s, PAIR_W), BF16)] * 2
        + [pltpu.VMEM((s, 2 * PAIR_W), BF16)] + [pltpu.VMEM((npat, s, PAIR_W), F32)] * 3
        + [pltpu.VMEM((ATTN_SLOTS, 2 * DIL_BLK, 2 * DIL_BLK), F32),
           pltpu.VMEM((ATTN_SLOTS, 2 * DIL_BLK, 2 * DIL_BLK), BF16)],
        compiler_params=pltpu.CompilerParams(dimension_semantics=("parallel", "parallel"),
                                             vmem_limit_bytes=VMEM_LIMIT),
        name="dil_attn",
    )(q, k, v, mwide, mfull)


def _split_bf16(x):
    hi = x.astype(BF16)
    return hi, (x - hi.astype(F32)).astype(BF16)


def _route(logits):
    lane = lax.broadcasted_iota(jnp.int32, logits.shape, 1)
    ninf = jnp.float32(-jnp.inf)

    def first_argmax(vals, vmax):
        return jnp.min(jnp.where(vals == vmax, lane, LANES), axis=-1, keepdims=True)

    gl = jnp.where(lane < N_GROUPS, logits, ninf)
    gmax = jnp.max(gl, axis=-1, keepdims=True)
    gsel = first_argmax(gl, gmax)
    gw = 1.0 / jnp.sum(jnp.exp(gl - gmax), axis=-1, keepdims=True)
    lo = N_GROUPS + EXPERTS_PER_GROUP * gsel
    el = jnp.where((lane >= lo) & (lane < lo + EXPERTS_PER_GROUP), logits, ninf)
    v0 = jnp.max(el, axis=-1, keepdims=True)
    i0 = first_argmax(el, v0)
    el = jnp.where(lane == i0, ninf, el)
    v1 = jnp.max(el, axis=-1, keepdims=True)
    i1 = first_argmax(el, v1)
    t = jnp.exp(v1 - v0)
    w0 = gw / (1.0 + t)
    w1 = gw * t / (1.0 + t)
    e0 = (i0 - N_GROUPS).astype(F32)
    e1 = (i1 - N_GROUPS).astype(F32)
    return jnp.where(lane == 0, e0, jnp.where(lane == 1, e1, jnp.where(lane == 2, w0, jnp.where(lane == 3, w1, 0.0))))


def _out_router_kernel(oa_ref, ob_ref, x_ref, ga_ref, gb_ref, wo_ref, gf_ref, wr_ref, br_ref,
                       x1_ref, h_ref, rt_ref, rtt_ref, cnt_ref):
    half = oa_ref.shape[1]

    @pl.when(pl.program_id(0) == 0)
    def _():
        cnt_ref[...] = jnp.zeros_like(cnt_ref)

    for r0 in range(0, oa_ref.shape[0], ROUTER_SUB):
        rows = pl.ds(r0, ROUTER_SUB)
        ya = _rms(oa_ref[rows, :], ga_ref[...]).astype(BF16)
        yb = _rms(ob_ref[rows, :], gb_ref[...]).astype(BF16)
        x1 = x_ref[rows, :] + _dot(ya, wo_ref[:half, :]) + _dot(yb, wo_ref[half:, :])
        x1_ref[rows, :] = x1
        h = _rms(x1, gf_ref[...])
        h_ref[rows, :] = _pack_bf16_pairs(h)
        h_hi, h_lo = _split_bf16(h)
        both = _dot(h_hi, wr_ref[...])
        logits = both[:, :LANES] + both[:, LANES:] + _dot(h_lo, wr_ref[:, :LANES]) + br_ref[...]
        rt = _route(logits)
        rt_ref[rows, :] = rt
        rtt = rt.T[:rtt_ref.shape[0], :]
        rtt_ref[:, rows] = rtt
        oh0, oh1 = _slot_one_hots(rtt)
        cnt_ref[...] += jnp.sum(oh0 + oh1, axis=1, keepdims=True)


def _out_router(oa, ob, x2, g_na, g_dil, w_o, g_ffn, w_rg, b_rg, w_re, b_re, tm):
    n, d = x2.shape
    half = d // 2
    wr = jnp.concatenate([w_rg.astype(F32), w_re.astype(F32).transpose(1, 0, 2).reshape(d, N_EXPERTS)], axis=1)
    wr = jnp.pad(wr, ((0, 0), (0, LANES - wr.shape[1])))
    wr_hi = wr.astype(BF16)
    wr_cat = jnp.concatenate([wr_hi, (wr - wr_hi.astype(F32)).astype(BF16)], axis=1)
    br =jnp.pad(jnp.concatenate([b_rg.astype(F32), b_re.astype(F32).reshape(-1)]), (0, LANES - N_GROUPS - N_EXPERTS))
    full = lambda shape: pl.BlockSpec(shape, lambda i: (0,) * len(shape))
    row = lambda w: pl.BlockSpec((tm, w), lambda i: (i, 0))
    return pl.pallas_call(
        _out_router_kernel,
        grid=(n // tm,),
        in_specs=[row(half), row(half), row(d), full((1, half)), full((1, half)), full((d, d)), full((1, d)),
                  full((d, 2 * LANES)), full((1, LANES))],
        out_specs=[row(d), row(half), row(LANES), pl.BlockSpec((8, tm), lambda i: (0, i)), full((LANES, LANES))],
        out_shape=[jax.ShapeDtypeStruct((n, d), F32), jax.ShapeDtypeStruct((n, half), jnp.uint32),
                   jax.ShapeDtypeStruct((n, LANES), F32), jax.ShapeDtypeStruct((8, n), F32),
                   jax.ShapeDtypeStruct((LANES, LANES), F32)],
        compiler_params=pltpu.CompilerParams(dimension_semantics=("arbitrary",), vmem_limit_bytes=VMEM_LIMIT),
        name="out_router",
    )(oa, ob, x2, g_na[None, :].astype(F32), g_dil[None, :].astype(F32), w_o.astype(BF16),
      g_ffn[None, :].astype(F32), wr_cat, br[None, :])


def _pack_bf16_pairs(x):
    w = x.shape[1] // 2
    bits = lax.bitcast_convert_type(x.astype(BF16).astype(F32), jnp.uint32)
    return bits[:, :w] | (bits[:, w:] >> 16)


def _unpack_bf16_pairs(u):
    hi = lax.bitcast_convert_type(u & jnp.uint32(0xFFFF0000), F32)
    lo = lax.bitcast_convert_type(u << 16, F32)
    return hi, lo


def _slot_one_hots(rtt):
    sub = lax.broadcasted_iota(jnp.int32, (LANES, rtt.shape[1]), 0).astype(F32)
    return (sub == rtt[0:1, :]).astype(F32), (sub == rtt[1:2, :]).astype(F32)


def _position_kernel(rtt_ref, start_ref, pos_ref, base_ref):
    tm = rtt_ref.shape[1]

    @pl.when(pl.program_id(0) == 0)
    def _():
        base_ref[...] = start_ref[...]

    oh0, oh1 = _slot_one_hots(rtt_ref[...])
    oh = oh0 + oh1
    earlier = lax.broadcasted_iota(jnp.int32, (tm, tm), 0) < lax.broadcasted_iota(jnp.int32, (tm, tm), 1)
    before = _dot(oh.astype(BF16), earlier.astype(BF16)) + base_ref[:, 0:1]
    p0 = jnp.sum(before * oh0, axis=0, keepdims=True)
    p1 = jnp.sum(before * oh1, axis=0, keepdims=True)
    row = lax.broadcasted_iota(jnp.int32, pos_ref.shape, 0)
    pos_ref[...] = jnp.where(row == 0, p0, jnp.where(row == 1, p1, 0.0)).astype(jnp.int32)
    base_ref[...] += jnp.sum(oh, axis=1, keepdims=True)


def _expert_positions(rtt, starts, tm):
    n = rtt.shape[1]
    return pl.pallas_call(
        _position_kernel,
        grid=(n // tm,),
        in_specs=[pl.BlockSpec((8, tm), lambda i: (0, i)), pl.BlockSpec((LANES, LANES), lambda i: (0, 0))],
        out_specs=pl.BlockSpec((8, tm), lambda i: (0, i)),
        out_shape=jax.ShapeDtypeStruct((8, n), jnp.int32),
        scratch_shapes=[pltpu.VMEM((LANES, LANES), F32)],
        compiler_params=pltpu.CompilerParams(dimension_semantics=("arbitrary",)),
        name="expert_positions",
    )(rtt, starts)


def _sc_mesh():
    return plsc.VectorSubcoreMesh(core_axis_name="c", subcore_axis_name="s",
                                  num_cores=SC_CORES, num_subcores=SC_SUBCORES)


def _sc_dispatch(hp, pos0, pos1, n_out):
    n, w = hp.shape
    workers = SC_CORES * SC_SUBCORES
    per = n // workers
    chunks = per // SC_CHUNK
    assert n % (workers * SC_CHUNK) == 0

    @functools.partial(
        pl.kernel, out_type=jax.ShapeDtypeStruct((n_out, w), hp.dtype), mesh=_sc_mesh(),
        scratch_types=[pltpu.VMEM((chunks, SC_CHUNK), jnp.int32), pltpu.VMEM((chunks, SC_CHUNK), jnp.int32),
                       pltpu.VMEM((SC_CHUNK, w), hp.dtype)],
        name="moe_dispatch")
    def body(h_hbm, p0_hbm, p1_hbm, xs_hbm, i0_v, i1_v, rows_v):
        wid = lax.axis_index("s") * SC_CORES + lax.axis_index("c")
        pltpu.sync_copy(p0_hbm.at[wid], i0_v)
        pltpu.sync_copy(p1_hbm.at[wid], i1_v)

        @pl.loop(0, chunks)
        def _(j):
            pltpu.sync_copy(h_hbm.at[pl.ds(wid * per + j * SC_CHUNK, SC_CHUNK)], rows_v)
            pltpu.sync_copy(rows_v, xs_hbm.at[i0_v.at[j]])
            pltpu.sync_copy(rows_v, xs_hbm.at[i1_v.at[j]])

    return body(hp, pos0.reshape(workers, chunks, SC_CHUNK), pos1.reshape(workers, chunks, SC_CHUNK))


def _sc_collect(ys, pos0, pos1):
    n = pos0.shape[0]
    w = ys.shape[1]
    workers = SC_CORES * SC_SUBCORES
    per = n // workers
    chunks = per // SC_CHUNK
    out = jax.ShapeDtypeStruct((n, w), ys.dtype)

    @functools.partial(
        pl.kernel, out_type=(out, out), mesh=_sc_mesh(),
        scratch_types=[pltpu.VMEM((chunks, SC_CHUNK), jnp.int32), pltpu.VMEM((chunks, SC_CHUNK), jnp.int32),
                       pltpu.VMEM((SC_CHUNK, w), ys.dtype)],
        name="moe_collect")
    def body(ys_hbm, p0_hbm, p1_hbm, y0_hbm, y1_hbm, i0_v, i1_v, rows_v):
        wid = lax.axis_index("s") * SC_CORES + lax.axis_index("c")
        pltpu.sync_copy(p0_hbm.at[wid], i0_v)
        pltpu.sync_copy(p1_hbm.at[wid], i1_v)

        @pl.loop(0, chunks)
        def _(j):
            dst = pl.ds(wid * per + j * SC_CHUNK, SC_CHUNK)
            pltpu.sync_copy(ys_hbm.at[i0_v.at[j]], rows_v)
            pltpu.sync_copy(rows_v, y0_hbm.at[dst])
            pltpu.sync_copy(ys_hbm.at[i1_v.at[j]], rows_v)
            pltpu.sync_copy(rows_v, y1_hbm.at[dst])

    return body(ys, pos0.reshape(workers, chunks, SC_CHUNK), pos1.reshape(workers, chunks, SC_CHUNK))


def _experts_kernel(te_ref, nt_ref, xs_ref, wg_ref, wu_ref, wd_ref, ys_ref):
    @pl.when(pl.program_id(0) < nt_ref[0])
    def _():
        half = wg_ref.shape[0] // 2
        hi, lo = _unpack_bf16_pairs(xs_ref[...])
        hi = hi.astype(BF16)
        lo = lo.astype(BF16)
        a = _dot(hi, wg_ref[:half, :]) + _dot(lo, wg_ref[half:, :])
        u = _dot(hi, wu_ref[:half, :]) + _dot(lo, wu_ref[half:, :])
        act = (a * jax.nn.sigmoid(a) * u).astype(BF16)
        ys_ref[...] = _pack_bf16_pairs(_dot(act, wd_ref[...]))


def _experts(xs, tile_expert, n_tiles, w_gate, w_up, w_down, tmg):
    rows, w = xs.shape
    ne, d, de = w_gate.shape
    live = lambda t, nt: jnp.minimum(t, nt[0] - 1)
    return pl.pallas_call(
        _experts_kernel,
        grid_spec=pltpu.PrefetchScalarGridSpec(
            num_scalar_prefetch=2,
            grid=(rows // tmg,),
            in_specs=[pl.BlockSpec((tmg, w), lambda t, te, nt: (live(t, nt), 0)),
                      pl.BlockSpec((None, d, de), lambda t, te, nt: (te[live(t, nt)], 0, 0)),
                      pl.BlockSpec((None, d, de), lambda t, te, nt: (te[live(t, nt)], 0, 0)),
                      pl.BlockSpec((None, de, d), lambda t, te, nt: (te[live(t, nt)], 0, 0))],
            out_specs=pl.BlockSpec((tmg, w), lambda t, te, nt: (live(t, nt), 0)),
        ),
        out_shape=jax.ShapeDtypeStruct((rows, w), jnp.uint32),
        compiler_params=pltpu.CompilerParams(dimension_semantics=("arbitrary",), vmem_limit_bytes=VMEM_LIMIT),
        name="experts",
    )(tile_expert, n_tiles, xs, w_gate.astype(BF16), w_up.astype(BF16), w_down.astype(BF16))


def _moe(hp, rtt, cnt, w_gate, w_up, w_down, tm, tmg):
    n = hp.shape[0]
    ne = w_gate.shape[0]
    counts = cnt[:ne, 0].astype(jnp.int32)
    padded = (counts + tmg - 1) // tmg * tmg
    ends = jnp.cumsum(padded)
    starts = jnp.pad((ends - padded).astype(F32), (0, LANES - ne))
    pos = _expert_positions(rtt, jnp.broadcast_to(starts[:, None], (LANES, LANES)), tm)
    pos0, pos1 = pos[0], pos[1]
    rows = 2 * n + ne * tmg
    tile_start = jnp.arange(rows // tmg, dtype=jnp.int32) * tmg
    tile_expert = jnp.minimum(jnp.sum(tile_start[:, None] >= ends[None, :], axis=1), ne - 1).astype(jnp.int32)
    n_tiles = (ends[-1:] // tmg).astype(jnp.int32)
    xs = _sc_dispatch(hp, pos0, pos1, rows)
    ys = _experts(xs, tile_expert, n_tiles, w_gate, w_up, w_down, tmg)
    return _sc_collect(ys, pos0, pos1)


def _ple_kernel(x1_ref, y0_ref, y1_ref, rt_ref, p_ref, g_ref, wg_ref, wp_ref, o_ref):
    for r0 in range(0, x1_ref.shape[0], PLE_SUB):
        rows = pl.ds(r0, PLE_SUB)
        rt = rt_ref[rows, :]
        y0 = jnp.concatenate(_unpack_bf16_pairs(y0_ref[rows, :]), axis=1)
        y1 = jnp.concatenate(_unpack_bf16_pairs(y1_ref[rows, :]), axis=1)
        x2 = x1_ref[rows, :] + rt[:, 2:3] * y0 + rt[:, 3:4] * y1
        gate = jax.nn.sigmoid(_dot(_rms(x2, g_ref[...]).astype(BF16), wg_ref[...]))
        o_ref[rows, :] = x2 + gate * _dot(p_ref[rows, :].astype(BF16), wp_ref[...])


def _ple(x1, y0, y1, rt, p2, g_ple, w_gate, w_proj, tm):
    n, d = x1.shape
    dp = p2.shape[1]
    full = lambda shape: pl.BlockSpec(shape, lambda i: (0,) * len(shape))
    row = lambda w: pl.BlockSpec((tm, w), lambda i: (i, 0))
    return pl.pallas_call(
        _ple_kernel,
        grid=(n // tm,),
        in_specs=[row(d), row(d // 2), row(d // 2), row(LANES), row(dp), full((1, d)), full((d, d)), full((dp, d))],
        out_specs=row(d),
        out_shape=jax.ShapeDtypeStruct((n, d), F32),
        compiler_params=pltpu.CompilerParams(dimension_semantics=("parallel",), vmem_limit_bytes=VMEM_LIMIT),
        name="ple",
    )(x1, y0, y1, rt, p2, g_ple[None, :].astype(F32), w_gate.astype(BF16), w_proj.astype(BF16))


def _layer(x, p_l, g_attn, w_qkv, q_norm_na, k_norm_na, rpb_na, q_norm_dil, k_norm_dil, g_out_na, g_out_dil,
           w_o, g_ffn, w_rg, b_rg, w_re, b_re, w_exp_gate, w_exp_up, w_exp_down, g_ple, w_ple_gate, w_ple_proj):
    b, s, d = x.shape
    n = b * s
    half = d // 2
    assert d == N_HEADS * HEAD_DIM and half == N_HEADS_NA * HEAD_DIM
    tm = 512
    assert s % tm == 0
    x2 = x.reshape(n, d)
    qa, ka, va, qb, kb, vb = _qkv_proj(x2, g_attn, w_qkv, q_norm_na, k_norm_na, q_norm_dil, k_norm_dil, s, tm)
    seq = lambda t: t.reshape(b, s, half)
    oa = _na_attention(seq(qa), seq(ka), seq(va), rpb_na).reshape(n, half)
    ob = _dil_attention(seq(qb), seq(kb), seq(vb)).reshape(n, half)
    x1, h, rt, rtt, cnt = _out_router(oa, ob, x2, g_out_na, g_out_dil, w_o, g_ffn, w_rg, b_rg, w_re, b_re, ROW_TILE)
    y0, y1 = _moe(h, rtt, cnt, w_exp_gate, w_exp_up, w_exp_down, 2 * tm, tm)
    out = _ple(x1, y0, y1, rt, p_l.reshape(n, -1), g_ple, w_ple_gate, w_ple_proj, ROW_TILE)
    return out.reshape(b, s, d)


def kernel(x, p, g_attn, w_qkv, q_norm_na, k_norm_na, rpb_na, q_norm_dil, k_norm_dil, g_out_na, g_out_dil, w_o,
           g_ffn, w_router_group, b_router_group, w_router_expert, b_router_expert, w_exp_gate, w_exp_up,
           w_exp_down, g_ple, w_ple_gate, w_ple_proj):
    for i in range(p.shape[0]):
        x = _layer(x, p[i], g_attn[i], w_qkv[i], q_norm_na[i], k_norm_na[i], rpb_na[i], q_norm_dil[i],
                   k_norm_dil[i], g_out_na[i], g_out_dil[i], w_o[i], g_ffn[i], w_router_group[i],
                   b_router_group[i], w_router_expert[i], b_router_expert[i], w_exp_gate[i], w_exp_up[i],
                   w_exp_down[i], g_ple[i], w_ple_gate[i], w_ple_proj[i])
    return x
```

```python
import functools

import numpy as np
import jax
import jax.numpy as jnp
from jax import lax
from jax.experimental import pallas as pl
from jax.experimental.pallas import tpu as pltpu
from jax.experimental.pallas import tpu_sc as plsc

HEAD_DIM = 64
N_HEADS = 16
N_HEADS_NA = 8
GRID_W = 64
NA_ROWS = 8
NA_COLS = 16
DIL_PAIRS = ((128, 1), (512, 4), (2048, 16))
ROPE_THETA = 10000.0
N_GROUPS = 4
EXPERTS_PER_GROUP = 8
N_EXPERTS = N_GROUPS * EXPERTS_PER_GROUP
EPS = 1e-6
NEG = -1e30
LOG2_E = 1.4426950408889634

LANES = 128
PAIR_W = 2 * HEAD_DIM
DIL_BLK = 128
ATTN_STEPS_PER_TRIP = 64
ATTN_SLOTS = 4
ROUTER_SUB = 512
PLE_SUB = 256
ROW_TILE = 1024
VMEM_LIMIT = 56 * 1024 * 1024
SC_CORES = 2
SC_SUBCORES = 16
SC_CHUNK = 128
TOKEN_CHUNKS = 2

F32 = jnp.float32
BF16 = jnp.bfloat16


def _dot(a, b):
    return jnp.dot(a, b, preferred_element_type=F32)


def _dot_nt(a, b):
    return lax.dot_general(a, b, (((1,), (1,)), ((), ())), preferred_element_type=F32)


def _rms(x, gain):
    return x * lax.rsqrt(jnp.mean(x * x, axis=-1, keepdims=True) + EPS) * gain


def _lane_first_half(shape):
    return lax.broadcasted_iota(jnp.int32, shape, len(shape) - 1) < HEAD_DIM


def _qkv_kernel(x_ref, g_ref, w_ref, gq_na_ref, gk_na_ref, gq_dil_ref, gk_dil_ref, cos_ref, sin_ref,
                hsum_ref, qa_ref, ka_ref, va_ref, qb_ref, kb_ref, vb_ref):
    d = x_ref.shape[1]
    half = d // 2
    scale = HEAD_DIM ** -0.5 * LOG2_E
    h = _rms(x_ref[...], g_ref[...]).astype(BF16)

    def proj(col):
        return _dot(h, w_ref[:, col:col + half])

    def head_norm(y, gain):
        sq = (y * y).astype(BF16)
        w = hsum_ref.shape[0]
        ms = jnp.concatenate([_dot(sq[:, c:c + w], hsum_ref[...]) for c in range(0, half, w)], axis=1)
        return y * lax.rsqrt(ms + EPS) * gain

    def rope(y):
        lane = lax.broadcasted_iota(jnp.int32, (y.shape[0], LANES), 1)
        lower = (lane % HEAD_DIM) < HEAD_DIM // 2
        cos = cos_ref[...]
        sin = sin_ref[...]
        outs = []
        for c in range(0, half, LANES):
            yc = y[:, c:c + LANES]
            up = pltpu.roll(yc, LANES - HEAD_DIM // 2, axis=1)
            down = pltpu.roll(yc, HEAD_DIM // 2, axis=1)
            outs.append(yc * cos + jnp.where(lower, up, down) * sin)
        return jnp.concatenate(outs, axis=1)

    qa_ref[...] = (head_norm(proj(0), gq_na_ref[...]) * scale).astype(BF16)
    qb_ref[...] = (rope(head_norm(proj(half), gq_dil_ref[...])) * scale).astype(BF16)
    ka_ref[...] = head_norm(proj(d), gk_na_ref[...]).astype(BF16)
    kb_ref[...] = rope(head_norm(proj(d + half), gk_dil_ref[...])).astype(BF16)
    va_ref[...] = proj(2 * d).astype(BF16)
    vb_ref[...] = proj(2 * d + half).astype(BF16)


def _qkv_proj(x2, g_attn, w_qkv, gq_na, gk_na, gq_dil, gk_dil, seq, tm):
    n, d = x2.shape
    half = d // 2
    pos = jnp.arange(seq, dtype=F32)
    inv = ROPE_THETA ** (-jnp.arange(HEAD_DIM // 2, dtype=F32) / (HEAD_DIM // 2))
    ang = pos[:, None] * inv[None, :]
    cos = jnp.tile(jnp.cos(ang), (1, LANES // (HEAD_DIM // 2)))
    sin = jnp.tile(jnp.concatenate([-jnp.sin(ang), jnp.sin(ang)], axis=1), (1, LANES // HEAD_DIM))
    hs_w = 2 * LANES
    blk = np.arange(hs_w) // HEAD_DIM
    hsum = jnp.asarray((blk[:, None] == blk[None, :]).astype(np.float32) / HEAD_DIM, BF16)
    tile_gain = lambda g: jnp.tile(g.astype(F32), half // HEAD_DIM)[None, :]
    steps_per_seq = seq // tm
    full = lambda shape: pl.BlockSpec(shape, lambda i: (0,) * len(shape))
    out = jax.ShapeDtypeStruct((n, half), BF16)
    return pl.pallas_call(
        _qkv_kernel,
        grid=(n // tm,),
        in_specs=[
            pl.BlockSpec((tm, d), lambda i: (i, 0)),
            full((1, d)),
            full((d, 3 * d)),
            full((1, half)), full((1, half)), full((1, half)), full((1, half)),
            pl.BlockSpec((tm, LANES), lambda i: (i % steps_per_seq, 0)),
            pl.BlockSpec((tm, LANES), lambda i: (i % steps_per_seq, 0)),
            full((hs_w, hs_w)),
        ],
        out_specs=[pl.BlockSpec((tm, half), lambda i: (i, 0))] * 6,
        out_shape=[out] * 6,
        compiler_params=pltpu.CompilerParams(dimension_semantics=("parallel",), vmem_limit_bytes=VMEM_LIMIT),
        name="qkv_proj",
    )(x2, g_attn[None, :].astype(F32), w_qkv.astype(BF16), tile_gain(gq_na), tile_gain(gk_na),
      tile_gain(gq_dil), tile_gain(gk_dil), cos, sin, hsum)


def _is_static(x):
    return isinstance(x, int)


def _clip(x, lo, hi):
    return min(max(x, lo), hi) if _is_static(x) else jnp.clip(x, lo, hi)


def _aligned_ds(start, size, align):
    return pl.ds(start if _is_static(start) else pl.multiple_of(start, align), size)


def _software_pipeline(n_items, stages, steps_per_trip):
    depth = len(stages)
    assert steps_per_trip % ATTN_SLOTS == 0

    def step(t, phase, static):
        for k in reversed(range(depth)):
            if static and not 0 <= t - k < n_items:
                continue
            stages[k](t - k, (phase - k) % ATTN_SLOTS)

    first_full = -(-(depth - 1) // ATTN_SLOTS) * ATTN_SLOTS
    trips = max(n_items - first_full, 0) // steps_per_trip
    if trips < 2:
        trips = 0
    looped_end = first_full + trips * steps_per_trip if trips else 0
    for t in range(first_full if trips else 0):
        step(t, t % ATTN_SLOTS, True)
    if trips:
        def body(i, carry):
            for j in range(steps_per_trip):
                step(first_full + i * steps_per_trip + j, j % ATTN_SLOTS, False)
            return carry

        lax.fori_loop(0, trips, body, 0)
    for t in range(looped_end, n_items + depth - 1):
        step(t, t % ATTN_SLOTS, True)


def _qk_stage(q, kwin, bias_a, bias_b, s_ref, slot, first):
    m = q.shape[0]
    w = kwin.shape[0]
    zero = jnp.zeros_like(q)
    s = _dot_nt(jnp.concatenate([jnp.where(first, q, zero), jnp.where(first, zero, q)], axis=0), kwin)
    s_ref[slot, :m, :w] = s[:m] + bias_a
    s_ref[slot, m:, :w] = s[m:] + bias_b


def _softmax_stage(s_ref, p_ref, slot, w):
    m = jnp.max(s_ref[slot, :, :w], axis=-1, keepdims=True)
    p_ref[slot, :, :w] = jnp.exp2(s_ref[slot, :, :w] - m).astype(BF16)
    return m


def _pv_stage(p_ref, slot, v_win, first):
    w = v_win.shape[0]
    r = _dot(p_ref[slot, :, :w], v_win)
    m = r.shape[0] // 2
    return jnp.where(first, r[:m, :PAIR_W], r[m:, :PAIR_W]), jnp.where(first, r[:m, PAIR_W:], r[m:, PAIR_W:])


def _with_ones(v):
    return jnp.concatenate([v, jnp.ones_like(v)], axis=1)


def _na_kernel(q_ref, k_ref, v_ref, bias_ref, o_ref, v1_ref, s_ref, p_ref):
    rows = q_ref.shape[0] // GRID_W
    win = NA_ROWS * GRID_W
    v1_ref[...] = _with_ones(v_ref[...])
    first = _lane_first_half((GRID_W, PAIR_W))

    def slices(r):
        rs = _clip(r - NA_ROWS // 2, 0, rows - NA_ROWS)
        return _aligned_ds(r * GRID_W, GRID_W, GRID_W), _aligned_ds(rs * GRID_W, win, GRID_W), r - rs

    def bias(head, delta):
        assert _is_static(delta)
        return jnp.concatenate([bias_ref[head, j - delta + NA_ROWS - 1] for j in range(0, NA_ROWS, 2)], axis=1)

    def qk(r, slot):
        qs, ks, delta = slices(r)
        _qk_stage(q_ref[qs, :], k_ref[ks, :], bias(0, delta), bias(1, delta), s_ref, slot, first)

    def softmax(r, slot):
        _softmax_stage(s_ref, p_ref, slot, win)

    def pv(r, slot):
        qs, ks, _ = slices(r)
        num, den = _pv_stage(p_ref, slot, v1_ref[ks, :], first)
        o_ref[qs, :] = num / den

    _software_pipeline(rows, (qk, softmax, pv), ATTN_STEPS_PER_TRIP)


def _na_bias_table(rpb):
    w = np.arange(GRID_W)
    cs = np.clip(w - NA_COLS // 2, 0, GRID_W - NA_COLS)
    kc = np.arange(GRID_W)
    valid = (kc[None, :] >= cs[:, None]) & (kc[None, :] < cs[:, None] + NA_COLS)
    coff = np.clip(kc[None, :] - w[:, None] + NA_COLS - 1, 0, 2 * NA_COLS - 2)
    tab = jnp.where(valid[None, None], rpb.astype(F32)[:, :, coff] * LOG2_E, NEG)
    return jnp.concatenate([tab[:, :-1], tab[:, 1:]], axis=-1)


def _na_attention(q, k, v, rpb):
    b, s, width = q.shape
    pairs = width // PAIR_W
    assert s % GRID_W == 0 and s // GRID_W >= NA_ROWS
    bias = _na_bias_table(rpb).reshape(pairs, 2, 2 * NA_ROWS - 2, GRID_W, 2 * GRID_W)
    qkv_spec = pl.BlockSpec((None, s, PAIR_W), lambda bi, j: (bi, 0, j))
    return pl.pallas_call(
        _na_kernel,
        grid=(b, pairs),
        in_specs=[qkv_spec, qkv_spec, qkv_spec,
                  pl.BlockSpec((None, 2, 2 * NA_ROWS - 2, GRID_W, 2 * GRID_W), lambda bi, j: (j, 0, 0, 0, 0))],
        out_specs=pl.BlockSpec((None, s, PAIR_W), lambda bi, j: (bi, 0, j)),
        out_shape=jax.ShapeDtypeStruct((b, s, width), F32),
        scratch_shapes=[pltpu.VMEM((s, 2 * PAIR_W), BF16),
                        pltpu.VMEM((ATTN_SLOTS, 2 * GRID_W, NA_ROWS * GRID_W), F32),
                        pltpu.VMEM((ATTN_SLOTS, 2 * GRID_W, NA_ROWS * GRID_W), BF16)],
        compiler_params=pltpu.CompilerParams(dimension_semantics=("parallel", "parallel"),
                                             vmem_limit_bytes=VMEM_LIMIT),
        name="na_attn",
    )(q, k, v, bias)


def _dil_kernel(q_ref, k_ref, v_ref, mwide_ref, mfull_ref, o_ref,
                f32a_ref, f32b_ref, qc_ref, kc_ref, vc_ref, acc_ref, den_ref, max_ref, s_ref, p_ref):
    s = q_ref.shape[0]
    for i, ref in enumerate((q_ref, k_ref, v_ref)):
        f32a_ref[i] = ref[...].astype(F32)
    first = _lane_first_half((DIL_BLK, PAIR_W))
    prev_ref, next_ref, prev_dil = f32a_ref, f32b_ref, 1

    for p, (window, dil) in enumerate(DIL_PAIRS):
        radius = window // (2 * dil)
        cls_len = s // dil
        nblk = cls_len // DIL_BLK
        wide = cls_len >= 2 * DIL_BLK
        win = 2 * DIL_BLK if wide else cls_len
        assert radius == DIL_BLK // 2 and cls_len % DIL_BLK == 0 and dil % prev_dil == 0

        ratio = dil // prev_dil
        keep = ratio > 1 and p + 1 < len(DIL_PAIRS)
        for c in range(dil):
            src = pl.ds((c % prev_dil) * (s // prev_dil) + c // prev_dil, cls_len, stride=ratio)
            dst = pl.ds(c * cls_len, cls_len)
            vals = [prev_ref[i, src, :] for i in range(3)]
            if keep:
                for i in range(3):
                    next_ref[i, dst, :] = vals[i]
            qc_ref[dst, :] = vals[0].astype(BF16)
            kc_ref[dst, :] = vals[1].astype(BF16)
            vc_ref[dst, :] = _with_ones(vals[2].astype(BF16))
        if keep:
            prev_ref, next_ref, prev_dil = next_ref, prev_ref, dil

        def slices(n):
            c, i = divmod(n, nblk) if _is_static(n) else (n // nblk, n % nblk)
            base = c * cls_len
            ws = _clip(i * DIL_BLK - radius, 0, cls_len - win)
            if _is_static(i):
                kind = 0 if i == 0 else (2 if i == nblk - 1 else 1)
            else:
                kind = jnp.where(i == 0, 0, jnp.where(i == nblk - 1, 2, 1))
            return (_aligned_ds(base + i * DIL_BLK, DIL_BLK, DIL_BLK), _aligned_ds(base + ws, win, radius), kind,
                    pl.ds(c + dil * DIL_BLK * i, DIL_BLK, stride=dil))

        def qk(n, slot):
            qs, ks, kind, _ = slices(n)
            mask = mwide_ref[kind] if wide else mfull_ref[...]
            _qk_stage(qc_ref[qs, :], kc_ref[ks, :], mask, mask, s_ref, slot, first)

        def softmax(n, slot):
            m = _softmax_stage(s_ref, p_ref, slot, win)
            max_ref[p, slices(n)[3], :] = jnp.where(first, m[:DIL_BLK], m[DIL_BLK:])

        def pv(n, slot):
            _, ks, _, tok = slices(n)
            num, den = _pv_stage(p_ref, slot, vc_ref[ks, :], first)
            acc_ref[p, tok, :] = num
            den_ref[p, tok, :] = den

        _software_pipeline(dil * nblk, (qk, softmax, pv), ATTN_STEPS_PER_TRIP)

    m = jnp.maximum(jnp.maximum(max_ref[0], max_ref[1]), max_ref[2])
    num = jnp.zeros_like(m)
    den = jnp.zeros_like(m)
    for p in range(len(DIL_PAIRS)):
        w = jnp.exp2(max_ref[p] - m)
        num = num + w * acc_ref[p]
        den = den + w * den_ref[p]
    o_ref[...] = num / den


def _band_mask(kind):
    radius = DIL_BLK // 2
    qq = np.arange(DIL_BLK)[:, None]
    if kind == "full":
        kk = np.arange(DIL_BLK)[None, :]
        shift = 0
    else:
        kk = np.arange(2 * DIL_BLK)[None, :]
        shift = {"first": 0, "inner": radius, "last": DIL_BLK}[kind]
    return np.where(np.abs(kk - qq - shift) <= radius, 0.0, NEG).astype(np.float32)


def _dil_attention(q, k, v):
    b, s, width = q.shape
    pairs = width // PAIR_W
    for window, dil in DIL_PAIRS:
        assert s % (window // 2) == 0 and (s // dil) % DIL_BLK == 0
    mwide = jnp.asarray(np.stack([_band_mask("first"), _band_mask("inner"), _band_mask("last")]))
    mfull = jnp.asarray(_band_mask("full"))
    qkv_spec = pl.BlockSpec((None, s, PAIR_W), lambda bi, j: (bi, 0, j))
    npat = len(DIL_PAIRS)
    return pl.pallas_call(
        _dil_kernel,
        grid=(b, pairs),
        in_specs=[qkv_spec, qkv_spec, qkv_spec,
                  pl.BlockSpec(mwide.shape, lambda bi, j: (0, 0, 0)),
                  pl.BlockSpec(mfull.shape, lambda bi, j: (0, 0))],
        out_specs=pl.BlockSpec((None, s, PAIR_W), lambda bi, j: (bi, 0, j)),
        out_shape=jax.ShapeDtypeStruct((b, s, width), F32),
        scratch_shapes=[pltpu.VMEM((3, s, PAIR_W), F32)] * 2 + [pltpu.VMEM((s, PAIR_W), BF16)] * 2
        + [pltpu.VMEM((s, 2 * PAIR_W), BF16)] + [pltpu.VMEM((npat, s, PAIR_W), F32)] * 3
        + [pltpu.VMEM((ATTN_SLOTS, 2 * DIL_BLK, 2 * DIL_BLK), F32),
           pltpu.VMEM((ATTN_SLOTS, 2 * DIL_BLK, 2 * DIL_BLK), BF16)],
        compiler_params=pltpu.CompilerParams(dimension_semantics=("parallel", "parallel"),
                                             vmem_limit_bytes=VMEM_LIMIT),
        name="dil_attn",
    )(q, k, v, mwide, mfull)


def _split_bf16(x):
    hi = x.astype(BF16)
    return hi, (x - hi.astype(F32)).astype(BF16)


def _route(logits):
    lane = lax.broadcasted_iota(jnp.int32, logits.shape, 1)
    ninf = jnp.float32(-jnp.inf)

    def first_argmax(vals, vmax):
        return jnp.min(jnp.where(vals == vmax, lane, LANES), axis=-1, keepdims=True)

    gl = jnp.where(lane < N_GROUPS, logits, ninf)
    gmax = jnp.max(gl, axis=-1, keepdims=True)
    gsel = first_argmax(gl, gmax)
    gw = 1.0 / jnp.sum(jnp.exp(gl - gmax), axis=-1, keepdims=True)
    lo = N_GROUPS + EXPERTS_PER_GROUP * gsel
    el = jnp.where((lane >= lo) & (lane < lo + EXPERTS_PER_GROUP), logits, ninf)
    v0 = jnp.max(el, axis=-1, keepdims=True)
    i0 = first_argmax(el, v0)
    el = jnp.where(lane == i0, ninf, el)
    v1 = jnp.max(el, axis=-1, keepdims=True)
    i1 = first_argmax(el, v1)
    t = jnp.exp(v1 - v0)
    w0 = gw / (1.0 + t)
    w1 = gw * t / (1.0 + t)
    e0 = (i0 - N_GROUPS).astype(F32)
    e1 = (i1 - N_GROUPS).astype(F32)
    return jnp.where(lane == 0, e0, jnp.where(lane == 1, e1, jnp.where(lane == 2, w0, jnp.where(lane == 3, w1, 0.0))))


def _out_router_kernel(oa_ref, ob_ref, x_ref, ga_ref, gb_ref, wo_ref, gf_ref, wr_ref, br_ref,
                       x1_ref, h_ref, rt_ref, rtt_ref, cnt_ref):
    half = oa_ref.shape[1]

    @pl.when(pl.program_id(0) == 0)
    def _():
        cnt_ref[...] = jnp.zeros_like(cnt_ref)

    for r0 in range(0, oa_ref.shape[0], ROUTER_SUB):
        rows = pl.ds(r0, ROUTER_SUB)
        ya = _rms(oa_ref[rows, :], ga_ref[...]).astype(BF16)
        yb = _rms(ob_ref[rows, :], gb_ref[...]).astype(BF16)
        x1 = x_ref[rows, :] + _dot(ya, wo_ref[:half, :]) + _dot(yb, wo_ref[half:, :])
        x1_ref[rows, :] = x1
        h = _rms(x1, gf_ref[...])
        h_ref[rows, :] = _pack_bf16_pairs(h)
        h_hi, h_lo = _split_bf16(h)
        both = _dot(h_hi, wr_ref[...])
        logits = both[:, :LANES] + both[:, LANES:] + _dot(h_lo, wr_ref[:, :LANES]) + br_ref[...]
        rt = _route(logits)
        rt_ref[rows, :] = rt
        rtt = rt.T[:rtt_ref.shape[0], :]
        rtt_ref[:, rows] = rtt
        oh0, oh1 = _slot_one_hots(rtt)
        cnt_ref[...] += jnp.sum(oh0 + oh1, axis=1, keepdims=True)


def _out_router(oa, ob, x2, g_na, g_dil, w_o, g_ffn, w_rg, b_rg, w_re, b_re, tm, row0, n):
    d = x2.shape[1]
    half = d // 2
    blk0 = row0 // tm
    wr = jnp.concatenate([w_rg.astype(F32), w_re.astype(F32).transpose(1, 0, 2).reshape(d, N_EXPERTS)], axis=1)
    wr = jnp.pad(wr, ((0, 0), (0, LANES - wr.shape[1])))
    wr_hi = wr.astype(BF16)
    wr_cat = jnp.concatenate([wr_hi, (wr - wr_hi.astype(F32)).astype(BF16)], axis=1)
    br =jnp.pad(jnp.concatenate([b_rg.astype(F32), b_re.astype(F32).reshape(-1)]), (0, LANES - N_GROUPS - N_EXPERTS))
    full = lambda shape: pl.BlockSpec(shape, lambda i: (0,) * len(shape))
    row = lambda w: pl.BlockSpec((tm, w), lambda i: (i, 0))
    row_in = lambda w: pl.BlockSpec((tm, w), lambda i: (i + blk0, 0))
    return pl.pallas_call(
        _out_router_kernel,
        grid=(n // tm,),
        in_specs=[row_in(half), row_in(half), row_in(d), full((1, half)), full((1, half)), full((d, d)),
                  full((1, d)), full((d, 2 * LANES)), full((1, LANES))],
        out_specs=[row(d), row(half), row(LANES), pl.BlockSpec((8, tm), lambda i: (0, i)), full((LANES, LANES))],
        out_shape=[jax.ShapeDtypeStruct((n, d), F32), jax.ShapeDtypeStruct((n, half), jnp.uint32),
                   jax.ShapeDtypeStruct((n, LANES), F32), jax.ShapeDtypeStruct((8, n), F32),
                   jax.ShapeDtypeStruct((LANES, LANES), F32)],
        compiler_params=pltpu.CompilerParams(dimension_semantics=("arbitrary",), vmem_limit_bytes=VMEM_LIMIT),
        name="out_router",
    )(oa, ob, x2, g_na[None, :].astype(F32), g_dil[None, :].astype(F32), w_o.astype(BF16),
      g_ffn[None, :].astype(F32), wr_cat, br[None, :])


def _pack_bf16_pairs(x):
    w = x.shape[1] // 2
    bits = lax.bitcast_convert_type(x.astype(BF16).astype(F32), jnp.uint32)
    return bits[:, :w] | (bits[:, w:] >> 16)


def _unpack_bf16_pairs(u):
    hi = lax.bitcast_convert_type(u & jnp.uint32(0xFFFF0000), F32)
    lo = lax.bitcast_convert_type(u << 16, F32)
    return hi, lo


def _slot_one_hots(rtt):
    sub = lax.broadcasted_iota(jnp.int32, (LANES, rtt.shape[1]), 0).astype(F32)
    return (sub == rtt[0:1, :]).astype(F32), (sub == rtt[1:2, :]).astype(F32)


def _position_kernel(rtt_ref, start_ref, pos_ref, base_ref):
    tm = rtt_ref.shape[1]

    @pl.when(pl.program_id(0) == 0)
    def _():
        base_ref[...] = start_ref[...]

    oh0, oh1 = _slot_one_hots(rtt_ref[...])
    oh = oh0 + oh1
    earlier = lax.broadcasted_iota(jnp.int32, (tm, tm), 0) < lax.broadcasted_iota(jnp.int32, (tm, tm), 1)
    before = _dot(oh.astype(BF16), earlier.astype(BF16)) + base_ref[:, 0:1]
    p0 = jnp.sum(before * oh0, axis=0, keepdims=True)
    p1 = jnp.sum(before * oh1, axis=0, keepdims=True)
    row = lax.broadcasted_iota(jnp.int32, pos_ref.shape, 0)
    pos_ref[...] = jnp.where(row == 0, p0, jnp.where(row == 1, p1, 0.0)).astype(jnp.int32)
    base_ref[...] += jnp.sum(oh, axis=1, keepdims=True)


def _expert_positions(rtt, starts, tm):
    n = rtt.shape[1]
    return pl.pallas_call(
        _position_kernel,
        grid=(n // tm,),
        in_specs=[pl.BlockSpec((8, tm), lambda i: (0, i)), pl.BlockSpec((LANES, LANES), lambda i: (0, 0))],
        out_specs=pl.BlockSpec((8, tm), lambda i: (0, i)),
        out_shape=jax.ShapeDtypeStruct((8, n), jnp.int32),
        scratch_shapes=[pltpu.VMEM((LANES, LANES), F32)],
        compiler_params=pltpu.CompilerParams(dimension_semantics=("arbitrary",)),
        name="expert_positions",
    )(rtt, starts)


def _sc_mesh():
    return plsc.VectorSubcoreMesh(core_axis_name="c", subcore_axis_name="s",
                                  num_cores=SC_CORES, num_subcores=SC_SUBCORES)


def _sc_dispatch(hp, pos0, pos1, n_out):
    n, w = hp.shape
    workers = SC_CORES * SC_SUBCORES
    per = n // workers
    chunks = per // SC_CHUNK
    assert n % (workers * SC_CHUNK) == 0

    @functools.partial(
        pl.kernel, out_type=jax.ShapeDtypeStruct((n_out, w), hp.dtype), mesh=_sc_mesh(),
        scratch_types=[pltpu.VMEM((chunks, SC_CHUNK), jnp.int32), pltpu.VMEM((chunks, SC_CHUNK), jnp.int32),
                       pltpu.VMEM((SC_CHUNK, w), hp.dtype)],
        name="moe_dispatch")
    def body(h_hbm, p0_hbm, p1_hbm, xs_hbm, i0_v, i1_v, rows_v):
        wid = lax.axis_index("s") * SC_CORES + lax.axis_index("c")
        pltpu.sync_copy(p0_hbm.at[wid], i0_v)
        pltpu.sync_copy(p1_hbm.at[wid], i1_v)

        @pl.loop(0, chunks)
        def _(j):
            pltpu.sync_copy(h_hbm.at[pl.ds(wid * per + j * SC_CHUNK, SC_CHUNK)], rows_v)
            pltpu.sync_copy(rows_v, xs_hbm.at[i0_v.at[j]])
            pltpu.sync_copy(rows_v, xs_hbm.at[i1_v.at[j]])

    return body(hp, pos0.reshape(workers, chunks, SC_CHUNK), pos1.reshape(workers, chunks, SC_CHUNK))


def _sc_collect(ys, pos0, pos1):
    n = pos0.shape[0]
    w = ys.shape[1]
    workers = SC_CORES * SC_SUBCORES
    per = n // workers
    chunks = per // SC_CHUNK
    out = jax.ShapeDtypeStruct((n, w), ys.dtype)

    @functools.partial(
        pl.kernel, out_type=(out, out), mesh=_sc_mesh(),
        scratch_types=[pltpu.VMEM((chunks, SC_CHUNK), jnp.int32), pltpu.VMEM((chunks, SC_CHUNK), jnp.int32),
                       pltpu.VMEM((SC_CHUNK, w), ys.dtype)],
        name="moe_collect")
    def body(ys_hbm, p0_hbm, p1_hbm, y0_hbm, y1_hbm, i0_v, i1_v, rows_v):
        wid = lax.axis_index("s") * SC_CORES + lax.axis_index("c")
        pltpu.sync_copy(p0_hbm.at[wid], i0_v)
        pltpu.sync_copy(p1_hbm.at[wid], i1_v)

        @pl.loop(0, chunks)
        def _(j):
            dst = pl.ds(wid * per + j * SC_CHUNK, SC_CHUNK)
            pltpu.sync_copy(ys_hbm.at[i0_v.at[j]], rows_v)
            pltpu.sync_copy(rows_v, y0_hbm.at[dst])
            pltpu.sync_copy(ys_hbm.at[i1_v.at[j]], rows_v)
            pltpu.sync_copy(rows_v, y1_hbm.at[dst])

    return body(ys, pos0.reshape(workers, chunks, SC_CHUNK), pos1.reshape(workers, chunks, SC_CHUNK))


def _experts_kernel(te_ref, nt_ref, xs_ref, wg_ref, wu_ref, wd_ref, ys_ref):
    @pl.when(pl.program_id(0) < nt_ref[0])
    def _():
        half = wg_ref.shape[0] // 2
        hi, lo = _unpack_bf16_pairs(xs_ref[...])
        hi = hi.astype(BF16)
        lo = lo.astype(BF16)
        a = _dot(hi, wg_ref[:half, :]) + _dot(lo, wg_ref[half:, :])
        u = _dot(hi, wu_ref[:half, :]) + _dot(lo, wu_ref[half:, :])
        act = (a * jax.nn.sigmoid(a) * u).astype(BF16)
        ys_ref[...] = _pack_bf16_pairs(_dot(act, wd_ref[...]))


def _experts(xs, tile_expert, n_tiles, w_gate, w_up, w_down, tmg):
    rows, w = xs.shape
    ne, d, de = w_gate.shape
    live = lambda t, nt: jnp.minimum(t, nt[0] - 1)
    return pl.pallas_call(
        _experts_kernel,
        grid_spec=pltpu.PrefetchScalarGridSpec(
            num_scalar_prefetch=2,
            grid=(rows // tmg,),
            in_specs=[pl.BlockSpec((tmg, w), lambda t, te, nt: (live(t, nt), 0)),
                      pl.BlockSpec((None, d, de), lambda t, te, nt: (te[live(t, nt)], 0, 0)),
                      pl.BlockSpec((None, d, de), lambda t, te, nt: (te[live(t, nt)], 0, 0)),
                      pl.BlockSpec((None, de, d), lambda t, te, nt: (te[live(t, nt)], 0, 0))],
            out_specs=pl.BlockSpec((tmg, w), lambda t, te, nt: (live(t, nt), 0)),
        ),
        out_shape=jax.ShapeDtypeStruct((rows, w), jnp.uint32),
        compiler_params=pltpu.CompilerParams(dimension_semantics=("arbitrary",), vmem_limit_bytes=VMEM_LIMIT),
        name="experts",
    )(tile_expert, n_tiles, xs, w_gate.astype(BF16), w_up.astype(BF16), w_down.astype(BF16))


def _moe(hp, rtt, cnt, w_gate, w_up, w_down, tm, tmg):
    n = hp.shape[0]
    ne = w_gate.shape[0]
    counts = cnt[:ne, 0].astype(jnp.int32)
    padded = (counts + tmg - 1) // tmg * tmg
    ends = jnp.cumsum(padded)
    starts = jnp.pad((ends - padded).astype(F32), (0, LANES - ne))
    pos = _expert_positions(rtt, jnp.broadcast_to(starts[:, None], (LANES, LANES)), tm)
    pos0, pos1 = pos[0], pos[1]
    rows = 2 * n + ne * tmg
    tile_start = jnp.arange(rows // tmg, dtype=jnp.int32) * tmg
    tile_expert = jnp.minimum(jnp.sum(tile_start[:, None] >= ends[None, :], axis=1), ne - 1).astype(jnp.int32)
    n_tiles = (ends[-1:] // tmg).astype(jnp.int32)
    xs = _sc_dispatch(hp, pos0, pos1, rows)
    ys = _experts(xs, tile_expert, n_tiles, w_gate, w_up, w_down, tmg)
    return _sc_collect(ys, pos0, pos1)


def _ple_kernel(x1_ref, y0_ref, y1_ref, rt_ref, p_ref, g_ref, wg_ref, wp_ref, o_ref):
    for r0 in range(0, x1_ref.shape[0], PLE_SUB):
        rows = pl.ds(r0, PLE_SUB)
        rt = rt_ref[rows, :]
        y0 = jnp.concatenate(_unpack_bf16_pairs(y0_ref[rows, :]), axis=1)
        y1 = jnp.concatenate(_unpack_bf16_pairs(y1_ref[rows, :]), axis=1)
        x2 = x1_ref[rows, :] + rt[:, 2:3] * y0 + rt[:, 3:4] * y1
        gate = jax.nn.sigmoid(_dot(_rms(x2, g_ref[...]).astype(BF16), wg_ref[...]))
        o_ref[rows, :] = x2 + gate * _dot(p_ref[rows, :].astype(BF16), wp_ref[...])


def _ple_kernel_into(prev_ref, *refs):
    del prev_ref
    _ple_kernel(*refs)


def _ple(x1, y0, y1, rt, p2, g_ple, w_gate, w_proj, tm, row0, out_prev):
    n, d = x1.shape
    n_all, dp = p2.shape
    blk0 = row0 // tm
    full = lambda shape: pl.BlockSpec(shape, lambda i: (0,) * len(shape))
    row = lambda w: pl.BlockSpec((tm, w), lambda i: (i, 0))
    row_all = lambda w: pl.BlockSpec((tm, w), lambda i: (i + blk0, 0))
    in_specs = [row(d), row(d // 2), row(d // 2), row(LANES), row_all(dp), full((1, d)), full((d, d)), full((dp, d))]
    args = (x1, y0, y1, rt, p2, g_ple[None, :].astype(F32), w_gate.astype(BF16), w_proj.astype(BF16))
    if out_prev is not None:
        in_specs = [pl.BlockSpec(memory_space=pl.ANY)] + in_specs
        args = (out_prev,) + args
    return pl.pallas_call(
        _ple_kernel if out_prev is None else _ple_kernel_into,
        grid=(n // tm,),
        in_specs=in_specs,
        out_specs=row_all(d),
        out_shape=jax.ShapeDtypeStruct((n_all, d), F32),
        input_output_aliases={} if out_prev is None else {0: 0},
        compiler_params=pltpu.CompilerParams(dimension_semantics=("parallel",), vmem_limit_bytes=VMEM_LIMIT),
        name="ple",
    )(*args)


def _layer(x, p_l, g_attn, w_qkv, q_norm_na, k_norm_na, rpb_na, q_norm_dil, k_norm_dil, g_out_na, g_out_dil,
           w_o, g_ffn, w_rg, b_rg, w_re, b_re, w_exp_gate, w_exp_up, w_exp_down, g_ple, w_ple_gate, w_ple_proj):
    b, s, d = x.shape
    n = b * s
    half = d // 2
    assert d == N_HEADS * HEAD_DIM and half == N_HEADS_NA * HEAD_DIM
    tm = 512
    assert s % tm == 0
    x2 = x.reshape(n, d)
    qa, ka, va, qb, kb, vb = _qkv_proj(x2, g_attn, w_qkv, q_norm_na, k_norm_na, q_norm_dil, k_norm_dil, s, tm)
    seq = lambda t: t.reshape(b, s, half)
    oa = _na_attention(seq(qa), seq(ka), seq(va), rpb_na).reshape(n, half)
    ob = _dil_attention(seq(qb), seq(kb), seq(vb)).reshape(n, half)
    p2 = p_l.reshape(n, -1)
    chunk_unit = SC_CORES * SC_SUBCORES * SC_CHUNK
    n_chunks = TOKEN_CHUNKS if n % (TOKEN_CHUNKS * chunk_unit) == 0 else 1
    chunk = n // n_chunks
    assert chunk % ROW_TILE == 0
    out = None
    for c in range(n_chunks):
        x1, h, rt, rtt, cnt = _out_router(oa, ob, x2, g_out_na, g_out_dil, w_o, g_ffn, w_rg, b_rg, w_re, b_re,
                                          ROW_TILE, c * chunk, chunk)
        y0, y1 = _moe(h, rtt, cnt, w_exp_gate, w_exp_up, w_exp_down, 2 * tm, tm)
        out = _ple(x1, y0, y1, rt, p2, g_ple, w_ple_gate, w_ple_proj, ROW_TILE, c * chunk, out)
    return out.reshape(b, s, d)


def kernel(x, p, g_attn, w_qkv, q_norm_na, k_norm_na, rpb_na, q_norm_dil, k_norm_dil, g_out_na, g_out_dil, w_o,
           g_ffn, w_router_group, b_router_group, w_router_expert, b_router_expert, w_exp_gate, w_exp_up,
           w_exp_down, g_ple, w_ple_gate, w_ple_proj):
    for i in range(p.shape[0]):
        x = _layer(x, p[i], g_attn[i], w_qkv[i], q_norm_na[i], k_norm_na[i], rpb_na[i], q_norm_dil[i],
                   k_norm_dil[i], g_out_na[i], g_out_dil[i], w_o[i], g_ffn[i], w_router_group[i],
                   b_router_group[i], w_router_expert[i], b_router_expert[i], w_exp_gate[i], w_exp_up[i],
                   w_exp_down[i], g_ple[i], w_ple_gate[i], w_ple_proj[i])
    return x
```

```python
import functools

import numpy as np
import jax
import jax.numpy as jnp
from jax import lax
from jax.experimental import pallas as pl
from jax.experimental.pallas import tpu as pltpu
from jax.experimental.pallas import tpu_sc as plsc

HEAD_DIM = 64
N_HEADS = 16
N_HEADS_NA = 8
GRID_W = 64
NA_ROWS = 8
NA_COLS = 16
DIL_PAIRS = ((128, 1), (512, 4), (2048, 16))
ROPE_THETA = 10000.0
N_GROUPS = 4
EXPERTS_PER_GROUP = 8
N_EXPERTS = N_GROUPS * EXPERTS_PER_GROUP
EPS = 1e-6
NEG = -1e30
LOG2_E = 1.4426950408889634

LANES = 128
PAIR_W = 2 * HEAD_DIM
DIL_BLK = 128
ATTN_STEPS_PER_TRIP = 64
ATTN_SLOTS = 4
ROUTER_SUB = 512
PLE_SUB = 256
ROW_TILE = 1024
VMEM_LIMIT = 56 * 1024 * 1024
SC_CORES = 2
SC_SUBCORES = 16
SC_CHUNK = 128
TOKEN_CHUNKS = 2

F32 = jnp.float32
BF16 = jnp.bfloat16


def _dot(a, b):
    return jnp.dot(a, b, preferred_element_type=F32)


def _dot_nt(a, b):
    return lax.dot_general(a, b, (((1,), (1,)), ((), ())), preferred_element_type=F32)


def _rms(x, gain):
    return x * lax.rsqrt(jnp.mean(x * x, axis=-1, keepdims=True) + EPS) * gain


def _lane_first_half(shape):
    return lax.broadcasted_iota(jnp.int32, shape, len(shape) - 1) < HEAD_DIM


def _qkv_kernel(x_ref, g_ref, w_ref, gq_na_ref, gk_na_ref, gq_dil_ref, gk_dil_ref, cos_ref, sin_ref,
                hsum_ref, qa_ref, ka_ref, va_ref, qb_ref, kb_ref, vb_ref, wb_ref):
    d = x_ref.shape[1]
    half = d // 2
    scale = HEAD_DIM ** -0.5 * LOG2_E

    @pl.when(pl.program_id(0) == 0)
    def _():
        wb_ref[...] = w_ref[...].astype(BF16)

    h = _rms(x_ref[...], g_ref[...]).astype(BF16)

    def proj(col):
        return _dot(h, wb_ref[:, col:col + half])

    def head_norm(y, gain):
        sq = (y * y).astype(BF16)
        w = hsum_ref.shape[0]
        ms = jnp.concatenate([_dot(sq[:, c:c + w], hsum_ref[...]) for c in range(0, half, w)], axis=1)
        return y * lax.rsqrt(ms + EPS) * gain

    def rope(y):
        lane = lax.broadcasted_iota(jnp.int32, (y.shape[0], LANES), 1)
        lower = (lane % HEAD_DIM) < HEAD_DIM // 2
        cos = cos_ref[...]
        sin = sin_ref[...]
        outs = []
        for c in range(0, half, LANES):
            yc = y[:, c:c + LANES]
            up = pltpu.roll(yc, LANES - HEAD_DIM // 2, axis=1)
            down = pltpu.roll(yc, HEAD_DIM // 2, axis=1)
            outs.append(yc * cos + jnp.where(lower, up, down) * sin)
        return jnp.concatenate(outs, axis=1)

    qa_ref[...] = (head_norm(proj(0), gq_na_ref[...]) * scale).astype(BF16)
    qb_ref[...] = (rope(head_norm(proj(half), gq_dil_ref[...])) * scale).astype(BF16)
    ka_ref[...] = head_norm(proj(d), gk_na_ref[...]).astype(BF16)
    kb_ref[...] = rope(head_norm(proj(d + half), gk_dil_ref[...])).astype(BF16)
    va_ref[...] = proj(2 * d).astype(BF16)
    vb_ref[...] = proj(2 * d + half).astype(BF16)


def _qkv_proj(x2, g_attn, w_qkv, gq_na, gk_na, gq_dil, gk_dil, seq, tm):
    n, d = x2.shape
    half = d // 2
    pos = jnp.arange(seq, dtype=F32)
    inv = ROPE_THETA ** (-jnp.arange(HEAD_DIM // 2, dtype=F32) / (HEAD_DIM // 2))
    ang = pos[:, None] * inv[None, :]
    cos = jnp.tile(jnp.cos(ang), (1, LANES // (HEAD_DIM // 2)))
    sin = jnp.tile(jnp.concatenate([-jnp.sin(ang), jnp.sin(ang)], axis=1), (1, LANES // HEAD_DIM))
    hs_w = 2 * LANES
    blk = np.arange(hs_w) // HEAD_DIM
    hsum = jnp.asarray((blk[:, None] == blk[None, :]).astype(np.float32) / HEAD_DIM, BF16)
    tile_gain = lambda g: jnp.tile(g.astype(F32), half // HEAD_DIM)[None, :]
    steps_per_seq = seq // tm
    full = lambda shape: pl.BlockSpec(shape, lambda i: (0,) * len(shape))
    out = jax.ShapeDtypeStruct((n, half), BF16)
    return pl.pallas_call(
        _qkv_kernel,
        grid=(n // tm,),
        in_specs=[
            pl.BlockSpec((tm, d), lambda i: (i, 0)),
            full((1, d)),
            pl.BlockSpec((d, 3 * d), lambda i: (0, 0), pipeline_mode=pl.Buffered(1)),
            full((1, half)), full((1, half)), full((1, half)), full((1, half)),
            pl.BlockSpec((tm, LANES), lambda i: (i % steps_per_seq, 0)),
            pl.BlockSpec((tm, LANES), lambda i: (i % steps_per_seq, 0)),
            full((hs_w, hs_w)),
        ],
        out_specs=[pl.BlockSpec((tm, half), lambda i: (i, 0))] * 6,
        out_shape=[out] * 6,
        scratch_shapes=[pltpu.VMEM((d, 3 * d), BF16)],
        compiler_params=pltpu.CompilerParams(dimension_semantics=("arbitrary",), vmem_limit_bytes=VMEM_LIMIT),
        name="qkv_proj",
    )(x2, g_attn[None, :].astype(F32), w_qkv, tile_gain(gq_na), tile_gain(gk_na),
      tile_gain(gq_dil), tile_gain(gk_dil), cos, sin, hsum)


def _is_static(x):
    return isinstance(x, int)


def _clip(x, lo, hi):
    return min(max(x, lo), hi) if _is_static(x) else jnp.clip(x, lo, hi)


def _aligned_ds(start, size, align):
    return pl.ds(start if _is_static(start) else pl.multiple_of(start, align), size)


def _software_pipeline(n_items, stages, steps_per_trip):
    depth = len(stages)
    assert steps_per_trip % ATTN_SLOTS == 0

    def step(t, phase, static):
        for k in reversed(range(depth)):
            if static and not 0 <= t - k < n_items:
                continue
            stages[k](t - k, (phase - k) % ATTN_SLOTS)

    first_full = -(-(depth - 1) // ATTN_SLOTS) * ATTN_SLOTS
    trips = max(n_items - first_full, 0) // steps_per_trip
    if trips < 2:
        trips = 0
    looped_end = first_full + trips * steps_per_trip if trips else 0
    for t in range(first_full if trips else 0):
        step(t, t % ATTN_SLOTS, True)
    if trips:
        def body(i, carry):
            for j in range(steps_per_trip):
                step(first_full + i * steps_per_trip + j, j % ATTN_SLOTS, False)
            return carry

        lax.fori_loop(0, trips, body, 0)
    for t in range(looped_end, n_items + depth - 1):
        step(t, t % ATTN_SLOTS, True)


def _qk_stage(q, kwin, bias_a, bias_b, s_ref, slot, first):
    m = q.shape[0]
    w = kwin.shape[0]
    zero = jnp.zeros_like(q)
    s = _dot_nt(jnp.concatenate([jnp.where(first, q, zero), jnp.where(first, zero, q)], axis=0), kwin)
    s_ref[slot, :m, :w] = s[:m] + bias_a
    s_ref[slot, m:, :w] = s[m:] + bias_b


def _softmax_stage(s_ref, p_ref, slot, w):
    m = jnp.max(s_ref[slot, :, :w], axis=-1, keepdims=True)
    p_ref[slot, :, :w] = jnp.exp2(s_ref[slot, :, :w] - m).astype(BF16)
    return m


def _pv_stage(p_ref, slot, v_win, first):
    w = v_win.shape[0]
    r = _dot(p_ref[slot, :, :w], v_win)
    m = r.shape[0] // 2
    return jnp.where(first, r[:m, :PAIR_W], r[m:, :PAIR_W]), jnp.where(first, r[:m, PAIR_W:], r[m:, PAIR_W:])


def _with_ones(v):
    return jnp.concatenate([v, jnp.ones_like(v)], axis=1)


def _na_kernel(q_ref, k_ref, v_ref, bias_ref, o_ref, v1_ref, s_ref, p_ref):
    rows = q_ref.shape[0] // GRID_W
    win = NA_ROWS * GRID_W
    v1_ref[...] = _with_ones(v_ref[...])
    first = _lane_first_half((GRID_W, PAIR_W))

    def slices(r):
        rs = _clip(r - NA_ROWS // 2, 0, rows - NA_ROWS)
        return _aligned_ds(r * GRID_W, GRID_W, GRID_W), _aligned_ds(rs * GRID_W, win, GRID_W), r - rs

    def bias(head, delta):
        return jnp.concatenate([bias_ref[head, j - delta + NA_ROWS - 1] for j in range(0, NA_ROWS, 2)], axis=1)

    def qk(r, slot):
        qs, ks, delta = slices(r)
        _qk_stage(q_ref[qs, :], k_ref[ks, :], bias(0, delta), bias(1, delta), s_ref, slot, first)

    def softmax(r, slot):
        _softmax_stage(s_ref, p_ref, slot, win)

    def pv(r, slot):
        qs, ks, _ = slices(r)
        num, den = _pv_stage(p_ref, slot, v1_ref[ks, :], first)
        o_ref[qs, :] = num / den

    _software_pipeline(rows, (qk, softmax, pv), ATTN_STEPS_PER_TRIP)


def _na_bias_table(rpb):
    w = np.arange(GRID_W)
    cs = np.clip(w - NA_COLS // 2, 0, GRID_W - NA_COLS)
    kc = np.arange(GRID_W)
    valid = (kc[None, :] >= cs[:, None]) & (kc[None, :] < cs[:, None] + NA_COLS)
    coff = np.clip(kc[None, :] - w[:, None] + NA_COLS - 1, 0, 2 * NA_COLS - 2)
    tab = jnp.where(valid[None, None], rpb.astype(F32)[:, :, coff] * LOG2_E, NEG)
    return jnp.concatenate([tab[:, :-1], tab[:, 1:]], axis=-1)


def _na_attention(q, k, v, rpb):
    b, s, width = q.shape
    pairs = width // PAIR_W
    assert s % GRID_W == 0 and s // GRID_W >= NA_ROWS
    bias = _na_bias_table(rpb).reshape(pairs, 2, 2 * NA_ROWS - 2, GRID_W, 2 * GRID_W)
    qkv_spec = pl.BlockSpec((None, s, PAIR_W), lambda bi, j: (bi, 0, j))
    return pl.pallas_call(
        _na_kernel,
        grid=(b, pairs),
        in_specs=[qkv_spec, qkv_spec, qkv_spec,
                  pl.BlockSpec((None, 2, 2 * NA_ROWS - 2, GRID_W, 2 * GRID_W), lambda bi, j: (j, 0, 0, 0, 0))],
        out_specs=pl.BlockSpec((None, s, PAIR_W), lambda bi, j: (bi, 0, j)),
        out_shape=jax.ShapeDtypeStruct((b, s, width), F32),
        scratch_shapes=[pltpu.VMEM((s, 2 * PAIR_W), BF16),
                        pltpu.VMEM((ATTN_SLOTS, 2 * GRID_W, NA_ROWS * GRID_W), F32),
                        pltpu.VMEM((ATTN_SLOTS, 2 * GRID_W, NA_ROWS * GRID_W), BF16)],
        compiler_params=pltpu.CompilerParams(dimension_semantics=("parallel", "parallel"),
                                             vmem_limit_bytes=VMEM_LIMIT),
        name="na_attn",
    )(q, k, v, bias)


def _dil_kernel(q_ref, k_ref, v_ref, mwide_ref, mfull_ref, o_ref,
                f32a_ref, f32b_ref, qc_ref, kc_ref, vc_ref, acc_ref, den_ref, max_ref, s_ref, p_ref):
    s = q_ref.shape[0]
    for i, ref in enumerate((q_ref, k_ref, v_ref)):
        f32a_ref[i] = ref[...].astype(F32)
    first = _lane_first_half((DIL_BLK, PAIR_W))
    prev_ref, next_ref, prev_dil = f32a_ref, f32b_ref, 1

    for p, (window, dil) in enumerate(DIL_PAIRS):
        radius = window // (2 * dil)
        cls_len = s // dil
        nblk = cls_len // DIL_BLK
        wide = cls_len >= 2 * DIL_BLK
        win = 2 * DIL_BLK if wide else cls_len
        assert radius == DIL_BLK // 2 and cls_len % DIL_BLK == 0 and dil % prev_dil == 0

        ratio = dil // prev_dil
        keep = ratio > 1 and p + 1 < len(DIL_PAIRS)
        for c in range(dil):
            src = pl.ds((c % prev_dil) * (s // prev_dil) + c // prev_dil, cls_len, stride=ratio)
            dst = pl.ds(c * cls_len, cls_len)
            vals = [prev_ref[i, src, :] for i in range(3)]
            if keep:
                for i in range(3):
                    next_ref[i, dst, :] = vals[i]
            qc_ref[dst, :] = vals[0].astype(BF16)
            kc_ref[dst, :] = vals[1].astype(BF16)
            vc_ref[dst, :] = _with_ones(vals[2].astype(BF16))
        if keep:
            prev_ref, next_ref, prev_dil = next_ref, prev_ref, dil

        def slices(n):
            c, i = divmod(n, nblk) if _is_static(n) else (n // nblk, n % nblk)
            base = c * cls_len
            ws = _clip(i * DIL_BLK - radius, 0, cls_len - win)
            if _is_static(i):
                kind = 0 if i == 0 else (2 if i == nblk - 1 else 1)
            else:
                kind = jnp.where(i == 0, 0, jnp.where(i == nblk - 1, 2, 1))
            return (_aligned_ds(base + i * DIL_BLK, DIL_BLK, DIL_BLK), _aligned_ds(base + ws, win, radius), kind,
                    pl.ds(c + dil * DIL_BLK * i, DIL_BLK, stride=dil))

        def qk(n, slot):
            qs, ks, kind, _ = slices(n)
            mask = mwide_ref[kind] if wide else mfull_ref[...]
            _qk_stage(qc_ref[qs, :], kc_ref[ks, :], mask, mask, s_ref, slot, first)

        def softmax(n, slot):
            m = _softmax_stage(s_ref, p_ref, slot, win)
            max_ref[p, slices(n)[3], :] = jnp.where(first, m[:DIL_BLK], m[DIL_BLK:])

        def pv(n, slot):
            _, ks, _, tok = slices(n)
            num, den = _pv_stage(p_ref, slot, vc_ref[ks, :], first)
            acc_ref[p, tok, :] = num
            den_ref[p, tok, :] = den

        _software_pipeline(dil * nblk, (qk, softmax, pv), ATTN_STEPS_PER_TRIP)

    m = jnp.maximum(jnp.maximum(max_ref[0], max_ref[1]), max_ref[2])
    num = jnp.zeros_like(m)
    den = jnp.zeros_like(m)
    for p in range(len(DIL_PAIRS)):
        w = jnp.exp2(max_ref[p] - m)
        num = num + w * acc_ref[p]
        den = den + w * den_ref[p]
    o_ref[...] = num / den


def _band_mask(kind):
    radius = DIL_BLK // 2
    qq = np.arange(DIL_BLK)[:, None]
    if kind == "full":
        kk = np.arange(DIL_BLK)[None, :]
        shift = 0
    else:
        kk = np.arange(2 * DIL_BLK)[None, :]
        shift = {"first": 0, "inner": radius, "last": DIL_BLK}[kind]
    return np.where(np.abs(kk - qq - shift) <= radius, 0.0, NEG).astype(np.float32)


def _dil_attention(q, k, v):
    b, s, width = q.shape
    pairs = width // PAIR_W
    for window, dil in DIL_PAIRS:
        assert s % (window // 2) == 0 and (s // dil) % DIL_BLK == 0
    mwide = jnp.asarray(np.stack([_band_mask("first"), _band_mask("inner"), _band_mask("last")]))
    mfull = jnp.asarray(_band_mask("full"))
    qkv_spec = pl.BlockSpec((None, s, PAIR_W), lambda bi, j: (bi, 0, j))
    npat = len(DIL_PAIRS)
    return pl.pallas_call(
        _dil_kernel,
        grid=(b, pairs),
        in_specs=[qkv_spec, qkv_spec, qkv_spec,
                  pl.BlockSpec(mwide.shape, lambda bi, j: (0, 0, 0)),
                  pl.BlockSpec(mfull.shape, lambda bi, j: (0, 0))],
        out_specs=pl.BlockSpec((None, s, PAIR_W), lambda bi, j: (bi, 0, j)),
        out_shape=jax.ShapeDtypeStruct((b, s, width), F32),
        scratch_shapes=[pltpu.VMEM((3, s, PAIR_W), F32)] * 2 + [pltpu.VMEM((s, PAIR_W), BF16)] * 2
        + [pltpu.VMEM((s, 2 * PAIR_W), BF16)] + [pltpu.VMEM((npat, s, PAIR_W), F32)] * 3
        + [pltpu.VMEM((ATTN_SLOTS, 2 * DIL_BLK, 2 * DIL_BLK), F32),
           pltpu.VMEM((ATTN_SLOTS, 2 * DIL_BLK, 2 * DIL_BLK), BF16)],
        compiler_params=pltpu.CompilerParams(dimension_semantics=("parallel", "parallel"),
                                             vmem_limit_bytes=VMEM_LIMIT),
        name="dil_attn",
    )(q, k, v, mwide, mfull)


def _split_bf16(x):
    hi = x.astype(BF16)
    return hi, (x - hi.astype(F32)).astype(BF16)


def _route(logits):
    lane = lax.broadcasted_iota(jnp.int32, logits.shape, 1)
    ninf = jnp.float32(-jnp.inf)

    def first_argmax(vals, vmax):
        return jnp.min(jnp.where(vals == vmax, lane, LANES), axis=-1, keepdims=True)

    gl = jnp.where(lane < N_GROUPS, logits, ninf)
    gmax = jnp.max(gl, axis=-1, keepdims=True)
    gsel = first_argmax(gl, gmax)
    gw = 1.0 / jnp.sum(jnp.exp(gl - gmax), axis=-1, keepdims=True)
    lo = N_GROUPS + EXPERTS_PER_GROUP * gsel
    el = jnp.where((lane >= lo) & (lane < lo + EXPERTS_PER_GROUP), logits, ninf)
    v0 = jnp.max(el, axis=-1, keepdims=True)
    i0 = first_argmax(el, v0)
    el = jnp.where(lane == i0, ninf, el)
    v1 = jnp.max(el, axis=-1, keepdims=True)
    i1 = first_argmax(el, v1)
    t = jnp.exp(v1 - v0)
    w0 = gw / (1.0 + t)
    w1 = gw * t / (1.0 + t)
    e0 = (i0 - N_GROUPS).astype(F32)
    e1 = (i1 - N_GROUPS).astype(F32)
    return jnp.where(lane == 0, e0, jnp.where(lane == 1, e1, jnp.where(lane == 2, w0, jnp.where(lane == 3, w1, 0.0))))


def _out_router_kernel(oa_ref, ob_ref, x_ref, ga_ref, gb_ref, wo_ref, gf_ref, wr_ref, br_ref,
                       x1_ref, h_ref, rt_ref, rtt_ref, cnt_ref):
    half = oa_ref.shape[1]

    @pl.when(pl.program_id(0) == 0)
    def _():
        cnt_ref[...] = jnp.zeros_like(cnt_ref)

    for r0 in range(0, oa_ref.shape[0], ROUTER_SUB):
        rows = pl.ds(r0, ROUTER_SUB)
        ya = _rms(oa_ref[rows, :], ga_ref[...]).astype(BF16)
        yb = _rms(ob_ref[rows, :], gb_ref[...]).astype(BF16)
        x1 = x_ref[rows, :] + _dot(ya, wo_ref[:half, :]) + _dot(yb, wo_ref[half:, :])
        x1_ref[rows, :] = x1
        h = _rms(x1, gf_ref[...])
        h_ref[rows, :] = _pack_bf16_pairs(h)
        h_hi, h_lo = _split_bf16(h)
        both = _dot(h_hi, wr_ref[...])
        logits = both[:, :LANES] + both[:, LANES:] + _dot(h_lo, wr_ref[:, :LANES]) + br_ref[...]
        rt = _route(logits)
        rt_ref[rows, :] = rt
        rtt = rt.T[:rtt_ref.shape[0], :]
        rtt_ref[:, rows] = rtt
        oh0, oh1 = _slot_one_hots(rtt)
        cnt_ref[...] += jnp.sum(oh0 + oh1, axis=1, keepdims=True)


def _out_router(oa, ob, x2, g_na, g_dil, w_o, g_ffn, w_rg, b_rg, w_re, b_re, tm, row0, n):
    d = x2.shape[1]
    half = d // 2
    blk0 = row0 // tm
    wr = jnp.concatenate([w_rg.astype(F32), w_re.astype(F32).transpose(1, 0, 2).reshape(d, N_EXPERTS)], axis=1)
    wr = jnp.pad(wr, ((0, 0), (0, LANES - wr.shape[1])))
    wr_hi = wr.astype(BF16)
    wr_cat = jnp.concatenate([wr_hi, (wr - wr_hi.astype(F32)).astype(BF16)], axis=1)
    br =jnp.pad(jnp.concatenate([b_rg.astype(F32), b_re.astype(F32).reshape(-1)]), (0, LANES - N_GROUPS - N_EXPERTS))
    full = lambda shape: pl.BlockSpec(shape, lambda i: (0,) * len(shape))
    row = lambda w: pl.BlockSpec((tm, w), lambda i: (i, 0))
    row_in = lambda w: pl.BlockSpec((tm, w), lambda i: (i + blk0, 0))
    return pl.pallas_call(
        _out_router_kernel,
        grid=(n // tm,),
        in_specs=[row_in(half), row_in(half), row_in(d), full((1, half)), full((1, half)), full((d, d)),
                  full((1, d)), full((d, 2 * LANES)), full((1, LANES))],
        out_specs=[row(d), row(half), row(LANES), pl.BlockSpec((8, tm), lambda i: (0, i)), full((LANES, LANES))],
        out_shape=[jax.ShapeDtypeStruct((n, d), F32), jax.ShapeDtypeStruct((n, half), jnp.uint32),
                   jax.ShapeDtypeStruct((n, LANES), F32), jax.ShapeDtypeStruct((8, n), F32),
                   jax.ShapeDtypeStruct((LANES, LANES), F32)],
        compiler_params=pltpu.CompilerParams(dimension_semantics=("arbitrary",), vmem_limit_bytes=VMEM_LIMIT),
        name="out_router",
    )(oa, ob, x2, g_na[None, :].astype(F32), g_dil[None, :].astype(F32), w_o.astype(BF16),
      g_ffn[None, :].astype(F32), wr_cat, br[None, :])


def _pack_bf16_pairs(x):
    w = x.shape[1] // 2
    bits = lax.bitcast_convert_type(x.astype(BF16).astype(F32), jnp.uint32)
    return bits[:, :w] | (bits[:, w:] >> 16)


def _unpack_bf16_pairs(u):
    hi = lax.bitcast_convert_type(u & jnp.uint32(0xFFFF0000), F32)
    lo = lax.bitcast_convert_type(u << 16, F32)
    return hi, lo


def _slot_one_hots(rtt):
    sub = lax.broadcasted_iota(jnp.int32, (LANES, rtt.shape[1]), 0).astype(F32)
    return (sub == rtt[0:1, :]).astype(F32), (sub == rtt[1:2, :]).astype(F32)


def _position_kernel(rtt_ref, start_ref, pos_ref, base_ref):
    tm = rtt_ref.shape[1]

    @pl.when(pl.program_id(0) == 0)
    def _():
        base_ref[...] = start_ref[...]

    oh0, oh1 = _slot_one_hots(rtt_ref[...])
    oh = oh0 + oh1
    earlier = lax.broadcasted_iota(jnp.int32, (tm, tm), 0) < lax.broadcasted_iota(jnp.int32, (tm, tm), 1)
    before = _dot(oh.astype(BF16), earlier.astype(BF16)) + base_ref[:, 0:1]
    p0 = jnp.sum(before * oh0, axis=0, keepdims=True)
    p1 = jnp.sum(before * oh1, axis=0, keepdims=True)
    row = lax.broadcasted_iota(jnp.int32, pos_ref.shape, 0)
    pos_ref[...] = jnp.where(row == 0, p0, jnp.where(row == 1, p1, 0.0)).astype(jnp.int32)
    base_ref[...] += jnp.sum(oh, axis=1, keepdims=True)


def _expert_positions(rtt, starts, tm):
    n = rtt.shape[1]
    return pl.pallas_call(
        _position_kernel,
        grid=(n // tm,),
        in_specs=[pl.BlockSpec((8, tm), lambda i: (0, i)), pl.BlockSpec((LANES, LANES), lambda i: (0, 0))],
        out_specs=pl.BlockSpec((8, tm), lambda i: (0, i)),
        out_shape=jax.ShapeDtypeStruct((8, n), jnp.int32),
        scratch_shapes=[pltpu.VMEM((LANES, LANES), F32)],
        compiler_params=pltpu.CompilerParams(dimension_semantics=("arbitrary",)),
        name="expert_positions",
    )(rtt, starts)


def _sc_mesh():
    return plsc.VectorSubcoreMesh(core_axis_name="c", subcore_axis_name="s",
                                  num_cores=SC_CORES, num_subcores=SC_SUBCORES)


def _sc_dispatch(hp, pos0, pos1, n_out):
    n, w = hp.shape
    workers = SC_CORES * SC_SUBCORES
    per = n // workers
    chunks = per // SC_CHUNK
    assert n % (workers * SC_CHUNK) == 0

    @functools.partial(
        pl.kernel, out_type=jax.ShapeDtypeStruct((n_out, w), hp.dtype), mesh=_sc_mesh(),
        scratch_types=[pltpu.VMEM((chunks, SC_CHUNK), jnp.int32), pltpu.VMEM((chunks, SC_CHUNK), jnp.int32),
                       pltpu.VMEM((SC_CHUNK, w), hp.dtype)],
        name="moe_dispatch")
    def body(h_hbm, p0_hbm, p1_hbm, xs_hbm, i0_v, i1_v, rows_v):
        wid = lax.axis_index("s") * SC_CORES + lax.axis_index("c")
        pltpu.sync_copy(p0_hbm.at[wid], i0_v)
        pltpu.sync_copy(p1_hbm.at[wid], i1_v)

        @pl.loop(0, chunks)
        def _(j):
            pltpu.sync_copy(h_hbm.at[pl.ds(wid * per + j * SC_CHUNK, SC_CHUNK)], rows_v)
            pltpu.sync_copy(rows_v, xs_hbm.at[i0_v.at[j]])
            pltpu.sync_copy(rows_v, xs_hbm.at[i1_v.at[j]])

    return body(hp, pos0.reshape(workers, chunks, SC_CHUNK), pos1.reshape(workers, chunks, SC_CHUNK))


def _sc_collect(ys, pos0, pos1):
    n = pos0.shape[0]
    w = ys.shape[1]
    workers = SC_CORES * SC_SUBCORES
    per = n // workers
    chunks = per // SC_CHUNK
    out = jax.ShapeDtypeStruct((n, w), ys.dtype)

    @functools.partial(
        pl.kernel, out_type=(out, out), mesh=_sc_mesh(),
        scratch_types=[pltpu.VMEM((chunks, SC_CHUNK), jnp.int32), pltpu.VMEM((chunks, SC_CHUNK), jnp.int32),
                       pltpu.VMEM((SC_CHUNK, w), ys.dtype)],
        name="moe_collect")
    def body(ys_hbm, p0_hbm, p1_hbm, y0_hbm, y1_hbm, i0_v, i1_v, rows_v):
        wid = lax.axis_index("s") * SC_CORES + lax.axis_index("c")
        pltpu.sync_copy(p0_hbm.at[wid], i0_v)
        pltpu.sync_copy(p1_hbm.at[wid], i1_v)

        @pl.loop(0, chunks)
        def _(j):
            dst = pl.ds(wid * per + j * SC_CHUNK, SC_CHUNK)
            pltpu.sync_copy(ys_hbm.at[i0_v.at[j]], rows_v)
            pltpu.sync_copy(rows_v, y0_hbm.at[dst])
            pltpu.sync_copy(ys_hbm.at[i1_v.at[j]], rows_v)
            pltpu.sync_copy(rows_v, y1_hbm.at[dst])

    return body(ys, pos0.reshape(workers, chunks, SC_CHUNK), pos1.reshape(workers, chunks, SC_CHUNK))


def _experts_kernel(te_ref, nt_ref, xs_ref, wg_ref, wu_ref, wd_ref, ys_ref, wg_bf, wu_bf, wd_bf):
    t = pl.program_id(0)
    live = t < nt_ref[0]

    @pl.when(live & ((t == 0) | (te_ref[t] != te_ref[jnp.maximum(t - 1, 0)])))
    def _():
        wg_bf[...] = wg_ref[...].astype(BF16)
        wu_bf[...] = wu_ref[...].astype(BF16)
        wd_bf[...] = wd_ref[...].astype(BF16)

    @pl.when(live)
    def _():
        half = wg_bf.shape[0] // 2
        hi, lo = _unpack_bf16_pairs(xs_ref[...])
        hi = hi.astype(BF16)
        lo = lo.astype(BF16)
        a = _dot(hi, wg_bf[:half, :]) + _dot(lo, wg_bf[half:, :])
        u = _dot(hi, wu_bf[:half, :]) + _dot(lo, wu_bf[half:, :])
        act = (a * jax.nn.sigmoid(a) * u).astype(BF16)
        ys_ref[...] = _pack_bf16_pairs(_dot(act, wd_bf[...]))


def _experts(xs, tile_expert, n_tiles, w_gate, w_up, w_down, tmg):
    rows, w = xs.shape
    ne, d, de = w_gate.shape
    live = lambda t, nt: jnp.minimum(t, nt[0] - 1)
    return pl.pallas_call(
        _experts_kernel,
        grid_spec=pltpu.PrefetchScalarGridSpec(
            num_scalar_prefetch=2,
            grid=(rows // tmg,),
            in_specs=[pl.BlockSpec((tmg, w), lambda t, te, nt: (live(t, nt), 0)),
                      pl.BlockSpec((None, d, de), lambda t, te, nt: (te[live(t, nt)], 0, 0)),
                      pl.BlockSpec((None, d, de), lambda t, te, nt: (te[live(t, nt)], 0, 0)),
                      pl.BlockSpec((None, de, d), lambda t, te, nt: (te[live(t, nt)], 0, 0))],
            out_specs=pl.BlockSpec((tmg, w), lambda t, te, nt: (live(t, nt), 0)),
            scratch_shapes=[pltpu.VMEM((d, de), BF16), pltpu.VMEM((d, de), BF16), pltpu.VMEM((de, d), BF16)],
        ),
        out_shape=jax.ShapeDtypeStruct((rows, w), jnp.uint32),
        compiler_params=pltpu.CompilerParams(dimension_semantics=("arbitrary",), vmem_limit_bytes=VMEM_LIMIT),
        name="experts",
    )(tile_expert, n_tiles, xs, w_gate, w_up, w_down)


def _moe(hp, rtt, cnt, w_gate, w_up, w_down, tm, tmg):
    n = hp.shape[0]
    ne = w_gate.shape[0]
    counts = cnt[:ne, 0].astype(jnp.int32)
    padded = (counts + tmg - 1) // tmg * tmg
    ends = jnp.cumsum(padded)
    starts = jnp.pad((ends - padded).astype(F32), (0, LANES - ne))
    pos = _expert_positions(rtt, jnp.broadcast_to(starts[:, None], (LANES, LANES)), tm)
    pos0, pos1 = pos[0], pos[1]
    rows = 2 * n + ne * tmg
    tile_start = jnp.arange(rows // tmg, dtype=jnp.int32) * tmg
    tile_expert = jnp.minimum(jnp.sum(tile_start[:, None] >= ends[None, :], axis=1), ne - 1).astype(jnp.int32)
    n_tiles = (ends[-1:] // tmg).astype(jnp.int32)
    xs = _sc_dispatch(hp, pos0, pos1, rows)
    ys = _experts(xs, tile_expert, n_tiles, w_gate, w_up, w_down, tmg)
    return _sc_collect(ys, pos0, pos1)


def _ple_kernel(x1_ref, y0_ref, y1_ref, rt_ref, p_ref, g_ref, wg_ref, wp_ref, o_ref):
    for r0 in range(0, x1_ref.shape[0], PLE_SUB):
        rows = pl.ds(r0, PLE_SUB)
        rt = rt_ref[rows, :]
        y0 = jnp.concatenate(_unpack_bf16_pairs(y0_ref[rows, :]), axis=1)
        y1 = jnp.concatenate(_unpack_bf16_pairs(y1_ref[rows, :]), axis=1)
        x2 = x1_ref[rows, :] + rt[:, 2:3] * y0 + rt[:, 3:4] * y1
        gate = jax.nn.sigmoid(_dot(_rms(x2, g_ref[...]).astype(BF16), wg_ref[...]))
        o_ref[rows, :] = x2 + gate * _dot(p_ref[rows, :].astype(BF16), wp_ref[...])


def _ple_kernel_into(prev_ref, *refs):
    del prev_ref
    _ple_kernel(*refs)


def _ple(x1, y0, y1, rt, p2, g_ple, w_gate, w_proj, tm, row0, out_prev):
    n, d = x1.shape
    n_all, dp = p2.shape
    blk0 = row0 // tm
    full = lambda shape: pl.BlockSpec(shape, lambda i: (0,) * len(shape))
    row = lambda w: pl.BlockSpec((tm, w), lambda i: (i, 0))
    row_all = lambda w: pl.BlockSpec((tm, w), lambda i: (i + blk0, 0))
    in_specs = [row(d), row(d // 2), row(d // 2), row(LANES), row_all(dp), full((1, d)), full((d, d)), full((dp, d))]
    args = (x1, y0, y1, rt, p2, g_ple[None, :].astype(F32), w_gate.astype(BF16), w_proj.astype(BF16))
    if out_prev is not None:
        in_specs = [pl.BlockSpec(memory_space=pl.ANY)] + in_specs
        args = (out_prev,) + args
    return pl.pallas_call(
        _ple_kernel if out_prev is None else _ple_kernel_into,
        grid=(n // tm,),
        in_specs=in_specs,
        out_specs=row_all(d),
        out_shape=jax.ShapeDtypeStruct((n_all, d), F32),
        input_output_aliases={} if out_prev is None else {0: 0},
        compiler_params=pltpu.CompilerParams(dimension_semantics=("parallel",), vmem_limit_bytes=VMEM_LIMIT),
        name="ple",
    )(*args)


def _layer(x, p_l, g_attn, w_qkv, q_norm_na, k_norm_na, rpb_na, q_norm_dil, k_norm_dil, g_out_na, g_out_dil,
           w_o, g_ffn, w_rg, b_rg, w_re, b_re, w_exp_gate, w_exp_up, w_exp_down, g_ple, w_ple_gate, w_ple_proj):
    b, s, d = x.shape
    n = b * s
    half = d // 2
    assert d == N_HEADS * HEAD_DIM and half == N_HEADS_NA * HEAD_DIM
    tm = 512
    assert s % tm == 0
    x2 = x.reshape(n, d)
    qa, ka, va, qb, kb, vb = _qkv_proj(x2, g_attn, w_qkv, q_norm_na, k_norm_na, q_norm_dil, k_norm_dil, s, tm)
    seq = lambda t: t.reshape(b, s, half)
    oa = _na_attention(seq(qa), seq(ka), seq(va), rpb_na).reshape(n, half)
    ob = _dil_attention(seq(qb), seq(kb), seq(vb)).reshape(n, half)
    p2 = p_l.reshape(n, -1)
    chunk_unit = SC_CORES * SC_SUBCORES * SC_CHUNK
    n_chunks = TOKEN_CHUNKS if n % (TOKEN_CHUNKS * chunk_unit) == 0 else 1
    chunk = n // n_chunks
    assert chunk % ROW_TILE == 0
    out = None
    for c in range(n_chunks):
        x1, h, rt, rtt, cnt = _out_router(oa, ob, x2, g_out_na, g_out_dil, w_o, g_ffn, w_rg, b_rg, w_re, b_re,
                                          ROW_TILE, c * chunk, chunk)
        y0, y1 = _moe(h, rtt, cnt, w_exp_gate, w_exp_up, w_exp_down, 2 * tm, tm)
        out = _ple(x1, y0, y1, rt, p2, g_ple, w_ple_gate, w_ple_proj, ROW_TILE, c * chunk, out)
    return out.reshape(b, s, d)


def kernel(x, p, g_attn, w_qkv, q_norm_na, k_norm_na, rpb_na, q_norm_dil, k_norm_dil, g_out_na, g_out_dil, w_o,
           g_ffn, w_router_group, b_router_group, w_router_expert, b_router_expert, w_exp_gate, w_exp_up,
           w_exp_down, g_ple, w_ple_gate, w_ple_proj):
    for i in range(p.shape[0]):
        x = _layer(x, p[i], g_attn[i], w_qkv[i], q_norm_na[i], k_norm_na[i], rpb_na[i], q_norm_dil[i],
                   k_norm_dil[i], g_out_na[i], g_out_dil[i], w_o[i], g_ffn[i], w_router_group[i],
                   b_router_group[i], w_router_expert[i], b_router_expert[i], w_exp_gate[i], w_exp_up[i],
                   w_exp_down[i], g_ple[i], w_ple_gate[i], w_ple_proj[i])
    return x
```

```python
import functools

import numpy as np
import jax
import jax.numpy as jnp
from jax import lax
from jax.experimental import pallas as pl
from jax.experimental.pallas import tpu as pltpu
from jax.experimental.pallas import tpu_sc as plsc

HEAD_DIM = 64
N_HEADS = 16
N_HEADS_NA = 8
GRID_W = 64
NA_ROWS = 8
NA_COLS = 16
DIL_PAIRS = ((128, 1), (512, 4), (2048, 16))
ROPE_THETA = 10000.0
N_GROUPS = 4
EXPERTS_PER_GROUP = 8
N_EXPERTS = N_GROUPS * EXPERTS_PER_GROUP
EPS = 1e-6
NEG = -1e30
LOG2_E = 1.4426950408889634

LANES = 128
PAIR_W = 2 * HEAD_DIM
DIL_BLK = 128
ATTN_STEPS_PER_TRIP = 64
ATTN_SLOTS = 4
ROUTER_SUB = 512
PLE_SUB = 256
ROW_TILE = 1024
VMEM_LIMIT = 56 * 1024 * 1024
SC_CORES = 2
SC_SUBCORES = 16
SC_CHUNK = 128
TOKEN_CHUNKS = 2

F32 = jnp.float32
BF16 = jnp.bfloat16


def _dot(a, b):
    return jnp.dot(a, b, preferred_element_type=F32)


def _dot_nt(a, b):
    return lax.dot_general(a, b, (((1,), (1,)), ((), ())), preferred_element_type=F32)


def _rms(x, gain):
    return x * lax.rsqrt(jnp.mean(x * x, axis=-1, keepdims=True) + EPS) * gain


def _lane_first_half(shape):
    return lax.broadcasted_iota(jnp.int32, shape, len(shape) - 1) < HEAD_DIM


def _qkv_kernel(x_ref, g_ref, w_ref, gq_na_ref, gk_na_ref, gq_dil_ref, gk_dil_ref, cos_ref, sin_ref,
                hsum_ref, qa_ref, ka_ref, va_ref, qb_ref, kb_ref, vb_ref, wb_ref):
    d = x_ref.shape[1]
    half = d // 2
    scale = HEAD_DIM ** -0.5 * LOG2_E

    @pl.when(pl.program_id(0) == 0)
    def _():
        wb_ref[...] = w_ref[...].astype(BF16)

    h = _rms(x_ref[...], g_ref[...]).astype(BF16)

    def proj(col):
        return _dot(h, wb_ref[:, col:col + half])

    def head_norm(y, gain):
        sq = (y * y).astype(BF16)
        w = hsum_ref.shape[0]
        ms = jnp.concatenate([_dot(sq[:, c:c + w], hsum_ref[...]) for c in range(0, half, w)], axis=1)
        return y * lax.rsqrt(ms + EPS) * gain

    def rope(y):
        lane = lax.broadcasted_iota(jnp.int32, (y.shape[0], LANES), 1)
        lower = (lane % HEAD_DIM) < HEAD_DIM // 2
        cos = cos_ref[...]
        sin = sin_ref[...]
        outs = []
        for c in range(0, half, LANES):
            yc = y[:, c:c + LANES]
            up = pltpu.roll(yc, LANES - HEAD_DIM // 2, axis=1)
            down = pltpu.roll(yc, HEAD_DIM // 2, axis=1)
            outs.append(yc * cos + jnp.where(lower, up, down) * sin)
        return jnp.concatenate(outs, axis=1)

    qa_ref[...] = (head_norm(proj(0), gq_na_ref[...]) * scale).astype(BF16)
    qb_ref[...] = (rope(head_norm(proj(half), gq_dil_ref[...])) * scale).astype(BF16)
    ka_ref[...] = head_norm(proj(d), gk_na_ref[...]).astype(BF16)
    kb_ref[...] = rope(head_norm(proj(d + half), gk_dil_ref[...])).astype(BF16)
    va_ref[...] = proj(2 * d).astype(BF16)
    vb_ref[...] = proj(2 * d + half).astype(BF16)


def _qkv_proj(x2, g_attn, w_qkv, gq_na, gk_na, gq_dil, gk_dil, seq, tm):
    n, d = x2.shape
    half = d // 2
    pos = jnp.arange(seq, dtype=F32)
    inv = ROPE_THETA ** (-jnp.arange(HEAD_DIM // 2, dtype=F32) / (HEAD_DIM // 2))
    ang = pos[:, None] * inv[None, :]
    cos = jnp.tile(jnp.cos(ang), (1, LANES // (HEAD_DIM // 2)))
    sin = jnp.tile(jnp.concatenate([-jnp.sin(ang), jnp.sin(ang)], axis=1), (1, LANES // HEAD_DIM))
    hs_w = 2 * LANES
    blk = np.arange(hs_w) // HEAD_DIM
    hsum = jnp.asarray((blk[:, None] == blk[None, :]).astype(np.float32) / HEAD_DIM, BF16)
    tile_gain = lambda g: jnp.tile(g.astype(F32), half // HEAD_DIM)[None, :]
    steps_per_seq = seq // tm
    full = lambda shape: pl.BlockSpec(shape, lambda i: (0,) * len(shape))
    out = jax.ShapeDtypeStruct((n, half), BF16)
    return pl.pallas_call(
        _qkv_kernel,
        grid=(n // tm,),
        in_specs=[
            pl.BlockSpec((tm, d), lambda i: (i, 0)),
            full((1, d)),
            pl.BlockSpec((d, 3 * d), lambda i: (0, 0), pipeline_mode=pl.Buffered(1)),
            full((1, half)), full((1, half)), full((1, half)), full((1, half)),
            pl.BlockSpec((tm, LANES), lambda i: (i % steps_per_seq, 0)),
            pl.BlockSpec((tm, LANES), lambda i: (i % steps_per_seq, 0)),
            full((hs_w, hs_w)),
        ],
        out_specs=[pl.BlockSpec((tm, half), lambda i: (i, 0))] * 6,
        out_shape=[out] * 6,
        scratch_shapes=[pltpu.VMEM((d, 3 * d), BF16)],
        compiler_params=pltpu.CompilerParams(dimension_semantics=("arbitrary",), vmem_limit_bytes=VMEM_LIMIT),
        name="qkv_proj",
    )(x2, g_attn[None, :].astype(F32), w_qkv, tile_gain(gq_na), tile_gain(gk_na),
      tile_gain(gq_dil), tile_gain(gk_dil), cos, sin, hsum)


def _is_static(x):
    return isinstance(x, int)


def _clip(x, lo, hi):
    return min(max(x, lo), hi) if _is_static(x) else jnp.clip(x, lo, hi)


def _aligned_ds(start, size, align):
    return pl.ds(start if _is_static(start) else pl.multiple_of(start, align), size)


def _software_pipeline(n_items, stages, steps_per_trip):
    depth = len(stages)
    assert steps_per_trip % ATTN_SLOTS == 0

    def step(t, phase, static):
        for k in reversed(range(depth)):
            if static and not 0 <= t - k < n_items:
                continue
            stages[k](t - k, (phase - k) % ATTN_SLOTS)

    first_full = -(-(depth - 1) // ATTN_SLOTS) * ATTN_SLOTS
    trips = max(n_items - first_full, 0) // steps_per_trip
    if trips < 2:
        trips = 0
    looped_end = first_full + trips * steps_per_trip if trips else 0
    for t in range(first_full if trips else 0):
        step(t, t % ATTN_SLOTS, True)
    if trips:
        def body(i, carry):
            for j in range(steps_per_trip):
                step(first_full + i * steps_per_trip + j, j % ATTN_SLOTS, False)
            return carry

        lax.fori_loop(0, trips, body, 0)
    for t in range(looped_end, n_items + depth - 1):
        step(t, t % ATTN_SLOTS, True)


def _qk_stage(q, kwin, bias_a, bias_b, s_ref, slot, first):
    m = q.shape[0]
    w = kwin.shape[0]
    zero = jnp.zeros_like(q)
    s = _dot_nt(jnp.concatenate([jnp.where(first, q, zero), jnp.where(first, zero, q)], axis=0), kwin)
    s_ref[slot, :m, :w] = s[:m] + bias_a
    s_ref[slot, m:, :w] = s[m:] + bias_b


def _softmax_stage(s_ref, p_ref, slot, w):
    m = jnp.max(s_ref[slot, :, :w], axis=-1, keepdims=True)
    p_ref[slot, :, :w] = jnp.exp2(s_ref[slot, :, :w] - m).astype(BF16)
    return m


def _pv_stage(p_ref, slot, v_win, first):
    w = v_win.shape[0]
    r = _dot(p_ref[slot, :, :w], v_win)
    m = r.shape[0] // 2
    return jnp.where(first, r[:m, :PAIR_W], r[m:, :PAIR_W]), jnp.where(first, r[:m, PAIR_W:], r[m:, PAIR_W:])


def _with_ones(v):
    return jnp.concatenate([v, jnp.ones_like(v)], axis=1)


def _na_kernel(q_ref, k_ref, v_ref, bias_ref, o_ref, v1_ref, s_ref, p_ref):
    rows = q_ref.shape[0] // GRID_W
    win = NA_ROWS * GRID_W
    v1_ref[...] = _with_ones(v_ref[...])
    first = _lane_first_half((GRID_W, PAIR_W))

    def slices(r):
        rs = _clip(r - NA_ROWS // 2, 0, rows - NA_ROWS)
        return _aligned_ds(r * GRID_W, GRID_W, GRID_W), _aligned_ds(rs * GRID_W, win, GRID_W), r - rs

    def bias(head, delta):
        return jnp.concatenate([bias_ref[head, j - delta + NA_ROWS - 1] for j in range(0, NA_ROWS, 2)], axis=1)

    def qk(r, slot):
        qs, ks, delta = slices(r)
        _qk_stage(q_ref[qs, :], k_ref[ks, :], bias(0, delta), bias(1, delta), s_ref, slot, first)

    def softmax(r, slot):
        _softmax_stage(s_ref, p_ref, slot, win)

    def pv(r, slot):
        qs, ks, _ = slices(r)
        num, den = _pv_stage(p_ref, slot, v1_ref[ks, :], first)
        o_ref[qs, :] = (num / den).astype(o_ref.dtype)

    _software_pipeline(rows, (qk, softmax, pv), ATTN_STEPS_PER_TRIP)


def _na_bias_table(rpb):
    w = np.arange(GRID_W)
    cs = np.clip(w - NA_COLS // 2, 0, GRID_W - NA_COLS)
    kc = np.arange(GRID_W)
    valid = (kc[None, :] >= cs[:, None]) & (kc[None, :] < cs[:, None] + NA_COLS)
    coff = np.clip(kc[None, :] - w[:, None] + NA_COLS - 1, 0, 2 * NA_COLS - 2)
    tab = jnp.where(valid[None, None], rpb.astype(F32)[:, :, coff] * LOG2_E, NEG)
    return jnp.concatenate([tab[:, :-1], tab[:, 1:]], axis=-1)


def _na_attention(q, k, v, rpb):
    b, s, width = q.shape
    pairs = width // PAIR_W
    assert s % GRID_W == 0 and s // GRID_W >= NA_ROWS
    bias = _na_bias_table(rpb).reshape(pairs, 2, 2 * NA_ROWS - 2, GRID_W, 2 * GRID_W)
    qkv_spec = pl.BlockSpec((None, s, PAIR_W), lambda bi, j: (bi, 0, j))
    return pl.pallas_call(
        _na_kernel,
        grid=(b, pairs),
        in_specs=[qkv_spec, qkv_spec, qkv_spec,
                  pl.BlockSpec((None, 2, 2 * NA_ROWS - 2, GRID_W, 2 * GRID_W), lambda bi, j: (j, 0, 0, 0, 0))],
        out_specs=pl.BlockSpec((None, s, PAIR_W), lambda bi, j: (bi, 0, j)),
        out_shape=jax.ShapeDtypeStruct((b, s, width), BF16),
        scratch_shapes=[pltpu.VMEM((s, 2 * PAIR_W), BF16),
                        pltpu.VMEM((ATTN_SLOTS, 2 * GRID_W, NA_ROWS * GRID_W), F32),
                        pltpu.VMEM((ATTN_SLOTS, 2 * GRID_W, NA_ROWS * GRID_W), BF16)],
        compiler_params=pltpu.CompilerParams(dimension_semantics=("parallel", "parallel"),
                                             vmem_limit_bytes=VMEM_LIMIT),
        name="na_attn",
    )(q, k, v, bias)


def _dil_kernel(q_ref, k_ref, v_ref, mwide_ref, mfull_ref, o_ref,
                f32a_ref, f32b_ref, qc_ref, kc_ref, vc_ref, acc_ref, den_ref, max_ref, s_ref, p_ref):
    s = q_ref.shape[0]
    for i, ref in enumerate((q_ref, k_ref, v_ref)):
        f32a_ref[i] = ref[...].astype(F32)
    first = _lane_first_half((DIL_BLK, PAIR_W))
    prev_ref, next_ref, prev_dil = f32a_ref, f32b_ref, 1

    for p, (window, dil) in enumerate(DIL_PAIRS):
        radius = window // (2 * dil)
        cls_len = s // dil
        nblk = cls_len // DIL_BLK
        wide = cls_len >= 2 * DIL_BLK
        win = 2 * DIL_BLK if wide else cls_len
        assert radius == DIL_BLK // 2 and cls_len % DIL_BLK == 0 and dil % prev_dil == 0

        ratio = dil // prev_dil
        keep = ratio > 1 and p + 1 < len(DIL_PAIRS)
        for c in range(dil):
            src = pl.ds((c % prev_dil) * (s // prev_dil) + c // prev_dil, cls_len, stride=ratio)
            dst = pl.ds(c * cls_len, cls_len)
            vals = [prev_ref[i, src, :] for i in range(3)]
            if keep:
                for i in range(3):
                    next_ref[i, dst, :] = vals[i]
            qc_ref[dst, :] = vals[0].astype(BF16)
            kc_ref[dst, :] = vals[1].astype(BF16)
            vc_ref[dst, :] = _with_ones(vals[2].astype(BF16))
        if keep:
            prev_ref, next_ref, prev_dil = next_ref, prev_ref, dil

        def slices(n):
            c, i = divmod(n, nblk) if _is_static(n) else (n // nblk, n % nblk)
            base = c * cls_len
            ws = _clip(i * DIL_BLK - radius, 0, cls_len - win)
            if _is_static(i):
                kind = 0 if i == 0 else (2 if i == nblk - 1 else 1)
            else:
                kind = jnp.where(i == 0, 0, jnp.where(i == nblk - 1, 2, 1))
            return (_aligned_ds(base + i * DIL_BLK, DIL_BLK, DIL_BLK), _aligned_ds(base + ws, win, radius), kind,
                    pl.ds(c + dil * DIL_BLK * i, DIL_BLK, stride=dil))

        def qk(n, slot):
            qs, ks, kind, _ = slices(n)
            mask = mwide_ref[kind] if wide else mfull_ref[...]
            _qk_stage(qc_ref[qs, :], kc_ref[ks, :], mask, mask, s_ref, slot, first)

        def softmax(n, slot):
            m = _softmax_stage(s_ref, p_ref, slot, win)
            max_ref[p, slices(n)[3], :] = jnp.where(first, m[:DIL_BLK], m[DIL_BLK:])

        def pv(n, slot):
            _, ks, _, tok = slices(n)
            num, den = _pv_stage(p_ref, slot, vc_ref[ks, :], first)
            acc_ref[p, tok, :] = num
            den_ref[p, tok, :] = den

        _software_pipeline(dil * nblk, (qk, softmax, pv), ATTN_STEPS_PER_TRIP)

    m = jnp.maximum(jnp.maximum(max_ref[0], max_ref[1]), max_ref[2])
    num = jnp.zeros_like(m)
    den = jnp.zeros_like(m)
    for p in range(len(DIL_PAIRS)):
        w = jnp.exp2(max_ref[p] - m)
        num = num + w * acc_ref[p]
        den = den + w * den_ref[p]
    o_ref[...] = (num / den).astype(o_ref.dtype)


def _band_mask(kind):
    radius = DIL_BLK // 2
    qq = np.arange(DIL_BLK)[:, None]
    if kind == "full":
        kk = np.arange(DIL_BLK)[None, :]
        shift = 0
    else:
        kk = np.arange(2 * DIL_BLK)[None, :]
        shift = {"first": 0, "inner": radius, "last": DIL_BLK}[kind]
    return np.where(np.abs(kk - qq - shift) <= radius, 0.0, NEG).astype(np.float32)


def _dil_attention(q, k, v):
    b, s, width = q.shape
    pairs = width // PAIR_W
    for window, dil in DIL_PAIRS:
        assert s % (window // 2) == 0 and (s // dil) % DIL_BLK == 0
    mwide = jnp.asarray(np.stack([_band_mask("first"), _band_mask("inner"), _band_mask("last")]))
    mfull = jnp.asarray(_band_mask("full"))
    qkv_spec = pl.BlockSpec((None, s, PAIR_W), lambda bi, j: (bi, 0, j))
    npat = len(DIL_PAIRS)
    return pl.pallas_call(
        _dil_kernel,
        grid=(b, pairs),
        in_specs=[qkv_spec, qkv_spec, qkv_spec,
                  pl.BlockSpec(mwide.shape, lambda bi, j: (0, 0, 0)),
                  pl.BlockSpec(mfull.shape, lambda bi, j: (0, 0))],
        out_specs=pl.BlockSpec((None, s, PAIR_W), lambda bi, j: (bi, 0, j)),
        out_shape=jax.ShapeDtypeStruct((b, s, width), BF16),
        scratch_shapes=[pltpu.VMEM((3, s, PAIR_W), F32)] * 2 + [pltpu.VMEM((s, PAIR_W), BF16)] * 2
        + [pltpu.VMEM((s, 2 * PAIR_W), BF16)] + [pltpu.VMEM((npat, s, PAIR_W), F32)] * 3
        + [pltpu.VMEM((ATTN_SLOTS, 2 * DIL_BLK, 2 * DIL_BLK), F32),
           pltpu.VMEM((ATTN_SLOTS, 2 * DIL_BLK, 2 * DIL_BLK), BF16)],
        compiler_params=pltpu.CompilerParams(dimension_semantics=("parallel", "parallel"),
                                             vmem_limit_bytes=VMEM_LIMIT),
        name="dil_attn",
    )(q, k, v, mwide, mfull)


def _split_bf16(x):
    hi = x.astype(BF16)
    return hi, (x - hi.astype(F32)).astype(BF16)


def _route(logits):
    lane = lax.broadcasted_iota(jnp.int32, logits.shape, 1)
    ninf = jnp.float32(-jnp.inf)

    def first_argmax(vals, vmax):
        return jnp.min(jnp.where(vals == vmax, lane, LANES), axis=-1, keepdims=True)

    gl = jnp.where(lane < N_GROUPS, logits, ninf)
    gmax = jnp.max(gl, axis=-1, keepdims=True)
    gsel = first_argmax(gl, gmax)
    gw = 1.0 / jnp.sum(jnp.exp(gl - gmax), axis=-1, keepdims=True)
    lo = N_GROUPS + EXPERTS_PER_GROUP * gsel
    el = jnp.where((lane >= lo) & (lane < lo + EXPERTS_PER_GROUP), logits, ninf)
    v0 = jnp.max(el, axis=-1, keepdims=True)
    i0 = first_argmax(el, v0)
    el = jnp.where(lane == i0, ninf, el)
    v1 = jnp.max(el, axis=-1, keepdims=True)
    i1 = first_argmax(el, v1)
    t = jnp.exp(v1 - v0)
    w0 = gw / (1.0 + t)
    w1 = gw * t / (1.0 + t)
    e0 = (i0 - N_GROUPS).astype(F32)
    e1 = (i1 - N_GROUPS).astype(F32)
    return jnp.where(lane == 0, e0, jnp.where(lane == 1, e1, jnp.where(lane == 2, w0, jnp.where(lane == 3, w1, 0.0))))


def _out_router_kernel(oa_ref, ob_ref, x_ref, ga_ref, gb_ref, wo_ref, gf_ref, wr_ref, br_ref,
                       x1_ref, h_ref, rt_ref, rtt_ref, cnt_ref):
    half = oa_ref.shape[1]

    @pl.when(pl.program_id(0) == 0)
    def _():
        cnt_ref[...] = jnp.zeros_like(cnt_ref)

    for r0 in range(0, oa_ref.shape[0], ROUTER_SUB):
        rows = pl.ds(r0, ROUTER_SUB)
        ya = _rms(oa_ref[rows, :].astype(F32), ga_ref[...]).astype(BF16)
        yb = _rms(ob_ref[rows, :].astype(F32), gb_ref[...]).astype(BF16)
        x1 = x_ref[rows, :] + _dot(ya, wo_ref[:half, :]) + _dot(yb, wo_ref[half:, :])
        x1_ref[rows, :] = x1
        h = _rms(x1, gf_ref[...])
        h_ref[rows, :] = _pack_bf16_pairs(h)
        h_hi, h_lo = _split_bf16(h)
        both = _dot(h_hi, wr_ref[...])
        logits = both[:, :LANES] + both[:, LANES:] + _dot(h_lo, wr_ref[:, :LANES]) + br_ref[...]
        rt = _route(logits)
        rt_ref[rows, :] = rt
        rtt = rt.T[:rtt_ref.shape[0], :]
        rtt_ref[:, rows] = rtt
        oh0, oh1 = _slot_one_hots(rtt)
        cnt_ref[...] += jnp.sum(oh0 + oh1, axis=1, keepdims=True)


def _out_router(oa, ob, x2, g_na, g_dil, w_o, g_ffn, w_rg, b_rg, w_re, b_re, tm, row0, n):
    d = x2.shape[1]
    half = d // 2
    blk0 = row0 // tm
    wr = jnp.concatenate([w_rg.astype(F32), w_re.astype(F32).transpose(1, 0, 2).reshape(d, N_EXPERTS)], axis=1)
    wr = jnp.pad(wr, ((0, 0), (0, LANES - wr.shape[1])))
    wr_hi = wr.astype(BF16)
    wr_cat = jnp.concatenate([wr_hi, (wr - wr_hi.astype(F32)).astype(BF16)], axis=1)
    br =jnp.pad(jnp.concatenate([b_rg.astype(F32), b_re.astype(F32).reshape(-1)]), (0, LANES - N_GROUPS - N_EXPERTS))
    full = lambda shape: pl.BlockSpec(shape, lambda i: (0,) * len(shape))
    row = lambda w: pl.BlockSpec((tm, w), lambda i: (i, 0))
    row_in = lambda w: pl.BlockSpec((tm, w), lambda i: (i + blk0, 0))
    return pl.pallas_call(
        _out_router_kernel,
        grid=(n // tm,),
        in_specs=[row_in(half), row_in(half), row_in(d), full((1, half)), full((1, half)), full((d, d)),
                  full((1, d)), full((d, 2 * LANES)), full((1, LANES))],
        out_specs=[row(d), row(half), row(LANES), pl.BlockSpec((8, tm), lambda i: (0, i)), full((LANES, LANES))],
        out_shape=[jax.ShapeDtypeStruct((n, d), F32), jax.ShapeDtypeStruct((n, half), jnp.uint32),
                   jax.ShapeDtypeStruct((n, LANES), F32), jax.ShapeDtypeStruct((8, n), F32),
                   jax.ShapeDtypeStruct((LANES, LANES), F32)],
        compiler_params=pltpu.CompilerParams(dimension_semantics=("arbitrary",), vmem_limit_bytes=VMEM_LIMIT),
        name="out_router",
    )(oa, ob, x2, g_na[None, :].astype(F32), g_dil[None, :].astype(F32), w_o.astype(BF16),
      g_ffn[None, :].astype(F32), wr_cat, br[None, :])


def _pack_bf16_pairs(x):
    w = x.shape[1] // 2
    bits = lax.bitcast_convert_type(x.astype(BF16).astype(F32), jnp.uint32)
    return bits[:, :w] | (bits[:, w:] >> 16)


def _unpack_bf16_pairs(u):
    hi = lax.bitcast_convert_type(u & jnp.uint32(0xFFFF0000), F32)
    lo = lax.bitcast_convert_type(u << 16, F32)
    return hi, lo


def _slot_one_hots(rtt):
    sub = lax.broadcasted_iota(jnp.int32, (LANES, rtt.shape[1]), 0).astype(F32)
    return (sub == rtt[0:1, :]).astype(F32), (sub == rtt[1:2, :]).astype(F32)


def _position_kernel(rtt_ref, start_ref, pos_ref, base_ref):
    tm = rtt_ref.shape[1]

    @pl.when(pl.program_id(0) == 0)
    def _():
        base_ref[...] = start_ref[...]

    oh0, oh1 = _slot_one_hots(rtt_ref[...])
    oh = oh0 + oh1
    earlier = lax.broadcasted_iota(jnp.int32, (tm, tm), 0) < lax.broadcasted_iota(jnp.int32, (tm, tm), 1)
    before = _dot(oh.astype(BF16), earlier.astype(BF16)) + base_ref[:, 0:1]
    p0 = jnp.sum(before * oh0, axis=0, keepdims=True)
    p1 = jnp.sum(before * oh1, axis=0, keepdims=True)
    row = lax.broadcasted_iota(jnp.int32, pos_ref.shape, 0)
    pos_ref[...] = jnp.where(row == 0, p0, jnp.where(row == 1, p1, 0.0)).astype(jnp.int32)
    base_ref[...] += jnp.sum(oh, axis=1, keepdims=True)


def _expert_positions(rtt, starts, tm):
    n = rtt.shape[1]
    return pl.pallas_call(
        _position_kernel,
        grid=(n // tm,),
        in_specs=[pl.BlockSpec((8, tm), lambda i: (0, i)), pl.BlockSpec((LANES, LANES), lambda i: (0, 0))],
        out_specs=pl.BlockSpec((8, tm), lambda i: (0, i)),
        out_shape=jax.ShapeDtypeStruct((8, n), jnp.int32),
        scratch_shapes=[pltpu.VMEM((LANES, LANES), F32)],
        compiler_params=pltpu.CompilerParams(dimension_semantics=("arbitrary",)),
        name="expert_positions",
    )(rtt, starts)


def _sc_mesh():
    return plsc.VectorSubcoreMesh(core_axis_name="c", subcore_axis_name="s",
                                  num_cores=SC_CORES, num_subcores=SC_SUBCORES)


def _sc_dispatch(hp, pos0, pos1, n_out):
    n, w = hp.shape
    workers = SC_CORES * SC_SUBCORES
    per = n // workers
    chunks = per // SC_CHUNK
    assert n % (workers * SC_CHUNK) == 0

    @functools.partial(
        pl.kernel, out_type=jax.ShapeDtypeStruct((n_out, w), hp.dtype), mesh=_sc_mesh(),
        scratch_types=[pltpu.VMEM((chunks, SC_CHUNK), jnp.int32), pltpu.VMEM((chunks, SC_CHUNK), jnp.int32),
                       pltpu.VMEM((SC_CHUNK, w), hp.dtype)],
        name="moe_dispatch")
    def body(h_hbm, p0_hbm, p1_hbm, xs_hbm, i0_v, i1_v, rows_v):
        wid = lax.axis_index("s") * SC_CORES + lax.axis_index("c")
        pltpu.sync_copy(p0_hbm.at[wid], i0_v)
        pltpu.sync_copy(p1_hbm.at[wid], i1_v)

        @pl.loop(0, chunks)
        def _(j):
            pltpu.sync_copy(h_hbm.at[pl.ds(wid * per + j * SC_CHUNK, SC_CHUNK)], rows_v)
            pltpu.sync_copy(rows_v, xs_hbm.at[i0_v.at[j]])
            pltpu.sync_copy(rows_v, xs_hbm.at[i1_v.at[j]])

    return body(hp, pos0.reshape(workers, chunks, SC_CHUNK), pos1.reshape(workers, chunks, SC_CHUNK))


def _sc_collect(ys, pos0, pos1):
    n = pos0.shape[0]
    w = ys.shape[1]
    workers = SC_CORES * SC_SUBCORES
    per = n // workers
    chunks = per // SC_CHUNK
    out = jax.ShapeDtypeStruct((n, w), ys.dtype)

    @functools.partial(
        pl.kernel, out_type=(out, out), mesh=_sc_mesh(),
        scratch_types=[pltpu.VMEM((chunks, SC_CHUNK), jnp.int32), pltpu.VMEM((chunks, SC_CHUNK), jnp.int32),
                       pltpu.VMEM((SC_CHUNK, w), ys.dtype)],
        name="moe_collect")
    def body(ys_hbm, p0_hbm, p1_hbm, y0_hbm, y1_hbm, i0_v, i1_v, rows_v):
        wid = lax.axis_index("s") * SC_CORES + lax.axis_index("c")
        pltpu.sync_copy(p0_hbm.at[wid], i0_v)
        pltpu.sync_copy(p1_hbm.at[wid], i1_v)

        @pl.loop(0, chunks)
        def _(j):
            dst = pl.ds(wid * per + j * SC_CHUNK, SC_CHUNK)
            pltpu.sync_copy(ys_hbm.at[i0_v.at[j]], rows_v)
            pltpu.sync_copy(rows_v, y0_hbm.at[dst])
            pltpu.sync_copy(ys_hbm.at[i1_v.at[j]], rows_v)
            pltpu.sync_copy(rows_v, y1_hbm.at[dst])

    return body(ys, pos0.reshape(workers, chunks, SC_CHUNK), pos1.reshape(workers, chunks, SC_CHUNK))


def _experts_kernel(te_ref, nt_ref, xs_ref, wg_ref, wu_ref, wd_ref, ys_ref, wg_bf, wu_bf, wd_bf):
    t = pl.program_id(0)
    live = t < nt_ref[0]

    @pl.when(live & ((t == 0) | (te_ref[t] != te_ref[jnp.maximum(t - 1, 0)])))
    def _():
        wg_bf[...] = wg_ref[...].astype(BF16)
        wu_bf[...] = wu_ref[...].astype(BF16)
        wd_bf[...] = wd_ref[...].astype(BF16)

    @pl.when(live)
    def _():
        half = wg_bf.shape[0] // 2
        hi, lo = _unpack_bf16_pairs(xs_ref[...])
        hi = hi.astype(BF16)
        lo = lo.astype(BF16)
        a = _dot(hi, wg_bf[:half, :]) + _dot(lo, wg_bf[half:, :])
        u = _dot(hi, wu_bf[:half, :]) + _dot(lo, wu_bf[half:, :])
        act = (a * jax.nn.sigmoid(a) * u).astype(BF16)
        ys_ref[...] = _pack_bf16_pairs(_dot(act, wd_bf[...]))


def _experts(xs, tile_expert, n_tiles, w_gate, w_up, w_down, tmg):
    rows, w = xs.shape
    ne, d, de = w_gate.shape
    live = lambda t, nt: jnp.minimum(t, nt[0] - 1)
    return pl.pallas_call(
        _experts_kernel,
        grid_spec=pltpu.PrefetchScalarGridSpec(
            num_scalar_prefetch=2,
            grid=(rows // tmg,),
            in_specs=[pl.BlockSpec((tmg, w), lambda t, te, nt: (live(t, nt), 0)),
                      pl.BlockSpec((None, d, de), lambda t, te, nt: (te[live(t, nt)], 0, 0)),
                      pl.BlockSpec((None, d, de), lambda t, te, nt: (te[live(t, nt)], 0, 0)),
                      pl.BlockSpec((None, de, d), lambda t, te, nt: (te[live(t, nt)], 0, 0))],
            out_specs=pl.BlockSpec((tmg, w), lambda t, te, nt: (live(t, nt), 0)),
            scratch_shapes=[pltpu.VMEM((d, de), BF16), pltpu.VMEM((d, de), BF16), pltpu.VMEM((de, d), BF16)],
        ),
        out_shape=jax.ShapeDtypeStruct((rows, w), jnp.uint32),
        compiler_params=pltpu.CompilerParams(dimension_semantics=("arbitrary",), vmem_limit_bytes=VMEM_LIMIT),
        name="experts",
    )(tile_expert, n_tiles, xs, w_gate, w_up, w_down)


def _moe(hp, rtt, cnt, w_gate, w_up, w_down, tm, tmg):
    n = hp.shape[0]
    ne = w_gate.shape[0]
    counts = cnt[:ne, 0].astype(jnp.int32)
    padded = (counts + tmg - 1) // tmg * tmg
    ends = jnp.cumsum(padded)
    starts = jnp.pad((ends - padded).astype(F32), (0, LANES - ne))
    pos = _expert_positions(rtt, jnp.broadcast_to(starts[:, None], (LANES, LANES)), tm)
    pos0, pos1 = pos[0], pos[1]
    rows = 2 * n + ne * tmg
    tile_start = jnp.arange(rows // tmg, dtype=jnp.int32) * tmg
    tile_expert = jnp.minimum(jnp.sum(tile_start[:, None] >= ends[None, :], axis=1), ne - 1).astype(jnp.int32)
    n_tiles = (ends[-1:] // tmg).astype(jnp.int32)
    xs = _sc_dispatch(hp, pos0, pos1, rows)
    ys = _experts(xs, tile_expert, n_tiles, w_gate, w_up, w_down, tmg)
    return _sc_collect(ys, pos0, pos1)


def _ple_kernel(x1_ref, y0_ref, y1_ref, rt_ref, p_ref, g_ref, wg_ref, wp_ref, o_ref):
    for r0 in range(0, x1_ref.shape[0], PLE_SUB):
        rows = pl.ds(r0, PLE_SUB)
        rt = rt_ref[rows, :]
        y0 = jnp.concatenate(_unpack_bf16_pairs(y0_ref[rows, :]), axis=1)
        y1 = jnp.concatenate(_unpack_bf16_pairs(y1_ref[rows, :]), axis=1)
        x2 = x1_ref[rows, :] + rt[:, 2:3] * y0 + rt[:, 3:4] * y1
        gate = jax.nn.sigmoid(_dot(_rms(x2, g_ref[...]).astype(BF16), wg_ref[...]))
        o_ref[rows, :] = x2 + gate * _dot(p_ref[rows, :].astype(BF16), wp_ref[...])


def _ple_kernel_into(prev_ref, *refs):
    del prev_ref
    _ple_kernel(*refs)


def _ple(x1, y0, y1, rt, p2, g_ple, w_gate, w_proj, tm, row0, out_prev):
    n, d = x1.shape
    n_all, dp = p2.shape
    blk0 = row0 // tm
    full = lambda shape: pl.BlockSpec(shape, lambda i: (0,) * len(shape))
    row = lambda w: pl.BlockSpec((tm, w), lambda i: (i, 0))
    row_all = lambda w: pl.BlockSpec((tm, w), lambda i: (i + blk0, 0))
    in_specs = [row(d), row(d // 2), row(d // 2), row(LANES), row_all(dp), full((1, d)), full((d, d)), full((dp, d))]
    args = (x1, y0, y1, rt, p2, g_ple[None, :].astype(F32), w_gate.astype(BF16), w_proj.astype(BF16))
    if out_prev is not None:
        in_specs = [pl.BlockSpec(memory_space=pl.ANY)] + in_specs
        args = (out_prev,) + args
    return pl.pallas_call(
        _ple_kernel if out_prev is None else _ple_kernel_into,
        grid=(n // tm,),
        in_specs=in_specs,
        out_specs=row_all(d),
        out_shape=jax.ShapeDtypeStruct((n_all, d), F32),
        input_output_aliases={} if out_prev is None else {0: 0},
        compiler_params=pltpu.CompilerParams(dimension_semantics=("parallel",), vmem_limit_bytes=VMEM_LIMIT),
        name="ple",
    )(*args)


def _layer(x, p_l, g_attn, w_qkv, q_norm_na, k_norm_na, rpb_na, q_norm_dil, k_norm_dil, g_out_na, g_out_dil,
           w_o, g_ffn, w_rg, b_rg, w_re, b_re, w_exp_gate, w_exp_up, w_exp_down, g_ple, w_ple_gate, w_ple_proj):
    b, s, d = x.shape
    n = b * s
    half = d // 2
    assert d == N_HEADS * HEAD_DIM and half == N_HEADS_NA * HEAD_DIM
    tm = 512
    assert s % tm == 0
    x2 = x.reshape(n, d)
    qa, ka, va, qb, kb, vb = _qkv_proj(x2, g_attn, w_qkv, q_norm_na, k_norm_na, q_norm_dil, k_norm_dil, s, tm)
    seq = lambda t: t.reshape(b, s, half)
    oa = _na_attention(seq(qa), seq(ka), seq(va), rpb_na).reshape(n, half)
    ob = _dil_attention(seq(qb), seq(kb), seq(vb)).reshape(n, half)
    p2 = p_l.reshape(n, -1)
    chunk_unit = SC_CORES * SC_SUBCORES * SC_CHUNK
    n_chunks = TOKEN_CHUNKS if n % (TOKEN_CHUNKS * chunk_unit) == 0 else 1
    chunk = n // n_chunks
    assert chunk % ROW_TILE == 0
    out = None
    for c in range(n_chunks):
        x1, h, rt, rtt, cnt = _out_router(oa, ob, x2, g_out_na, g_out_dil, w_o, g_ffn, w_rg, b_rg, w_re, b_re,
                                          ROW_TILE, c * chunk, chunk)
        y0, y1 = _moe(h, rtt, cnt, w_exp_gate, w_exp_up, w_exp_down, 2 * tm, tm)
        out = _ple(x1, y0, y1, rt, p2, g_ple, w_ple_gate, w_ple_proj, ROW_TILE, c * chunk, out)
    return out.reshape(b, s, d)


def kernel(x, p, g_attn, w_qkv, q_norm_na, k_norm_na, rpb_na, q_norm_dil, k_norm_dil, g_out_na, g_out_dil, w_o,
           g_ffn, w_router_group, b_router_group, w_router_expert, b_router_expert, w_exp_gate, w_exp_up,
           w_exp_down, g_ple, w_ple_gate, w_ple_proj):
    for i in range(p.shape[0]):
        x = _layer(x, p[i], g_attn[i], w_qkv[i], q_norm_na[i], k_norm_na[i], rpb_na[i], q_norm_dil[i],
                   k_norm_dil[i], g_out_na[i], g_out_dil[i], w_o[i], g_ffn[i], w_router_group[i],
                   b_router_group[i], w_router_expert[i], b_router_expert[i], w_exp_gate[i], w_exp_up[i],
                   w_exp_down[i], g_ple[i], w_ple_gate[i], w_ple_proj[i])
    return x
```

```python
import functools

import numpy as np
import jax
import jax.numpy as jnp
from jax import lax
from jax.experimental import pallas as pl
from jax.experimental.pallas import tpu as pltpu
from jax.experimental.pallas import tpu_sc as plsc

HEAD_DIM = 64
N_HEADS = 16
N_HEADS_NA = 8
GRID_W = 64
NA_ROWS = 8
NA_COLS = 16
DIL_PAIRS = ((128, 1), (512, 4), (2048, 16))
ROPE_THETA = 10000.0
N_GROUPS = 4
EXPERTS_PER_GROUP = 8
N_EXPERTS = N_GROUPS * EXPERTS_PER_GROUP
EPS = 1e-6
NEG = -1e30
LOG2_E = 1.4426950408889634

LANES = 128
PAIR_W = 2 * HEAD_DIM
DIL_BLK = 128
ATTN_STEPS_PER_TRIP = 64
ATTN_SLOTS = 4
ROUTER_SUB = 512
PLE_SUB = 256
ROW_TILE = 1024
VMEM_LIMIT = 56 * 1024 * 1024
SC_CORES = 2
SC_SUBCORES = 16
SC_CHUNK = 128
EXPERT_LEAD = 3
EXPERT_SLOTS = 4
TOKEN_CHUNKS = 2

F32 = jnp.float32
BF16 = jnp.bfloat16


def _dot(a, b):
    return jnp.dot(a, b, preferred_element_type=F32)


def _dot_nt(a, b):
    return lax.dot_general(a, b, (((1,), (1,)), ((), ())), preferred_element_type=F32)


def _rms(x, gain):
    return x * lax.rsqrt(jnp.mean(x * x, axis=-1, keepdims=True) + EPS) * gain


def _lane_first_half(shape):
    return lax.broadcasted_iota(jnp.int32, shape, len(shape) - 1) < HEAD_DIM


def _qkv_kernel(x_ref, g_ref, w_ref, gq_na_ref, gk_na_ref, gq_dil_ref, gk_dil_ref, cos_ref, sin_ref,
                hsum_ref, qa_ref, ka_ref, va_ref, qb_ref, kb_ref, vb_ref, wb_ref):
    d = x_ref.shape[1]
    half = d // 2
    scale = HEAD_DIM ** -0.5 * LOG2_E

    @pl.when(pl.program_id(0) == 0)
    def _():
        wb_ref[...] = w_ref[...].astype(BF16)

    h = _rms(x_ref[...], g_ref[...]).astype(BF16)

    def proj(col):
        return _dot(h, wb_ref[:, col:col + half])

    def head_norm(y, gain):
        sq = (y * y).astype(BF16)
        w = hsum_ref.shape[0]
        ms = jnp.concatenate([_dot(sq[:, c:c + w], hsum_ref[...]) for c in range(0, half, w)], axis=1)
        return y * lax.rsqrt(ms + EPS) * gain

    def rope(y):
        lane = lax.broadcasted_iota(jnp.int32, (y.shape[0], LANES), 1)
        lower = (lane % HEAD_DIM) < HEAD_DIM // 2
        cos = cos_ref[...]
        sin = sin_ref[...]
        outs = []
        for c in range(0, half, LANES):
            yc = y[:, c:c + LANES]
            up = pltpu.roll(yc, LANES - HEAD_DIM // 2, axis=1)
            down = pltpu.roll(yc, HEAD_DIM // 2, axis=1)
            outs.append(yc * cos + jnp.where(lower, up, down) * sin)
        return jnp.concatenate(outs, axis=1)

    qa_ref[...] = (head_norm(proj(0), gq_na_ref[...]) * scale).astype(BF16)
    qb_ref[...] = (rope(head_norm(proj(half), gq_dil_ref[...])) * scale).astype(BF16)
    ka_ref[...] = head_norm(proj(d), gk_na_ref[...]).astype(BF16)
    kb_ref[...] = rope(head_norm(proj(d + half), gk_dil_ref[...])).astype(BF16)
    va_ref[...] = proj(2 * d).astype(BF16)
    vb_ref[...] = proj(2 * d + half).astype(BF16)


def _qkv_proj(x2, g_attn, w_qkv, gq_na, gk_na, gq_dil, gk_dil, seq, tm):
    n, d = x2.shape
    half = d // 2
    pos = jnp.arange(seq, dtype=F32)
    inv = ROPE_THETA ** (-jnp.arange(HEAD_DIM // 2, dtype=F32) / (HEAD_DIM // 2))
    ang = pos[:, None] * inv[None, :]
    cos = jnp.tile(jnp.cos(ang), (1, LANES // (HEAD_DIM // 2)))
    sin = jnp.tile(jnp.concatenate([-jnp.sin(ang), jnp.sin(ang)], axis=1), (1, LANES // HEAD_DIM))
    hs_w = 2 * LANES
    blk = np.arange(hs_w) // HEAD_DIM
    hsum = jnp.asarray((blk[:, None] == blk[None, :]).astype(np.float32) / HEAD_DIM, BF16)
    tile_gain = lambda g: jnp.tile(g.astype(F32), half // HEAD_DIM)[None, :]
    steps_per_seq = seq // tm
    full = lambda shape: pl.BlockSpec(shape, lambda i: (0,) * len(shape))
    out = jax.ShapeDtypeStruct((n, half), BF16)
    return pl.pallas_call(
        _qkv_kernel,
        grid=(n // tm,),
        in_specs=[
            pl.BlockSpec((tm, d), lambda i: (i, 0)),
            full((1, d)),
            pl.BlockSpec((d, 3 * d), lambda i: (0, 0), pipeline_mode=pl.Buffered(1)),
            full((1, half)), full((1, half)), full((1, half)), full((1, half)),
            pl.BlockSpec((tm, LANES), lambda i: (i % steps_per_seq, 0)),
            pl.BlockSpec((tm, LANES), lambda i: (i % steps_per_seq, 0)),
            full((hs_w, hs_w)),
        ],
        out_specs=[pl.BlockSpec((tm, half), lambda i: (i, 0))] * 6,
        out_shape=[out] * 6,
        scratch_shapes=[pltpu.VMEM((d, 3 * d), BF16)],
        compiler_params=pltpu.CompilerParams(dimension_semantics=("arbitrary",), vmem_limit_bytes=VMEM_LIMIT),
        name="qkv_proj",
    )(x2, g_attn[None, :].astype(F32), w_qkv, tile_gain(gq_na), tile_gain(gk_na),
      tile_gain(gq_dil), tile_gain(gk_dil), cos, sin, hsum)


def _is_static(x):
    return isinstance(x, int)


def _clip(x, lo, hi):
    return min(max(x, lo), hi) if _is_static(x) else jnp.clip(x, lo, hi)


def _aligned_ds(start, size, align):
    return pl.ds(start if _is_static(start) else pl.multiple_of(start, align), size)


def _software_pipeline(n_items, stages, steps_per_trip):
    depth = len(stages)
    assert steps_per_trip % ATTN_SLOTS == 0

    def step(t, phase, static):
        for k in reversed(range(depth)):
            if static and not 0 <= t - k < n_items:
                continue
            stages[k](t - k, (phase - k) % ATTN_SLOTS)

    first_full = -(-(depth - 1) // ATTN_SLOTS) * ATTN_SLOTS
    trips = max(n_items - first_full, 0) // steps_per_trip
    if trips < 2:
        trips = 0
    looped_end = first_full + trips * steps_per_trip if trips else 0
    for t in range(first_full if trips else 0):
        step(t, t % ATTN_SLOTS, True)
    if trips:
        def body(i, carry):
            for j in range(steps_per_trip):
                step(first_full + i * steps_per_trip + j, j % ATTN_SLOTS, False)
            return carry

        lax.fori_loop(0, trips, body, 0)
    for t in range(looped_end, n_items + depth - 1):
        step(t, t % ATTN_SLOTS, True)


def _qk_stage(q, kwin, bias_a, bias_b, s_ref, slot, first):
    m = q.shape[0]
    w = kwin.shape[0]
    zero = jnp.zeros_like(q)
    s = _dot_nt(jnp.concatenate([jnp.where(first, q, zero), jnp.where(first, zero, q)], axis=0), kwin)
    s_ref[slot, :m, :w] = s[:m] + bias_a
    s_ref[slot, m:, :w] = s[m:] + bias_b


def _softmax_stage(s_ref, p_ref, slot, w):
    m = jnp.max(s_ref[slot, :, :w], axis=-1, keepdims=True)
    p_ref[slot, :, :w] = jnp.exp2(s_ref[slot, :, :w] - m).astype(BF16)
    return m


def _pv_stage(p_ref, slot, v_win, first):
    w = v_win.shape[0]
    r = _dot(p_ref[slot, :, :w], v_win)
    m = r.shape[0] // 2
    return jnp.where(first, r[:m, :PAIR_W], r[m:, :PAIR_W]), jnp.where(first, r[:m, PAIR_W:], r[m:, PAIR_W:])


def _with_ones(v):
    return jnp.concatenate([v, jnp.ones_like(v)], axis=1)


def _na_kernel(q_ref, k_ref, v_ref, bias_ref, o_ref, v1_ref, s_ref, p_ref):
    rows = q_ref.shape[0] // GRID_W
    win = NA_ROWS * GRID_W
    v1_ref[...] = _with_ones(v_ref[...])
    first = _lane_first_half((GRID_W, PAIR_W))

    def slices(r):
        rs = _clip(r - NA_ROWS // 2, 0, rows - NA_ROWS)
        return _aligned_ds(r * GRID_W, GRID_W, GRID_W), _aligned_ds(rs * GRID_W, win, GRID_W), r - rs

    def bias(head, delta):
        return jnp.concatenate([bias_ref[head, j - delta + NA_ROWS - 1] for j in range(0, NA_ROWS, 2)], axis=1)

    def qk(r, slot):
        qs, ks, delta = slices(r)
        _qk_stage(q_ref[qs, :], k_ref[ks, :], bias(0, delta), bias(1, delta), s_ref, slot, first)

    def softmax(r, slot):
        _softmax_stage(s_ref, p_ref, slot, win)

    def pv(r, slot):
        qs, ks, _ = slices(r)
        num, den = _pv_stage(p_ref, slot, v1_ref[ks, :], first)
        o_ref[qs, :] = (num / den).astype(o_ref.dtype)

    _software_pipeline(rows, (qk, softmax, pv), ATTN_STEPS_PER_TRIP)


def _na_bias_table(rpb):
    w = np.arange(GRID_W)
    cs = np.clip(w - NA_COLS // 2, 0, GRID_W - NA_COLS)
    kc = np.arange(GRID_W)
    valid = (kc[None, :] >= cs[:, None]) & (kc[None, :] < cs[:, None] + NA_COLS)
    coff = np.clip(kc[None, :] - w[:, None] + NA_COLS - 1, 0, 2 * NA_COLS - 2)
    tab = jnp.where(valid[None, None], rpb.astype(F32)[:, :, coff] * LOG2_E, NEG)
    return jnp.concatenate([tab[:, :-1], tab[:, 1:]], axis=-1)


def _na_attention(q, k, v, rpb):
    b, s, width = q.shape
    pairs = width // PAIR_W
    assert s % GRID_W == 0 and s // GRID_W >= NA_ROWS
    bias = _na_bias_table(rpb).reshape(pairs, 2, 2 * NA_ROWS - 2, GRID_W, 2 * GRID_W)
    qkv_spec = pl.BlockSpec((None, s, PAIR_W), lambda bi, j: (bi, 0, j))
    return pl.pallas_call(
        _na_kernel,
        grid=(b, pairs),
        in_specs=[qkv_spec, qkv_spec, qkv_spec,
                  pl.BlockSpec((None, 2, 2 * NA_ROWS - 2, GRID_W, 2 * GRID_W), lambda bi, j: (j, 0, 0, 0, 0))],
        out_specs=pl.BlockSpec((None, s, PAIR_W), lambda bi, j: (bi, 0, j)),
        out_shape=jax.ShapeDtypeStruct((b, s, width), BF16),
        scratch_shapes=[pltpu.VMEM((s, 2 * PAIR_W), BF16),
                        pltpu.VMEM((ATTN_SLOTS, 2 * GRID_W, NA_ROWS * GRID_W), F32),
                        pltpu.VMEM((ATTN_SLOTS, 2 * GRID_W, NA_ROWS * GRID_W), BF16)],
        compiler_params=pltpu.CompilerParams(dimension_semantics=("parallel", "parallel"),
                                             vmem_limit_bytes=VMEM_LIMIT),
        name="na_attn",
    )(q, k, v, bias)


def _dil_kernel(q_ref, k_ref, v_ref, mwide_ref, mfull_ref, o_ref,
                f32a_ref, f32b_ref, qc_ref, kc_ref, vc_ref, acc_ref, den_ref, max_ref, s_ref, p_ref):
    s = q_ref.shape[0]
    for i, ref in enumerate((q_ref, k_ref, v_ref)):
        f32a_ref[i] = ref[...].astype(F32)
    first = _lane_first_half((DIL_BLK, PAIR_W))
    prev_ref, next_ref, prev_dil = f32a_ref, f32b_ref, 1

    for p, (window, dil) in enumerate(DIL_PAIRS):
        radius = window // (2 * dil)
        cls_len = s // dil
        nblk = cls_len // DIL_BLK
        wide = cls_len >= 2 * DIL_BLK
        win = 2 * DIL_BLK if wide else cls_len
        assert radius == DIL_BLK // 2 and cls_len % DIL_BLK == 0 and dil % prev_dil == 0

        ratio = dil // prev_dil
        keep = ratio > 1 and p + 1 < len(DIL_PAIRS)
        for c in range(dil):
            src = pl.ds((c % prev_dil) * (s // prev_dil) + c // prev_dil, cls_len, stride=ratio)
            dst = pl.ds(c * cls_len, cls_len)
            vals = [prev_ref[i, src, :] for i in range(3)]
            if keep:
                for i in range(3):
                    next_ref[i, dst, :] = vals[i]
            qc_ref[dst, :] = vals[0].astype(BF16)
            kc_ref[dst, :] = vals[1].astype(BF16)
            vc_ref[dst, :] = _with_ones(vals[2].astype(BF16))
        if keep:
            prev_ref, next_ref, prev_dil = next_ref, prev_ref, dil

        def slices(n):
            c, i = divmod(n, nblk) if _is_static(n) else (n // nblk, n % nblk)
            base = c * cls_len
            ws = _clip(i * DIL_BLK - radius, 0, cls_len - win)
            if _is_static(i):
                kind = 0 if i == 0 else (2 if i == nblk - 1 else 1)
            else:
                kind = jnp.where(i == 0, 0, jnp.where(i == nblk - 1, 2, 1))
            return (_aligned_ds(base + i * DIL_BLK, DIL_BLK, DIL_BLK), _aligned_ds(base + ws, win, radius), kind,
                    pl.ds(c + dil * DIL_BLK * i, DIL_BLK, stride=dil))

        def qk(n, slot):
            qs, ks, kind, _ = slices(n)
            mask = mwide_ref[kind] if wide else mfull_ref[...]
            _qk_stage(qc_ref[qs, :], kc_ref[ks, :], mask, mask, s_ref, slot, first)

        def softmax(n, slot):
            m = _softmax_stage(s_ref, p_ref, slot, win)
            max_ref[p, slices(n)[3], :] = jnp.where(first, m[:DIL_BLK], m[DIL_BLK:])

        def pv(n, slot):
            _, ks, _, tok = slices(n)
            num, den = _pv_stage(p_ref, slot, vc_ref[ks, :], first)
            acc_ref[p, tok, :] = num
            den_ref[p, tok, :] = den

        _software_pipeline(dil * nblk, (qk, softmax, pv), ATTN_STEPS_PER_TRIP)

    m = jnp.maximum(jnp.maximum(max_ref[0], max_ref[1]), max_ref[2])
    num = jnp.zeros_like(m)
    den = jnp.zeros_like(m)
    for p in range(len(DIL_PAIRS)):
        w = jnp.exp2(max_ref[p] - m)
        num = num + w * acc_ref[p]
        den = den + w * den_ref[p]
    o_ref[...] = (num / den).astype(o_ref.dtype)


def _band_mask(kind):
    radius = DIL_BLK // 2
    qq = np.arange(DIL_BLK)[:, None]
    if kind == "full":
        kk = np.arange(DIL_BLK)[None, :]
        shift = 0
    else:
        kk = np.arange(2 * DIL_BLK)[None, :]
        shift = {"first": 0, "inner": radius, "last": DIL_BLK}[kind]
    return np.where(np.abs(kk - qq - shift) <= radius, 0.0, NEG).astype(np.float32)


def _dil_attention(q, k, v):
    b, s, width = q.shape
    pairs = width // PAIR_W
    for window, dil in DIL_PAIRS:
        assert s % (window // 2) == 0 and (s // dil) % DIL_BLK == 0
    mwide = jnp.asarray(np.stack([_band_mask("first"), _band_mask("inner"), _band_mask("last")]))
    mfull = jnp.asarray(_band_mask("full"))
    qkv_spec = pl.BlockSpec((None, s, PAIR_W), lambda bi, j: (bi, 0, j))
    npat = len(DIL_PAIRS)
    return pl.pallas_call(
        _dil_kernel,
        grid=(b, pairs),
        in_specs=[qkv_spec, qkv_spec, qkv_spec,
                  pl.BlockSpec(mwide.shape, lambda bi, j: (0, 0, 0)),
                  pl.BlockSpec(mfull.shape, lambda bi, j: (0, 0))],
        out_specs=pl.BlockSpec((None, s, PAIR_W), lambda bi, j: (bi, 0, j)),
        out_shape=jax.ShapeDtypeStruct((b, s, width), BF16),
        scratch_shapes=[pltpu.VMEM((3, s, PAIR_W), F32)] * 2 + [pltpu.VMEM((s, PAIR_W), BF16)] * 2
        + [pltpu.VMEM((s, 2 * PAIR_W), BF16)] + [pltpu.VMEM((npat, s, PAIR_W), F32)] * 3
        + [pltpu.VMEM((ATTN_SLOTS, 2 * DIL_BLK, 2 * DIL_BLK), F32),
           pltpu.VMEM((ATTN_SLOTS, 2 * DIL_BLK, 2 * DIL_BLK), BF16)],
        compiler_params=pltpu.CompilerParams(dimension_semantics=("parallel", "parallel"),
                                             vmem_limit_bytes=VMEM_LIMIT),
        name="dil_attn",
    )(q, k, v, mwide, mfull)


def _split_bf16(x):
    hi = x.astype(BF16)
    return hi, (x - hi.astype(F32)).astype(BF16)


def _route(logits):
    lane = lax.broadcasted_iota(jnp.int32, logits.shape, 1)
    ninf = jnp.float32(-jnp.inf)

    def first_argmax(vals, vmax):
        return jnp.min(jnp.where(vals == vmax, lane, LANES), axis=-1, keepdims=True)

    gl = jnp.where(lane < N_GROUPS, logits, ninf)
    gmax = jnp.max(gl, axis=-1, keepdims=True)
    gsel = first_argmax(gl, gmax)
    gw = 1.0 / jnp.sum(jnp.exp(gl - gmax), axis=-1, keepdims=True)
    lo = N_GROUPS + EXPERTS_PER_GROUP * gsel
    el = jnp.where((lane >= lo) & (lane < lo + EXPERTS_PER_GROUP), logits, ninf)
    v0 = jnp.max(el, axis=-1, keepdims=True)
    i0 = first_argmax(el, v0)
    el = jnp.where(lane == i0, ninf, el)
    v1 = jnp.max(el, axis=-1, keepdims=True)
    i1 = first_argmax(el, v1)
    t = jnp.exp(v1 - v0)
    w0 = gw / (1.0 + t)
    w1 = gw * t / (1.0 + t)
    e0 = (i0 - N_GROUPS).astype(F32)
    e1 = (i1 - N_GROUPS).astype(F32)
    return jnp.where(lane == 0, e0, jnp.where(lane == 1, e1, jnp.where(lane == 2, w0, jnp.where(lane == 3, w1, 0.0))))


def _out_router_kernel(oa_ref, ob_ref, x_ref, ga_ref, gb_ref, wo_ref, gf_ref, wr_ref, br_ref,
                       x1_ref, h_ref, rt_ref, rtt_ref, cnt_ref):
    half = oa_ref.shape[1]

    @pl.when(pl.program_id(0) == 0)
    def _():
        cnt_ref[...] = jnp.zeros_like(cnt_ref)

    for r0 in range(0, oa_ref.shape[0], ROUTER_SUB):
        rows = pl.ds(r0, ROUTER_SUB)
        ya = _rms(oa_ref[rows, :].astype(F32), ga_ref[...]).astype(BF16)
        yb = _rms(ob_ref[rows, :].astype(F32), gb_ref[...]).astype(BF16)
        x1 = x_ref[rows, :] + _dot(ya, wo_ref[:half, :]) + _dot(yb, wo_ref[half:, :])
        x1_ref[rows, :] = x1
        h = _rms(x1, gf_ref[...])
        h_ref[rows, :] = _pack_bf16_pairs(h)
        h_hi, h_lo = _split_bf16(h)
        both = _dot(h_hi, wr_ref[...])
        logits = both[:, :LANES] + both[:, LANES:] + _dot(h_lo, wr_ref[:, :LANES]) + br_ref[...]
        rt = _route(logits)
        rt_ref[rows, :] = rt
        rtt = rt.T[:rtt_ref.shape[0], :]
        rtt_ref[:, rows] = rtt
        oh0, oh1 = _slot_one_hots(rtt)
        cnt_ref[...] += jnp.sum(oh0 + oh1, axis=1, keepdims=True)


def _out_router(oa, ob, x2, g_na, g_dil, w_o, g_ffn, w_rg, b_rg, w_re, b_re, tm, row0, n):
    d = x2.shape[1]
    half = d // 2
    blk0 = row0 // tm
    wr = jnp.concatenate([w_rg.astype(F32), w_re.astype(F32).transpose(1, 0, 2).reshape(d, N_EXPERTS)], axis=1)
    wr = jnp.pad(wr, ((0, 0), (0, LANES - wr.shape[1])))
    wr_hi = wr.astype(BF16)
    wr_cat = jnp.concatenate([wr_hi, (wr - wr_hi.astype(F32)).astype(BF16)], axis=1)
    br =jnp.pad(jnp.concatenate([b_rg.astype(F32), b_re.astype(F32).reshape(-1)]), (0, LANES - N_GROUPS - N_EXPERTS))
    full = lambda shape: pl.BlockSpec(shape, lambda i: (0,) * len(shape))
    row = lambda w: pl.BlockSpec((tm, w), lambda i: (i, 0))
    row_in = lambda w: pl.BlockSpec((tm, w), lambda i: (i + blk0, 0))
    return pl.pallas_call(
        _out_router_kernel,
        grid=(n // tm,),
        in_specs=[row_in(half), row_in(half), row_in(d), full((1, half)), full((1, half)), full((d, d)),
                  full((1, d)), full((d, 2 * LANES)), full((1, LANES))],
        out_specs=[row(d), row(half), row(LANES), pl.BlockSpec((8, tm), lambda i: (0, i)), full((LANES, LANES))],
        out_shape=[jax.ShapeDtypeStruct((n, d), F32), jax.ShapeDtypeStruct((n, half), jnp.uint32),
                   jax.ShapeDtypeStruct((n, LANES), F32), jax.ShapeDtypeStruct((8, n), F32),
                   jax.ShapeDtypeStruct((LANES, LANES), F32)],
        compiler_params=pltpu.CompilerParams(dimension_semantics=("arbitrary",), vmem_limit_bytes=VMEM_LIMIT),
        name="out_router",
    )(oa, ob, x2, g_na[None, :].astype(F32), g_dil[None, :].astype(F32), w_o.astype(BF16),
      g_ffn[None, :].astype(F32), wr_cat, br[None, :])


def _pack_bf16_pairs(x):
    w = x.shape[1] // 2
    bits = lax.bitcast_convert_type(x.astype(BF16).astype(F32), jnp.uint32)
    return bits[:, :w] | (bits[:, w:] >> 16)


def _unpack_bf16_pairs(u):
    hi = lax.bitcast_convert_type(u & jnp.uint32(0xFFFF0000), F32)
    lo = lax.bitcast_convert_type(u << 16, F32)
    return hi, lo


def _slot_one_hots(rtt):
    sub = lax.broadcasted_iota(jnp.int32, (LANES, rtt.shape[1]), 0).astype(F32)
    return (sub == rtt[0:1, :]).astype(F32), (sub == rtt[1:2, :]).astype(F32)


def _position_kernel(rtt_ref, start_ref, pos_ref, base_ref):
    tm = rtt_ref.shape[1]

    @pl.when(pl.program_id(0) == 0)
    def _():
        base_ref[...] = start_ref[...]

    oh0, oh1 = _slot_one_hots(rtt_ref[...])
    oh = oh0 + oh1
    earlier = lax.broadcasted_iota(jnp.int32, (tm, tm), 0) < lax.broadcasted_iota(jnp.int32, (tm, tm), 1)
    before = _dot(oh.astype(BF16), earlier.astype(BF16)) + base_ref[:, 0:1]
    p0 = jnp.sum(before * oh0, axis=0, keepdims=True)
    p1 = jnp.sum(before * oh1, axis=0, keepdims=True)
    row = lax.broadcasted_iota(jnp.int32, pos_ref.shape, 0)
    pos_ref[...] = jnp.where(row == 0, p0, jnp.where(row == 1, p1, 0.0)).astype(jnp.int32)
    base_ref[...] += jnp.sum(oh, axis=1, keepdims=True)


def _expert_positions(rtt, starts, tm):
    n = rtt.shape[1]
    return pl.pallas_call(
        _position_kernel,
        grid=(n // tm,),
        in_specs=[pl.BlockSpec((8, tm), lambda i: (0, i)), pl.BlockSpec((LANES, LANES), lambda i: (0, 0))],
        out_specs=pl.BlockSpec((8, tm), lambda i: (0, i)),
        out_shape=jax.ShapeDtypeStruct((8, n), jnp.int32),
        scratch_shapes=[pltpu.VMEM((LANES, LANES), F32)],
        compiler_params=pltpu.CompilerParams(dimension_semantics=("arbitrary",)),
        name="expert_positions",
    )(rtt, starts)


def _sc_mesh():
    return plsc.VectorSubcoreMesh(core_axis_name="c", subcore_axis_name="s",
                                  num_cores=SC_CORES, num_subcores=SC_SUBCORES)


def _sc_dispatch(hp, pos0, pos1, n_out):
    n, w = hp.shape
    workers = SC_CORES * SC_SUBCORES
    per = n // workers
    chunks = per // SC_CHUNK
    assert n % (workers * SC_CHUNK) == 0

    @functools.partial(
        pl.kernel, out_type=jax.ShapeDtypeStruct((n_out, w), hp.dtype), mesh=_sc_mesh(),
        scratch_types=[pltpu.VMEM((chunks, SC_CHUNK), jnp.int32), pltpu.VMEM((chunks, SC_CHUNK), jnp.int32),
                       pltpu.VMEM((SC_CHUNK, w), hp.dtype)],
        name="moe_dispatch")
    def body(h_hbm, p0_hbm, p1_hbm, xs_hbm, i0_v, i1_v, rows_v):
        wid = lax.axis_index("s") * SC_CORES + lax.axis_index("c")
        pltpu.sync_copy(p0_hbm.at[wid], i0_v)
        pltpu.sync_copy(p1_hbm.at[wid], i1_v)

        @pl.loop(0, chunks)
        def _(j):
            pltpu.sync_copy(h_hbm.at[pl.ds(wid * per + j * SC_CHUNK, SC_CHUNK)], rows_v)
            pltpu.sync_copy(rows_v, xs_hbm.at[i0_v.at[j]])
            pltpu.sync_copy(rows_v, xs_hbm.at[i1_v.at[j]])

    return body(hp, pos0.reshape(workers, chunks, SC_CHUNK), pos1.reshape(workers, chunks, SC_CHUNK))


def _sc_collect(ys, pos0, pos1):
    n = pos0.shape[0]
    w = ys.shape[1]
    workers = SC_CORES * SC_SUBCORES
    per = n // workers
    chunks = per // SC_CHUNK
    out = jax.ShapeDtypeStruct((n, w), ys.dtype)

    @functools.partial(
        pl.kernel, out_type=(out, out), mesh=_sc_mesh(),
        scratch_types=[pltpu.VMEM((chunks, SC_CHUNK), jnp.int32), pltpu.VMEM((chunks, SC_CHUNK), jnp.int32),
                       pltpu.VMEM((SC_CHUNK, w), ys.dtype)],
        name="moe_collect")
    def body(ys_hbm, p0_hbm, p1_hbm, y0_hbm, y1_hbm, i0_v, i1_v, rows_v):
        wid = lax.axis_index("s") * SC_CORES + lax.axis_index("c")
        pltpu.sync_copy(p0_hbm.at[wid], i0_v)
        pltpu.sync_copy(p1_hbm.at[wid], i1_v)

        @pl.loop(0, chunks)
        def _(j):
            dst = pl.ds(wid * per + j * SC_CHUNK, SC_CHUNK)
            pltpu.sync_copy(ys_hbm.at[i0_v.at[j]], rows_v)
            pltpu.sync_copy(rows_v, y0_hbm.at[dst])
            pltpu.sync_copy(ys_hbm.at[i1_v.at[j]], rows_v)
            pltpu.sync_copy(rows_v, y1_hbm.at[dst])

    return body(ys, pos0.reshape(workers, chunks, SC_CHUNK), pos1.reshape(workers, chunks, SC_CHUNK))


def _experts_kernel(plan_ref, nt_ref, xs_ref, wg_ref, wu_ref, wd_ref, ys_ref, wg_bf, wu_bf, wd_bf):
    g = pl.program_id(0)
    prev = jnp.maximum(g - 1, 0)

    for row, (src, dst) in enumerate(((wg_ref, wg_bf), (wu_ref, wu_bf), (wd_ref, wd_bf))):
        @pl.when((g == 0) | (plan_ref[row, g] != plan_ref[row, prev]))
        def _(row=row, src=src, dst=dst):
            dst[plan_ref[row + 3, g]] = src[...].astype(BF16)

    @pl.when((g >= EXPERT_LEAD) & (g < nt_ref[0] + EXPERT_LEAD))
    def _():
        slot = plan_ref[6, g]
        half = wg_bf.shape[1] // 2
        hi, lo = _unpack_bf16_pairs(xs_ref[...])
        hi = hi.astype(BF16)
        lo = lo.astype(BF16)
        a = _dot(hi, wg_bf[slot, :half, :]) + _dot(lo, wg_bf[slot, half:, :])
        u = _dot(hi, wu_bf[slot, :half, :]) + _dot(lo, wu_bf[slot, half:, :])
        act = (a * jax.nn.sigmoid(a) * u).astype(BF16)
        ys_ref[...] = _pack_bf16_pairs(_dot(act, wd_bf[slot]))


def _experts(xs, tile_expert, n_tiles, w_gate, w_up, w_down, tmg):
    rows, w = xs.shape
    ne, d, de = w_gate.shape
    steps = rows // tmg + EXPERT_LEAD
    g = jnp.arange(steps, dtype=jnp.int32)
    run = jnp.concatenate([jnp.zeros((1,), jnp.int32),
                           jnp.cumsum((tile_expert[1:] != tile_expert[:-1]).astype(jnp.int32))])
    tile_at = lambda lag: jnp.clip(g - lag, 0, n_tiles[0] - 1)
    plan = jnp.stack([tile_expert[tile_at(lag)] for lag in range(EXPERT_LEAD)]
                     + [run[tile_at(lag)] % EXPERT_SLOTS for lag in range(EXPERT_LEAD + 1)]
                     + [tile_at(EXPERT_LEAD)]).astype(jnp.int32)
    return pl.pallas_call(
        _experts_kernel,
        grid_spec=pltpu.PrefetchScalarGridSpec(
            num_scalar_prefetch=2,
            grid=(steps,),
            in_specs=[pl.BlockSpec((tmg, w), lambda s, plan, nt: (plan[2 * EXPERT_LEAD + 1, s], 0)),
                      pl.BlockSpec((None, d, de), lambda s, plan, nt: (plan[0, s], 0, 0)),
                      pl.BlockSpec((None, d, de), lambda s, plan, nt: (plan[1, s], 0, 0)),
                      pl.BlockSpec((None, de, d), lambda s, plan, nt: (plan[2, s], 0, 0))],
            out_specs=pl.BlockSpec((tmg, w), lambda s, plan, nt: (plan[2 * EXPERT_LEAD + 1, s], 0)),
            scratch_shapes=[pltpu.VMEM((EXPERT_SLOTS, d, de), BF16), pltpu.VMEM((EXPERT_SLOTS, d, de), BF16),
                            pltpu.VMEM((EXPERT_SLOTS, de, d), BF16)],
        ),
        out_shape=jax.ShapeDtypeStruct((rows, w), jnp.uint32),
        compiler_params=pltpu.CompilerParams(dimension_semantics=("arbitrary",), vmem_limit_bytes=VMEM_LIMIT),
        name="experts",
    )(plan, n_tiles, xs, w_gate, w_up, w_down)


def _moe(hp, rtt, cnt, w_gate, w_up, w_down, tm, tmg):
    n = hp.shape[0]
    ne = w_gate.shape[0]
    counts = cnt[:ne, 0].astype(jnp.int32)
    padded = (counts + tmg - 1) // tmg * tmg
    ends = jnp.cumsum(padded)
    starts = jnp.pad((ends - padded).astype(F32), (0, LANES - ne))
    pos = _expert_positions(rtt, jnp.broadcast_to(starts[:, None], (LANES, LANES)), tm)
    pos0, pos1 = pos[0], pos[1]
    rows = 2 * n + ne * tmg
    tile_start = jnp.arange(rows // tmg, dtype=jnp.int32) * tmg
    tile_expert = jnp.minimum(jnp.sum(tile_start[:, None] >= ends[None, :], axis=1), ne - 1).astype(jnp.int32)
    n_tiles = (ends[-1:] // tmg).astype(jnp.int32)
    xs = _sc_dispatch(hp, pos0, pos1, rows)
    ys = _experts(xs, tile_expert, n_tiles, w_gate, w_up, w_down, tmg)
    return _sc_collect(ys, pos0, pos1)


def _ple_kernel(x1_ref, y0_ref, y1_ref, rt_ref, p_ref, g_ref, wg_ref, wp_ref, o_ref):
    for r0 in range(0, x1_ref.shape[0], PLE_SUB):
        rows = pl.ds(r0, PLE_SUB)
        rt = rt_ref[rows, :]
        y0 = jnp.concatenate(_unpack_bf16_pairs(y0_ref[rows, :]), axis=1)
        y1 = jnp.concatenate(_unpack_bf16_pairs(y1_ref[rows, :]), axis=1)
        x2 = x1_ref[rows, :] + rt[:, 2:3] * y0 + rt[:, 3:4] * y1
        gate = jax.nn.sigmoid(_dot(_rms(x2, g_ref[...]).astype(BF16), wg_ref[...]))
        o_ref[rows, :] = x2 + gate * _dot(p_ref[rows, :].astype(BF16), wp_ref[...])


def _ple_kernel_into(prev_ref, *refs):
    del prev_ref
    _ple_kernel(*refs)


def _ple(x1, y0, y1, rt, p2, g_ple, w_gate, w_proj, tm, row0, out_prev):
    n, d = x1.shape
    n_all, dp = p2.shape
    blk0 = row0 // tm
    full = lambda shape: pl.BlockSpec(shape, lambda i: (0,) * len(shape))
    row = lambda w: pl.BlockSpec((tm, w), lambda i: (i, 0))
    row_all = lambda w: pl.BlockSpec((tm, w), lambda i: (i + blk0, 0))
    in_specs = [row(d), row(d // 2), row(d // 2), row(LANES), row_all(dp), full((1, d)), full((d, d)), full((dp, d))]
    args = (x1, y0, y1, rt, p2, g_ple[None, :].astype(F32), w_gate.astype(BF16), w_proj.astype(BF16))
    if out_prev is not None:
        in_specs = [pl.BlockSpec(memory_space=pl.ANY)] + in_specs
        args = (out_prev,) + args
    return pl.pallas_call(
        _ple_kernel if out_prev is None else _ple_kernel_into,
        grid=(n // tm,),
        in_specs=in_specs,
        out_specs=row_all(d),
        out_shape=jax.ShapeDtypeStruct((n_all, d), F32),
        input_output_aliases={} if out_prev is None else {0: 0},
        compiler_params=pltpu.CompilerParams(dimension_semantics=("parallel",), vmem_limit_bytes=VMEM_LIMIT),
        name="ple",
    )(*args)


def _layer(x, p_l, g_attn, w_qkv, q_norm_na, k_norm_na, rpb_na, q_norm_dil, k_norm_dil, g_out_na, g_out_dil,
           w_o, g_ffn, w_rg, b_rg, w_re, b_re, w_exp_gate, w_exp_up, w_exp_down, g_ple, w_ple_gate, w_ple_proj):
    b, s, d = x.shape
    n = b * s
    half = d // 2
    assert d == N_HEADS * HEAD_DIM and half == N_HEADS_NA * HEAD_DIM
    tm = 512
    assert s % tm == 0
    x2 = x.reshape(n, d)
    qa, ka, va, qb, kb, vb = _qkv_proj(x2, g_attn, w_qkv, q_norm_na, k_norm_na, q_norm_dil, k_norm_dil, s, tm)
    seq = lambda t: t.reshape(b, s, half)
    oa = _na_attention(seq(qa), seq(ka), seq(va), rpb_na).reshape(n, half)
    ob = _dil_attention(seq(qb), seq(kb), seq(vb)).reshape(n, half)
    p2 = p_l.reshape(n, -1)
    chunk_unit = SC_CORES * SC_SUBCORES * SC_CHUNK
    n_chunks = TOKEN_CHUNKS if n % (TOKEN_CHUNKS * chunk_unit) == 0 else 1
    chunk = n // n_chunks
    assert chunk % ROW_TILE == 0
    out = None
    for c in range(n_chunks):
        x1, h, rt, rtt, cnt = _out_router(oa, ob, x2, g_out_na, g_out_dil, w_o, g_ffn, w_rg, b_rg, w_re, b_re,
                                          ROW_TILE, c * chunk, chunk)
        y0, y1 = _moe(h, rtt, cnt, w_exp_gate, w_exp_up, w_exp_down, 2 * tm, tm)
        out = _ple(x1, y0, y1, rt, p2, g_ple, w_ple_gate, w_ple_proj, ROW_TILE, c * chunk, out)
    return out.reshape(b, s, d)


def kernel(x, p, g_attn, w_qkv, q_norm_na, k_norm_na, rpb_na, q_norm_dil, k_norm_dil, g_out_na, g_out_dil, w_o,
           g_ffn, w_router_group, b_router_group, w_router_expert, b_router_expert, w_exp_gate, w_exp_up,
           w_exp_down, g_ple, w_ple_gate, w_ple_proj):
    for i in range(p.shape[0]):
        x = _layer(x, p[i], g_attn[i], w_qkv[i], q_norm_na[i], k_norm_na[i], rpb_na[i], q_norm_dil[i],
                   k_norm_dil[i], g_out_na[i], g_out_dil[i], w_o[i], g_ffn[i], w_router_group[i],
                   b_router_group[i], w_router_expert[i], b_router_expert[i], w_exp_gate[i], w_exp_up[i],
                   w_exp_down[i], g_ple[i], w_ple_gate[i], w_ple_proj[i])
    return x
```

```python
import functools

import numpy as np
import jax
import jax.numpy as jnp
from jax import lax
from jax.experimental import pallas as pl
from jax.experimental.pallas import tpu as pltpu
from jax.experimental.pallas import tpu_sc as plsc

HEAD_DIM = 64
N_HEADS = 16
N_HEADS_NA = 8
GRID_W = 64
NA_ROWS = 8
NA_COLS = 16
DIL_PAIRS = ((128, 1), (512, 4), (2048, 16))
ROPE_THETA = 10000.0
N_GROUPS = 4
EXPERTS_PER_GROUP = 8
N_EXPERTS = N_GROUPS * EXPERTS_PER_GROUP
EPS = 1e-6
NEG = -1e30
LOG2_E = 1.4426950408889634

LANES = 128
PAIR_W = 2 * HEAD_DIM
DIL_BLK = 128
ATTN_STEPS_PER_TRIP = 64
ATTN_SLOTS = 4
ROUTER_SUB = 512
PLE_SUB = 256
ROW_TILE = 1024
VMEM_LIMIT = 56 * 1024 * 1024
SC_CORES = 2
SC_SUBCORES = 16
SC_CHUNK = 128
EXPERT_LEAD = 3
EXPERT_SLOTS = 4
TOKEN_CHUNKS = 1

F32 = jnp.float32
BF16 = jnp.bfloat16


def _dot(a, b):
    return jnp.dot(a, b, preferred_element_type=F32)


def _dot_nt(a, b):
    return lax.dot_general(a, b, (((1,), (1,)), ((), ())), preferred_element_type=F32)


def _rms(x, gain):
    return x * lax.rsqrt(jnp.mean(x * x, axis=-1, keepdims=True) + EPS) * gain


def _lane_first_half(shape):
    return lax.broadcasted_iota(jnp.int32, shape, len(shape) - 1) < HEAD_DIM


def _qkv_kernel(x_ref, g_ref, w_ref, gq_na_ref, gk_na_ref, gq_dil_ref, gk_dil_ref, cos_ref, sin_ref,
                hsum_ref, qa_ref, ka_ref, va_ref, qb_ref, kb_ref, vb_ref, wb_ref):
    d = x_ref.shape[1]
    half = d // 2
    scale = HEAD_DIM ** -0.5 * LOG2_E

    @pl.when(pl.program_id(0) == 0)
    def _():
        wb_ref[...] = w_ref[...].astype(BF16)

    h = _rms(x_ref[...], g_ref[...]).astype(BF16)

    def proj(col):
        return _dot(h, wb_ref[:, col:col + half])

    def head_norm(y, gain):
        sq = (y * y).astype(BF16)
        w = hsum_ref.shape[0]
        ms = jnp.concatenate([_dot(sq[:, c:c + w], hsum_ref[...]) for c in range(0, half, w)], axis=1)
        return y * lax.rsqrt(ms + EPS) * gain

    def rope(y):
        lane = lax.broadcasted_iota(jnp.int32, (y.shape[0], LANES), 1)
        lower = (lane % HEAD_DIM) < HEAD_DIM // 2
        cos = cos_ref[...]
        sin = sin_ref[...]
        outs = []
        for c in range(0, half, LANES):
            yc = y[:, c:c + LANES]
            up = pltpu.roll(yc, LANES - HEAD_DIM // 2, axis=1)
            down = pltpu.roll(yc, HEAD_DIM // 2, axis=1)
            outs.append(yc * cos + jnp.where(lower, up, down) * sin)
        return jnp.concatenate(outs, axis=1)

    qa_ref[...] = (head_norm(proj(0), gq_na_ref[...]) * scale).astype(BF16)
    qb_ref[...] = (rope(head_norm(proj(half), gq_dil_ref[...])) * scale).astype(BF16)
    ka_ref[...] = head_norm(proj(d), gk_na_ref[...]).astype(BF16)
    kb_ref[...] = rope(head_norm(proj(d + half), gk_dil_ref[...])).astype(BF16)
    va_ref[...] = proj(2 * d).astype(BF16)
    vb_ref[...] = proj(2 * d + half).astype(BF16)


def _qkv_proj(x2, g_attn, w_qkv, gq_na, gk_na, gq_dil, gk_dil, seq, tm):
    n, d = x2.shape
    half = d // 2
    pos = jnp.arange(seq, dtype=F32)
    inv = ROPE_THETA ** (-jnp.arange(HEAD_DIM // 2, dtype=F32) / (HEAD_DIM // 2))
    ang = pos[:, None] * inv[None, :]
    cos = jnp.tile(jnp.cos(ang), (1, LANES // (HEAD_DIM // 2)))
    sin = jnp.tile(jnp.concatenate([-jnp.sin(ang), jnp.sin(ang)], axis=1), (1, LANES // HEAD_DIM))
    hs_w = 2 * LANES
    blk = np.arange(hs_w) // HEAD_DIM
    hsum = jnp.asarray((blk[:, None] == blk[None, :]).astype(np.float32) / HEAD_DIM, BF16)
    tile_gain = lambda g: jnp.tile(g.astype(F32), half // HEAD_DIM)[None, :]
    steps_per_seq = seq // tm
    full = lambda shape: pl.BlockSpec(shape, lambda i: (0,) * len(shape))
    out = jax.ShapeDtypeStruct((n, half), BF16)
    return pl.pallas_call(
        _qkv_kernel,
        grid=(n // tm,),
        in_specs=[
            pl.BlockSpec((tm, d), lambda i: (i, 0)),
            full((1, d)),
            pl.BlockSpec((d, 3 * d), lambda i: (0, 0), pipeline_mode=pl.Buffered(1)),
            full((1, half)), full((1, half)), full((1, half)), full((1, half)),
            pl.BlockSpec((tm, LANES), lambda i: (i % steps_per_seq, 0)),
            pl.BlockSpec((tm, LANES), lambda i: (i % steps_per_seq, 0)),
            full((hs_w, hs_w)),
        ],
        out_specs=[pl.BlockSpec((tm, half), lambda i: (i, 0))] * 6,
        out_shape=[out] * 6,
        scratch_shapes=[pltpu.VMEM((d, 3 * d), BF16)],
        compiler_params=pltpu.CompilerParams(dimension_semantics=("arbitrary",), vmem_limit_bytes=VMEM_LIMIT),
        name="qkv_proj",
    )(x2, g_attn[None, :].astype(F32), w_qkv, tile_gain(gq_na), tile_gain(gk_na),
      tile_gain(gq_dil), tile_gain(gk_dil), cos, sin, hsum)


def _is_static(x):
    return isinstance(x, int)


def _clip(x, lo, hi):
    return min(max(x, lo), hi) if _is_static(x) else jnp.clip(x, lo, hi)


def _aligned_ds(start, size, align):
    return pl.ds(start if _is_static(start) else pl.multiple_of(start, align), size)


def _software_pipeline(n_items, stages, steps_per_trip):
    depth = len(stages)
    assert steps_per_trip % ATTN_SLOTS == 0

    def step(t, phase, static):
        for k in reversed(range(depth)):
            if static and not 0 <= t - k < n_items:
                continue
            stages[k](t - k, (phase - k) % ATTN_SLOTS)

    first_full = -(-(depth - 1) // ATTN_SLOTS) * ATTN_SLOTS
    trips = max(n_items - first_full, 0) // steps_per_trip
    if trips < 2:
        trips = 0
    looped_end = first_full + trips * steps_per_trip if trips else 0
    for t in range(first_full if trips else 0):
        step(t, t % ATTN_SLOTS, True)
    if trips:
        def body(i, carry):
            for j in range(steps_per_trip):
                step(first_full + i * steps_per_trip + j, j % ATTN_SLOTS, False)
            return carry

        lax.fori_loop(0, trips, body, 0)
    for t in range(looped_end, n_items + depth - 1):
        step(t, t % ATTN_SLOTS, True)


def _qk_stage(q, kwin, bias_a, bias_b, s_ref, slot, first):
    m = q.shape[0]
    w = kwin.shape[0]
    zero = jnp.zeros_like(q)
    s = _dot_nt(jnp.concatenate([jnp.where(first, q, zero), jnp.where(first, zero, q)], axis=0), kwin)
    s_ref[slot, :m, :w] = s[:m] + bias_a
    s_ref[slot, m:, :w] = s[m:] + bias_b


def _softmax_stage(s_ref, p_ref, slot, w):
    m = jnp.max(s_ref[slot, :, :w], axis=-1, keepdims=True)
    p_ref[slot, :, :w] = jnp.exp2(s_ref[slot, :, :w] - m).astype(BF16)
    return m


def _pv_stage(p_ref, slot, v_win, first):
    w = v_win.shape[0]
    r = _dot(p_ref[slot, :, :w], v_win)
    m = r.shape[0] // 2
    return jnp.where(first, r[:m, :PAIR_W], r[m:, :PAIR_W]), jnp.where(first, r[:m, PAIR_W:], r[m:, PAIR_W:])


def _with_ones(v):
    return jnp.concatenate([v, jnp.ones_like(v)], axis=1)


def _na_kernel(q_ref, k_ref, v_ref, bias_ref, o_ref, v1_ref, s_ref, p_ref):
    rows = q_ref.shape[0] // GRID_W
    win = NA_ROWS * GRID_W
    v1_ref[...] = _with_ones(v_ref[...])
    first = _lane_first_half((GRID_W, PAIR_W))

    def slices(r):
        rs = _clip(r - NA_ROWS // 2, 0, rows - NA_ROWS)
        return _aligned_ds(r * GRID_W, GRID_W, GRID_W), _aligned_ds(rs * GRID_W, win, GRID_W), r - rs

    def bias(head, delta):
        return jnp.concatenate([bias_ref[head, j - delta + NA_ROWS - 1] for j in range(0, NA_ROWS, 2)], axis=1)

    def qk(r, slot):
        qs, ks, delta = slices(r)
        _qk_stage(q_ref[qs, :], k_ref[ks, :], bias(0, delta), bias(1, delta), s_ref, slot, first)

    def softmax(r, slot):
        _softmax_stage(s_ref, p_ref, slot, win)

    def pv(r, slot):
        qs, ks, _ = slices(r)
        num, den = _pv_stage(p_ref, slot, v1_ref[ks, :], first)
        o_ref[qs, :] = (num / den).astype(o_ref.dtype)

    _software_pipeline(rows, (qk, softmax, pv), ATTN_STEPS_PER_TRIP)


def _na_bias_table(rpb):
    w = np.arange(GRID_W)
    cs = np.clip(w - NA_COLS // 2, 0, GRID_W - NA_COLS)
    kc = np.arange(GRID_W)
    valid = (kc[None, :] >= cs[:, None]) & (kc[None, :] < cs[:, None] + NA_COLS)
    coff = np.clip(kc[None, :] - w[:, None] + NA_COLS - 1, 0, 2 * NA_COLS - 2)
    tab = jnp.where(valid[None, None], rpb.astype(F32)[:, :, coff] * LOG2_E, NEG)
    return jnp.concatenate([tab[:, :-1], tab[:, 1:]], axis=-1)


def _na_attention(q, k, v, rpb):
    b, s, width = q.shape
    pairs = width // PAIR_W
    assert s % GRID_W == 0 and s // GRID_W >= NA_ROWS
    bias = _na_bias_table(rpb).reshape(pairs, 2, 2 * NA_ROWS - 2, GRID_W, 2 * GRID_W)
    qkv_spec = pl.BlockSpec((None, s, PAIR_W), lambda bi, j: (bi, 0, j))
    return pl.pallas_call(
        _na_kernel,
        grid=(b, pairs),
        in_specs=[qkv_spec, qkv_spec, qkv_spec,
                  pl.BlockSpec((None, 2, 2 * NA_ROWS - 2, GRID_W, 2 * GRID_W), lambda bi, j: (j, 0, 0, 0, 0))],
        out_specs=pl.BlockSpec((None, s, PAIR_W), lambda bi, j: (bi, 0, j)),
        out_shape=jax.ShapeDtypeStruct((b, s, width), BF16),
        scratch_shapes=[pltpu.VMEM((s, 2 * PAIR_W), BF16),
                        pltpu.VMEM((ATTN_SLOTS, 2 * GRID_W, NA_ROWS * GRID_W), F32),
                        pltpu.VMEM((ATTN_SLOTS, 2 * GRID_W, NA_ROWS * GRID_W), BF16)],
        compiler_params=pltpu.CompilerParams(dimension_semantics=("parallel", "parallel"),
                                             vmem_limit_bytes=VMEM_LIMIT),
        name="na_attn",
    )(q, k, v, bias)


def _dil_kernel(q_ref, k_ref, v_ref, mwide_ref, mfull_ref, o_ref,
                f32a_ref, f32b_ref, qc_ref, kc_ref, vc_ref, acc_ref, den_ref, max_ref, s_ref, p_ref):
    s = q_ref.shape[0]
    for i, ref in enumerate((q_ref, k_ref, v_ref)):
        f32a_ref[i] = ref[...].astype(F32)
    first = _lane_first_half((DIL_BLK, PAIR_W))
    prev_ref, next_ref, prev_dil = f32a_ref, f32b_ref, 1

    for p, (window, dil) in enumerate(DIL_PAIRS):
        radius = window // (2 * dil)
        cls_len = s // dil
        nblk = cls_len // DIL_BLK
        wide = cls_len >= 2 * DIL_BLK
        win = 2 * DIL_BLK if wide else cls_len
        assert radius == DIL_BLK // 2 and cls_len % DIL_BLK == 0 and dil % prev_dil == 0

        ratio = dil // prev_dil
        keep = ratio > 1 and p + 1 < len(DIL_PAIRS)
        for c in range(dil):
            src = pl.ds((c % prev_dil) * (s // prev_dil) + c // prev_dil, cls_len, stride=ratio)
            dst = pl.ds(c * cls_len, cls_len)
            vals = [prev_ref[i, src, :] for i in range(3)]
            if keep:
                for i in range(3):
                    next_ref[i, dst, :] = vals[i]
            qc_ref[dst, :] = vals[0].astype(BF16)
            kc_ref[dst, :] = vals[1].astype(BF16)
            vc_ref[dst, :] = _with_ones(vals[2].astype(BF16))
        if keep:
            prev_ref, next_ref, prev_dil = next_ref, prev_ref, dil

        def slices(n):
            c, i = divmod(n, nblk) if _is_static(n) else (n // nblk, n % nblk)
            base = c * cls_len
            ws = _clip(i * DIL_BLK - radius, 0, cls_len - win)
            if _is_static(i):
                kind = 0 if i == 0 else (2 if i == nblk - 1 else 1)
            else:
                kind = jnp.where(i == 0, 0, jnp.where(i == nblk - 1, 2, 1))
            return (_aligned_ds(base + i * DIL_BLK, DIL_BLK, DIL_BLK), _aligned_ds(base + ws, win, radius), kind,
                    pl.ds(c + dil * DIL_BLK * i, DIL_BLK, stride=dil))

        def qk(n, slot):
            qs, ks, kind, _ = slices(n)
            mask = mwide_ref[kind] if wide else mfull_ref[...]
            _qk_stage(qc_ref[qs, :], kc_ref[ks, :], mask, mask, s_ref, slot, first)

        def softmax(n, slot):
            m = _softmax_stage(s_ref, p_ref, slot, win)
            max_ref[p, slices(n)[3], :] = jnp.where(first, m[:DIL_BLK], m[DIL_BLK:])

        def pv(n, slot):
            _, ks, _, tok = slices(n)
            num, den = _pv_stage(p_ref, slot, vc_ref[ks, :], first)
            acc_ref[p, tok, :] = num
            den_ref[p, tok, :] = den

        _software_pipeline(dil * nblk, (qk, softmax, pv), ATTN_STEPS_PER_TRIP)

    m = jnp.maximum(jnp.maximum(max_ref[0], max_ref[1]), max_ref[2])
    num = jnp.zeros_like(m)
    den = jnp.zeros_like(m)
    for p in range(len(DIL_PAIRS)):
        w = jnp.exp2(max_ref[p] - m)
        num = num + w * acc_ref[p]
        den = den + w * den_ref[p]
    o_ref[...] = (num / den).astype(o_ref.dtype)


def _band_mask(kind):
    radius = DIL_BLK // 2
    qq = np.arange(DIL_BLK)[:, None]
    if kind == "full":
        kk = np.arange(DIL_BLK)[None, :]
        shift = 0
    else:
        kk = np.arange(2 * DIL_BLK)[None, :]
        shift = {"first": 0, "inner": radius, "last": DIL_BLK}[kind]
    return np.where(np.abs(kk - qq - shift) <= radius, 0.0, NEG).astype(np.float32)


def _dil_attention(q, k, v):
    b, s, width = q.shape
    pairs = width // PAIR_W
    for window, dil in DIL_PAIRS:
        assert s % (window // 2) == 0 and (s // dil) % DIL_BLK == 0
    mwide = jnp.asarray(np.stack([_band_mask("first"), _band_mask("inner"), _band_mask("last")]))
    mfull = jnp.asarray(_band_mask("full"))
    qkv_spec = pl.BlockSpec((None, s, PAIR_W), lambda bi, j: (bi, 0, j))
    npat = len(DIL_PAIRS)
    return pl.pallas_call(
        _dil_kernel,
        grid=(b, pairs),
        in_specs=[qkv_spec, qkv_spec, qkv_spec,
                  pl.BlockSpec(mwide.shape, lambda bi, j: (0, 0, 0)),
                  pl.BlockSpec(mfull.shape, lambda bi, j: (0, 0))],
        out_specs=pl.BlockSpec((None, s, PAIR_W), lambda bi, j: (bi, 0, j)),
        out_shape=jax.ShapeDtypeStruct((b, s, width), BF16),
        scratch_shapes=[pltpu.VMEM((3, s, PAIR_W), F32)] * 2 + [pltpu.VMEM((s, PAIR_W), BF16)] * 2
        + [pltpu.VMEM((s, 2 * PAIR_W), BF16)] + [pltpu.VMEM((npat, s, PAIR_W), F32)] * 3
        + [pltpu.VMEM((ATTN_SLOTS, 2 * DIL_BLK, 2 * DIL_BLK), F32),
           pltpu.VMEM((ATTN_SLOTS, 2 * DIL_BLK, 2 * DIL_BLK), BF16)],
        compiler_params=pltpu.CompilerParams(dimension_semantics=("parallel", "parallel"),
                                             vmem_limit_bytes=VMEM_LIMIT),
        name="dil_attn",
    )(q, k, v, mwide, mfull)


def _split_bf16(x):
    hi = x.astype(BF16)
    return hi, (x - hi.astype(F32)).astype(BF16)


def _route(logits):
    lane = lax.broadcasted_iota(jnp.int32, logits.shape, 1)
    ninf = jnp.float32(-jnp.inf)

    def first_argmax(vals, vmax):
        return jnp.min(jnp.where(vals == vmax, lane, LANES), axis=-1, keepdims=True)

    gl = jnp.where(lane < N_GROUPS, logits, ninf)
    gmax = jnp.max(gl, axis=-1, keepdims=True)
    gsel = first_argmax(gl, gmax)
    gw = 1.0 / jnp.sum(jnp.exp(gl - gmax), axis=-1, keepdims=True)
    lo = N_GROUPS + EXPERTS_PER_GROUP * gsel
    el = jnp.where((lane >= lo) & (lane < lo + EXPERTS_PER_GROUP), logits, ninf)
    v0 = jnp.max(el, axis=-1, keepdims=True)
    i0 = first_argmax(el, v0)
    el = jnp.where(lane == i0, ninf, el)
    v1 = jnp.max(el, axis=-1, keepdims=True)
    i1 = first_argmax(el, v1)
    t = jnp.exp(v1 - v0)
    w0 = gw / (1.0 + t)
    w1 = gw * t / (1.0 + t)
    e0 = (i0 - N_GROUPS).astype(F32)
    e1 = (i1 - N_GROUPS).astype(F32)
    return jnp.where(lane == 0, e0, jnp.where(lane == 1, e1, jnp.where(lane == 2, w0, jnp.where(lane == 3, w1, 0.0))))


def _out_router_kernel(oa_ref, ob_ref, x_ref, ga_ref, gb_ref, wo_ref, gf_ref, wr_ref, br_ref,
                       x1_ref, h_ref, rt_ref, rtt_ref, cnt_ref):
    half = oa_ref.shape[1]

    @pl.when(pl.program_id(0) == 0)
    def _():
        cnt_ref[...] = jnp.zeros_like(cnt_ref)

    for r0 in range(0, oa_ref.shape[0], ROUTER_SUB):
        rows = pl.ds(r0, ROUTER_SUB)
        ya = _rms(oa_ref[rows, :].astype(F32), ga_ref[...]).astype(BF16)
        yb = _rms(ob_ref[rows, :].astype(F32), gb_ref[...]).astype(BF16)
        x1 = x_ref[rows, :] + _dot(ya, wo_ref[:half, :]) + _dot(yb, wo_ref[half:, :])
        x1_ref[rows, :] = x1
        h = _rms(x1, gf_ref[...])
        h_ref[rows, :] = _pack_bf16_pairs(h)
        h_hi, h_lo = _split_bf16(h)
        both = _dot(h_hi, wr_ref[...])
        logits = both[:, :LANES] + both[:, LANES:] + _dot(h_lo, wr_ref[:, :LANES]) + br_ref[...]
        rt = _route(logits)
        rt_ref[rows, :] = rt
        rtt = rt.T[:rtt_ref.shape[0], :]
        rtt_ref[:, rows] = rtt
        oh0, oh1 = _slot_one_hots(rtt)
        cnt_ref[...] += jnp.sum(oh0 + oh1, axis=1, keepdims=True)


def _out_router(oa, ob, x2, g_na, g_dil, w_o, g_ffn, w_rg, b_rg, w_re, b_re, tm, row0, n):
    d = x2.shape[1]
    half = d // 2
    blk0 = row0 // tm
    wr = jnp.concatenate([w_rg.astype(F32), w_re.astype(F32).transpose(1, 0, 2).reshape(d, N_EXPERTS)], axis=1)
    wr = jnp.pad(wr, ((0, 0), (0, LANES - wr.shape[1])))
    wr_hi = wr.astype(BF16)
    wr_cat = jnp.concatenate([wr_hi, (wr - wr_hi.astype(F32)).astype(BF16)], axis=1)
    br =jnp.pad(jnp.concatenate([b_rg.astype(F32), b_re.astype(F32).reshape(-1)]), (0, LANES - N_GROUPS - N_EXPERTS))
    full = lambda shape: pl.BlockSpec(shape, lambda i: (0,) * len(shape))
    row = lambda w: pl.BlockSpec((tm, w), lambda i: (i, 0))
    row_in = lambda w: pl.BlockSpec((tm, w), lambda i: (i + blk0, 0))
    return pl.pallas_call(
        _out_router_kernel,
        grid=(n // tm,),
        in_specs=[row_in(half), row_in(half), row_in(d), full((1, half)), full((1, half)), full((d, d)),
                  full((1, d)), full((d, 2 * LANES)), full((1, LANES))],
        out_specs=[row(d), row(half), row(LANES), pl.BlockSpec((8, tm), lambda i: (0, i)), full((LANES, LANES))],
        out_shape=[jax.ShapeDtypeStruct((n, d), F32), jax.ShapeDtypeStruct((n, half), jnp.uint32),
                   jax.ShapeDtypeStruct((n, LANES), F32), jax.ShapeDtypeStruct((8, n), F32),
                   jax.ShapeDtypeStruct((LANES, LANES), F32)],
        compiler_params=pltpu.CompilerParams(dimension_semantics=("arbitrary",), vmem_limit_bytes=VMEM_LIMIT),
        name="out_router",
    )(oa, ob, x2, g_na[None, :].astype(F32), g_dil[None, :].astype(F32), w_o.astype(BF16),
      g_ffn[None, :].astype(F32), wr_cat, br[None, :])


def _pack_bf16_pairs(x):
    w = x.shape[1] // 2
    bits = lax.bitcast_convert_type(x.astype(BF16).astype(F32), jnp.uint32)
    return bits[:, :w] | (bits[:, w:] >> 16)


def _unpack_bf16_pairs(u):
    hi = lax.bitcast_convert_type(u & jnp.uint32(0xFFFF0000), F32)
    lo = lax.bitcast_convert_type(u << 16, F32)
    return hi, lo


def _slot_one_hots(rtt):
    sub = lax.broadcasted_iota(jnp.int32, (LANES, rtt.shape[1]), 0).astype(F32)
    return (sub == rtt[0:1, :]).astype(F32), (sub == rtt[1:2, :]).astype(F32)


def _position_kernel(rtt_ref, start_ref, pos_ref, base_ref):
    tm = rtt_ref.shape[1]

    @pl.when(pl.program_id(0) == 0)
    def _():
        base_ref[...] = start_ref[...]

    oh0, oh1 = _slot_one_hots(rtt_ref[...])
    oh = oh0 + oh1
    earlier = lax.broadcasted_iota(jnp.int32, (tm, tm), 0) < lax.broadcasted_iota(jnp.int32, (tm, tm), 1)
    before = _dot(oh.astype(BF16), earlier.astype(BF16)) + base_ref[:, 0:1]
    p0 = jnp.sum(before * oh0, axis=0, keepdims=True)
    p1 = jnp.sum(before * oh1, axis=0, keepdims=True)
    row = lax.broadcasted_iota(jnp.int32, pos_ref.shape, 0)
    pos_ref[...] = jnp.where(row == 0, p0, jnp.where(row == 1, p1, 0.0)).astype(jnp.int32)
    base_ref[...] += jnp.sum(oh, axis=1, keepdims=True)


def _expert_positions(rtt, starts, tm):
    n = rtt.shape[1]
    return pl.pallas_call(
        _position_kernel,
        grid=(n // tm,),
        in_specs=[pl.BlockSpec((8, tm), lambda i: (0, i)), pl.BlockSpec((LANES, LANES), lambda i: (0, 0))],
        out_specs=pl.BlockSpec((8, tm), lambda i: (0, i)),
        out_shape=jax.ShapeDtypeStruct((8, n), jnp.int32),
        scratch_shapes=[pltpu.VMEM((LANES, LANES), F32)],
        compiler_params=pltpu.CompilerParams(dimension_semantics=("arbitrary",)),
        name="expert_positions",
    )(rtt, starts)


def _sc_mesh():
    return plsc.VectorSubcoreMesh(core_axis_name="c", subcore_axis_name="s",
                                  num_cores=SC_CORES, num_subcores=SC_SUBCORES)


def _sc_dispatch(hp, pos0, pos1, n_out):
    n, w = hp.shape
    workers = SC_CORES * SC_SUBCORES
    per = n // workers
    chunks = per // SC_CHUNK
    assert n % (workers * SC_CHUNK) == 0

    @functools.partial(
        pl.kernel, out_type=jax.ShapeDtypeStruct((n_out, w), hp.dtype), mesh=_sc_mesh(),
        scratch_types=[pltpu.VMEM((chunks, SC_CHUNK), jnp.int32), pltpu.VMEM((chunks, SC_CHUNK), jnp.int32),
                       pltpu.VMEM((SC_CHUNK, w), hp.dtype)],
        name="moe_dispatch")
    def body(h_hbm, p0_hbm, p1_hbm, xs_hbm, i0_v, i1_v, rows_v):
        wid = lax.axis_index("s") * SC_CORES + lax.axis_index("c")
        pltpu.sync_copy(p0_hbm.at[wid], i0_v)
        pltpu.sync_copy(p1_hbm.at[wid], i1_v)

        @pl.loop(0, chunks)
        def _(j):
            pltpu.sync_copy(h_hbm.at[pl.ds(wid * per + j * SC_CHUNK, SC_CHUNK)], rows_v)
            pltpu.sync_copy(rows_v, xs_hbm.at[i0_v.at[j]])
            pltpu.sync_copy(rows_v, xs_hbm.at[i1_v.at[j]])

    return body(hp, pos0.reshape(workers, chunks, SC_CHUNK), pos1.reshape(workers, chunks, SC_CHUNK))


def _sc_collect(ys, pos0, pos1):
    n = pos0.shape[0]
    w = ys.shape[1]
    workers = SC_CORES * SC_SUBCORES
    per = n // workers
    chunks = per // SC_CHUNK
    out = jax.ShapeDtypeStruct((n, w), ys.dtype)

    @functools.partial(
        pl.kernel, out_type=(out, out), mesh=_sc_mesh(),
        scratch_types=[pltpu.VMEM((chunks, SC_CHUNK), jnp.int32), pltpu.VMEM((chunks, SC_CHUNK), jnp.int32),
                       pltpu.VMEM((SC_CHUNK, w), ys.dtype)],
        name="moe_collect")
    def body(ys_hbm, p0_hbm, p1_hbm, y0_hbm, y1_hbm, i0_v, i1_v, rows_v):
        wid = lax.axis_index("s") * SC_CORES + lax.axis_index("c")
        pltpu.sync_copy(p0_hbm.at[wid], i0_v)
        pltpu.sync_copy(p1_hbm.at[wid], i1_v)

        @pl.loop(0, chunks)
        def _(j):
            dst = pl.ds(wid * per + j * SC_CHUNK, SC_CHUNK)
            pltpu.sync_copy(ys_hbm.at[i0_v.at[j]], rows_v)
            pltpu.sync_copy(rows_v, y0_hbm.at[dst])
            pltpu.sync_copy(ys_hbm.at[i1_v.at[j]], rows_v)
            pltpu.sync_copy(rows_v, y1_hbm.at[dst])

    return body(ys, pos0.reshape(workers, chunks, SC_CHUNK), pos1.reshape(workers, chunks, SC_CHUNK))


def _experts_kernel(plan_ref, nt_ref, xs_ref, wg_ref, wu_ref, wd_ref, ys_ref, wg_bf, wu_bf, wd_bf):
    g = pl.program_id(0)
    prev = jnp.maximum(g - 1, 0)

    for row, (src, dst) in enumerate(((wg_ref, wg_bf), (wu_ref, wu_bf), (wd_ref, wd_bf))):
        @pl.when((g == 0) | (plan_ref[row, g] != plan_ref[row, prev]))
        def _(row=row, src=src, dst=dst):
            dst[plan_ref[row + 3, g]] = src[...].astype(BF16)

    @pl.when((g >= EXPERT_LEAD) & (g < nt_ref[0] + EXPERT_LEAD))
    def _():
        slot = plan_ref[6, g]
        half = wg_bf.shape[1] // 2
        hi, lo = _unpack_bf16_pairs(xs_ref[...])
        hi = hi.astype(BF16)
        lo = lo.astype(BF16)
        a = _dot(hi, wg_bf[slot, :half, :]) + _dot(lo, wg_bf[slot, half:, :])
        u = _dot(hi, wu_bf[slot, :half, :]) + _dot(lo, wu_bf[slot, half:, :])
        act = (a * jax.nn.sigmoid(a) * u).astype(BF16)
        ys_ref[...] = _pack_bf16_pairs(_dot(act, wd_bf[slot]))


def _experts(xs, tile_expert, n_tiles, w_gate, w_up, w_down, tmg):
    rows, w = xs.shape
    ne, d, de = w_gate.shape
    steps = rows // tmg + EXPERT_LEAD
    g = jnp.arange(steps, dtype=jnp.int32)
    run = jnp.concatenate([jnp.zeros((1,), jnp.int32),
                           jnp.cumsum((tile_expert[1:] != tile_expert[:-1]).astype(jnp.int32))])
    tile_at = lambda lag: jnp.clip(g - lag, 0, n_tiles[0] - 1)
    plan = jnp.stack([tile_expert[tile_at(lag)] for lag in range(EXPERT_LEAD)]
                     + [run[tile_at(lag)] % EXPERT_SLOTS for lag in range(EXPERT_LEAD + 1)]
                     + [tile_at(EXPERT_LEAD)]).astype(jnp.int32)
    return pl.pallas_call(
        _experts_kernel,
        grid_spec=pltpu.PrefetchScalarGridSpec(
            num_scalar_prefetch=2,
            grid=(steps,),
            in_specs=[pl.BlockSpec((tmg, w), lambda s, plan, nt: (plan[2 * EXPERT_LEAD + 1, s], 0)),
                      pl.BlockSpec((None, d, de), lambda s, plan, nt: (plan[0, s], 0, 0)),
                      pl.BlockSpec((None, d, de), lambda s, plan, nt: (plan[1, s], 0, 0)),
                      pl.BlockSpec((None, de, d), lambda s, plan, nt: (plan[2, s], 0, 0))],
            out_specs=pl.BlockSpec((tmg, w), lambda s, plan, nt: (plan[2 * EXPERT_LEAD + 1, s], 0)),
            scratch_shapes=[pltpu.VMEM((EXPERT_SLOTS, d, de), BF16), pltpu.VMEM((EXPERT_SLOTS, d, de), BF16),
                            pltpu.VMEM((EXPERT_SLOTS, de, d), BF16)],
        ),
        out_shape=jax.ShapeDtypeStruct((rows, w), jnp.uint32),
        compiler_params=pltpu.CompilerParams(dimension_semantics=("arbitrary",), vmem_limit_bytes=VMEM_LIMIT),
        name="experts",
    )(plan, n_tiles, xs, w_gate, w_up, w_down)


def _moe(hp, rtt, cnt, w_gate, w_up, w_down, tm, tmg):
    n = hp.shape[0]
    ne = w_gate.shape[0]
    counts = cnt[:ne, 0].astype(jnp.int32)
    padded = (counts + tmg - 1) // tmg * tmg
    ends = jnp.cumsum(padded)
    starts = jnp.pad((ends - padded).astype(F32), (0, LANES - ne))
    pos = _expert_positions(rtt, jnp.broadcast_to(starts[:, None], (LANES, LANES)), tm)
    pos0, pos1 = pos[0], pos[1]
    rows = 2 * n + ne * tmg
    tile_start = jnp.arange(rows // tmg, dtype=jnp.int32) * tmg
    tile_expert = jnp.minimum(jnp.sum(tile_start[:, None] >= ends[None, :], axis=1), ne - 1).astype(jnp.int32)
    n_tiles = (ends[-1:] // tmg).astype(jnp.int32)
    xs = _sc_dispatch(hp, pos0, pos1, rows)
    ys = _experts(xs, tile_expert, n_tiles, w_gate, w_up, w_down, tmg)
    return _sc_collect(ys, pos0, pos1)


def _ple_kernel(x1_ref, y0_ref, y1_ref, rt_ref, p_ref, g_ref, wg_ref, wp_ref, o_ref):
    for r0 in range(0, x1_ref.shape[0], PLE_SUB):
        rows = pl.ds(r0, PLE_SUB)
        rt = rt_ref[rows, :]
        y0 = jnp.concatenate(_unpack_bf16_pairs(y0_ref[rows, :]), axis=1)
        y1 = jnp.concatenate(_unpack_bf16_pairs(y1_ref[rows, :]), axis=1)
        x2 = x1_ref[rows, :] + rt[:, 2:3] * y0 + rt[:, 3:4] * y1
        gate = jax.nn.sigmoid(_dot(_rms(x2, g_ref[...]).astype(BF16), wg_ref[...]))
        o_ref[rows, :] = x2 + gate * _dot(p_ref[rows, :].astype(BF16), wp_ref[...])


def _ple_kernel_into(prev_ref, *refs):
    del prev_ref
    _ple_kernel(*refs)


def _ple(x1, y0, y1, rt, p2, g_ple, w_gate, w_proj, tm, row0, out_prev):
    n, d = x1.shape
    n_all, dp = p2.shape
    blk0 = row0 // tm
    full = lambda shape: pl.BlockSpec(shape, lambda i: (0,) * len(shape))
    row = lambda w: pl.BlockSpec((tm, w), lambda i: (i, 0))
    row_all = lambda w: pl.BlockSpec((tm, w), lambda i: (i + blk0, 0))
    in_specs = [row(d), row(d // 2), row(d // 2), row(LANES), row_all(dp), full((1, d)), full((d, d)), full((dp, d))]
    args = (x1, y0, y1, rt, p2, g_ple[None, :].astype(F32), w_gate.astype(BF16), w_proj.astype(BF16))
    if out_prev is not None:
        in_specs = [pl.BlockSpec(memory_space=pl.ANY)] + in_specs
        args = (out_prev,) + args
    return pl.pallas_call(
        _ple_kernel if out_prev is None else _ple_kernel_into,
        grid=(n // tm,),
        in_specs=in_specs,
        out_specs=row_all(d),
        out_shape=jax.ShapeDtypeStruct((n_all, d), F32),
        input_output_aliases={} if out_prev is None else {0: 0},
        compiler_params=pltpu.CompilerParams(dimension_semantics=("parallel",), vmem_limit_bytes=VMEM_LIMIT),
        name="ple",
    )(*args)


def _layer(x, p_l, g_attn, w_qkv, q_norm_na, k_norm_na, rpb_na, q_norm_dil, k_norm_dil, g_out_na, g_out_dil,
           w_o, g_ffn, w_rg, b_rg, w_re, b_re, w_exp_gate, w_exp_up, w_exp_down, g_ple, w_ple_gate, w_ple_proj):
    b, s, d = x.shape
    n = b * s
    half = d // 2
    assert d == N_HEADS * HEAD_DIM and half == N_HEADS_NA * HEAD_DIM
    tm = 512
    assert s % tm == 0
    x2 = x.reshape(n, d)
    qa, ka, va, qb, kb, vb = _qkv_proj(x2, g_attn, w_qkv, q_norm_na, k_norm_na, q_norm_dil, k_norm_dil, s, tm)
    seq = lambda t: t.reshape(b, s, half)
    oa = _na_attention(seq(qa), seq(ka), seq(va), rpb_na).reshape(n, half)
    ob = _dil_attention(seq(qb), seq(kb), seq(vb)).reshape(n, half)
    p2 = p_l.reshape(n, -1)
    chunk_unit = SC_CORES * SC_SUBCORES * SC_CHUNK
    n_chunks = TOKEN_CHUNKS if n % (TOKEN_CHUNKS * chunk_unit) == 0 else 1
    chunk = n // n_chunks
    assert chunk % ROW_TILE == 0
    out = None
    for c in range(n_chunks):
        x1, h, rt, rtt, cnt = _out_router(oa, ob, x2, g_out_na, g_out_dil, w_o, g_ffn, w_rg, b_rg, w_re, b_re,
                                          ROW_TILE, c * chunk, chunk)
        y0, y1 = _moe(h, rtt, cnt, w_exp_gate, w_exp_up, w_exp_down, 2 * tm, tm)
        out = _ple(x1, y0, y1, rt, p2, g_ple, w_ple_gate, w_ple_proj, ROW_TILE, c * chunk, out)
    return out.reshape(b, s, d)


def kernel(x, p, g_attn, w_qkv, q_norm_na, k_norm_na, rpb_na, q_norm_dil, k_norm_dil, g_out_na, g_out_dil, w_o,
           g_ffn, w_router_group, b_router_group, w_router_expert, b_router_expert, w_exp_gate, w_exp_up,
           w_exp_down, g_ple, w_ple_gate, w_ple_proj):
    for i in range(p.shape[0]):
        x = _layer(x, p[i], g_attn[i], w_qkv[i], q_norm_na[i], k_norm_na[i], rpb_na[i], q_norm_dil[i],
                   k_norm_dil[i], g_out_na[i], g_out_dil[i], w_o[i], g_ffn[i], w_router_group[i],
                   b_router_group[i], w_router_expert[i], b_router_expert[i], w_exp_gate[i], w_exp_up[i],
                   w_exp_down[i], g_ple[i], w_ple_gate[i], w_ple_proj[i])
    return x
```

```python
import functools

import numpy as np
import jax
import jax.numpy as jnp
from jax import lax
from jax.experimental import pallas as pl
from jax.experimental.pallas import tpu as pltpu
from jax.experimental.pallas import tpu_sc as plsc

HEAD_DIM = 64
N_HEADS = 16
N_HEADS_NA = 8
GRID_W = 64
NA_ROWS = 8
NA_COLS = 16
DIL_PAIRS = ((128, 1), (512, 4), (2048, 16))
ROPE_THETA = 10000.0
N_GROUPS = 4
EXPERTS_PER_GROUP = 8
N_EXPERTS = N_GROUPS * EXPERTS_PER_GROUP
EPS = 1e-6
NEG = -1e30
LOG2_E = 1.4426950408889634

LANES = 128
PAIR_W = 2 * HEAD_DIM
DIL_BLK = 128
ATTN_STEPS_PER_TRIP = 64
ATTN_SLOTS = 4
ATTN_STAGE_LAG = 2
ROUTER_SUB = 512
PLE_SUB = 256
ROW_TILE = 1024
QKV_TILE = 1024
VMEM_LIMIT = 56 * 1024 * 1024
SC_CORES = 2
SC_SUBCORES = 16
SC_CHUNK = 128
EXPERT_LEAD = 3
EXPERT_SLOTS = 4
TOKEN_CHUNKS = 2

F32 = jnp.float32
BF16 = jnp.bfloat16


def _dot(a, b):
    return jnp.dot(a, b, preferred_element_type=F32)


def _dot_nt(a, b):
    return lax.dot_general(a, b, (((1,), (1,)), ((), ())), preferred_element_type=F32)


def _rms(x, gain):
    return x * lax.rsqrt(jnp.mean(x * x, axis=-1, keepdims=True) + EPS) * gain


def _lane_first_half(shape):
    return lax.broadcasted_iota(jnp.int32, shape, len(shape) - 1) < HEAD_DIM


def _qkv_kernel(x_ref, g_ref, w_ref, gq_na_ref, gk_na_ref, gq_dil_ref, gk_dil_ref, cos_ref, sin_ref,
                hsum_ref, qa_ref, ka_ref, va_ref, qb_ref, kb_ref, vb_ref, wb_ref):
    d = x_ref.shape[1]
    half = d // 2
    scale = HEAD_DIM ** -0.5 * LOG2_E

    @pl.when(pl.program_id(0) == 0)
    def _():
        wb_ref[...] = w_ref[...].astype(BF16)

    h = _rms(x_ref[...], g_ref[...]).astype(BF16)

    def proj(col):
        return _dot(h, wb_ref[:, col:col + half])

    def head_norm(y, gain):
        sq = (y * y).astype(BF16)
        w = hsum_ref.shape[0]
        ms = jnp.concatenate([_dot(sq[:, c:c + w], hsum_ref[...]) for c in range(0, half, w)], axis=1)
        return y * lax.rsqrt(ms + EPS) * gain

    def rope(y):
        lane = lax.broadcasted_iota(jnp.int32, (y.shape[0], LANES), 1)
        lower = (lane % HEAD_DIM) < HEAD_DIM // 2
        cos = cos_ref[...]
        sin = sin_ref[...]
        outs = []
        for c in range(0, half, LANES):
            yc = y[:, c:c + LANES]
            up = pltpu.roll(yc, LANES - HEAD_DIM // 2, axis=1)
            down = pltpu.roll(yc, HEAD_DIM // 2, axis=1)
            outs.append(yc * cos + jnp.where(lower, up, down) * sin)
        return jnp.concatenate(outs, axis=1)

    qa_ref[...] = (head_norm(proj(0), gq_na_ref[...]) * scale).astype(BF16)
    qb_ref[...] = (rope(head_norm(proj(half), gq_dil_ref[...])) * scale).astype(BF16)
    ka_ref[...] = head_norm(proj(d), gk_na_ref[...]).astype(BF16)
    kb_ref[...] = rope(head_norm(proj(d + half), gk_dil_ref[...])).astype(BF16)
    va_ref[...] = proj(2 * d).astype(BF16)
    vb_ref[...] = proj(2 * d + half).astype(BF16)


def _qkv_proj(x2, g_attn, w_qkv, gq_na, gk_na, gq_dil, gk_dil, seq, tm):
    n, d = x2.shape
    half = d // 2
    pos = jnp.arange(seq, dtype=F32)
    inv = ROPE_THETA ** (-jnp.arange(HEAD_DIM // 2, dtype=F32) / (HEAD_DIM // 2))
    ang = pos[:, None] * inv[None, :]
    cos = jnp.tile(jnp.cos(ang), (1, LANES // (HEAD_DIM // 2)))
    sin = jnp.tile(jnp.concatenate([-jnp.sin(ang), jnp.sin(ang)], axis=1), (1, LANES // HEAD_DIM))
    hs_w = 2 * LANES
    blk = np.arange(hs_w) // HEAD_DIM
    hsum = jnp.asarray((blk[:, None] == blk[None, :]).astype(np.float32) / HEAD_DIM, BF16)
    tile_gain = lambda g: jnp.tile(g.astype(F32), half // HEAD_DIM)[None, :]
    steps_per_seq = seq // tm
    full = lambda shape: pl.BlockSpec(shape, lambda i: (0,) * len(shape))
    out = jax.ShapeDtypeStruct((n, half), BF16)
    return pl.pallas_call(
        _qkv_kernel,
        grid=(n // tm,),
        in_specs=[
            pl.BlockSpec((tm, d), lambda i: (i, 0)),
            full((1, d)),
            pl.BlockSpec((d, 3 * d), lambda i: (0, 0), pipeline_mode=pl.Buffered(1)),
            full((1, half)), full((1, half)), full((1, half)), full((1, half)),
            pl.BlockSpec((tm, LANES), lambda i: (i % steps_per_seq, 0)),
            pl.BlockSpec((tm, LANES), lambda i: (i % steps_per_seq, 0)),
            full((hs_w, hs_w)),
        ],
        out_specs=[pl.BlockSpec((tm, half), lambda i: (i, 0))] * 6,
        out_shape=[out] * 6,
        scratch_shapes=[pltpu.VMEM((d, 3 * d), BF16)],
        compiler_params=pltpu.CompilerParams(dimension_semantics=("arbitrary",), vmem_limit_bytes=VMEM_LIMIT),
        name="qkv_proj",
    )(x2, g_attn[None, :].astype(F32), w_qkv, tile_gain(gq_na), tile_gain(gk_na),
      tile_gain(gq_dil), tile_gain(gk_dil), cos, sin, hsum)


def _is_static(x):
    return isinstance(x, int)


def _clip(x, lo, hi):
    return min(max(x, lo), hi) if _is_static(x) else jnp.clip(x, lo, hi)


def _aligned_ds(start, size, align):
    return pl.ds(start if _is_static(start) else pl.multiple_of(start, align), size)


def _software_pipeline(n_items, stages, steps_per_trip):
    depth = (len(stages) - 1) * ATTN_STAGE_LAG + 1
    assert steps_per_trip % ATTN_SLOTS == 0 and depth <= ATTN_SLOTS + 1

    def step(t, phase, static):
        for k in reversed(range(len(stages))):
            lag = k * ATTN_STAGE_LAG
            if static and not 0 <= t - lag < n_items:
                continue
            stages[k](t - lag, (phase - lag) % ATTN_SLOTS)

    first_full = -(-(depth - 1) // ATTN_SLOTS) * ATTN_SLOTS
    trips = max(n_items - first_full, 0) // steps_per_trip
    if trips < 2:
        trips = 0
    looped_end = first_full + trips * steps_per_trip if trips else 0
    for t in range(first_full if trips else 0):
        step(t, t % ATTN_SLOTS, True)
    if trips:
        def body(i, carry):
            for j in range(steps_per_trip):
                step(first_full + i * steps_per_trip + j, j % ATTN_SLOTS, False)
            return carry

        lax.fori_loop(0, trips, body, 0)
    for t in range(looped_end, n_items + depth - 1):
        step(t, t % ATTN_SLOTS, True)


def _qk_stage(q, kwin, bias_a, bias_b, s_ref, slot, first):
    m = q.shape[0]
    w = kwin.shape[0]
    zero = jnp.zeros_like(q)
    s = _dot_nt(jnp.concatenate([jnp.where(first, q, zero), jnp.where(first, zero, q)], axis=0), kwin)
    s_ref[slot, :m, :w] = s[:m] + bias_a
    s_ref[slot, m:, :w] = s[m:] + bias_b


def _softmax_stage(s_ref, p_ref, slot, w):
    m = jnp.max(s_ref[slot, :, :w], axis=-1, keepdims=True)
    p_ref[slot, :, :w] = jnp.exp2(s_ref[slot, :, :w] - m).astype(BF16)
    return m


def _pv_stage(p_ref, slot, v_win, first):
    w = v_win.shape[0]
    r = _dot(p_ref[slot, :, :w], v_win)
    m = r.shape[0] // 2
    return jnp.where(first, r[:m, :PAIR_W], r[m:, :PAIR_W]), jnp.where(first, r[:m, PAIR_W:], r[m:, PAIR_W:])


def _with_ones(v):
    return jnp.concatenate([v, jnp.ones_like(v)], axis=1)


def _na_kernel(q_ref, k_ref, v_ref, bias_ref, o_ref, v1_ref, s_ref, p_ref):
    rows = q_ref.shape[0] // GRID_W
    win = NA_ROWS * GRID_W
    v1_ref[...] = _with_ones(v_ref[...])
    first = _lane_first_half((GRID_W, PAIR_W))

    def slices(r):
        rs = _clip(r - NA_ROWS // 2, 0, rows - NA_ROWS)
        return _aligned_ds(r * GRID_W, GRID_W, GRID_W), _aligned_ds(rs * GRID_W, win, GRID_W), r - rs

    def bias(head, delta):
        return jnp.concatenate([bias_ref[head, j - delta + NA_ROWS - 1] for j in range(0, NA_ROWS, 2)], axis=1)

    def qk(r, slot):
        qs, ks, delta = slices(r)
        _qk_stage(q_ref[qs, :], k_ref[ks, :], bias(0, delta), bias(1, delta), s_ref, slot, first)

    def softmax(r, slot):
        _softmax_stage(s_ref, p_ref, slot, win)

    def pv(r, slot):
        qs, ks, _ = slices(r)
        num, den = _pv_stage(p_ref, slot, v1_ref[ks, :], first)
        o_ref[qs, :] = (num / den).astype(o_ref.dtype)

    _software_pipeline(rows, (qk, softmax, pv), ATTN_STEPS_PER_TRIP)


def _na_bias_table(rpb):
    w = np.arange(GRID_W)
    cs = np.clip(w - NA_COLS // 2, 0, GRID_W - NA_COLS)
    kc = np.arange(GRID_W)
    valid = (kc[None, :] >= cs[:, None]) & (kc[None, :] < cs[:, None] + NA_COLS)
    coff = np.clip(kc[None, :] - w[:, None] + NA_COLS - 1, 0, 2 * NA_COLS - 2)
    tab = jnp.where(valid[None, None], rpb.astype(F32)[:, :, coff] * LOG2_E, NEG)
    return jnp.concatenate([tab[:, :-1], tab[:, 1:]], axis=-1)


def _na_attention(q, k, v, rpb):
    b, s, width = q.shape
    pairs = width // PAIR_W
    assert s % GRID_W == 0 and s // GRID_W >= NA_ROWS
    bias = _na_bias_table(rpb).reshape(pairs, 2, 2 * NA_ROWS - 2, GRID_W, 2 * GRID_W)
    qkv_spec = pl.BlockSpec((None, s, PAIR_W), lambda bi, j: (bi, 0, j))
    return pl.pallas_call(
        _na_kernel,
        grid=(b, pairs),
        in_specs=[qkv_spec, qkv_spec, qkv_spec,
                  pl.BlockSpec((None, 2, 2 * NA_ROWS - 2, GRID_W, 2 * GRID_W), lambda bi, j: (j, 0, 0, 0, 0))],
        out_specs=pl.BlockSpec((None, s, PAIR_W), lambda bi, j: (bi, 0, j)),
        out_shape=jax.ShapeDtypeStruct((b, s, width), BF16),
        scratch_shapes=[pltpu.VMEM((s, 2 * PAIR_W), BF16),
                        pltpu.VMEM((ATTN_SLOTS, 2 * GRID_W, NA_ROWS * GRID_W), F32),
                        pltpu.VMEM((ATTN_SLOTS, 2 * GRID_W, NA_ROWS * GRID_W), BF16)],
        compiler_params=pltpu.CompilerParams(dimension_semantics=("parallel", "parallel"),
                                             vmem_limit_bytes=VMEM_LIMIT),
        name="na_attn",
    )(q, k, v, bias)


def _dil_kernel(q_ref, k_ref, v_ref, mwide_ref, mfull_ref, o_ref,
                f32a_ref, f32b_ref, qc_ref, kc_ref, vc_ref, acc_ref, den_ref, max_ref, s_ref, p_ref):
    s = q_ref.shape[0]
    for i, ref in enumerate((q_ref, k_ref, v_ref)):
        f32a_ref[i] = ref[...].astype(F32)
    first = _lane_first_half((DIL_BLK, PAIR_W))
    prev_ref, next_ref, prev_dil = f32a_ref, f32b_ref, 1

    for p, (window, dil) in enumerate(DIL_PAIRS):
        radius = window // (2 * dil)
        cls_len = s // dil
        nblk = cls_len // DIL_BLK
        wide = cls_len >= 2 * DIL_BLK
        win = 2 * DIL_BLK if wide else cls_len
        assert radius == DIL_BLK // 2 and cls_len % DIL_BLK == 0 and dil % prev_dil == 0

        ratio = dil // prev_dil
        keep = ratio > 1 and p + 1 < len(DIL_PAIRS)
        for c in range(dil):
            src = pl.ds((c % prev_dil) * (s // prev_dil) + c // prev_dil, cls_len, stride=ratio)
            dst = pl.ds(c * cls_len, cls_len)
            vals = [prev_ref[i, src, :] for i in range(3)]
            if keep:
                for i in range(3):
                    next_ref[i, dst, :] = vals[i]
            qc_ref[dst, :] = vals[0].astype(BF16)
            kc_ref[dst, :] = vals[1].astype(BF16)
            vc_ref[dst, :] = _with_ones(vals[2].astype(BF16))
        if keep:
            prev_ref, next_ref, prev_dil = next_ref, prev_ref, dil

        def slices(n):
            c, i = divmod(n, nblk) if _is_static(n) else (n // nblk, n % nblk)
            base = c * cls_len
            ws = _clip(i * DIL_BLK - radius, 0, cls_len - win)
            if _is_static(i):
                kind = 0 if i == 0 else (2 if i == nblk - 1 else 1)
            else:
                kind = jnp.where(i == 0, 0, jnp.where(i == nblk - 1, 2, 1))
            return (_aligned_ds(base + i * DIL_BLK, DIL_BLK, DIL_BLK), _aligned_ds(base + ws, win, radius), kind,
                    pl.ds(c + dil * DIL_BLK * i, DIL_BLK, stride=dil))

        def qk(n, slot):
            qs, ks, kind, _ = slices(n)
            mask = mwide_ref[kind] if wide else mfull_ref[...]
            _qk_stage(qc_ref[qs, :], kc_ref[ks, :], mask, mask, s_ref, slot, first)

        def softmax(n, slot):
            m = _softmax_stage(s_ref, p_ref, slot, win)
            max_ref[p, slices(n)[3], :] = jnp.where(first, m[:DIL_BLK], m[DIL_BLK:])

        def pv(n, slot):
            _, ks, _, tok = slices(n)
            num, den = _pv_stage(p_ref, slot, vc_ref[ks, :], first)
            acc_ref[p, tok, :] = num
            den_ref[p, tok, :] = den

        _software_pipeline(dil * nblk, (qk, softmax, pv), ATTN_STEPS_PER_TRIP)

    m = jnp.maximum(jnp.maximum(max_ref[0], max_ref[1]), max_ref[2])
    num = jnp.zeros_like(m)
    den = jnp.zeros_like(m)
    for p in range(len(DIL_PAIRS)):
        w = jnp.exp2(max_ref[p] - m)
        num = num + w * acc_ref[p]
        den = den + w * den_ref[p]
    o_ref[...] = (num / den).astype(o_ref.dtype)


def _band_mask(kind):
    radius = DIL_BLK // 2
    qq = np.arange(DIL_BLK)[:, None]
    if kind == "full":
        kk = np.arange(DIL_BLK)[None, :]
        shift = 0
    else:
        kk = np.arange(2 * DIL_BLK)[None, :]
        shift = {"first": 0, "inner": radius, "last": DIL_BLK}[kind]
    return np.where(np.abs(kk - qq - shift) <= radius, 0.0, NEG).astype(np.float32)


def _dil_attention(q, k, v):
    b, s, width = q.shape
    pairs = width // PAIR_W
    for window, dil in DIL_PAIRS:
        assert s % (window // 2) == 0 and (s // dil) % DIL_BLK == 0
    mwide = jnp.asarray(np.stack([_band_mask("first"), _band_mask("inner"), _band_mask("last")]))
    mfull = jnp.asarray(_band_mask("full"))
    qkv_spec = pl.BlockSpec((None, s, PAIR_W), lambda bi, j: (bi, 0, j))
    npat = len(DIL_PAIRS)
    return pl.pallas_call(
        _dil_kernel,
        grid=(b, pairs),
        in_specs=[qkv_spec, qkv_spec, qkv_spec,
                  pl.BlockSpec(mwide.shape, lambda bi, j: (0, 0, 0)),
                  pl.BlockSpec(mfull.shape, lambda bi, j: (0, 0))],
        out_specs=pl.BlockSpec((None, s, PAIR_W), lambda bi, j: (bi, 0, j)),
        out_shape=jax.ShapeDtypeStruct((b, s, width), BF16),
        scratch_shapes=[pltpu.VMEM((3, s, PAIR_W), F32)] * 2 + [pltpu.VMEM((s, PAIR_W), BF16)] * 2
        + [pltpu.VMEM((s, 2 * PAIR_W), BF16)] + [pltpu.VMEM((npat, s, PAIR_W), F32)] * 3
        + [pltpu.VMEM((ATTN_SLOTS, 2 * DIL_BLK, 2 * DIL_BLK), F32),
           pltpu.VMEM((ATTN_SLOTS, 2 * DIL_BLK, 2 * DIL_BLK), BF16)],
        compiler_params=pltpu.CompilerParams(dimension_semantics=("parallel", "parallel"),
                                             vmem_limit_bytes=VMEM_LIMIT),
        name="dil_attn",
    )(q, k, v, mwide, mfull)


def _split_bf16(x):
    hi = x.astype(BF16)
    return hi, (x - hi.astype(F32)).astype(BF16)


def _route(logits):
    lane = lax.broadcasted_iota(jnp.int32, logits.shape, 1)
    ninf = jnp.float32(-jnp.inf)

    def first_argmax(vals, vmax):
        return jnp.min(jnp.where(vals == vmax, lane, LANES), axis=-1, keepdims=True)

    gl = jnp.where(lane < N_GROUPS, logits, ninf)
    gmax = jnp.max(gl, axis=-1, keepdims=True)
    gsel = first_argmax(gl, gmax)
    gw = 1.0 / jnp.sum(jnp.exp(gl - gmax), axis=-1, keepdims=True)
    lo = N_GROUPS + EXPERTS_PER_GROUP * gsel
    el = jnp.where((lane >= lo) & (lane < lo + EXPERTS_PER_GROUP), logits, ninf)
    v0 = jnp.max(el, axis=-1, keepdims=True)
    i0 = first_argmax(el, v0)
    el = jnp.where(lane == i0, ninf, el)
    v1 = jnp.max(el, axis=-1, keepdims=True)
    i1 = first_argmax(el, v1)
    t = jnp.exp(v1 - v0)
    w0 = gw / (1.0 + t)
    w1 = gw * t / (1.0 + t)
    e0 = (i0 - N_GROUPS).astype(F32)
    e1 = (i1 - N_GROUPS).astype(F32)
    return jnp.where(lane == 0, e0, jnp.where(lane == 1, e1, jnp.where(lane == 2, w0, jnp.where(lane == 3, w1, 0.0))))


def _out_router_kernel(oa_ref, ob_ref, x_ref, ga_ref, gb_ref, wo_ref, gf_ref, wr_ref, br_ref,
                       x1_ref, h_ref, rt_ref, rtt_ref, cnt_ref):
    half = oa_ref.shape[1]

    @pl.when(pl.program_id(0) == 0)
    def _():
        cnt_ref[...] = jnp.zeros_like(cnt_ref)

    for r0 in range(0, oa_ref.shape[0], ROUTER_SUB):
        rows = pl.ds(r0, ROUTER_SUB)
        ya = _rms(oa_ref[rows, :].astype(F32), ga_ref[...]).astype(BF16)
        yb = _rms(ob_ref[rows, :].astype(F32), gb_ref[...]).astype(BF16)
        x1 = x_ref[rows, :] + _dot(ya, wo_ref[:half, :]) + _dot(yb, wo_ref[half:, :])
        x1_ref[rows, :] = x1
        h = _rms(x1, gf_ref[...])
        h_ref[rows, :] = _pack_bf16_pairs(h)
        h_hi, h_lo = _split_bf16(h)
        both = _dot(h_hi, wr_ref[...])
        logits = both[:, :LANES] + both[:, LANES:] + _dot(h_lo, wr_ref[:, :LANES]) + br_ref[...]
        rt = _route(logits)
        rt_ref[rows, :] = rt
        rtt = rt.T[:rtt_ref.shape[0], :]
        rtt_ref[:, rows] = rtt
        oh0, oh1 = _slot_one_hots(rtt)
        cnt_ref[...] += jnp.sum(oh0 + oh1, axis=1, keepdims=True)


def _out_router(oa, ob, x2, g_na, g_dil, w_o, g_ffn, w_rg, b_rg, w_re, b_re, tm, row0, n):
    d = x2.shape[1]
    half = d // 2
    blk0 = row0 // tm
    wr = jnp.concatenate([w_rg.astype(F32), w_re.astype(F32).transpose(1, 0, 2).reshape(d, N_EXPERTS)], axis=1)
    wr = jnp.pad(wr, ((0, 0), (0, LANES - wr.shape[1])))
    wr_hi = wr.astype(BF16)
    wr_cat = jnp.concatenate([wr_hi, (wr - wr_hi.astype(F32)).astype(BF16)], axis=1)
    br =jnp.pad(jnp.concatenate([b_rg.astype(F32), b_re.astype(F32).reshape(-1)]), (0, LANES - N_GROUPS - N_EXPERTS))
    full = lambda shape: pl.BlockSpec(shape, lambda i: (0,) * len(shape))
    row = lambda w: pl.BlockSpec((tm, w), lambda i: (i, 0))
    row_in = lambda w: pl.BlockSpec((tm, w), lambda i: (i + blk0, 0))
    return pl.pallas_call(
        _out_router_kernel,
        grid=(n // tm,),
        in_specs=[row_in(half), row_in(half), row_in(d), full((1, half)), full((1, half)), full((d, d)),
                  full((1, d)), full((d, 2 * LANES)), full((1, LANES))],
        out_specs=[row(d), row(half), row(LANES), pl.BlockSpec((8, tm), lambda i: (0, i)), full((LANES, LANES))],
        out_shape=[jax.ShapeDtypeStruct((n, d), F32), jax.ShapeDtypeStruct((n, half), jnp.uint32),
                   jax.ShapeDtypeStruct((n, LANES), F32), jax.ShapeDtypeStruct((8, n), F32),
                   jax.ShapeDtypeStruct((LANES, LANES), F32)],
        compiler_params=pltpu.CompilerParams(dimension_semantics=("arbitrary",), vmem_limit_bytes=VMEM_LIMIT),
        name="out_router",
    )(oa, ob, x2, g_na[None, :].astype(F32), g_dil[None, :].astype(F32), w_o.astype(BF16),
      g_ffn[None, :].astype(F32), wr_cat, br[None, :])


def _pack_bf16_pairs(x):
    w = x.shape[1] // 2
    bits = lax.bitcast_convert_type(x.astype(BF16).astype(F32), jnp.uint32)
    return bits[:, :w] | (bits[:, w:] >> 16)


def _unpack_bf16_pairs(u):
    hi = lax.bitcast_convert_type(u & jnp.uint32(0xFFFF0000), F32)
    lo = lax.bitcast_convert_type(u << 16, F32)
    return hi, lo


def _slot_one_hots(rtt):
    sub = lax.broadcasted_iota(jnp.int32, (LANES, rtt.shape[1]), 0).astype(F32)
    return (sub == rtt[0:1, :]).astype(F32), (sub == rtt[1:2, :]).astype(F32)


def _position_kernel(rtt_ref, start_ref, pos_ref, base_ref):
    tm = rtt_ref.shape[1]

    @pl.when(pl.program_id(0) == 0)
    def _():
        base_ref[...] = start_ref[...]

    oh0, oh1 = _slot_one_hots(rtt_ref[...])
    oh = oh0 + oh1
    earlier = lax.broadcasted_iota(jnp.int32, (tm, tm), 0) < lax.broadcasted_iota(jnp.int32, (tm, tm), 1)
    before = _dot(oh.astype(BF16), earlier.astype(BF16)) + base_ref[:, 0:1]
    p0 = jnp.sum(before * oh0, axis=0, keepdims=True)
    p1 = jnp.sum(before * oh1, axis=0, keepdims=True)
    row = lax.broadcasted_iota(jnp.int32, pos_ref.shape, 0)
    pos_ref[...] = jnp.where(row == 0, p0, jnp.where(row == 1, p1, 0.0)).astype(jnp.int32)
    base_ref[...] += jnp.sum(oh, axis=1, keepdims=True)


def _expert_positions(rtt, starts, tm):
    n = rtt.shape[1]
    return pl.pallas_call(
        _position_kernel,
        grid=(n // tm,),
        in_specs=[pl.BlockSpec((8, tm), lambda i: (0, i)), pl.BlockSpec((LANES, LANES), lambda i: (0, 0))],
        out_specs=pl.BlockSpec((8, tm), lambda i: (0, i)),
        out_shape=jax.ShapeDtypeStruct((8, n), jnp.int32),
        scratch_shapes=[pltpu.VMEM((LANES, LANES), F32)],
        compiler_params=pltpu.CompilerParams(dimension_semantics=("arbitrary",)),
        name="expert_positions",
    )(rtt, starts)


def _sc_mesh():
    return plsc.VectorSubcoreMesh(core_axis_name="c", subcore_axis_name="s",
                                  num_cores=SC_CORES, num_subcores=SC_SUBCORES)


def _sc_dispatch(hp, pos0, pos1, n_out):
    n, w = hp.shape
    workers = SC_CORES * SC_SUBCORES
    per = n // workers
    chunks = per // SC_CHUNK
    assert n % (workers * SC_CHUNK) == 0

    @functools.partial(
        pl.kernel, out_type=jax.ShapeDtypeStruct((n_out, w), hp.dtype), mesh=_sc_mesh(),
        scratch_types=[pltpu.VMEM((chunks, SC_CHUNK), jnp.int32), pltpu.VMEM((chunks, SC_CHUNK), jnp.int32),
                       pltpu.VMEM((SC_CHUNK, w), hp.dtype)],
        name="moe_dispatch")
    def body(h_hbm, p0_hbm, p1_hbm, xs_hbm, i0_v, i1_v, rows_v):
        wid = lax.axis_index("s") * SC_CORES + lax.axis_index("c")
        pltpu.sync_copy(p0_hbm.at[wid], i0_v)
        pltpu.sync_copy(p1_hbm.at[wid], i1_v)

        @pl.loop(0, chunks)
        def _(j):
            pltpu.sync_copy(h_hbm.at[pl.ds(wid * per + j * SC_CHUNK, SC_CHUNK)], rows_v)
            pltpu.sync_copy(rows_v, xs_hbm.at[i0_v.at[j]])
            pltpu.sync_copy(rows_v, xs_hbm.at[i1_v.at[j]])

    return body(hp, pos0.reshape(workers, chunks, SC_CHUNK), pos1.reshape(workers, chunks, SC_CHUNK))


def _sc_collect(ys, pos0, pos1):
    n = pos0.shape[0]
    w = ys.shape[1]
    workers = SC_CORES * SC_SUBCORES
    per = n // workers
    chunks = per // SC_CHUNK
    out = jax.ShapeDtypeStruct((n, w), ys.dtype)

    @functools.partial(
        pl.kernel, out_type=(out, out), mesh=_sc_mesh(),
        scratch_types=[pltpu.VMEM((chunks, SC_CHUNK), jnp.int32), pltpu.VMEM((chunks, SC_CHUNK), jnp.int32),
                       pltpu.VMEM((SC_CHUNK, w), ys.dtype)],
        name="moe_collect")
    def body(ys_hbm, p0_hbm, p1_hbm, y0_hbm, y1_hbm, i0_v, i1_v, rows_v):
        wid = lax.axis_index("s") * SC_CORES + lax.axis_index("c")
        pltpu.sync_copy(p0_hbm.at[wid], i0_v)
        pltpu.sync_copy(p1_hbm.at[wid], i1_v)

        @pl.loop(0, chunks)
        def _(j):
            dst = pl.ds(wid * per + j * SC_CHUNK, SC_CHUNK)
            pltpu.sync_copy(ys_hbm.at[i0_v.at[j]], rows_v)
            pltpu.sync_copy(rows_v, y0_hbm.at[dst])
            pltpu.sync_copy(ys_hbm.at[i1_v.at[j]], rows_v)
            pltpu.sync_copy(rows_v, y1_hbm.at[dst])

    return body(ys, pos0.reshape(workers, chunks, SC_CHUNK), pos1.reshape(workers, chunks, SC_CHUNK))


def _experts_kernel(plan_ref, nt_ref, xs_ref, wg_ref, wu_ref, wd_ref, ys_ref, wg_bf, wu_bf, wd_bf):
    g = pl.program_id(0)
    prev = jnp.maximum(g - 1, 0)

    for row, (src, dst) in enumerate(((wg_ref, wg_bf), (wu_ref, wu_bf), (wd_ref, wd_bf))):
        @pl.when((g == 0) | (plan_ref[row, g] != plan_ref[row, prev]))
        def _(row=row, src=src, dst=dst):
            dst[plan_ref[row + 3, g]] = src[...].astype(BF16)

    @pl.when((g >= EXPERT_LEAD) & (g < nt_ref[0] + EXPERT_LEAD))
    def _():
        slot = plan_ref[6, g]
        half = wg_bf.shape[1] // 2
        hi, lo = _unpack_bf16_pairs(xs_ref[...])
        hi = hi.astype(BF16)
        lo = lo.astype(BF16)
        a = _dot(hi, wg_bf[slot, :half, :]) + _dot(lo, wg_bf[slot, half:, :])
        u = _dot(hi, wu_bf[slot, :half, :]) + _dot(lo, wu_bf[slot, half:, :])
        act = (a * jax.nn.sigmoid(a) * u).astype(BF16)
        ys_ref[...] = _pack_bf16_pairs(_dot(act, wd_bf[slot]))


def _experts(xs, tile_expert, n_tiles, w_gate, w_up, w_down, tmg):
    rows, w = xs.shape
    ne, d, de = w_gate.shape
    steps = rows // tmg + EXPERT_LEAD
    g = jnp.arange(steps, dtype=jnp.int32)
    run = jnp.concatenate([jnp.zeros((1,), jnp.int32),
                           jnp.cumsum((tile_expert[1:] != tile_expert[:-1]).astype(jnp.int32))])
    tile_at = lambda lag: jnp.clip(g - lag, 0, n_tiles[0] - 1)
    plan = jnp.stack([tile_expert[tile_at(lag)] for lag in range(EXPERT_LEAD)]
                     + [run[tile_at(lag)] % EXPERT_SLOTS for lag in range(EXPERT_LEAD + 1)]
                     + [tile_at(EXPERT_LEAD)]).astype(jnp.int32)
    return pl.pallas_call(
        _experts_kernel,
        grid_spec=pltpu.PrefetchScalarGridSpec(
            num_scalar_prefetch=2,
            grid=(steps,),
            in_specs=[pl.BlockSpec((tmg, w), lambda s, plan, nt: (plan[2 * EXPERT_LEAD + 1, s], 0)),
                      pl.BlockSpec((None, d, de), lambda s, plan, nt: (plan[0, s], 0, 0)),
                      pl.BlockSpec((None, d, de), lambda s, plan, nt: (plan[1, s], 0, 0)),
                      pl.BlockSpec((None, de, d), lambda s, plan, nt: (plan[2, s], 0, 0))],
            out_specs=pl.BlockSpec((tmg, w), lambda s, plan, nt: (plan[2 * EXPERT_LEAD + 1, s], 0)),
            scratch_shapes=[pltpu.VMEM((EXPERT_SLOTS, d, de), BF16), pltpu.VMEM((EXPERT_SLOTS, d, de), BF16),
                            pltpu.VMEM((EXPERT_SLOTS, de, d), BF16)],
        ),
        out_shape=jax.ShapeDtypeStruct((rows, w), jnp.uint32),
        compiler_params=pltpu.CompilerParams(dimension_semantics=("arbitrary",), vmem_limit_bytes=VMEM_LIMIT),
        name="experts",
    )(plan, n_tiles, xs, w_gate, w_up, w_down)


def _moe(hp, rtt, cnt, w_gate, w_up, w_down, tm, tmg):
    n = hp.shape[0]
    ne = w_gate.shape[0]
    counts = cnt[:ne, 0].astype(jnp.int32)
    padded = (counts + tmg - 1) // tmg * tmg
    ends = jnp.cumsum(padded)
    starts = jnp.pad((ends - padded).astype(F32), (0, LANES - ne))
    pos = _expert_positions(rtt, jnp.broadcast_to(starts[:, None], (LANES, LANES)), tm)
    pos0, pos1 = pos[0], pos[1]
    rows = 2 * n + ne * tmg
    tile_start = jnp.arange(rows // tmg, dtype=jnp.int32) * tmg
    tile_expert = jnp.minimum(jnp.sum(tile_start[:, None] >= ends[None, :], axis=1), ne - 1).astype(jnp.int32)
    n_tiles = (ends[-1:] // tmg).astype(jnp.int32)
    xs = _sc_dispatch(hp, pos0, pos1, rows)
    ys = _experts(xs, tile_expert, n_tiles, w_gate, w_up, w_down, tmg)
    return _sc_collect(ys, pos0, pos1)


def _ple_kernel(x1_ref, y0_ref, y1_ref, rt_ref, p_ref, g_ref, wg_ref, wp_ref, o_ref):
    for r0 in range(0, x1_ref.shape[0], PLE_SUB):
        rows = pl.ds(r0, PLE_SUB)
        rt = rt_ref[rows, :]
        y0 = jnp.concatenate(_unpack_bf16_pairs(y0_ref[rows, :]), axis=1)
        y1 = jnp.concatenate(_unpack_bf16_pairs(y1_ref[rows, :]), axis=1)
        x2 = x1_ref[rows, :] + rt[:, 2:3] * y0 + rt[:, 3:4] * y1
        gate = jax.nn.sigmoid(_dot(_rms(x2, g_ref[...]).astype(BF16), wg_ref[...]))
        o_ref[rows, :] = x2 + gate * _dot(p_ref[rows, :].astype(BF16), wp_ref[...])


def _ple_kernel_into(prev_ref, *refs):
    del prev_ref
    _ple_kernel(*refs)


def _ple(x1, y0, y1, rt, p2, g_ple, w_gate, w_proj, tm, row0, out_prev):
    n, d = x1.shape
    n_all, dp = p2.shape
    blk0 = row0 // tm
    full = lambda shape: pl.BlockSpec(shape, lambda i: (0,) * len(shape))
    row = lambda w: pl.BlockSpec((tm, w), lambda i: (i, 0))
    row_all = lambda w: pl.BlockSpec((tm, w), lambda i: (i + blk0, 0))
    in_specs = [row(d), row(d // 2), row(d // 2), row(LANES), row_all(dp), full((1, d)), full((d, d)), full((dp, d))]
    args = (x1, y0, y1, rt, p2, g_ple[None, :].astype(F32), w_gate.astype(BF16), w_proj.astype(BF16))
    if out_prev is not None:
        in_specs = [pl.BlockSpec(memory_space=pl.ANY)] + in_specs
        args = (out_prev,) + args
    return pl.pallas_call(
        _ple_kernel if out_prev is None else _ple_kernel_into,
        grid=(n // tm,),
        in_specs=in_specs,
        out_specs=row_all(d),
        out_shape=jax.ShapeDtypeStruct((n_all, d), F32),
        input_output_aliases={} if out_prev is None else {0: 0},
        compiler_params=pltpu.CompilerParams(dimension_semantics=("parallel",), vmem_limit_bytes=VMEM_LIMIT),
        name="ple",
    )(*args)


def _layer(x, p_l, g_attn, w_qkv, q_norm_na, k_norm_na, rpb_na, q_norm_dil, k_norm_dil, g_out_na, g_out_dil,
           w_o, g_ffn, w_rg, b_rg, w_re, b_re, w_exp_gate, w_exp_up, w_exp_down, g_ple, w_ple_gate, w_ple_proj):
    b, s, d = x.shape
    n = b * s
    half = d // 2
    assert d == N_HEADS * HEAD_DIM and half == N_HEADS_NA * HEAD_DIM
    tm = 512
    assert s % tm == 0 and s % QKV_TILE == 0
    x2 = x.reshape(n, d)
    qa, ka, va, qb, kb, vb = _qkv_proj(x2, g_attn, w_qkv, q_norm_na, k_norm_na, q_norm_dil, k_norm_dil, s, QKV_TILE)
    seq = lambda t: t.reshape(b, s, half)
    oa = _na_attention(seq(qa), seq(ka), seq(va), rpb_na).reshape(n, half)
    ob = _dil_attention(seq(qb), seq(kb), seq(vb)).reshape(n, half)
    p2 = p_l.reshape(n, -1)
    chunk_unit = SC_CORES * SC_SUBCORES * SC_CHUNK
    n_chunks = TOKEN_CHUNKS if n % (TOKEN_CHUNKS * chunk_unit) == 0 else 1
    chunk = n // n_chunks
    assert chunk % ROW_TILE == 0
    out = None
    for c in range(n_chunks):
        x1, h, rt, rtt, cnt = _out_router(oa, ob, x2, g_out_na, g_out_dil, w_o, g_ffn, w_rg, b_rg, w_re, b_re,
                                          ROW_TILE, c * chunk, chunk)
        y0, y1 = _moe(h, rtt, cnt, w_exp_gate, w_exp_up, w_exp_down, 2 * tm, tm)
        out = _ple(x1, y0, y1, rt, p2, g_ple, w_ple_gate, w_ple_proj, ROW_TILE, c * chunk, out)
    return out.reshape(b, s, d)


def kernel(x, p, g_attn, w_qkv, q_norm_na, k_norm_na, rpb_na, q_norm_dil, k_norm_dil, g_out_na, g_out_dil, w_o,
           g_ffn, w_router_group, b_router_group, w_router_expert, b_router_expert, w_exp_gate, w_exp_up,
           w_exp_down, g_ple, w_ple_gate, w_ple_proj):
    for i in range(p.shape[0]):
        x = _layer(x, p[i], g_attn[i], w_qkv[i], q_norm_na[i], k_norm_na[i], rpb_na[i], q_norm_dil[i],
                   k_norm_dil[i], g_out_na[i], g_out_dil[i], w_o[i], g_ffn[i], w_router_group[i],
                   b_router_group[i], w_router_expert[i], b_router_expert[i], w_exp_gate[i], w_exp_up[i],
                   w_exp_down[i], g_ple[i], w_ple_gate[i], w_ple_proj[i])
    return x
```

```python
import functools

import numpy as np
import jax
import jax.numpy as jnp
from jax import lax
from jax.experimental import pallas as pl
from jax.experimental.pallas import tpu as pltpu
from jax.experimental.pallas import tpu_sc as plsc

HEAD_DIM = 64
N_HEADS = 16
N_HEADS_NA = 8
GRID_W = 64
NA_ROWS = 8
NA_COLS = 16
DIL_PAIRS = ((128, 1), (512, 4), (2048, 16))
ROPE_THETA = 10000.0
N_GROUPS = 4
EXPERTS_PER_GROUP = 8
N_EXPERTS = N_GROUPS * EXPERTS_PER_GROUP
EPS = 1e-6
NEG = -1e30
LOG2_E = 1.4426950408889634

LANES = 128
PAIR_W = 2 * HEAD_DIM
DIL_BLK = 128
SLAB_DIL = 4
ATTN_STEPS_PER_TRIP = 64
ATTN_SLOTS = 4
ATTN_STAGE_LAG = 2
ROUTER_SUB = 512
PLE_SUB = 256
ROW_TILE = 1024
QKV_TILE = 1024
VMEM_LIMIT = 56 * 1024 * 1024
SC_CORES = 2
SC_SUBCORES = 16
SC_CHUNK = 128
EXPERT_LEAD = 3
EXPERT_SLOTS = 4
TOKEN_CHUNKS = 2

F32 = jnp.float32
BF16 = jnp.bfloat16


def _dot(a, b):
    return jnp.dot(a, b, preferred_element_type=F32)


def _dot_nt(a, b):
    return lax.dot_general(a, b, (((1,), (1,)), ((), ())), preferred_element_type=F32)


def _rms(x, gain):
    return x * lax.rsqrt(jnp.mean(x * x, axis=-1, keepdims=True) + EPS) * gain


def _lane_first_half(shape):
    return lax.broadcasted_iota(jnp.int32, shape, len(shape) - 1) < HEAD_DIM


def _qkv_kernel(x_ref, g_ref, w_ref, gq_na_ref, gk_na_ref, gq_dil_ref, gk_dil_ref, cos_ref, sin_ref,
                hsum_ref, qa_ref, ka_ref, va_ref, qb_ref, kb_ref, vb_ref, wb_ref):
    d = x_ref.shape[1]
    half = d // 2
    scale = HEAD_DIM ** -0.5 * LOG2_E

    @pl.when(pl.program_id(0) == 0)
    def _():
        wb_ref[...] = w_ref[...].astype(BF16)

    h = _rms(x_ref[...], g_ref[...]).astype(BF16)

    def proj(col):
        return _dot(h, wb_ref[:, col:col + half])

    def head_norm(y, gain):
        sq = (y * y).astype(BF16)
        w = hsum_ref.shape[0]
        ms = jnp.concatenate([_dot(sq[:, c:c + w], hsum_ref[...]) for c in range(0, half, w)], axis=1)
        return y * lax.rsqrt(ms + EPS) * gain

    def rope(y):
        lane = lax.broadcasted_iota(jnp.int32, (y.shape[0], LANES), 1)
        lower = (lane % HEAD_DIM) < HEAD_DIM // 2
        cos = cos_ref[...]
        sin = sin_ref[...]
        outs = []
        for c in range(0, half, LANES):
            yc = y[:, c:c + LANES]
            up = pltpu.roll(yc, LANES - HEAD_DIM // 2, axis=1)
            down = pltpu.roll(yc, HEAD_DIM // 2, axis=1)
            outs.append(yc * cos + jnp.where(lower, up, down) * sin)
        return jnp.concatenate(outs, axis=1)

    qa_ref[...] = (head_norm(proj(0), gq_na_ref[...]) * scale).astype(BF16)
    qb_ref[...] = (rope(head_norm(proj(half), gq_dil_ref[...])) * scale).astype(BF16)
    ka_ref[...] = head_norm(proj(d), gk_na_ref[...]).astype(BF16)
    kb_ref[...] = rope(head_norm(proj(d + half), gk_dil_ref[...])).astype(BF16)
    va_ref[...] = proj(2 * d).astype(BF16)
    vb_ref[...] = proj(2 * d + half).astype(BF16)


def _qkv_proj(x2, g_attn, w_qkv, gq_na, gk_na, gq_dil, gk_dil, seq, tm):
    n, d = x2.shape
    half = d // 2
    pos = jnp.arange(seq, dtype=F32)
    inv = ROPE_THETA ** (-jnp.arange(HEAD_DIM // 2, dtype=F32) / (HEAD_DIM // 2))
    ang = pos[:, None] * inv[None, :]
    cos = jnp.tile(jnp.cos(ang), (1, LANES // (HEAD_DIM // 2)))
    sin = jnp.tile(jnp.concatenate([-jnp.sin(ang), jnp.sin(ang)], axis=1), (1, LANES // HEAD_DIM))
    hs_w = 2 * LANES
    blk = np.arange(hs_w) // HEAD_DIM
    hsum = jnp.asarray((blk[:, None] == blk[None, :]).astype(np.float32) / HEAD_DIM, BF16)
    tile_gain = lambda g: jnp.tile(g.astype(F32), half // HEAD_DIM)[None, :]
    steps_per_seq = seq // tm
    full = lambda shape: pl.BlockSpec(shape, lambda i: (0,) * len(shape))
    out = jax.ShapeDtypeStruct((n, half), BF16)
    return pl.pallas_call(
        _qkv_kernel,
        grid=(n // tm,),
        in_specs=[
            pl.BlockSpec((tm, d), lambda i: (i, 0)),
            full((1, d)),
            pl.BlockSpec((d, 3 * d), lambda i: (0, 0), pipeline_mode=pl.Buffered(1)),
            full((1, half)), full((1, half)), full((1, half)), full((1, half)),
            pl.BlockSpec((tm, LANES), lambda i: (i % steps_per_seq, 0)),
            pl.BlockSpec((tm, LANES), lambda i: (i % steps_per_seq, 0)),
            full((hs_w, hs_w)),
        ],
        out_specs=[pl.BlockSpec((tm, half), lambda i: (i, 0))] * 6,
        out_shape=[out] * 6,
        scratch_shapes=[pltpu.VMEM((d, 3 * d), BF16)],
        compiler_params=pltpu.CompilerParams(dimension_semantics=("arbitrary",), vmem_limit_bytes=VMEM_LIMIT),
        name="qkv_proj",
    )(x2, g_attn[None, :].astype(F32), w_qkv, tile_gain(gq_na), tile_gain(gk_na),
      tile_gain(gq_dil), tile_gain(gk_dil), cos, sin, hsum)


def _is_static(x):
    return isinstance(x, int)


def _clip(x, lo, hi):
    return min(max(x, lo), hi) if _is_static(x) else jnp.clip(x, lo, hi)


def _aligned_ds(start, size, align):
    return pl.ds(start if _is_static(start) else pl.multiple_of(start, align), size)


def _software_pipeline(n_items, stages, steps_per_trip):
    depth = (len(stages) - 1) * ATTN_STAGE_LAG + 1
    assert steps_per_trip % ATTN_SLOTS == 0 and depth <= ATTN_SLOTS + 1

    def step(t, phase, static):
        for k in reversed(range(len(stages))):
            lag = k * ATTN_STAGE_LAG
            if static and not 0 <= t - lag < n_items:
                continue
            stages[k](t - lag, (phase - lag) % ATTN_SLOTS)

    first_full = -(-(depth - 1) // ATTN_SLOTS) * ATTN_SLOTS
    trips = max(n_items - first_full, 0) // steps_per_trip
    if trips < 2:
        trips = 0
    looped_end = first_full + trips * steps_per_trip if trips else 0
    for t in range(first_full if trips else 0):
        step(t, t % ATTN_SLOTS, True)
    if trips:
        def body(i, carry):
            for j in range(steps_per_trip):
                step(first_full + i * steps_per_trip + j, j % ATTN_SLOTS, False)
            return carry

        lax.fori_loop(0, trips, body, 0)
    for t in range(looped_end, n_items + depth - 1):
        step(t, t % ATTN_SLOTS, True)


def _qk_stage(q, kwin, bias_a, bias_b, s_ref, slot, first):
    m = q.shape[0]
    w = kwin.shape[0]
    zero = jnp.zeros_like(q)
    s = _dot_nt(jnp.concatenate([jnp.where(first, q, zero), jnp.where(first, zero, q)], axis=0), kwin)
    s_ref[slot, :m, :w] = s[:m] + bias_a
    s_ref[slot, m:, :w] = s[m:] + bias_b


def _softmax_stage(s_ref, p_ref, slot, w):
    m = jnp.max(s_ref[slot, :, :w], axis=-1, keepdims=True)
    p_ref[slot, :, :w] = jnp.exp2(s_ref[slot, :, :w] - m).astype(BF16)
    return m


def _pv_stage(p_ref, slot, v_win, first):
    w = v_win.shape[0]
    r = _dot(p_ref[slot, :, :w], v_win)
    m = r.shape[0] // 2
    return jnp.where(first, r[:m, :PAIR_W], r[m:, :PAIR_W]), jnp.where(first, r[:m, PAIR_W:], r[m:, PAIR_W:])


def _with_ones(v):
    return jnp.concatenate([v, jnp.ones_like(v)], axis=1)


def _na_kernel(q_ref, k_ref, v_ref, bias_ref, o_ref, v1_ref, s_ref, p_ref):
    rows = q_ref.shape[0] // GRID_W
    win = NA_ROWS * GRID_W
    v1_ref[...] = _with_ones(v_ref[...])
    first = _lane_first_half((GRID_W, PAIR_W))

    def slices(r):
        rs = _clip(r - NA_ROWS // 2, 0, rows - NA_ROWS)
        return _aligned_ds(r * GRID_W, GRID_W, GRID_W), _aligned_ds(rs * GRID_W, win, GRID_W), r - rs

    def bias(head, delta):
        return jnp.concatenate([bias_ref[head, j - delta + NA_ROWS - 1] for j in range(0, NA_ROWS, 2)], axis=1)

    def qk(r, slot):
        qs, ks, delta = slices(r)
        _qk_stage(q_ref[qs, :], k_ref[ks, :], bias(0, delta), bias(1, delta), s_ref, slot, first)

    def softmax(r, slot):
        _softmax_stage(s_ref, p_ref, slot, win)

    def pv(r, slot):
        qs, ks, _ = slices(r)
        num, den = _pv_stage(p_ref, slot, v1_ref[ks, :], first)
        o_ref[qs, :] = (num / den).astype(o_ref.dtype)

    _software_pipeline(rows, (qk, softmax, pv), ATTN_STEPS_PER_TRIP)


def _na_bias_table(rpb):
    w = np.arange(GRID_W)
    cs = np.clip(w - NA_COLS // 2, 0, GRID_W - NA_COLS)
    kc = np.arange(GRID_W)
    valid = (kc[None, :] >= cs[:, None]) & (kc[None, :] < cs[:, None] + NA_COLS)
    coff = np.clip(kc[None, :] - w[:, None] + NA_COLS - 1, 0, 2 * NA_COLS - 2)
    tab = jnp.where(valid[None, None], rpb.astype(F32)[:, :, coff] * LOG2_E, NEG)
    return jnp.concatenate([tab[:, :-1], tab[:, 1:]], axis=-1)


def _na_attention(q, k, v, rpb):
    b, s, width = q.shape
    pairs = width // PAIR_W
    assert s % GRID_W == 0 and s // GRID_W >= NA_ROWS
    bias = _na_bias_table(rpb).reshape(pairs, 2, 2 * NA_ROWS - 2, GRID_W, 2 * GRID_W)
    qkv_spec = pl.BlockSpec((None, s, PAIR_W), lambda bi, j: (bi, 0, j))
    return pl.pallas_call(
        _na_kernel,
        grid=(b, pairs),
        in_specs=[qkv_spec, qkv_spec, qkv_spec,
                  pl.BlockSpec((None, 2, 2 * NA_ROWS - 2, GRID_W, 2 * GRID_W), lambda bi, j: (j, 0, 0, 0, 0))],
        out_specs=pl.BlockSpec((None, s, PAIR_W), lambda bi, j: (bi, 0, j)),
        out_shape=jax.ShapeDtypeStruct((b, s, width), BF16),
        scratch_shapes=[pltpu.VMEM((s, 2 * PAIR_W), BF16),
                        pltpu.VMEM((ATTN_SLOTS, 2 * GRID_W, NA_ROWS * GRID_W), F32),
                        pltpu.VMEM((ATTN_SLOTS, 2 * GRID_W, NA_ROWS * GRID_W), BF16)],
        compiler_params=pltpu.CompilerParams(dimension_semantics=("parallel", "parallel"),
                                             vmem_limit_bytes=VMEM_LIMIT),
        name="na_attn",
    )(q, k, v, bias)


def _dil_kernel(q_ref, k_ref, v_ref, mwide_ref, mfull_ref, o_ref,
                f32a_ref, f32b_ref, qc_ref, kc_ref, vc_ref, acc_ref, den_ref, max_ref, s_ref, p_ref):
    s = q_ref.shape[0]
    for i, ref in enumerate((q_ref, k_ref, v_ref)):
        f32a_ref[i] = ref[...].astype(F32)
    first = _lane_first_half((DIL_BLK, PAIR_W))
    prev_ref, next_ref, prev_dil = f32a_ref, f32b_ref, 1

    for p, (window, dil) in enumerate(DIL_PAIRS):
        radius = window // (2 * dil)
        cls_len = s // dil
        nblk = cls_len // DIL_BLK
        wide = cls_len >= 2 * DIL_BLK
        win = 2 * DIL_BLK if wide else cls_len
        assert radius == DIL_BLK // 2 and cls_len % DIL_BLK == 0 and dil % prev_dil == 0
        assert dil == 1 or dil % SLAB_DIL == 0

        ratio = dil // prev_dil
        keep = ratio > 1 and p + 1 < len(DIL_PAIRS)
        for c in range(dil):
            src = pl.ds((c % prev_dil) * (s // prev_dil) + c // prev_dil, cls_len, stride=ratio)
            dst = pl.ds(c * cls_len, cls_len)
            vals = [prev_ref[i, src, :] for i in range(3)]
            if keep:
                for i in range(3):
                    next_ref[i, dst, :] = vals[i]
            qc_ref[dst, :] = vals[0].astype(BF16)
            kc_ref[dst, :] = vals[1].astype(BF16)
            vc_ref[dst, :] = _with_ones(vals[2].astype(BF16))
        if keep:
            prev_ref, next_ref, prev_dil = next_ref, prev_ref, dil

        def slices(n):
            c, i = divmod(n, nblk) if _is_static(n) else (n // nblk, n % nblk)
            base = c * cls_len
            ws = _clip(i * DIL_BLK - radius, 0, cls_len - win)
            if _is_static(i):
                kind = 0 if i == 0 else (2 if i == nblk - 1 else 1)
            else:
                kind = jnp.where(i == 0, 0, jnp.where(i == nblk - 1, 2, 1))
            if dil == 1:
                res = pl.ds(i * DIL_BLK, DIL_BLK)
            else:
                sub = dil // SLAB_DIL
                res = pl.ds((c % SLAB_DIL) * (s // SLAB_DIL) + c // SLAB_DIL + sub * DIL_BLK * i, DIL_BLK, stride=sub)
            return (_aligned_ds(base + i * DIL_BLK, DIL_BLK, DIL_BLK), _aligned_ds(base + ws, win, radius), kind, res)

        def qk(n, slot):
            qs, ks, kind, _ = slices(n)
            mask = mwide_ref[kind] if wide else mfull_ref[...]
            _qk_stage(qc_ref[qs, :], kc_ref[ks, :], mask, mask, s_ref, slot, first)

        def softmax(n, slot):
            m = _softmax_stage(s_ref, p_ref, slot, win)
            max_ref[p, slices(n)[3], :] = jnp.where(first, m[:DIL_BLK], m[DIL_BLK:])

        def pv(n, slot):
            _, ks, _, tok = slices(n)
            num, den = _pv_stage(p_ref, slot, vc_ref[ks, :], first)
            acc_ref[p, tok, :] = num
            den_ref[p, tok, :] = den

        _software_pipeline(dil * nblk, (qk, softmax, pv), ATTN_STEPS_PER_TRIP)

    out_f32 = f32a_ref.at[0]
    quarter = s // SLAB_DIL
    for c in range(SLAB_DIL):
        rows = [pl.ds(c, quarter, stride=SLAB_DIL) if dil == 1 else pl.ds(c * quarter, quarter)
                for _, dil in DIL_PAIRS]
        maxima = [max_ref[p, rows[p], :] for p in range(len(DIL_PAIRS))]
        m = functools.reduce(jnp.maximum, maxima)
        num = jnp.zeros_like(m)
        den = jnp.zeros_like(m)
        for p in range(len(DIL_PAIRS)):
            w = jnp.exp2(maxima[p] - m)
            num = num + w * acc_ref[p, rows[p], :]
            den = den + w * den_ref[p, rows[p], :]
        out_f32[pl.ds(c, quarter, stride=SLAB_DIL), :] = num / den
    o_ref[...] = out_f32[...].astype(o_ref.dtype)


def _band_mask(kind):
    radius = DIL_BLK // 2
    qq = np.arange(DIL_BLK)[:, None]
    if kind == "full":
        kk = np.arange(DIL_BLK)[None, :]
        shift = 0
    else:
        kk = np.arange(2 * DIL_BLK)[None, :]
        shift = {"first": 0, "inner": radius, "last": DIL_BLK}[kind]
    return np.where(np.abs(kk - qq - shift) <= radius, 0.0, NEG).astype(np.float32)


def _dil_attention(q, k, v):
    b, s, width = q.shape
    pairs = width // PAIR_W
    for window, dil in DIL_PAIRS:
        assert s % (window // 2) == 0 and (s // dil) % DIL_BLK == 0
    mwide = jnp.asarray(np.stack([_band_mask("first"), _band_mask("inner"), _band_mask("last")]))
    mfull = jnp.asarray(_band_mask("full"))
    qkv_spec = pl.BlockSpec((None, s, PAIR_W), lambda bi, j: (bi, 0, j))
    npat = len(DIL_PAIRS)
    return pl.pallas_call(
        _dil_kernel,
        grid=(b, pairs),
        in_specs=[qkv_spec, qkv_spec, qkv_spec,
                  pl.BlockSpec(mwide.shape, lambda bi, j: (0, 0, 0)),
                  pl.BlockSpec(mfull.shape, lambda bi, j: (0, 0))],
        out_specs=pl.BlockSpec((None, s, PAIR_W), lambda bi, j: (bi, 0, j)),
        out_shape=jax.ShapeDtypeStruct((b, s, width), BF16),
        scratch_shapes=[pltpu.VMEM((3, s, PAIR_W), F32)] * 2 + [pltpu.VMEM((s, PAIR_W), BF16)] * 2
        + [pltpu.VMEM((s, 2 * PAIR_W), BF16)] + [pltpu.VMEM((npat, s, PAIR_W), F32)] * 3
        + [pltpu.VMEM((ATTN_SLOTS, 2 * DIL_BLK, 2 * DIL_BLK), F32),
           pltpu.VMEM((ATTN_SLOTS, 2 * DIL_BLK, 2 * DIL_BLK), BF16)],
        compiler_params=pltpu.CompilerParams(dimension_semantics=("parallel", "parallel"),
                                             vmem_limit_bytes=VMEM_LIMIT),
        name="dil_attn",
    )(q, k, v, mwide, mfull)


def _split_bf16(x):
    hi = x.astype(BF16)
    return hi, (x - hi.astype(F32)).astype(BF16)


def _route(logits):
    lane = lax.broadcasted_iota(jnp.int32, logits.shape, 1)
    ninf = jnp.float32(-jnp.inf)

    def first_argmax(vals, vmax):
        return jnp.min(jnp.where(vals == vmax, lane, LANES), axis=-1, keepdims=True)

    gl = jnp.where(lane < N_GROUPS, logits, ninf)
    gmax = jnp.max(gl, axis=-1, keepdims=True)
    gsel = first_argmax(gl, gmax)
    gw = 1.0 / jnp.sum(jnp.exp(gl - gmax), axis=-1, keepdims=True)
    lo = N_GROUPS + EXPERTS_PER_GROUP * gsel
    el = jnp.where((lane >= lo) & (lane < lo + EXPERTS_PER_GROUP), logits, ninf)
    v0 = jnp.max(el, axis=-1, keepdims=True)
    i0 = first_argmax(el, v0)
    el = jnp.where(lane == i0, ninf, el)
    v1 = jnp.max(el, axis=-1, keepdims=True)
    i1 = first_argmax(el, v1)
    t = jnp.exp(v1 - v0)
    w0 = gw / (1.0 + t)
    w1 = gw * t / (1.0 + t)
    e0 = (i0 - N_GROUPS).astype(F32)
    e1 = (i1 - N_GROUPS).astype(F32)
    return jnp.where(lane == 0, e0, jnp.where(lane == 1, e1, jnp.where(lane == 2, w0, jnp.where(lane == 3, w1, 0.0))))


def _out_router_kernel(oa_ref, ob_ref, x_ref, ga_ref, gb_ref, wo_ref, gf_ref, wr_ref, br_ref,
                       x1_ref, h_ref, rt_ref, rtt_ref, cnt_ref):
    half = oa_ref.shape[1]

    @pl.when(pl.program_id(0) == 0)
    def _():
        cnt_ref[...] = jnp.zeros_like(cnt_ref)

    for r0 in range(0, oa_ref.shape[0], ROUTER_SUB):
        rows = pl.ds(r0, ROUTER_SUB)
        ya = _rms(oa_ref[rows, :].astype(F32), ga_ref[...]).astype(BF16)
        yb = _rms(ob_ref[rows, :].astype(F32), gb_ref[...]).astype(BF16)
        x1 = x_ref[rows, :] + _dot(ya, wo_ref[:half, :]) + _dot(yb, wo_ref[half:, :])
        x1_ref[rows, :] = x1
        h = _rms(x1, gf_ref[...])
        h_ref[rows, :] = _pack_bf16_pairs(h)
        h_hi, h_lo = _split_bf16(h)
        both = _dot(h_hi, wr_ref[...])
        logits = both[:, :LANES] + both[:, LANES:] + _dot(h_lo, wr_ref[:, :LANES]) + br_ref[...]
        rt = _route(logits)
        rt_ref[rows, :] = rt
        rtt = rt.T[:rtt_ref.shape[0], :]
        rtt_ref[:, rows] = rtt
        oh0, oh1 = _slot_one_hots(rtt)
        cnt_ref[...] += jnp.sum(oh0 + oh1, axis=1, keepdims=True)


def _out_router(oa, ob, x2, g_na, g_dil, w_o, g_ffn, w_rg, b_rg, w_re, b_re, tm, row0, n):
    d = x2.shape[1]
    half = d // 2
    blk0 = row0 // tm
    wr = jnp.concatenate([w_rg.astype(F32), w_re.astype(F32).transpose(1, 0, 2).reshape(d, N_EXPERTS)], axis=1)
    wr = jnp.pad(wr, ((0, 0), (0, LANES - wr.shape[1])))
    wr_hi = wr.astype(BF16)
    wr_cat = jnp.concatenate([wr_hi, (wr - wr_hi.astype(F32)).astype(BF16)], axis=1)
    br =jnp.pad(jnp.concatenate([b_rg.astype(F32), b_re.astype(F32).reshape(-1)]), (0, LANES - N_GROUPS - N_EXPERTS))
    full = lambda shape: pl.BlockSpec(shape, lambda i: (0,) * len(shape))
    row = lambda w: pl.BlockSpec((tm, w), lambda i: (i, 0))
    row_in = lambda w: pl.BlockSpec((tm, w), lambda i: (i + blk0, 0))
    return pl.pallas_call(
        _out_router_kernel,
        grid=(n // tm,),
        in_specs=[row_in(half), row_in(half), row_in(d), full((1, half)), full((1, half)), full((d, d)),
                  full((1, d)), full((d, 2 * LANES)), full((1, LANES))],
        out_specs=[row(d), row(half), row(LANES), pl.BlockSpec((8, tm), lambda i: (0, i)), full((LANES, LANES))],
        out_shape=[jax.ShapeDtypeStruct((n, d), F32), jax.ShapeDtypeStruct((n, half), jnp.uint32),
                   jax.ShapeDtypeStruct((n, LANES), F32), jax.ShapeDtypeStruct((8, n), F32),
                   jax.ShapeDtypeStruct((LANES, LANES), F32)],
        compiler_params=pltpu.CompilerParams(dimension_semantics=("arbitrary",), vmem_limit_bytes=VMEM_LIMIT),
        name="out_router",
    )(oa, ob, x2, g_na[None, :].astype(F32), g_dil[None, :].astype(F32), w_o.astype(BF16),
      g_ffn[None, :].astype(F32), wr_cat, br[None, :])


def _pack_bf16_pairs(x):
    w = x.shape[1] // 2
    bits = lax.bitcast_convert_type(x.astype(BF16).astype(F32), jnp.uint32)
    return bits[:, :w] | (bits[:, w:] >> 16)


def _unpack_bf16_pairs(u):
    hi = lax.bitcast_convert_type(u & jnp.uint32(0xFFFF0000), F32)
    lo = lax.bitcast_convert_type(u << 16, F32)
    return hi, lo


def _slot_one_hots(rtt):
    sub = lax.broadcasted_iota(jnp.int32, (LANES, rtt.shape[1]), 0).astype(F32)
    return (sub == rtt[0:1, :]).astype(F32), (sub == rtt[1:2, :]).astype(F32)


def _position_kernel(rtt_ref, start_ref, pos_ref, base_ref):
    tm = rtt_ref.shape[1]

    @pl.when(pl.program_id(0) == 0)
    def _():
        base_ref[...] = start_ref[...]

    oh0, oh1 = _slot_one_hots(rtt_ref[...])
    oh = oh0 + oh1
    earlier = lax.broadcasted_iota(jnp.int32, (tm, tm), 0) < lax.broadcasted_iota(jnp.int32, (tm, tm), 1)
    before = _dot(oh.astype(BF16), earlier.astype(BF16)) + base_ref[:, 0:1]
    p0 = jnp.sum(before * oh0, axis=0, keepdims=True)
    p1 = jnp.sum(before * oh1, axis=0, keepdims=True)
    row = lax.broadcasted_iota(jnp.int32, pos_ref.shape, 0)
    pos_ref[...] = jnp.where(row == 0, p0, jnp.where(row == 1, p1, 0.0)).astype(jnp.int32)
    base_ref[...] += jnp.sum(oh, axis=1, keepdims=True)


def _expert_positions(rtt, starts, tm):
    n = rtt.shape[1]
    return pl.pallas_call(
        _position_kernel,
        grid=(n // tm,),
        in_specs=[pl.BlockSpec((8, tm), lambda i: (0, i)), pl.BlockSpec((LANES, LANES), lambda i: (0, 0))],
        out_specs=pl.BlockSpec((8, tm), lambda i: (0, i)),
        out_shape=jax.ShapeDtypeStruct((8, n), jnp.int32),
        scratch_shapes=[pltpu.VMEM((LANES, LANES), F32)],
        compiler_params=pltpu.CompilerParams(dimension_semantics=("arbitrary",)),
        name="expert_positions",
    )(rtt, starts)


def _sc_mesh():
    return plsc.VectorSubcoreMesh(core_axis_name="c", subcore_axis_name="s",
                                  num_cores=SC_CORES, num_subcores=SC_SUBCORES)


def _sc_dispatch(hp, pos0, pos1, n_out):
    n, w = hp.shape
    workers = SC_CORES * SC_SUBCORES
    per = n // workers
    chunks = per // SC_CHUNK
    assert n % (workers * SC_CHUNK) == 0

    @functools.partial(
        pl.kernel, out_type=jax.ShapeDtypeStruct((n_out, w), hp.dtype), mesh=_sc_mesh(),
        scratch_types=[pltpu.VMEM((chunks, SC_CHUNK), jnp.int32), pltpu.VMEM((chunks, SC_CHUNK), jnp.int32),
                       pltpu.VMEM((SC_CHUNK, w), hp.dtype)],
        name="moe_dispatch")
    def body(h_hbm, p0_hbm, p1_hbm, xs_hbm, i0_v, i1_v, rows_v):
        wid = lax.axis_index("s") * SC_CORES + lax.axis_index("c")
        pltpu.sync_copy(p0_hbm.at[wid], i0_v)
        pltpu.sync_copy(p1_hbm.at[wid], i1_v)

        @pl.loop(0, chunks)
        def _(j):
            pltpu.sync_copy(h_hbm.at[pl.ds(wid * per + j * SC_CHUNK, SC_CHUNK)], rows_v)
            pltpu.sync_copy(rows_v, xs_hbm.at[i0_v.at[j]])
            pltpu.sync_copy(rows_v, xs_hbm.at[i1_v.at[j]])

    return body(hp, pos0.reshape(workers, chunks, SC_CHUNK), pos1.reshape(workers, chunks, SC_CHUNK))


def _sc_collect(ys, pos0, pos1):
    n = pos0.shape[0]
    w = ys.shape[1]
    workers = SC_CORES * SC_SUBCORES
    per = n // workers
    chunks = per // SC_CHUNK
    out = jax.ShapeDtypeStruct((n, w), ys.dtype)

    @functools.partial(
        pl.kernel, out_type=(out, out), mesh=_sc_mesh(),
        scratch_types=[pltpu.VMEM((chunks, SC_CHUNK), jnp.int32), pltpu.VMEM((chunks, SC_CHUNK), jnp.int32),
                       pltpu.VMEM((SC_CHUNK, w), ys.dtype)],
        name="moe_collect")
    def body(ys_hbm, p0_hbm, p1_hbm, y0_hbm, y1_hbm, i0_v, i1_v, rows_v):
        wid = lax.axis_index("s") * SC_CORES + lax.axis_index("c")
        pltpu.sync_copy(p0_hbm.at[wid], i0_v)
        pltpu.sync_copy(p1_hbm.at[wid], i1_v)

        @pl.loop(0, chunks)
        def _(j):
            dst = pl.ds(wid * per + j * SC_CHUNK, SC_CHUNK)
            pltpu.sync_copy(ys_hbm.at[i0_v.at[j]], rows_v)
            pltpu.sync_copy(rows_v, y0_hbm.at[dst])
            pltpu.sync_copy(ys_hbm.at[i1_v.at[j]], rows_v)
            pltpu.sync_copy(rows_v, y1_hbm.at[dst])

    return body(ys, pos0.reshape(workers, chunks, SC_CHUNK), pos1.reshape(workers, chunks, SC_CHUNK))


def _experts_kernel(plan_ref, nt_ref, xs_ref, wg_ref, wu_ref, wd_ref, ys_ref, wg_bf, wu_bf, wd_bf):
    g = pl.program_id(0)
    prev = jnp.maximum(g - 1, 0)

    for row, (src, dst) in enumerate(((wg_ref, wg_bf), (wu_ref, wu_bf), (wd_ref, wd_bf))):
        @pl.when((g == 0) | (plan_ref[row, g] != plan_ref[row, prev]))
        def _(row=row, src=src, dst=dst):
            dst[plan_ref[row + 3, g]] = src[...].astype(BF16)

    @pl.when((g >= EXPERT_LEAD) & (g < nt_ref[0] + EXPERT_LEAD))
    def _():
        slot = plan_ref[6, g]
        half = wg_bf.shape[1] // 2
        hi, lo = _unpack_bf16_pairs(xs_ref[...])
        hi = hi.astype(BF16)
        lo = lo.astype(BF16)
        a = _dot(hi, wg_bf[slot, :half, :]) + _dot(lo, wg_bf[slot, half:, :])
        u = _dot(hi, wu_bf[slot, :half, :]) + _dot(lo, wu_bf[slot, half:, :])
        act = (a * jax.nn.sigmoid(a) * u).astype(BF16)
        ys_ref[...] = _pack_bf16_pairs(_dot(act, wd_bf[slot]))


def _experts(xs, tile_expert, n_tiles, w_gate, w_up, w_down, tmg):
    rows, w = xs.shape
    ne, d, de = w_gate.shape
    steps = rows // tmg + EXPERT_LEAD
    g = jnp.arange(steps, dtype=jnp.int32)
    run = jnp.concatenate([jnp.zeros((1,), jnp.int32),
                           jnp.cumsum((tile_expert[1:] != tile_expert[:-1]).astype(jnp.int32))])
    tile_at = lambda lag: jnp.clip(g - lag, 0, n_tiles[0] - 1)
    plan = jnp.stack([tile_expert[tile_at(lag)] for lag in range(EXPERT_LEAD)]
                     + [run[tile_at(lag)] % EXPERT_SLOTS for lag in range(EXPERT_LEAD + 1)]
                     + [tile_at(EXPERT_LEAD)]).astype(jnp.int32)
    return pl.pallas_call(
        _experts_kernel,
        grid_spec=pltpu.PrefetchScalarGridSpec(
            num_scalar_prefetch=2,
            grid=(steps,),
            in_specs=[pl.BlockSpec((tmg, w), lambda s, plan, nt: (plan[2 * EXPERT_LEAD + 1, s], 0)),
                      pl.BlockSpec((None, d, de), lambda s, plan, nt: (plan[0, s], 0, 0)),
                      pl.BlockSpec((None, d, de), lambda s, plan, nt: (plan[1, s], 0, 0)),
                      pl.BlockSpec((None, de, d), lambda s, plan, nt: (plan[2, s], 0, 0))],
            out_specs=pl.BlockSpec((tmg, w), lambda s, plan, nt: (plan[2 * EXPERT_LEAD + 1, s], 0)),
            scratch_shapes=[pltpu.VMEM((EXPERT_SLOTS, d, de), BF16), pltpu.VMEM((EXPERT_SLOTS, d, de), BF16),
                            pltpu.VMEM((EXPERT_SLOTS, de, d), BF16)],
        ),
        out_shape=jax.ShapeDtypeStruct((rows, w), jnp.uint32),
        compiler_params=pltpu.CompilerParams(dimension_semantics=("arbitrary",), vmem_limit_bytes=VMEM_LIMIT),
        name="experts",
    )(plan, n_tiles, xs, w_gate, w_up, w_down)


def _moe(hp, rtt, cnt, w_gate, w_up, w_down, tm, tmg):
    n = hp.shape[0]
    ne = w_gate.shape[0]
    counts = cnt[:ne, 0].astype(jnp.int32)
    padded = (counts + tmg - 1) // tmg * tmg
    ends = jnp.cumsum(padded)
    starts = jnp.pad((ends - padded).astype(F32), (0, LANES - ne))
    pos = _expert_positions(rtt, jnp.broadcast_to(starts[:, None], (LANES, LANES)), tm)
    pos0, pos1 = pos[0], pos[1]
    rows = 2 * n + ne * tmg
    tile_start = jnp.arange(rows // tmg, dtype=jnp.int32) * tmg
    tile_expert = jnp.minimum(jnp.sum(tile_start[:, None] >= ends[None, :], axis=1), ne - 1).astype(jnp.int32)
    n_tiles = (ends[-1:] // tmg).astype(jnp.int32)
    xs = _sc_dispatch(hp, pos0, pos1, rows)
    ys = _experts(xs, tile_expert, n_tiles, w_gate, w_up, w_down, tmg)
    return _sc_collect(ys, pos0, pos1)


def _ple_kernel(x1_ref, y0_ref, y1_ref, rt_ref, p_ref, g_ref, wg_ref, wp_ref, o_ref):
    for r0 in range(0, x1_ref.shape[0], PLE_SUB):
        rows = pl.ds(r0, PLE_SUB)
        rt = rt_ref[rows, :]
        y0 = jnp.concatenate(_unpack_bf16_pairs(y0_ref[rows, :]), axis=1)
        y1 = jnp.concatenate(_unpack_bf16_pairs(y1_ref[rows, :]), axis=1)
        x2 = x1_ref[rows, :] + rt[:, 2:3] * y0 + rt[:, 3:4] * y1
        gate = jax.nn.sigmoid(_dot(_rms(x2, g_ref[...]).astype(BF16), wg_ref[...]))
        o_ref[rows, :] = x2 + gate * _dot(p_ref[rows, :].astype(BF16), wp_ref[...])


def _ple_kernel_into(prev_ref, *refs):
    del prev_ref
    _ple_kernel(*refs)


def _ple(x1, y0, y1, rt, p2, g_ple, w_gate, w_proj, tm, row0, out_prev):
    n, d = x1.shape
    n_all, dp = p2.shape
    blk0 = row0 // tm
    full = lambda shape: pl.BlockSpec(shape, lambda i: (0,) * len(shape))
    row = lambda w: pl.BlockSpec((tm, w), lambda i: (i, 0))
    row_all = lambda w: pl.BlockSpec((tm, w), lambda i: (i + blk0, 0))
    in_specs = [row(d), row(d // 2), row(d // 2), row(LANES), row_all(dp), full((1, d)), full((d, d)), full((dp, d))]
    args = (x1, y0, y1, rt, p2, g_ple[None, :].astype(F32), w_gate.astype(BF16), w_proj.astype(BF16))
    if out_prev is not None:
        in_specs = [pl.BlockSpec(memory_space=pl.ANY)] + in_specs
        args = (out_prev,) + args
    return pl.pallas_call(
        _ple_kernel if out_prev is None else _ple_kernel_into,
        grid=(n // tm,),
        in_specs=in_specs,
        out_specs=row_all(d),
        out_shape=jax.ShapeDtypeStruct((n_all, d), F32),
        input_output_aliases={} if out_prev is None else {0: 0},
        compiler_params=pltpu.CompilerParams(dimension_semantics=("parallel",), vmem_limit_bytes=VMEM_LIMIT),
        name="ple",
    )(*args)


def _layer(x, p_l, g_attn, w_qkv, q_norm_na, k_norm_na, rpb_na, q_norm_dil, k_norm_dil, g_out_na, g_out_dil,
           w_o, g_ffn, w_rg, b_rg, w_re, b_re, w_exp_gate, w_exp_up, w_exp_down, g_ple, w_ple_gate, w_ple_proj):
    b, s, d = x.shape
    n = b * s
    half = d // 2
    assert d == N_HEADS * HEAD_DIM and half == N_HEADS_NA * HEAD_DIM
    tm = 512
    assert s % tm == 0 and s % QKV_TILE == 0
    x2 = x.reshape(n, d)
    qa, ka, va, qb, kb, vb = _qkv_proj(x2, g_attn, w_qkv, q_norm_na, k_norm_na, q_norm_dil, k_norm_dil, s, QKV_TILE)
    seq = lambda t: t.reshape(b, s, half)
    oa = _na_attention(seq(qa), seq(ka), seq(va), rpb_na).reshape(n, half)
    ob = _dil_attention(seq(qb), seq(kb), seq(vb)).reshape(n, half)
    p2 = p_l.reshape(n, -1)
    chunk_unit = SC_CORES * SC_SUBCORES * SC_CHUNK
    n_chunks = TOKEN_CHUNKS if n % (TOKEN_CHUNKS * chunk_unit) == 0 else 1
    chunk = n // n_chunks
    assert chunk % ROW_TILE == 0
    out = None
    for c in range(n_chunks):
        x1, h, rt, rtt, cnt = _out_router(oa, ob, x2, g_out_na, g_out_dil, w_o, g_ffn, w_rg, b_rg, w_re, b_re,
                                          ROW_TILE, c * chunk, chunk)
        y0, y1 = _moe(h, rtt, cnt, w_exp_gate, w_exp_up, w_exp_down, 2 * tm, tm)
        out = _ple(x1, y0, y1, rt, p2, g_ple, w_ple_gate, w_ple_proj, ROW_TILE, c * chunk, out)
    return out.reshape(b, s, d)


def kernel(x, p, g_attn, w_qkv, q_norm_na, k_norm_na, rpb_na, q_norm_dil, k_norm_dil, g_out_na, g_out_dil, w_o,
           g_ffn, w_router_group, b_router_group, w_router_expert, b_router_expert, w_exp_gate, w_exp_up,
           w_exp_down, g_ple, w_ple_gate, w_ple_proj):
    for i in range(p.shape[0]):
        x = _layer(x, p[i], g_attn[i], w_qkv[i], q_norm_na[i], k_norm_na[i], rpb_na[i], q_norm_dil[i],
                   k_norm_dil[i], g_out_na[i], g_out_dil[i], w_o[i], g_ffn[i], w_router_group[i],
                   b_router_group[i], w_router_expert[i], b_router_expert[i], w_exp_gate[i], w_exp_up[i],
                   w_exp_down[i], g_ple[i], w_ple_gate[i], w_ple_proj[i])
    return x
```

```python
import functools

import numpy as np
import jax
import jax.numpy as jnp
from jax import lax
from jax.experimental import pallas as pl
from jax.experimental.pallas import tpu as pltpu
from jax.experimental.pallas import tpu_sc as plsc

HEAD_DIM = 64
N_HEADS = 16
N_HEADS_NA = 8
GRID_W = 64
NA_ROWS = 8
NA_COLS = 16
DIL_PAIRS = ((128, 1), (512, 4), (2048, 16))
ROPE_THETA = 10000.0
N_GROUPS = 4
EXPERTS_PER_GROUP = 8
N_EXPERTS = N_GROUPS * EXPERTS_PER_GROUP
EPS = 1e-6
NEG = -1e30
LOG2_E = 1.4426950408889634

LANES = 128
PAIR_W = 2 * HEAD_DIM
DIL_BLK = 128
SLAB_DIL = 4
ATTN_SLOTS = 4
ATTN_STAGE_LAG = 2
ROUTER_SUB = 512
PLE_SUB = 256
ROW_TILE = 1024
QKV_TILE = 1024
VMEM_LIMIT = 56 * 1024 * 1024
SC_CORES = 2
SC_SUBCORES = 16
SC_CHUNK = 128
EXPERT_LEAD = 3
EXPERT_SLOTS = 4
TOKEN_CHUNKS = 2

F32 = jnp.float32
BF16 = jnp.bfloat16


def _dot(a, b):
    return jnp.dot(a, b, preferred_element_type=F32)


def _dot_nt(a, b):
    return lax.dot_general(a, b, (((1,), (1,)), ((), ())), preferred_element_type=F32)


def _rms(x, gain):
    return x * lax.rsqrt(jnp.mean(x * x, axis=-1, keepdims=True) + EPS) * gain


def _lane_first_half(shape):
    return lax.broadcasted_iota(jnp.int32, shape, len(shape) - 1) < HEAD_DIM


def _qkv_kernel(x_ref, g_ref, w_ref, gq_na_ref, gk_na_ref, gq_dil_ref, gk_dil_ref, cos_ref, sin_ref,
                hsum_ref, qa_ref, ka_ref, va_ref, qb_ref, kb_ref, vb_ref, wb_ref):
    d = x_ref.shape[1]
    half = d // 2
    scale = HEAD_DIM ** -0.5 * LOG2_E

    @pl.when(pl.program_id(0) == 0)
    def _():
        wb_ref[...] = w_ref[...].astype(BF16)

    h = _rms(x_ref[...], g_ref[...]).astype(BF16)

    def proj(col):
        return _dot(h, wb_ref[:, col:col + half])

    def head_norm(y, gain):
        sq = (y * y).astype(BF16)
        w = hsum_ref.shape[0]
        ms = jnp.concatenate([_dot(sq[:, c:c + w], hsum_ref[...]) for c in range(0, half, w)], axis=1)
        return y * lax.rsqrt(ms + EPS) * gain

    def rope(y):
        lane = lax.broadcasted_iota(jnp.int32, (y.shape[0], LANES), 1)
        lower = (lane % HEAD_DIM) < HEAD_DIM // 2
        cos = cos_ref[...]
        sin = sin_ref[...]
        outs = []
        for c in range(0, half, LANES):
            yc = y[:, c:c + LANES]
            up = pltpu.roll(yc, LANES - HEAD_DIM // 2, axis=1)
            down = pltpu.roll(yc, HEAD_DIM // 2, axis=1)
            outs.append(yc * cos + jnp.where(lower, up, down) * sin)
        return jnp.concatenate(outs, axis=1)

    qa_ref[...] = (head_norm(proj(0), gq_na_ref[...]) * scale).astype(BF16)
    qb_ref[...] = (rope(head_norm(proj(half), gq_dil_ref[...])) * scale).astype(BF16)
    ka_ref[...] = head_norm(proj(d), gk_na_ref[...]).astype(BF16)
    kb_ref[...] = rope(head_norm(proj(d + half), gk_dil_ref[...])).astype(BF16)
    va_ref[...] = proj(2 * d).astype(BF16)
    vb_ref[...] = proj(2 * d + half).astype(BF16)


def _qkv_proj(x2, g_attn, w_qkv, gq_na, gk_na, gq_dil, gk_dil, seq, tm):
    n, d = x2.shape
    half = d // 2
    inv = ROPE_THETA ** (-np.arange(HEAD_DIM // 2, dtype=np.float64) / (HEAD_DIM // 2))
    ang = np.arange(seq, dtype=np.float64)[:, None] * inv[None, :]
    cos = jnp.asarray(np.tile(np.cos(ang), (1, LANES // (HEAD_DIM // 2))), F32)
    sin = jnp.asarray(np.tile(np.concatenate([-np.sin(ang), np.sin(ang)], axis=1), (1, LANES // HEAD_DIM)), F32)
    hs_w = 2 * LANES
    blk = np.arange(hs_w) // HEAD_DIM
    hsum = jnp.asarray((blk[:, None] == blk[None, :]).astype(np.float32) / HEAD_DIM, BF16)
    tile_gain = lambda g: jnp.tile(g.astype(F32), half // HEAD_DIM)[None, :]
    steps_per_seq = seq // tm
    full = lambda shape: pl.BlockSpec(shape, lambda i: (0,) * len(shape))
    out = jax.ShapeDtypeStruct((n, half), BF16)
    return pl.pallas_call(
        _qkv_kernel,
        grid=(n // tm,),
        in_specs=[
            pl.BlockSpec((tm, d), lambda i: (i, 0)),
            full((1, d)),
            pl.BlockSpec((d, 3 * d), lambda i: (0, 0), pipeline_mode=pl.Buffered(1)),
            full((1, half)), full((1, half)), full((1, half)), full((1, half)),
            pl.BlockSpec((tm, LANES), lambda i: (i % steps_per_seq, 0)),
            pl.BlockSpec((tm, LANES), lambda i: (i % steps_per_seq, 0)),
            full((hs_w, hs_w)),
        ],
        out_specs=[pl.BlockSpec((tm, half), lambda i: (i, 0))] * 6,
        out_shape=[out] * 6,
        scratch_shapes=[pltpu.VMEM((d, 3 * d), BF16)],
        compiler_params=pltpu.CompilerParams(dimension_semantics=("arbitrary",), vmem_limit_bytes=VMEM_LIMIT),
        name="qkv_proj",
    )(x2, g_attn[None, :].astype(F32), w_qkv, tile_gain(gq_na), tile_gain(gk_na),
      tile_gain(gq_dil), tile_gain(gk_dil), cos, sin, hsum)


def _clip(x, lo, hi):
    return min(max(x, lo), hi)


def _software_pipeline(n_items, stages):
    last_lag = (len(stages) - 1) * ATTN_STAGE_LAG
    assert last_lag <= ATTN_SLOTS
    for t in range(n_items + last_lag):
        for k in reversed(range(len(stages))):
            item = t - k * ATTN_STAGE_LAG
            if 0 <= item < n_items:
                stages[k](item, item % ATTN_SLOTS)


def _qk_stage(q, kwin, bias_a, bias_b, s_ref, slot, first):
    m = q.shape[0]
    w = kwin.shape[0]
    zero = jnp.zeros_like(q)
    s = _dot_nt(jnp.concatenate([jnp.where(first, q, zero), jnp.where(first, zero, q)], axis=0), kwin)
    s_ref[slot, :m, :w] = s[:m] + bias_a
    s_ref[slot, m:, :w] = s[m:] + bias_b


def _softmax_stage(s_ref, p_ref, slot, w):
    m = jnp.max(s_ref[slot, :, :w], axis=-1, keepdims=True)
    p_ref[slot, :, :w] = jnp.exp2(s_ref[slot, :, :w] - m).astype(BF16)
    return m


def _pv_stage(p_ref, slot, v_win, first):
    w = v_win.shape[0]
    r = _dot(p_ref[slot, :, :w], v_win)
    m = r.shape[0] // 2
    return jnp.where(first, r[:m, :PAIR_W], r[m:, :PAIR_W]), jnp.where(first, r[:m, PAIR_W:], r[m:, PAIR_W:])


def _with_ones(v):
    return jnp.concatenate([v, jnp.ones_like(v)], axis=1)


def _na_kernel(q_ref, k_ref, v_ref, bias_ref, o_ref, v1_ref, s_ref, p_ref):
    rows = q_ref.shape[0] // GRID_W
    win = NA_ROWS * GRID_W
    v1_ref[...] = _with_ones(v_ref[...])
    first = _lane_first_half((GRID_W, PAIR_W))

    def slices(r):
        rs = _clip(r - NA_ROWS // 2, 0, rows - NA_ROWS)
        return pl.ds(r * GRID_W, GRID_W), pl.ds(rs * GRID_W, win), r - rs

    def bias(head, delta):
        return jnp.concatenate([bias_ref[head, j - delta + NA_ROWS - 1] for j in range(0, NA_ROWS, 2)], axis=1)

    def qk(r, slot):
        qs, ks, delta = slices(r)
        _qk_stage(q_ref[qs, :], k_ref[ks, :], bias(0, delta), bias(1, delta), s_ref, slot, first)

    def softmax(r, slot):
        _softmax_stage(s_ref, p_ref, slot, win)

    def pv(r, slot):
        qs, ks, _ = slices(r)
        num, den = _pv_stage(p_ref, slot, v1_ref[ks, :], first)
        o_ref[qs, :] = (num / den).astype(o_ref.dtype)

    _software_pipeline(rows, (qk, softmax, pv))


def _na_bias_table(rpb):
    w = np.arange(GRID_W)
    cs = np.clip(w - NA_COLS // 2, 0, GRID_W - NA_COLS)
    kc = np.arange(GRID_W)
    valid = (kc[None, :] >= cs[:, None]) & (kc[None, :] < cs[:, None] + NA_COLS)
    coff = np.clip(kc[None, :] - w[:, None] + NA_COLS - 1, 0, 2 * NA_COLS - 2)
    pick = np.zeros((2 * NA_COLS - 1, GRID_W * GRID_W), np.float32)
    pick[coff.reshape(-1), np.arange(GRID_W * GRID_W)] = 1.0
    tab = jnp.einsum("hrc,cx->hrx", rpb.astype(F32), jnp.asarray(pick), precision=lax.Precision.HIGHEST)
    tab = jnp.where(valid[None, None], tab.reshape(tab.shape[:2] + valid.shape) * LOG2_E, NEG)
    return jnp.concatenate([tab[:, :-1], tab[:, 1:]], axis=-1)


def _na_attention(q, k, v, rpb):
    b, s, width = q.shape
    pairs = width // PAIR_W
    assert s % GRID_W == 0 and s // GRID_W >= NA_ROWS
    bias = _na_bias_table(rpb).reshape(pairs, 2, 2 * NA_ROWS - 2, GRID_W, 2 * GRID_W)
    qkv_spec = pl.BlockSpec((None, s, PAIR_W), lambda bi, j: (bi, 0, j))
    return pl.pallas_call(
        _na_kernel,
        grid=(b, pairs),
        in_specs=[qkv_spec, qkv_spec, qkv_spec,
                  pl.BlockSpec((None, 2, 2 * NA_ROWS - 2, GRID_W, 2 * GRID_W), lambda bi, j: (j, 0, 0, 0, 0))],
        out_specs=pl.BlockSpec((None, s, PAIR_W), lambda bi, j: (bi, 0, j)),
        out_shape=jax.ShapeDtypeStruct((b, s, width), BF16),
        scratch_shapes=[pltpu.VMEM((s, 2 * PAIR_W), BF16),
                        pltpu.VMEM((ATTN_SLOTS, 2 * GRID_W, NA_ROWS * GRID_W), F32),
                        pltpu.VMEM((ATTN_SLOTS, 2 * GRID_W, NA_ROWS * GRID_W), BF16)],
        compiler_params=pltpu.CompilerParams(dimension_semantics=("parallel", "parallel"),
                                             vmem_limit_bytes=VMEM_LIMIT),
        name="na_attn",
    )(q, k, v, bias)


def _dil_kernel(q_ref, k_ref, v_ref, mwide_ref, mfull_ref, o_ref,
                f32a_ref, f32b_ref, qc_ref, kc_ref, vc_ref, acc_ref, den_ref, max_ref, s_ref, p_ref):
    s = q_ref.shape[0]
    for i, ref in enumerate((q_ref, k_ref, v_ref)):
        f32a_ref[i] = ref[...].astype(F32)
    first = _lane_first_half((DIL_BLK, PAIR_W))
    prev_ref, next_ref, prev_dil = f32a_ref, f32b_ref, 1

    for p, (window, dil) in enumerate(DIL_PAIRS):
        radius = window // (2 * dil)
        cls_len = s // dil
        nblk = cls_len // DIL_BLK
        wide = cls_len >= 2 * DIL_BLK
        win = 2 * DIL_BLK if wide else cls_len
        assert radius == DIL_BLK // 2 and cls_len % DIL_BLK == 0 and dil % prev_dil == 0
        assert dil == 1 or dil % SLAB_DIL == 0

        ratio = dil // prev_dil
        keep = ratio > 1 and p + 1 < len(DIL_PAIRS)
        for c in range(dil):
            src = pl.ds((c % prev_dil) * (s // prev_dil) + c // prev_dil, cls_len, stride=ratio)
            dst = pl.ds(c * cls_len, cls_len)
            vals = [prev_ref[i, src, :] for i in range(3)]
            if keep:
                for i in range(3):
                    next_ref[i, dst, :] = vals[i]
            qc_ref[dst, :] = vals[0].astype(BF16)
            kc_ref[dst, :] = vals[1].astype(BF16)
            vc_ref[dst, :] = _with_ones(vals[2].astype(BF16))
        if keep:
            prev_ref, next_ref, prev_dil = next_ref, prev_ref, dil

        def slices(n):
            c, i = divmod(n, nblk)
            base = c * cls_len
            ws = _clip(i * DIL_BLK - radius, 0, cls_len - win)
            kind = 0 if i == 0 else (2 if i == nblk - 1 else 1)
            if dil == 1:
                res = pl.ds(i * DIL_BLK, DIL_BLK)
            else:
                sub = dil // SLAB_DIL
                res = pl.ds((c % SLAB_DIL) * (s // SLAB_DIL) + c // SLAB_DIL + sub * DIL_BLK * i, DIL_BLK, stride=sub)
            return pl.ds(base + i * DIL_BLK, DIL_BLK), pl.ds(base + ws, win), kind, res

        def qk(n, slot):
            qs, ks, kind, _ = slices(n)
            mask = mwide_ref[kind] if wide else mfull_ref[...]
            _qk_stage(qc_ref[qs, :], kc_ref[ks, :], mask, mask, s_ref, slot, first)

        def softmax(n, slot):
            m = _softmax_stage(s_ref, p_ref, slot, win)
            max_ref[p, slices(n)[3], :] = jnp.where(first, m[:DIL_BLK], m[DIL_BLK:])

        def pv(n, slot):
            _, ks, _, tok = slices(n)
            num, den = _pv_stage(p_ref, slot, vc_ref[ks, :], first)
            acc_ref[p, tok, :] = num
            den_ref[p, tok, :] = den

        _software_pipeline(dil * nblk, (qk, softmax, pv))

    out_f32 = f32a_ref.at[0]
    quarter = s // SLAB_DIL
    for c in range(SLAB_DIL):
        rows = [pl.ds(c, quarter, stride=SLAB_DIL) if dil == 1 else pl.ds(c * quarter, quarter)
                for _, dil in DIL_PAIRS]
        maxima = [max_ref[p, rows[p], :] for p in range(len(DIL_PAIRS))]
        m = functools.reduce(jnp.maximum, maxima)
        num = jnp.zeros_like(m)
        den = jnp.zeros_like(m)
        for p in range(len(DIL_PAIRS)):
            w = jnp.exp2(maxima[p] - m)
            num = num + w * acc_ref[p, rows[p], :]
            den = den + w * den_ref[p, rows[p], :]
        out_f32[pl.ds(c, quarter, stride=SLAB_DIL), :] = num / den
    o_ref[...] = out_f32[...].astype(o_ref.dtype)


def _band_mask(kind):
    radius = DIL_BLK // 2
    qq = np.arange(DIL_BLK)[:, None]
    if kind == "full":
        kk = np.arange(DIL_BLK)[None, :]
        shift = 0
    else:
        kk = np.arange(2 * DIL_BLK)[None, :]
        shift = {"first": 0, "inner": radius, "last": DIL_BLK}[kind]
    return np.where(np.abs(kk - qq - shift) <= radius, 0.0, NEG).astype(np.float32)


def _dil_attention(q, k, v):
    b, s, width = q.shape
    pairs = width // PAIR_W
    for window, dil in DIL_PAIRS:
        assert s % (window // 2) == 0 and (s // dil) % DIL_BLK == 0
    mwide = jnp.asarray(np.stack([_band_mask("first"), _band_mask("inner"), _band_mask("last")]))
    mfull = jnp.asarray(_band_mask("full"))
    qkv_spec = pl.BlockSpec((None, s, PAIR_W), lambda bi, j: (bi, 0, j))
    npat = len(DIL_PAIRS)
    return pl.pallas_call(
        _dil_kernel,
        grid=(b, pairs),
        in_specs=[qkv_spec, qkv_spec, qkv_spec,
                  pl.BlockSpec(mwide.shape, lambda bi, j: (0, 0, 0)),
                  pl.BlockSpec(mfull.shape, lambda bi, j: (0, 0))],
        out_specs=pl.BlockSpec((None, s, PAIR_W), lambda bi, j: (bi, 0, j)),
        out_shape=jax.ShapeDtypeStruct((b, s, width), BF16),
        scratch_shapes=[pltpu.VMEM((3, s, PAIR_W), F32)] * 2 + [pltpu.VMEM((s, PAIR_W), BF16)] * 2
        + [pltpu.VMEM((s, 2 * PAIR_W), BF16)] + [pltpu.VMEM((npat, s, PAIR_W), F32)] * 3
        + [pltpu.VMEM((ATTN_SLOTS, 2 * DIL_BLK, 2 * DIL_BLK), F32),
           pltpu.VMEM((ATTN_SLOTS, 2 * DIL_BLK, 2 * DIL_BLK), BF16)],
        compiler_params=pltpu.CompilerParams(dimension_semantics=("parallel", "parallel"),
                                             vmem_limit_bytes=VMEM_LIMIT),
        name="dil_attn",
    )(q, k, v, mwide, mfull)


def _split_bf16(x):
    hi = x.astype(BF16)
    return hi, (x - hi.astype(F32)).astype(BF16)


def _route(logits):
    lane = lax.broadcasted_iota(jnp.int32, logits.shape, 1)
    ninf = jnp.float32(-jnp.inf)

    def first_argmax(vals, vmax):
        return jnp.min(jnp.where(vals == vmax, lane, LANES), axis=-1, keepdims=True)

    gl = jnp.where(lane < N_GROUPS, logits, ninf)
    gmax = jnp.max(gl, axis=-1, keepdims=True)
    gsel = first_argmax(gl, gmax)
    gw = 1.0 / jnp.sum(jnp.exp(gl - gmax), axis=-1, keepdims=True)
    lo = N_GROUPS + EXPERTS_PER_GROUP * gsel
    el = jnp.where((lane >= lo) & (lane < lo + EXPERTS_PER_GROUP), logits, ninf)
    v0 = jnp.max(el, axis=-1, keepdims=True)
    i0 = first_argmax(el, v0)
    el = jnp.where(lane == i0, ninf, el)
    v1 = jnp.max(el, axis=-1, keepdims=True)
    i1 = first_argmax(el, v1)
    t = jnp.exp(v1 - v0)
    w0 = gw / (1.0 + t)
    w1 = gw * t / (1.0 + t)
    e0 = (i0 - N_GROUPS).astype(F32)
    e1 = (i1 - N_GROUPS).astype(F32)
    return jnp.where(lane == 0, e0, jnp.where(lane == 1, e1, jnp.where(lane == 2, w0, jnp.where(lane == 3, w1, 0.0))))


def _out_router_kernel(oa_ref, ob_ref, x_ref, ga_ref, gb_ref, wo_ref, gf_ref, wr_ref, br_ref,
                       x1_ref, h_ref, rt_ref, rtt_ref, cnt_ref):
    half = oa_ref.shape[1]

    @pl.when(pl.program_id(0) == 0)
    def _():
        cnt_ref[...] = jnp.zeros_like(cnt_ref)

    for r0 in range(0, oa_ref.shape[0], ROUTER_SUB):
        rows = pl.ds(r0, ROUTER_SUB)
        ya = _rms(oa_ref[rows, :].astype(F32), ga_ref[...]).astype(BF16)
        yb = _rms(ob_ref[rows, :].astype(F32), gb_ref[...]).astype(BF16)
        x1 = x_ref[rows, :] + _dot(ya, wo_ref[:half, :]) + _dot(yb, wo_ref[half:, :])
        x1_ref[rows, :] = x1
        h = _rms(x1, gf_ref[...])
        h_ref[rows, :] = _pack_bf16_pairs(h)
        h_hi, h_lo = _split_bf16(h)
        both = _dot(h_hi, wr_ref[...])
        logits = both[:, :LANES] + both[:, LANES:] + _dot(h_lo, wr_ref[:, :LANES]) + br_ref[...]
        rt = _route(logits)
        rt_ref[rows, :] = rt
        rtt = rt.T[:rtt_ref.shape[0], :]
        rtt_ref[:, rows] = rtt
        oh0, oh1 = _slot_one_hots(rtt)
        cnt_ref[...] += jnp.sum(oh0 + oh1, axis=1, keepdims=True)


def _out_router(oa, ob, x2, g_na, g_dil, w_o, g_ffn, w_rg, b_rg, w_re, b_re, tm, row0, n):
    d = x2.shape[1]
    half = d // 2
    blk0 = row0 // tm
    wr = jnp.concatenate([w_rg.astype(F32), w_re.astype(F32).transpose(1, 0, 2).reshape(d, N_EXPERTS)], axis=1)
    wr = jnp.pad(wr, ((0, 0), (0, LANES - wr.shape[1])))
    wr_hi = wr.astype(BF16)
    wr_cat = jnp.concatenate([wr_hi, (wr - wr_hi.astype(F32)).astype(BF16)], axis=1)
    br =jnp.pad(jnp.concatenate([b_rg.astype(F32), b_re.astype(F32).reshape(-1)]), (0, LANES - N_GROUPS - N_EXPERTS))
    full = lambda shape: pl.BlockSpec(shape, lambda i: (0,) * len(shape))
    row = lambda w: pl.BlockSpec((tm, w), lambda i: (i, 0))
    row_in = lambda w: pl.BlockSpec((tm, w), lambda i: (i + blk0, 0))
    return pl.pallas_call(
        _out_router_kernel,
        grid=(n // tm,),
        in_specs=[row_in(half), row_in(half), row_in(d), full((1, half)), full((1, half)), full((d, d)),
                  full((1, d)), full((d, 2 * LANES)), full((1, LANES))],
        out_specs=[row(d), row(half), row(LANES), pl.BlockSpec((8, tm), lambda i: (0, i)), full((LANES, LANES))],
        out_shape=[jax.ShapeDtypeStruct((n, d), F32), jax.ShapeDtypeStruct((n, half), jnp.uint32),
                   jax.ShapeDtypeStruct((n, LANES), F32), jax.ShapeDtypeStruct((8, n), F32),
                   jax.ShapeDtypeStruct((LANES, LANES), F32)],
        compiler_params=pltpu.CompilerParams(dimension_semantics=("arbitrary",), vmem_limit_bytes=VMEM_LIMIT),
        name="out_router",
    )(oa, ob, x2, g_na[None, :].astype(F32), g_dil[None, :].astype(F32), w_o.astype(BF16),
      g_ffn[None, :].astype(F32), wr_cat, br[None, :])


def _pack_bf16_pairs(x):
    w = x.shape[1] // 2
    bits = lax.bitcast_convert_type(x.astype(BF16).astype(F32), jnp.uint32)
    return bits[:, :w] | (bits[:, w:] >> 16)


def _unpack_bf16_pairs(u):
    hi = lax.bitcast_convert_type(u & jnp.uint32(0xFFFF0000), F32)
    lo = lax.bitcast_convert_type(u << 16, F32)
    return hi, lo


def _slot_one_hots(rtt):
    sub = lax.broadcasted_iota(jnp.int32, (LANES, rtt.shape[1]), 0).astype(F32)
    return (sub == rtt[0:1, :]).astype(F32), (sub == rtt[1:2, :]).astype(F32)


def _position_kernel(rtt_ref, start_ref, pos_ref, base_ref):
    tm = rtt_ref.shape[1]

    @pl.when(pl.program_id(0) == 0)
    def _():
        base_ref[...] = start_ref[...]

    oh0, oh1 = _slot_one_hots(rtt_ref[...])
    oh = oh0 + oh1
    earlier = lax.broadcasted_iota(jnp.int32, (tm, tm), 0) < lax.broadcasted_iota(jnp.int32, (tm, tm), 1)
    before = _dot(oh.astype(BF16), earlier.astype(BF16)) + base_ref[:, 0:1]
    p0 = jnp.sum(before * oh0, axis=0, keepdims=True)
    p1 = jnp.sum(before * oh1, axis=0, keepdims=True)
    row = lax.broadcasted_iota(jnp.int32, pos_ref.shape, 0)
    pos_ref[...] = jnp.where(row == 0, p0, jnp.where(row == 1, p1, 0.0)).astype(jnp.int32)
    base_ref[...] += jnp.sum(oh, axis=1, keepdims=True)


def _expert_positions(rtt, starts, tm):
    n = rtt.shape[1]
    return pl.pallas_call(
        _position_kernel,
        grid=(n // tm,),
        in_specs=[pl.BlockSpec((8, tm), lambda i: (0, i)), pl.BlockSpec((LANES, LANES), lambda i: (0, 0))],
        out_specs=pl.BlockSpec((8, tm), lambda i: (0, i)),
        out_shape=jax.ShapeDtypeStruct((8, n), jnp.int32),
        scratch_shapes=[pltpu.VMEM((LANES, LANES), F32)],
        compiler_params=pltpu.CompilerParams(dimension_semantics=("arbitrary",)),
        name="expert_positions",
    )(rtt, starts)


def _sc_mesh():
    return plsc.VectorSubcoreMesh(core_axis_name="c", subcore_axis_name="s",
                                  num_cores=SC_CORES, num_subcores=SC_SUBCORES)


def _sc_dispatch(hp, pos0, pos1, n_out):
    n, w = hp.shape
    workers = SC_CORES * SC_SUBCORES
    per = n // workers
    chunks = per // SC_CHUNK
    assert n % (workers * SC_CHUNK) == 0

    @functools.partial(
        pl.kernel, out_type=jax.ShapeDtypeStruct((n_out, w), hp.dtype), mesh=_sc_mesh(),
        scratch_types=[pltpu.VMEM((chunks, SC_CHUNK), jnp.int32), pltpu.VMEM((chunks, SC_CHUNK), jnp.int32),
                       pltpu.VMEM((SC_CHUNK, w), hp.dtype)],
        name="moe_dispatch")
    def body(h_hbm, p0_hbm, p1_hbm, xs_hbm, i0_v, i1_v, rows_v):
        wid = lax.axis_index("s") * SC_CORES + lax.axis_index("c")
        pltpu.sync_copy(p0_hbm.at[wid], i0_v)
        pltpu.sync_copy(p1_hbm.at[wid], i1_v)

        @pl.loop(0, chunks)
        def _(j):
            pltpu.sync_copy(h_hbm.at[pl.ds(wid * per + j * SC_CHUNK, SC_CHUNK)], rows_v)
            pltpu.sync_copy(rows_v, xs_hbm.at[i0_v.at[j]])
            pltpu.sync_copy(rows_v, xs_hbm.at[i1_v.at[j]])

    return body(hp, pos0.reshape(workers, chunks, SC_CHUNK), pos1.reshape(workers, chunks, SC_CHUNK))


def _sc_collect(ys, pos0, pos1):
    n = pos0.shape[0]
    w = ys.shape[1]
    workers = SC_CORES * SC_SUBCORES
    per = n // workers
    chunks = per // SC_CHUNK
    out = jax.ShapeDtypeStruct((n, w), ys.dtype)

    @functools.partial(
        pl.kernel, out_type=(out, out), mesh=_sc_mesh(),
        scratch_types=[pltpu.VMEM((chunks, SC_CHUNK), jnp.int32), pltpu.VMEM((chunks, SC_CHUNK), jnp.int32),
                       pltpu.VMEM((SC_CHUNK, w), ys.dtype)],
        name="moe_collect")
    def body(ys_hbm, p0_hbm, p1_hbm, y0_hbm, y1_hbm, i0_v, i1_v, rows_v):
        wid = lax.axis_index("s") * SC_CORES + lax.axis_index("c")
        pltpu.sync_copy(p0_hbm.at[wid], i0_v)
        pltpu.sync_copy(p1_hbm.at[wid], i1_v)

        @pl.loop(0, chunks)
        def _(j):
            dst = pl.ds(wid * per + j * SC_CHUNK, SC_CHUNK)
            pltpu.sync_copy(ys_hbm.at[i0_v.at[j]], rows_v)
            pltpu.sync_copy(rows_v, y0_hbm.at[dst])
            pltpu.sync_copy(ys_hbm.at[i1_v.at[j]], rows_v)
            pltpu.sync_copy(rows_v, y1_hbm.at[dst])

    return body(ys, pos0.reshape(workers, chunks, SC_CHUNK), pos1.reshape(workers, chunks, SC_CHUNK))


def _experts_kernel(plan_ref, nt_ref, xs_ref, wg_ref, wu_ref, wd_ref, ys_ref, wg_bf, wu_bf, wd_bf):
    g = pl.program_id(0)
    prev = jnp.maximum(g - 1, 0)

    for row, (src, dst) in enumerate(((wg_ref, wg_bf), (wu_ref, wu_bf), (wd_ref, wd_bf))):
        @pl.when((g == 0) | (plan_ref[row, g] != plan_ref[row, prev]))
        def _(row=row, src=src, dst=dst):
            dst[plan_ref[row + 3, g]] = src[...].astype(BF16)

    @pl.when((g >= EXPERT_LEAD) & (g < nt_ref[0] + EXPERT_LEAD))
    def _():
        slot = plan_ref[6, g]
        half = wg_bf.shape[1] // 2
        hi, lo = _unpack_bf16_pairs(xs_ref[...])
        hi = hi.astype(BF16)
        lo = lo.astype(BF16)
        a = _dot(hi, wg_bf[slot, :half, :]) + _dot(lo, wg_bf[slot, half:, :])
        u = _dot(hi, wu_bf[slot, :half, :]) + _dot(lo, wu_bf[slot, half:, :])
        act = (a * jax.nn.sigmoid(a) * u).astype(BF16)
        ys_ref[...] = _pack_bf16_pairs(_dot(act, wd_bf[slot]))


def _experts(xs, tile_expert, n_tiles, w_gate, w_up, w_down, tmg):
    rows, w = xs.shape
    ne, d, de = w_gate.shape
    steps = rows // tmg + EXPERT_LEAD
    run = jnp.concatenate([jnp.zeros((1,), jnp.int32),
                           jnp.cumsum((tile_expert[1:] != tile_expert[:-1]).astype(jnp.int32))])
    tile = jnp.clip(jnp.arange(-EXPERT_LEAD, steps, dtype=jnp.int32), 0, n_tiles[0] - 1)
    expert, slot = tile_expert[tile], run[tile] % EXPERT_SLOTS
    at = lambda a, lag: a[EXPERT_LEAD - lag:EXPERT_LEAD - lag + steps]
    plan = jnp.stack([at(expert, lag) for lag in range(EXPERT_LEAD)]
                     + [at(slot, lag) for lag in range(EXPERT_LEAD + 1)]
                     + [at(tile, EXPERT_LEAD)]).astype(jnp.int32)
    return pl.pallas_call(
        _experts_kernel,
        grid_spec=pltpu.PrefetchScalarGridSpec(
            num_scalar_prefetch=2,
            grid=(steps,),
            in_specs=[pl.BlockSpec((tmg, w), lambda s, plan, nt: (plan[2 * EXPERT_LEAD + 1, s], 0)),
                      pl.BlockSpec((None, d, de), lambda s, plan, nt: (plan[0, s], 0, 0)),
                      pl.BlockSpec((None, d, de), lambda s, plan, nt: (plan[1, s], 0, 0)),
                      pl.BlockSpec((None, de, d), lambda s, plan, nt: (plan[2, s], 0, 0))],
            out_specs=pl.BlockSpec((tmg, w), lambda s, plan, nt: (plan[2 * EXPERT_LEAD + 1, s], 0)),
            scratch_shapes=[pltpu.VMEM((EXPERT_SLOTS, d, de), BF16), pltpu.VMEM((EXPERT_SLOTS, d, de), BF16),
                            pltpu.VMEM((EXPERT_SLOTS, de, d), BF16)],
        ),
        out_shape=jax.ShapeDtypeStruct((rows, w), jnp.uint32),
        compiler_params=pltpu.CompilerParams(dimension_semantics=("arbitrary",), vmem_limit_bytes=VMEM_LIMIT),
        name="experts",
    )(plan, n_tiles, xs, w_gate, w_up, w_down)


def _moe(hp, rtt, cnt, w_gate, w_up, w_down, tm, tmg):
    n = hp.shape[0]
    ne = w_gate.shape[0]
    counts = cnt[:ne, 0].astype(jnp.int32)
    padded = (counts + tmg - 1) // tmg * tmg
    ends = jnp.cumsum(padded)
    starts = jnp.pad((ends - padded).astype(F32), (0, LANES - ne))
    pos = _expert_positions(rtt, jnp.broadcast_to(starts[:, None], (LANES, LANES)), tm)
    pos0, pos1 = pos[0], pos[1]
    rows = 2 * n + ne * tmg
    tile_start = jnp.arange(rows // tmg, dtype=jnp.int32) * tmg
    tile_expert = jnp.minimum(jnp.sum(tile_start[:, None] >= ends[None, :], axis=1), ne - 1).astype(jnp.int32)
    n_tiles = (ends[-1:] // tmg).astype(jnp.int32)
    xs = _sc_dispatch(hp, pos0, pos1, rows)
    ys = _experts(xs, tile_expert, n_tiles, w_gate, w_up, w_down, tmg)
    return _sc_collect(ys, pos0, pos1)


def _ple_kernel(x1_ref, y0_ref, y1_ref, rt_ref, p_ref, g_ref, wg_ref, wp_ref, o_ref):
    for r0 in range(0, x1_ref.shape[0], PLE_SUB):
        rows = pl.ds(r0, PLE_SUB)
        rt = rt_ref[rows, :]
        y0 = jnp.concatenate(_unpack_bf16_pairs(y0_ref[rows, :]), axis=1)
        y1 = jnp.concatenate(_unpack_bf16_pairs(y1_ref[rows, :]), axis=1)
        x2 = x1_ref[rows, :] + rt[:, 2:3] * y0 + rt[:, 3:4] * y1
        gate = jax.nn.sigmoid(_dot(_rms(x2, g_ref[...]).astype(BF16), wg_ref[...]))
        o_ref[rows, :] = x2 + gate * _dot(p_ref[rows, :].astype(BF16), wp_ref[...])


def _ple_kernel_into(prev_ref, *refs):
    del prev_ref
    _ple_kernel(*refs)


def _ple(x1, y0, y1, rt, p2, g_ple, w_gate, w_proj, tm, row0, out_prev):
    n, d = x1.shape
    n_all, dp = p2.shape
    blk0 = row0 // tm
    full = lambda shape: pl.BlockSpec(shape, lambda i: (0,) * len(shape))
    row = lambda w: pl.BlockSpec((tm, w), lambda i: (i, 0))
    row_all = lambda w: pl.BlockSpec((tm, w), lambda i: (i + blk0, 0))
    in_specs = [row(d), row(d // 2), row(d // 2), row(LANES), row_all(dp), full((1, d)), full((d, d)), full((dp, d))]
    args = (x1, y0, y1, rt, p2, g_ple[None, :].astype(F32), w_gate.astype(BF16), w_proj.astype(BF16))
    if out_prev is not None:
        in_specs = [pl.BlockSpec(memory_space=pl.ANY)] + in_specs
        args = (out_prev,) + args
    return pl.pallas_call(
        _ple_kernel if out_prev is None else _ple_kernel_into,
        grid=(n // tm,),
        in_specs=in_specs,
        out_specs=row_all(d),
        out_shape=jax.ShapeDtypeStruct((n_all, d), F32),
        input_output_aliases={} if out_prev is None else {0: 0},
        compiler_params=pltpu.CompilerParams(dimension_semantics=("parallel",), vmem_limit_bytes=VMEM_LIMIT),
        name="ple",
    )(*args)


def _layer(x, p_l, g_attn, w_qkv, q_norm_na, k_norm_na, rpb_na, q_norm_dil, k_norm_dil, g_out_na, g_out_dil,
           w_o, g_ffn, w_rg, b_rg, w_re, b_re, w_exp_gate, w_exp_up, w_exp_down, g_ple, w_ple_gate, w_ple_proj):
    b, s, d = x.shape
    n = b * s
    half = d // 2
    assert d == N_HEADS * HEAD_DIM and half == N_HEADS_NA * HEAD_DIM
    tm = 512
    assert s % tm == 0 and s % QKV_TILE == 0
    x2 = x.reshape(n, d)
    qa, ka, va, qb, kb, vb = _qkv_proj(x2, g_attn, w_qkv, q_norm_na, k_norm_na, q_norm_dil, k_norm_dil, s, QKV_TILE)
    seq = lambda t: t.reshape(b, s, half)
    oa = _na_attention(seq(qa), seq(ka), seq(va), rpb_na).reshape(n, half)
    ob = _dil_attention(seq(qb), seq(kb), seq(vb)).reshape(n, half)
    p2 = p_l.reshape(n, -1)
    chunk_unit = SC_CORES * SC_SUBCORES * SC_CHUNK
    n_chunks = TOKEN_CHUNKS if n % (TOKEN_CHUNKS * chunk_unit) == 0 else 1
    chunk = n // n_chunks
    assert chunk % ROW_TILE == 0
    out = None
    for c in range(n_chunks):
        x1, h, rt, rtt, cnt = _out_router(oa, ob, x2, g_out_na, g_out_dil, w_o, g_ffn, w_rg, b_rg, w_re, b_re,
                                          ROW_TILE, c * chunk, chunk)
        y0, y1 = _moe(h, rtt, cnt, w_exp_gate, w_exp_up, w_exp_down, 2 * tm, tm)
        out = _ple(x1, y0, y1, rt, p2, g_ple, w_ple_gate, w_ple_proj, ROW_TILE, c * chunk, out)
    return out.reshape(b, s, d)


def kernel(x, p, g_attn, w_qkv, q_norm_na, k_norm_na, rpb_na, q_norm_dil, k_norm_dil, g_out_na, g_out_dil, w_o,
           g_ffn, w_router_group, b_router_group, w_router_expert, b_router_expert, w_exp_gate, w_exp_up,
           w_exp_down, g_ple, w_ple_gate, w_ple_proj):
    for i in range(p.shape[0]):
        x = _layer(x, p[i], g_attn[i], w_qkv[i], q_norm_na[i], k_norm_na[i], rpb_na[i], q_norm_dil[i],
                   k_norm_dil[i], g_out_na[i], g_out_dil[i], w_o[i], g_ffn[i], w_router_group[i],
                   b_router_group[i], w_router_expert[i], b_router_expert[i], w_exp_gate[i], w_exp_up[i],
                   w_exp_down[i], g_ple[i], w_ple_gate[i], w_ple_proj[i])
    return x
```

```python
import functools

import numpy as np
import jax
import jax.numpy as jnp
from jax import lax
from jax.experimental import pallas as pl
from jax.experimental.pallas import tpu as pltpu
from jax.experimental.pallas import tpu_sc as plsc

HEAD_DIM = 64
N_HEADS = 16
N_HEADS_NA = 8
GRID_W = 64
NA_ROWS = 8
NA_COLS = 16
DIL_PAIRS = ((128, 1), (512, 4), (2048, 16))
ROPE_THETA = 10000.0
N_GROUPS = 4
EXPERTS_PER_GROUP = 8
N_EXPERTS = N_GROUPS * EXPERTS_PER_GROUP
EPS = 1e-6
NEG = -1e30
LOG2_E = 1.4426950408889634

LANES = 128
PAIR_W = 2 * HEAD_DIM
DIL_BLK = 128
SLAB_DIL = 4
ATTN_SLOTS = 4
ATTN_STAGE_LAG = 2
ROUTER_SUB = 512
PLE_SUB = 256
ROW_TILE = 1024
QKV_TILE = 1024
VMEM_LIMIT = 56 * 1024 * 1024
SC_CORES = 2
SC_SUBCORES = 16
SC_CHUNK = 128
EXPERT_LEAD = 3
EXPERT_SLOTS = 4
TOKEN_SPLIT = (3, 1)

F32 = jnp.float32
BF16 = jnp.bfloat16


def _dot(a, b):
    return jnp.dot(a, b, preferred_element_type=F32)


def _dot_nt(a, b):
    return lax.dot_general(a, b, (((1,), (1,)), ((), ())), preferred_element_type=F32)


def _rms(x, gain):
    return x * lax.rsqrt(jnp.mean(x * x, axis=-1, keepdims=True) + EPS) * gain


def _lane_first_half(shape):
    return lax.broadcasted_iota(jnp.int32, shape, len(shape) - 1) < HEAD_DIM


def _qkv_kernel(x_ref, g_ref, w_ref, gq_na_ref, gk_na_ref, gq_dil_ref, gk_dil_ref, cos_ref, sin_ref,
                hsum_ref, qa_ref, ka_ref, va_ref, qb_ref, kb_ref, vb_ref, wb_ref):
    d = x_ref.shape[1]
    half = d // 2
    scale = HEAD_DIM ** -0.5 * LOG2_E

    @pl.when(pl.program_id(0) == 0)
    def _():
        wb_ref[...] = w_ref[...].astype(BF16)

    h = _rms(x_ref[...], g_ref[...]).astype(BF16)

    def proj(col):
        return _dot(h, wb_ref[:, col:col + half])

    def head_norm(y, gain):
        sq = (y * y).astype(BF16)
        w = hsum_ref.shape[0]
        ms = jnp.concatenate([_dot(sq[:, c:c + w], hsum_ref[...]) for c in range(0, half, w)], axis=1)
        return y * lax.rsqrt(ms + EPS) * gain

    def rope(y):
        lane = lax.broadcasted_iota(jnp.int32, (y.shape[0], LANES), 1)
        lower = (lane % HEAD_DIM) < HEAD_DIM // 2
        cos = cos_ref[...]
        sin = sin_ref[...]
        outs = []
        for c in range(0, half, LANES):
            yc = y[:, c:c + LANES]
            up = pltpu.roll(yc, LANES - HEAD_DIM // 2, axis=1)
            down = pltpu.roll(yc, HEAD_DIM // 2, axis=1)
            outs.append(yc * cos + jnp.where(lower, up, down) * sin)
        return jnp.concatenate(outs, axis=1)

    qa_ref[...] = (head_norm(proj(0), gq_na_ref[...]) * scale).astype(BF16)
    qb_ref[...] = (rope(head_norm(proj(half), gq_dil_ref[...])) * scale).astype(BF16)
    ka_ref[...] = head_norm(proj(d), gk_na_ref[...]).astype(BF16)
    kb_ref[...] = rope(head_norm(proj(d + half), gk_dil_ref[...])).astype(BF16)
    va_ref[...] = proj(2 * d).astype(BF16)
    vb_ref[...] = proj(2 * d + half).astype(BF16)


def _qkv_proj(x2, g_attn, w_qkv, gq_na, gk_na, gq_dil, gk_dil, seq, tm):
    n, d = x2.shape
    half = d // 2
    inv = ROPE_THETA ** (-np.arange(HEAD_DIM // 2, dtype=np.float64) / (HEAD_DIM // 2))
    ang = np.arange(seq, dtype=np.float64)[:, None] * inv[None, :]
    cos = jnp.asarray(np.tile(np.cos(ang), (1, LANES // (HEAD_DIM // 2))), F32)
    sin = jnp.asarray(np.tile(np.concatenate([-np.sin(ang), np.sin(ang)], axis=1), (1, LANES // HEAD_DIM)), F32)
    hs_w = 2 * LANES
    blk = np.arange(hs_w) // HEAD_DIM
    hsum = jnp.asarray((blk[:, None] == blk[None, :]).astype(np.float32) / HEAD_DIM, BF16)
    tile_gain = lambda g: jnp.tile(g.astype(F32), half // HEAD_DIM)[None, :]
    steps_per_seq = seq // tm
    full = lambda shape: pl.BlockSpec(shape, lambda i: (0,) * len(shape))
    out = jax.ShapeDtypeStruct((n, half), BF16)
    return pl.pallas_call(
        _qkv_kernel,
        grid=(n // tm,),
        in_specs=[
            pl.BlockSpec((tm, d), lambda i: (i, 0)),
            full((1, d)),
            pl.BlockSpec((d, 3 * d), lambda i: (0, 0), pipeline_mode=pl.Buffered(1)),
            full((1, half)), full((1, half)), full((1, half)), full((1, half)),
            pl.BlockSpec((tm, LANES), lambda i: (i % steps_per_seq, 0)),
            pl.BlockSpec((tm, LANES), lambda i: (i % steps_per_seq, 0)),
            full((hs_w, hs_w)),
        ],
        out_specs=[pl.BlockSpec((tm, half), lambda i: (i, 0))] * 6,
        out_shape=[out] * 6,
        scratch_shapes=[pltpu.VMEM((d, 3 * d), BF16)],
        compiler_params=pltpu.CompilerParams(dimension_semantics=("arbitrary",), vmem_limit_bytes=VMEM_LIMIT),
        name="qkv_proj",
    )(x2, g_attn[None, :].astype(F32), w_qkv, tile_gain(gq_na), tile_gain(gk_na),
      tile_gain(gq_dil), tile_gain(gk_dil), cos, sin, hsum)


def _clip(x, lo, hi):
    return min(max(x, lo), hi)


def _software_pipeline(n_items, stages):
    last_lag = (len(stages) - 1) * ATTN_STAGE_LAG
    assert last_lag <= ATTN_SLOTS
    for t in range(n_items + last_lag):
        for k in reversed(range(len(stages))):
            item = t - k * ATTN_STAGE_LAG
            if 0 <= item < n_items:
                stages[k](item, item % ATTN_SLOTS)


def _qk_stage(q, kwin, bias_a, bias_b, s_ref, slot, first):
    m = q.shape[0]
    w = kwin.shape[0]
    zero = jnp.zeros_like(q)
    s = _dot_nt(jnp.concatenate([jnp.where(first, q, zero), jnp.where(first, zero, q)], axis=0), kwin)
    s_ref[slot, :m, :w] = s[:m] + bias_a
    s_ref[slot, m:, :w] = s[m:] + bias_b


def _softmax_stage(s_ref, p_ref, slot, w):
    m = jnp.max(s_ref[slot, :, :w], axis=-1, keepdims=True)
    p_ref[slot, :, :w] = jnp.exp2(s_ref[slot, :, :w] - m).astype(BF16)
    return m


def _pv_stage(p_ref, slot, v_win, first):
    w = v_win.shape[0]
    r = _dot(p_ref[slot, :, :w], v_win)
    m = r.shape[0] // 2
    return jnp.where(first, r[:m, :PAIR_W], r[m:, :PAIR_W]), jnp.where(first, r[:m, PAIR_W:], r[m:, PAIR_W:])


def _with_ones(v):
    return jnp.concatenate([v, jnp.ones_like(v)], axis=1)


def _na_kernel(q_ref, k_ref, v_ref, bias_ref, o_ref, v1_ref, s_ref, p_ref):
    rows = q_ref.shape[0] // GRID_W
    win = NA_ROWS * GRID_W
    v1_ref[...] = _with_ones(v_ref[...])
    first = _lane_first_half((GRID_W, PAIR_W))

    def slices(r):
        rs = _clip(r - NA_ROWS // 2, 0, rows - NA_ROWS)
        return pl.ds(r * GRID_W, GRID_W), pl.ds(rs * GRID_W, win), r - rs

    def bias(head, delta):
        return jnp.concatenate([bias_ref[head, j - delta + NA_ROWS - 1] for j in range(0, NA_ROWS, 2)], axis=1)

    def qk(r, slot):
        qs, ks, delta = slices(r)
        _qk_stage(q_ref[qs, :], k_ref[ks, :], bias(0, delta), bias(1, delta), s_ref, slot, first)

    def softmax(r, slot):
        _softmax_stage(s_ref, p_ref, slot, win)

    def pv(r, slot):
        qs, ks, _ = slices(r)
        num, den = _pv_stage(p_ref, slot, v1_ref[ks, :], first)
        o_ref[qs, :] = (num / den).astype(o_ref.dtype)

    _software_pipeline(rows, (qk, softmax, pv))


def _na_bias_table(rpb):
    w = np.arange(GRID_W)
    cs = np.clip(w - NA_COLS // 2, 0, GRID_W - NA_COLS)
    kc = np.arange(GRID_W)
    valid = (kc[None, :] >= cs[:, None]) & (kc[None, :] < cs[:, None] + NA_COLS)
    coff = np.clip(kc[None, :] - w[:, None] + NA_COLS - 1, 0, 2 * NA_COLS - 2)
    pick = np.zeros((2 * NA_COLS - 1, GRID_W * GRID_W), np.float32)
    pick[coff.reshape(-1), np.arange(GRID_W * GRID_W)] = 1.0
    tab = jnp.einsum("hrc,cx->hrx", rpb.astype(F32), jnp.asarray(pick), precision=lax.Precision.HIGHEST)
    tab = jnp.where(valid[None, None], tab.reshape(tab.shape[:2] + valid.shape) * LOG2_E, NEG)
    return jnp.concatenate([tab[:, :-1], tab[:, 1:]], axis=-1)


def _na_attention(q, k, v, rpb):
    b, s, width = q.shape
    pairs = width // PAIR_W
    assert s % GRID_W == 0 and s // GRID_W >= NA_ROWS
    bias = _na_bias_table(rpb).reshape(pairs, 2, 2 * NA_ROWS - 2, GRID_W, 2 * GRID_W)
    qkv_spec = pl.BlockSpec((None, s, PAIR_W), lambda bi, j: (bi, 0, j))
    return pl.pallas_call(
        _na_kernel,
        grid=(b, pairs),
        in_specs=[qkv_spec, qkv_spec, qkv_spec,
                  pl.BlockSpec((None, 2, 2 * NA_ROWS - 2, GRID_W, 2 * GRID_W), lambda bi, j: (j, 0, 0, 0, 0))],
        out_specs=pl.BlockSpec((None, s, PAIR_W), lambda bi, j: (bi, 0, j)),
        out_shape=jax.ShapeDtypeStruct((b, s, width), BF16),
        scratch_shapes=[pltpu.VMEM((s, 2 * PAIR_W), BF16),
                        pltpu.VMEM((ATTN_SLOTS, 2 * GRID_W, NA_ROWS * GRID_W), F32),
                        pltpu.VMEM((ATTN_SLOTS, 2 * GRID_W, NA_ROWS * GRID_W), BF16)],
        compiler_params=pltpu.CompilerParams(dimension_semantics=("parallel", "parallel"),
                                             vmem_limit_bytes=VMEM_LIMIT),
        name="na_attn",
    )(q, k, v, bias)


def _dil_kernel(q_ref, k_ref, v_ref, mwide_ref, mfull_ref, o_ref,
                f32a_ref, f32b_ref, qc_ref, kc_ref, vc_ref, acc_ref, den_ref, max_ref, s_ref, p_ref):
    s = q_ref.shape[0]
    for i, ref in enumerate((q_ref, k_ref, v_ref)):
        f32a_ref[i] = ref[...].astype(F32)
    first = _lane_first_half((DIL_BLK, PAIR_W))
    prev_ref, next_ref, prev_dil = f32a_ref, f32b_ref, 1

    for p, (window, dil) in enumerate(DIL_PAIRS):
        radius = window // (2 * dil)
        cls_len = s // dil
        nblk = cls_len // DIL_BLK
        wide = cls_len >= 2 * DIL_BLK
        win = 2 * DIL_BLK if wide else cls_len
        assert radius == DIL_BLK // 2 and cls_len % DIL_BLK == 0 and dil % prev_dil == 0
        assert dil == 1 or dil % SLAB_DIL == 0

        ratio = dil // prev_dil
        keep = ratio > 1 and p + 1 < len(DIL_PAIRS)
        for c in range(dil):
            src = pl.ds((c % prev_dil) * (s // prev_dil) + c // prev_dil, cls_len, stride=ratio)
            dst = pl.ds(c * cls_len, cls_len)
            vals = [prev_ref[i, src, :] for i in range(3)]
            if keep:
                for i in range(3):
                    next_ref[i, dst, :] = vals[i]
            qc_ref[dst, :] = vals[0].astype(BF16)
            kc_ref[dst, :] = vals[1].astype(BF16)
            vc_ref[dst, :] = _with_ones(vals[2].astype(BF16))
        if keep:
            prev_ref, next_ref, prev_dil = next_ref, prev_ref, dil

        def slices(n):
            c, i = divmod(n, nblk)
            base = c * cls_len
            ws = _clip(i * DIL_BLK - radius, 0, cls_len - win)
            kind = 0 if i == 0 else (2 if i == nblk - 1 else 1)
            if dil == 1:
                res = pl.ds(i * DIL_BLK, DIL_BLK)
            else:
                sub = dil // SLAB_DIL
                res = pl.ds((c % SLAB_DIL) * (s // SLAB_DIL) + c // SLAB_DIL + sub * DIL_BLK * i, DIL_BLK, stride=sub)
            return pl.ds(base + i * DIL_BLK, DIL_BLK), pl.ds(base + ws, win), kind, res

        def qk(n, slot):
            qs, ks, kind, _ = slices(n)
            mask = mwide_ref[kind] if wide else mfull_ref[...]
            _qk_stage(qc_ref[qs, :], kc_ref[ks, :], mask, mask, s_ref, slot, first)

        def softmax(n, slot):
            m = _softmax_stage(s_ref, p_ref, slot, win)
            max_ref[p, slices(n)[3], :] = jnp.where(first, m[:DIL_BLK], m[DIL_BLK:])

        def pv(n, slot):
            _, ks, _, tok = slices(n)
            num, den = _pv_stage(p_ref, slot, vc_ref[ks, :], first)
            acc_ref[p, tok, :] = num
            den_ref[p, tok, :] = den

        _software_pipeline(dil * nblk, (qk, softmax, pv))

    out_f32 = f32a_ref.at[0]
    quarter = s // SLAB_DIL
    for c in range(SLAB_DIL):
        rows = [pl.ds(c, quarter, stride=SLAB_DIL) if dil == 1 else pl.ds(c * quarter, quarter)
                for _, dil in DIL_PAIRS]
        maxima = [max_ref[p, rows[p], :] for p in range(len(DIL_PAIRS))]
        m = functools.reduce(jnp.maximum, maxima)
        num = jnp.zeros_like(m)
        den = jnp.zeros_like(m)
        for p in range(len(DIL_PAIRS)):
            w = jnp.exp2(maxima[p] - m)
            num = num + w * acc_ref[p, rows[p], :]
            den = den + w * den_ref[p, rows[p], :]
        out_f32[pl.ds(c, quarter, stride=SLAB_DIL), :] = num / den
    o_ref[...] = out_f32[...].astype(o_ref.dtype)


def _band_mask(kind):
    radius = DIL_BLK // 2
    qq = np.arange(DIL_BLK)[:, None]
    if kind == "full":
        kk = np.arange(DIL_BLK)[None, :]
        shift = 0
    else:
        kk = np.arange(2 * DIL_BLK)[None, :]
        shift = {"first": 0, "inner": radius, "last": DIL_BLK}[kind]
    return np.where(np.abs(kk - qq - shift) <= radius, 0.0, NEG).astype(np.float32)


def _dil_attention(q, k, v):
    b, s, width = q.shape
    pairs = width // PAIR_W
    for window, dil in DIL_PAIRS:
        assert s % (window // 2) == 0 and (s // dil) % DIL_BLK == 0
    mwide = jnp.asarray(np.stack([_band_mask("first"), _band_mask("inner"), _band_mask("last")]))
    mfull = jnp.asarray(_band_mask("full"))
    qkv_spec = pl.BlockSpec((None, s, PAIR_W), lambda bi, j: (bi, 0, j))
    npat = len(DIL_PAIRS)
    return pl.pallas_call(
        _dil_kernel,
        grid=(b, pairs),
        in_specs=[qkv_spec, qkv_spec, qkv_spec,
                  pl.BlockSpec(mwide.shape, lambda bi, j: (0, 0, 0)),
                  pl.BlockSpec(mfull.shape, lambda bi, j: (0, 0))],
        out_specs=pl.BlockSpec((None, s, PAIR_W), lambda bi, j: (bi, 0, j)),
        out_shape=jax.ShapeDtypeStruct((b, s, width), BF16),
        scratch_shapes=[pltpu.VMEM((3, s, PAIR_W), F32)] * 2 + [pltpu.VMEM((s, PAIR_W), BF16)] * 2
        + [pltpu.VMEM((s, 2 * PAIR_W), BF16)] + [pltpu.VMEM((npat, s, PAIR_W), F32)] * 3
        + [pltpu.VMEM((ATTN_SLOTS, 2 * DIL_BLK, 2 * DIL_BLK), F32),
           pltpu.VMEM((ATTN_SLOTS, 2 * DIL_BLK, 2 * DIL_BLK), BF16)],
        compiler_params=pltpu.CompilerParams(dimension_semantics=("parallel", "parallel"),
                                             vmem_limit_bytes=VMEM_LIMIT),
        name="dil_attn",
    )(q, k, v, mwide, mfull)


def _split_bf16(x):
    hi = x.astype(BF16)
    return hi, (x - hi.astype(F32)).astype(BF16)


def _route(logits):
    lane = lax.broadcasted_iota(jnp.int32, logits.shape, 1)
    ninf = jnp.float32(-jnp.inf)

    def first_argmax(vals, vmax):
        return jnp.min(jnp.where(vals == vmax, lane, LANES), axis=-1, keepdims=True)

    gl = jnp.where(lane < N_GROUPS, logits, ninf)
    gmax = jnp.max(gl, axis=-1, keepdims=True)
    gsel = first_argmax(gl, gmax)
    gw = 1.0 / jnp.sum(jnp.exp(gl - gmax), axis=-1, keepdims=True)
    lo = N_GROUPS + EXPERTS_PER_GROUP * gsel
    el = jnp.where((lane >= lo) & (lane < lo + EXPERTS_PER_GROUP), logits, ninf)
    v0 = jnp.max(el, axis=-1, keepdims=True)
    i0 = first_argmax(el, v0)
    el = jnp.where(lane == i0, ninf, el)
    v1 = jnp.max(el, axis=-1, keepdims=True)
    i1 = first_argmax(el, v1)
    t = jnp.exp(v1 - v0)
    w0 = gw / (1.0 + t)
    w1 = gw * t / (1.0 + t)
    e0 = (i0 - N_GROUPS).astype(F32)
    e1 = (i1 - N_GROUPS).astype(F32)
    return jnp.where(lane == 0, e0, jnp.where(lane == 1, e1, jnp.where(lane == 2, w0, jnp.where(lane == 3, w1, 0.0))))


def _out_router_kernel(oa_ref, ob_ref, x_ref, ga_ref, gb_ref, wo_ref, gf_ref, wr_ref, br_ref,
                       x1_ref, h_ref, rt_ref, rtt_ref, cnt_ref):
    half = oa_ref.shape[1]

    @pl.when(pl.program_id(0) == 0)
    def _():
        cnt_ref[...] = jnp.zeros_like(cnt_ref)

    for r0 in range(0, oa_ref.shape[0], ROUTER_SUB):
        rows = pl.ds(r0, ROUTER_SUB)
        ya = _rms(oa_ref[rows, :].astype(F32), ga_ref[...]).astype(BF16)
        yb = _rms(ob_ref[rows, :].astype(F32), gb_ref[...]).astype(BF16)
        x1 = x_ref[rows, :] + _dot(ya, wo_ref[:half, :]) + _dot(yb, wo_ref[half:, :])
        x1_ref[rows, :] = x1
        h = _rms(x1, gf_ref[...])
        h_ref[rows, :] = _pack_bf16_pairs(h)
        h_hi, h_lo = _split_bf16(h)
        both = _dot(h_hi, wr_ref[...])
        logits = both[:, :LANES] + both[:, LANES:] + _dot(h_lo, wr_ref[:, :LANES]) + br_ref[...]
        rt = _route(logits)
        rt_ref[rows, :] = rt
        rtt = rt.T[:rtt_ref.shape[0], :]
        rtt_ref[:, rows] = rtt
        oh0, oh1 = _slot_one_hots(rtt)
        cnt_ref[...] += jnp.sum(oh0 + oh1, axis=1, keepdims=True)


def _out_router(oa, ob, x2, g_na, g_dil, w_o, g_ffn, w_rg, b_rg, w_re, b_re, tm, row0, n):
    d = x2.shape[1]
    half = d // 2
    blk0 = row0 // tm
    wr = jnp.concatenate([w_rg.astype(F32), w_re.astype(F32).transpose(1, 0, 2).reshape(d, N_EXPERTS)], axis=1)
    wr = jnp.pad(wr, ((0, 0), (0, LANES - wr.shape[1])))
    wr_hi = wr.astype(BF16)
    wr_cat = jnp.concatenate([wr_hi, (wr - wr_hi.astype(F32)).astype(BF16)], axis=1)
    br =jnp.pad(jnp.concatenate([b_rg.astype(F32), b_re.astype(F32).reshape(-1)]), (0, LANES - N_GROUPS - N_EXPERTS))
    full = lambda shape: pl.BlockSpec(shape, lambda i: (0,) * len(shape))
    row = lambda w: pl.BlockSpec((tm, w), lambda i: (i, 0))
    row_in = lambda w: pl.BlockSpec((tm, w), lambda i: (i + blk0, 0))
    return pl.pallas_call(
        _out_router_kernel,
        grid=(n // tm,),
        in_specs=[row_in(half), row_in(half), row_in(d), full((1, half)), full((1, half)), full((d, d)),
                  full((1, d)), full((d, 2 * LANES)), full((1, LANES))],
        out_specs=[row(d), row(half), row(LANES), pl.BlockSpec((8, tm), lambda i: (0, i)), full((LANES, LANES))],
        out_shape=[jax.ShapeDtypeStruct((n, d), F32), jax.ShapeDtypeStruct((n, half), jnp.uint32),
                   jax.ShapeDtypeStruct((n, LANES), F32), jax.ShapeDtypeStruct((8, n), F32),
                   jax.ShapeDtypeStruct((LANES, LANES), F32)],
        compiler_params=pltpu.CompilerParams(dimension_semantics=("arbitrary",), vmem_limit_bytes=VMEM_LIMIT),
        name="out_router",
    )(oa, ob, x2, g_na[None, :].astype(F32), g_dil[None, :].astype(F32), w_o.astype(BF16),
      g_ffn[None, :].astype(F32), wr_cat, br[None, :])


def _pack_bf16_pairs(x):
    w = x.shape[1] // 2
    bits = lax.bitcast_convert_type(x.astype(BF16).astype(F32), jnp.uint32)
    return bits[:, :w] | (bits[:, w:] >> 16)


def _unpack_bf16_pairs(u):
    hi = lax.bitcast_convert_type(u & jnp.uint32(0xFFFF0000), F32)
    lo = lax.bitcast_convert_type(u << 16, F32)
    return hi, lo


def _slot_one_hots(rtt):
    sub = lax.broadcasted_iota(jnp.int32, (LANES, rtt.shape[1]), 0).astype(F32)
    return (sub == rtt[0:1, :]).astype(F32), (sub == rtt[1:2, :]).astype(F32)


def _position_kernel(rtt_ref, start_ref, pos_ref, base_ref):
    tm = rtt_ref.shape[1]

    @pl.when(pl.program_id(0) == 0)
    def _():
        base_ref[...] = start_ref[...]

    oh0, oh1 = _slot_one_hots(rtt_ref[...])
    oh = oh0 + oh1
    earlier = lax.broadcasted_iota(jnp.int32, (tm, tm), 0) < lax.broadcasted_iota(jnp.int32, (tm, tm), 1)
    before = _dot(oh.astype(BF16), earlier.astype(BF16)) + base_ref[:, 0:1]
    p0 = jnp.sum(before * oh0, axis=0, keepdims=True)
    p1 = jnp.sum(before * oh1, axis=0, keepdims=True)
    row = lax.broadcasted_iota(jnp.int32, pos_ref.shape, 0)
    pos_ref[...] = jnp.where(row == 0, p0, jnp.where(row == 1, p1, 0.0)).astype(jnp.int32)
    base_ref[...] += jnp.sum(oh, axis=1, keepdims=True)


def _expert_positions(rtt, starts, tm):
    n = rtt.shape[1]
    return pl.pallas_call(
        _position_kernel,
        grid=(n // tm,),
        in_specs=[pl.BlockSpec((8, tm), lambda i: (0, i)), pl.BlockSpec((LANES, LANES), lambda i: (0, 0))],
        out_specs=pl.BlockSpec((8, tm), lambda i: (0, i)),
        out_shape=jax.ShapeDtypeStruct((8, n), jnp.int32),
        scratch_shapes=[pltpu.VMEM((LANES, LANES), F32)],
        compiler_params=pltpu.CompilerParams(dimension_semantics=("arbitrary",)),
        name="expert_positions",
    )(rtt, starts)


def _sc_mesh():
    return plsc.VectorSubcoreMesh(core_axis_name="c", subcore_axis_name="s",
                                  num_cores=SC_CORES, num_subcores=SC_SUBCORES)


def _sc_dispatch(hp, pos0, pos1, n_out):
    n, w = hp.shape
    workers = SC_CORES * SC_SUBCORES
    per = n // workers
    chunks = per // SC_CHUNK
    assert n % (workers * SC_CHUNK) == 0

    @functools.partial(
        pl.kernel, out_type=jax.ShapeDtypeStruct((n_out, w), hp.dtype), mesh=_sc_mesh(),
        scratch_types=[pltpu.VMEM((chunks, SC_CHUNK), jnp.int32), pltpu.VMEM((chunks, SC_CHUNK), jnp.int32),
                       pltpu.VMEM((SC_CHUNK, w), hp.dtype)],
        name="moe_dispatch")
    def body(h_hbm, p0_hbm, p1_hbm, xs_hbm, i0_v, i1_v, rows_v):
        wid = lax.axis_index("s") * SC_CORES + lax.axis_index("c")
        pltpu.sync_copy(p0_hbm.at[wid], i0_v)
        pltpu.sync_copy(p1_hbm.at[wid], i1_v)

        @pl.loop(0, chunks)
        def _(j):
            pltpu.sync_copy(h_hbm.at[pl.ds(wid * per + j * SC_CHUNK, SC_CHUNK)], rows_v)
            pltpu.sync_copy(rows_v, xs_hbm.at[i0_v.at[j]])
            pltpu.sync_copy(rows_v, xs_hbm.at[i1_v.at[j]])

    return body(hp, pos0.reshape(workers, chunks, SC_CHUNK), pos1.reshape(workers, chunks, SC_CHUNK))


def _sc_collect(ys, pos0, pos1):
    n = pos0.shape[0]
    w = ys.shape[1]
    workers = SC_CORES * SC_SUBCORES
    per = n // workers
    chunks = per // SC_CHUNK
    out = jax.ShapeDtypeStruct((n, w), ys.dtype)

    @functools.partial(
        pl.kernel, out_type=(out, out), mesh=_sc_mesh(),
        scratch_types=[pltpu.VMEM((chunks, SC_CHUNK), jnp.int32), pltpu.VMEM((chunks, SC_CHUNK), jnp.int32),
                       pltpu.VMEM((SC_CHUNK, w), ys.dtype)],
        name="moe_collect")
    def body(ys_hbm, p0_hbm, p1_hbm, y0_hbm, y1_hbm, i0_v, i1_v, rows_v):
        wid = lax.axis_index("s") * SC_CORES + lax.axis_index("c")
        pltpu.sync_copy(p0_hbm.at[wid], i0_v)
        pltpu.sync_copy(p1_hbm.at[wid], i1_v)

        @pl.loop(0, chunks)
        def _(j):
            dst = pl.ds(wid * per + j * SC_CHUNK, SC_CHUNK)
            pltpu.sync_copy(ys_hbm.at[i0_v.at[j]], rows_v)
            pltpu.sync_copy(rows_v, y0_hbm.at[dst])
            pltpu.sync_copy(ys_hbm.at[i1_v.at[j]], rows_v)
            pltpu.sync_copy(rows_v, y1_hbm.at[dst])

    return body(ys, pos0.reshape(workers, chunks, SC_CHUNK), pos1.reshape(workers, chunks, SC_CHUNK))


def _experts_kernel(plan_ref, nt_ref, xs_ref, wg_ref, wu_ref, wd_ref, ys_ref, wg_bf, wu_bf, wd_bf):
    g = pl.program_id(0)
    prev = jnp.maximum(g - 1, 0)

    for row, (src, dst) in enumerate(((wg_ref, wg_bf), (wu_ref, wu_bf), (wd_ref, wd_bf))):
        @pl.when((g == 0) | (plan_ref[row, g] != plan_ref[row, prev]))
        def _(row=row, src=src, dst=dst):
            dst[plan_ref[row + 3, g]] = src[...].astype(BF16)

    @pl.when((g >= EXPERT_LEAD) & (g < nt_ref[0] + EXPERT_LEAD))
    def _():
        slot = plan_ref[6, g]
        half = wg_bf.shape[1] // 2
        hi, lo = _unpack_bf16_pairs(xs_ref[...])
        hi = hi.astype(BF16)
        lo = lo.astype(BF16)
        a = _dot(hi, wg_bf[slot, :half, :]) + _dot(lo, wg_bf[slot, half:, :])
        u = _dot(hi, wu_bf[slot, :half, :]) + _dot(lo, wu_bf[slot, half:, :])
        act = (a * jax.nn.sigmoid(a) * u).astype(BF16)
        ys_ref[...] = _pack_bf16_pairs(_dot(act, wd_bf[slot]))


def _experts(xs, tile_expert, n_tiles, w_gate, w_up, w_down, tmg):
    rows, w = xs.shape
    ne, d, de = w_gate.shape
    steps = rows // tmg + EXPERT_LEAD
    run = jnp.concatenate([jnp.zeros((1,), jnp.int32),
                           jnp.cumsum((tile_expert[1:] != tile_expert[:-1]).astype(jnp.int32))])
    tile = jnp.clip(jnp.arange(-EXPERT_LEAD, steps, dtype=jnp.int32), 0, n_tiles[0] - 1)
    expert, slot = tile_expert[tile], run[tile] % EXPERT_SLOTS
    at = lambda a, lag: a[EXPERT_LEAD - lag:EXPERT_LEAD - lag + steps]
    plan = jnp.stack([at(expert, lag) for lag in range(EXPERT_LEAD)]
                     + [at(slot, lag) for lag in range(EXPERT_LEAD + 1)]
                     + [at(tile, EXPERT_LEAD)]).astype(jnp.int32)
    return pl.pallas_call(
        _experts_kernel,
        grid_spec=pltpu.PrefetchScalarGridSpec(
            num_scalar_prefetch=2,
            grid=(steps,),
            in_specs=[pl.BlockSpec((tmg, w), lambda s, plan, nt: (plan[2 * EXPERT_LEAD + 1, s], 0)),
                      pl.BlockSpec((None, d, de), lambda s, plan, nt: (plan[0, s], 0, 0)),
                      pl.BlockSpec((None, d, de), lambda s, plan, nt: (plan[1, s], 0, 0)),
                      pl.BlockSpec((None, de, d), lambda s, plan, nt: (plan[2, s], 0, 0))],
            out_specs=pl.BlockSpec((tmg, w), lambda s, plan, nt: (plan[2 * EXPERT_LEAD + 1, s], 0)),
            scratch_shapes=[pltpu.VMEM((EXPERT_SLOTS, d, de), BF16), pltpu.VMEM((EXPERT_SLOTS, d, de), BF16),
                            pltpu.VMEM((EXPERT_SLOTS, de, d), BF16)],
        ),
        out_shape=jax.ShapeDtypeStruct((rows, w), jnp.uint32),
        compiler_params=pltpu.CompilerParams(dimension_semantics=("arbitrary",), vmem_limit_bytes=VMEM_LIMIT),
        name="experts",
    )(plan, n_tiles, xs, w_gate, w_up, w_down)


def _moe(hp, rtt, cnt, w_gate, w_up, w_down, tm, tmg):
    n = hp.shape[0]
    ne = w_gate.shape[0]
    counts = cnt[:ne, 0].astype(jnp.int32)
    padded = (counts + tmg - 1) // tmg * tmg
    ends = jnp.cumsum(padded)
    starts = jnp.pad((ends - padded).astype(F32), (0, LANES - ne))
    pos = _expert_positions(rtt, jnp.broadcast_to(starts[:, None], (LANES, LANES)), tm)
    pos0, pos1 = pos[0], pos[1]
    rows = 2 * n + ne * tmg
    tile_start = jnp.arange(rows // tmg, dtype=jnp.int32) * tmg
    tile_expert = jnp.minimum(jnp.sum(tile_start[:, None] >= ends[None, :], axis=1), ne - 1).astype(jnp.int32)
    n_tiles = (ends[-1:] // tmg).astype(jnp.int32)
    xs = _sc_dispatch(hp, pos0, pos1, rows)
    ys = _experts(xs, tile_expert, n_tiles, w_gate, w_up, w_down, tmg)
    return _sc_collect(ys, pos0, pos1)


def _ple_kernel(x1_ref, y0_ref, y1_ref, rt_ref, p_ref, g_ref, wg_ref, wp_ref, o_ref):
    for r0 in range(0, x1_ref.shape[0], PLE_SUB):
        rows = pl.ds(r0, PLE_SUB)
        rt = rt_ref[rows, :]
        y0 = jnp.concatenate(_unpack_bf16_pairs(y0_ref[rows, :]), axis=1)
        y1 = jnp.concatenate(_unpack_bf16_pairs(y1_ref[rows, :]), axis=1)
        x2 = x1_ref[rows, :] + rt[:, 2:3] * y0 + rt[:, 3:4] * y1
        gate = jax.nn.sigmoid(_dot(_rms(x2, g_ref[...]).astype(BF16), wg_ref[...]))
        o_ref[rows, :] = x2 + gate * _dot(p_ref[rows, :].astype(BF16), wp_ref[...])


def _ple_kernel_into(prev_ref, *refs):
    del prev_ref
    _ple_kernel(*refs)


def _ple(x1, y0, y1, rt, p2, g_ple, w_gate, w_proj, tm, row0, out_prev):
    n, d = x1.shape
    n_all, dp = p2.shape
    blk0 = row0 // tm
    full = lambda shape: pl.BlockSpec(shape, lambda i: (0,) * len(shape))
    row = lambda w: pl.BlockSpec((tm, w), lambda i: (i, 0))
    row_all = lambda w: pl.BlockSpec((tm, w), lambda i: (i + blk0, 0))
    in_specs = [row(d), row(d // 2), row(d // 2), row(LANES), row_all(dp), full((1, d)), full((d, d)), full((dp, d))]
    args = (x1, y0, y1, rt, p2, g_ple[None, :].astype(F32), w_gate.astype(BF16), w_proj.astype(BF16))
    if out_prev is not None:
        in_specs = [pl.BlockSpec(memory_space=pl.ANY)] + in_specs
        args = (out_prev,) + args
    return pl.pallas_call(
        _ple_kernel if out_prev is None else _ple_kernel_into,
        grid=(n // tm,),
        in_specs=in_specs,
        out_specs=row_all(d),
        out_shape=jax.ShapeDtypeStruct((n_all, d), F32),
        input_output_aliases={} if out_prev is None else {0: 0},
        compiler_params=pltpu.CompilerParams(dimension_semantics=("parallel",), vmem_limit_bytes=VMEM_LIMIT),
        name="ple",
    )(*args)


def _layer(x, p_l, g_attn, w_qkv, q_norm_na, k_norm_na, rpb_na, q_norm_dil, k_norm_dil, g_out_na, g_out_dil,
           w_o, g_ffn, w_rg, b_rg, w_re, b_re, w_exp_gate, w_exp_up, w_exp_down, g_ple, w_ple_gate, w_ple_proj):
    b, s, d = x.shape
    n = b * s
    half = d // 2
    assert d == N_HEADS * HEAD_DIM and half == N_HEADS_NA * HEAD_DIM
    tm = 512
    assert s % tm == 0 and s % QKV_TILE == 0
    x2 = x.reshape(n, d)
    qa, ka, va, qb, kb, vb = _qkv_proj(x2, g_attn, w_qkv, q_norm_na, k_norm_na, q_norm_dil, k_norm_dil, s, QKV_TILE)
    seq = lambda t: t.reshape(b, s, half)
    oa = _na_attention(seq(qa), seq(ka), seq(va), rpb_na).reshape(n, half)
    ob = _dil_attention(seq(qb), seq(kb), seq(vb)).reshape(n, half)
    p2 = p_l.reshape(n, -1)
    chunk_unit = SC_CORES * SC_SUBCORES * SC_CHUNK
    parts = sum(TOKEN_SPLIT)
    sizes = [n * k // parts for k in TOKEN_SPLIT] if n % (parts * chunk_unit) == 0 else [n]
    assert all(size % ROW_TILE == 0 for size in sizes)
    out = None
    row0 = 0
    for size in sizes:
        x1, h, rt, rtt, cnt = _out_router(oa, ob, x2, g_out_na, g_out_dil, w_o, g_ffn, w_rg, b_rg, w_re, b_re,
                                          ROW_TILE, row0, size)
        y0, y1 = _moe(h, rtt, cnt, w_exp_gate, w_exp_up, w_exp_down, 2 * tm, tm)
        out = _ple(x1, y0, y1, rt, p2, g_ple, w_ple_gate, w_ple_proj, ROW_TILE, row0, out)
        row0 += size
    return out.reshape(b, s, d)


def kernel(x, p, g_attn, w_qkv, q_norm_na, k_norm_na, rpb_na, q_norm_dil, k_norm_dil, g_out_na, g_out_dil, w_o,
           g_ffn, w_router_group, b_router_group, w_router_expert, b_router_expert, w_exp_gate, w_exp_up,
           w_exp_down, g_ple, w_ple_gate, w_ple_proj):
    for i in range(p.shape[0]):
        x = _layer(x, p[i], g_attn[i], w_qkv[i], q_norm_na[i], k_norm_na[i], rpb_na[i], q_norm_dil[i],
                   k_norm_dil[i], g_out_na[i], g_out_dil[i], w_o[i], g_ffn[i], w_router_group[i],
                   b_router_group[i], w_router_expert[i], b_router_expert[i], w_exp_gate[i], w_exp_up[i],
                   w_exp_down[i], g_ple[i], w_ple_gate[i], w_ple_proj[i])
    return x
```

```python
import functools

import numpy as np
import jax
import jax.numpy as jnp
from jax import lax
from jax.experimental import pallas as pl
from jax.experimental.pallas import tpu as pltpu
from jax.experimental.pallas import tpu_sc as plsc

HEAD_DIM = 64
N_HEADS = 16
N_HEADS_NA = 8
GRID_W = 64
NA_ROWS = 8
NA_COLS = 16
DIL_PAIRS = ((128, 1), (512, 4), (2048, 16))
ROPE_THETA = 10000.0
N_GROUPS = 4
EXPERTS_PER_GROUP = 8
N_EXPERTS = N_GROUPS * EXPERTS_PER_GROUP
EPS = 1e-6
NEG = -1e30
LOG2_E = 1.4426950408889634

LANES = 128
PAIR_W = 2 * HEAD_DIM
DIL_BLK = 128
SLAB_DIL = 4
ATTN_SLOTS = 4
ATTN_STAGE_LAG = 2
ROUTER_SUB = 512
PLE_SUB = 256
ROW_TILE = 1024
QKV_TILE = 1024
VMEM_LIMIT = 56 * 1024 * 1024
SC_CORES = 2
SC_SUBCORES = 16
SC_CHUNK = 128
EXPERT_LEAD = 3
EXPERT_SLOTS = 4
TOKEN_CHUNKS = 2

F32 = jnp.float32
BF16 = jnp.bfloat16


def _dot(a, b):
    return jnp.dot(a, b, preferred_element_type=F32)


def _dot_nt(a, b):
    return lax.dot_general(a, b, (((1,), (1,)), ((), ())), preferred_element_type=F32)


def _rms(x, gain):
    return x * lax.rsqrt(jnp.mean(x * x, axis=-1, keepdims=True) + EPS) * gain


def _lane_first_half(shape):
    return lax.broadcasted_iota(jnp.int32, shape, len(shape) - 1) < HEAD_DIM


def _qkv_kernel(x_ref, g_ref, w_ref, gq_na_ref, gk_na_ref, gq_dil_ref, gk_dil_ref, cos_ref, sin_ref,
                hsum_ref, qa_ref, ka_ref, va_ref, qb_ref, kb_ref, vb_ref, wb_ref):
    d = x_ref.shape[1]
    half = d // 2
    scale = HEAD_DIM ** -0.5 * LOG2_E

    @pl.when(pl.program_id(0) == 0)
    def _():
        wb_ref[...] = w_ref[...].astype(BF16)

    h = _rms(x_ref[...], g_ref[...]).astype(BF16)

    def proj(col):
        return _dot(h, wb_ref[:, col:col + half])

    def head_norm(y, gain):
        sq = (y * y).astype(BF16)
        w = hsum_ref.shape[0]
        ms = jnp.concatenate([_dot(sq[:, c:c + w], hsum_ref[...]) for c in range(0, half, w)], axis=1)
        return y * lax.rsqrt(ms + EPS) * gain

    def rope(y):
        lane = lax.broadcasted_iota(jnp.int32, (y.shape[0], LANES), 1)
        lower = (lane % HEAD_DIM) < HEAD_DIM // 2
        cos = cos_ref[...]
        sin = sin_ref[...]
        outs = []
        for c in range(0, half, LANES):
            yc = y[:, c:c + LANES]
            up = pltpu.roll(yc, LANES - HEAD_DIM // 2, axis=1)
            down = pltpu.roll(yc, HEAD_DIM // 2, axis=1)
            outs.append(yc * cos + jnp.where(lower, up, down) * sin)
        return jnp.concatenate(outs, axis=1)

    qa_ref[...] = (head_norm(proj(0), gq_na_ref[...]) * scale).astype(BF16)
    qb_ref[...] = (rope(head_norm(proj(half), gq_dil_ref[...])) * scale).astype(BF16)
    ka_ref[...] = head_norm(proj(d), gk_na_ref[...]).astype(BF16)
    kb_ref[...] = rope(head_norm(proj(d + half), gk_dil_ref[...])).astype(BF16)
    va_ref[...] = proj(2 * d).astype(BF16)
    vb_ref[...] = proj(2 * d + half).astype(BF16)


def _qkv_proj(x2, g_attn, w_qkv, gq_na, gk_na, gq_dil, gk_dil, seq, tm):
    n, d = x2.shape
    half = d // 2
    inv = ROPE_THETA ** (-np.arange(HEAD_DIM // 2, dtype=np.float64) / (HEAD_DIM // 2))
    ang = np.arange(seq, dtype=np.float64)[:, None] * inv[None, :]
    cos = jnp.asarray(np.tile(np.cos(ang), (1, LANES // (HEAD_DIM // 2))), F32)
    sin = jnp.asarray(np.tile(np.concatenate([-np.sin(ang), np.sin(ang)], axis=1), (1, LANES // HEAD_DIM)), F32)
    hs_w = 2 * LANES
    blk = np.arange(hs_w) // HEAD_DIM
    hsum = jnp.asarray((blk[:, None] == blk[None, :]).astype(np.float32) / HEAD_DIM, BF16)
    tile_gain = lambda g: jnp.tile(g.astype(F32), half // HEAD_DIM)[None, :]
    steps_per_seq = seq // tm
    full = lambda shape: pl.BlockSpec(shape, lambda i: (0,) * len(shape))
    out = jax.ShapeDtypeStruct((n, half), BF16)
    return pl.pallas_call(
        _qkv_kernel,
        grid=(n // tm,),
        in_specs=[
            pl.BlockSpec((tm, d), lambda i: (i, 0)),
            full((1, d)),
            pl.BlockSpec((d, 3 * d), lambda i: (0, 0), pipeline_mode=pl.Buffered(1)),
            full((1, half)), full((1, half)), full((1, half)), full((1, half)),
            pl.BlockSpec((tm, LANES), lambda i: (i % steps_per_seq, 0)),
            pl.BlockSpec((tm, LANES), lambda i: (i % steps_per_seq, 0)),
            full((hs_w, hs_w)),
        ],
        out_specs=[pl.BlockSpec((tm, half), lambda i: (i, 0))] * 6,
        out_shape=[out] * 6,
        scratch_shapes=[pltpu.VMEM((d, 3 * d), BF16)],
        compiler_params=pltpu.CompilerParams(dimension_semantics=("arbitrary",), vmem_limit_bytes=VMEM_LIMIT),
        name="qkv_proj",
    )(x2, g_attn[None, :].astype(F32), w_qkv, tile_gain(gq_na), tile_gain(gk_na),
      tile_gain(gq_dil), tile_gain(gk_dil), cos, sin, hsum)


def _clip(x, lo, hi):
    return min(max(x, lo), hi)


def _software_pipeline(n_items, stages):
    last_lag = (len(stages) - 1) * ATTN_STAGE_LAG
    assert last_lag <= ATTN_SLOTS
    for t in range(n_items + last_lag):
        for k in reversed(range(len(stages))):
            item = t - k * ATTN_STAGE_LAG
            if 0 <= item < n_items:
                stages[k](item, item % ATTN_SLOTS)


def _qk_stage(q, kwin, bias_a, bias_b, s_ref, slot, first):
    m = q.shape[0]
    w = kwin.shape[0]
    zero = jnp.zeros_like(q)
    s = _dot_nt(jnp.concatenate([jnp.where(first, q, zero), jnp.where(first, zero, q)], axis=0), kwin)
    s_ref[slot, :m, :w] = s[:m] + bias_a
    s_ref[slot, m:, :w] = s[m:] + bias_b


def _softmax_stage(s_ref, p_ref, slot, w):
    m = jnp.max(s_ref[slot, :, :w], axis=-1, keepdims=True)
    p_ref[slot, :, :w] = jnp.exp2(s_ref[slot, :, :w] - m).astype(BF16)
    return m


def _pv_stage(p_ref, slot, v_win, first):
    w = v_win.shape[0]
    r = _dot(p_ref[slot, :, :w], v_win)
    m = r.shape[0] // 2
    return jnp.where(first, r[:m, :PAIR_W], r[m:, :PAIR_W]), jnp.where(first, r[:m, PAIR_W:], r[m:, PAIR_W:])


def _with_ones(v):
    return jnp.concatenate([v, jnp.ones_like(v)], axis=1)


def _na_kernel(q_ref, k_ref, v_ref, bias_ref, o_ref, v1_ref, s_ref, p_ref):
    rows = q_ref.shape[0] // GRID_W
    win = NA_ROWS * GRID_W
    v1_ref[...] = _with_ones(v_ref[...])
    first = _lane_first_half((GRID_W, PAIR_W))

    def slices(r):
        rs = _clip(r - NA_ROWS // 2, 0, rows - NA_ROWS)
        return pl.ds(r * GRID_W, GRID_W), pl.ds(rs * GRID_W, win), r - rs

    def bias(head, delta):
        return jnp.concatenate([bias_ref[head, j - delta + NA_ROWS - 1] for j in range(0, NA_ROWS, 2)], axis=1)

    def qk(r, slot):
        qs, ks, delta = slices(r)
        _qk_stage(q_ref[qs, :], k_ref[ks, :], bias(0, delta), bias(1, delta), s_ref, slot, first)

    def softmax(r, slot):
        _softmax_stage(s_ref, p_ref, slot, win)

    def pv(r, slot):
        qs, ks, _ = slices(r)
        num, den = _pv_stage(p_ref, slot, v1_ref[ks, :], first)
        o_ref[qs, :] = (num / den).astype(o_ref.dtype)

    _software_pipeline(rows, (qk, softmax, pv))


def _na_bias_table(rpb):
    w = np.arange(GRID_W)
    cs = np.clip(w - NA_COLS // 2, 0, GRID_W - NA_COLS)
    kc = np.arange(GRID_W)
    valid = (kc[None, :] >= cs[:, None]) & (kc[None, :] < cs[:, None] + NA_COLS)
    coff = np.clip(kc[None, :] - w[:, None] + NA_COLS - 1, 0, 2 * NA_COLS - 2)
    pick = np.zeros((2 * NA_COLS - 1, GRID_W * GRID_W), np.float32)
    pick[coff.reshape(-1), np.arange(GRID_W * GRID_W)] = 1.0
    tab = jnp.einsum("hrc,cx->hrx", rpb.astype(F32), jnp.asarray(pick), precision=lax.Precision.HIGHEST)
    tab = jnp.where(valid[None, None], tab.reshape(tab.shape[:2] + valid.shape) * LOG2_E, NEG)
    return jnp.concatenate([tab[:, :-1], tab[:, 1:]], axis=-1)


def _na_attention(q, k, v, rpb):
    b, s, width = q.shape
    pairs = width // PAIR_W
    assert s % GRID_W == 0 and s // GRID_W >= NA_ROWS
    bias = _na_bias_table(rpb).reshape(pairs, 2, 2 * NA_ROWS - 2, GRID_W, 2 * GRID_W)
    qkv_spec = pl.BlockSpec((None, s, PAIR_W), lambda bi, j: (bi, 0, j))
    return pl.pallas_call(
        _na_kernel,
        grid=(b, pairs),
        in_specs=[qkv_spec, qkv_spec, qkv_spec,
                  pl.BlockSpec((None, 2, 2 * NA_ROWS - 2, GRID_W, 2 * GRID_W), lambda bi, j: (j, 0, 0, 0, 0))],
        out_specs=pl.BlockSpec((None, s, PAIR_W), lambda bi, j: (bi, 0, j)),
        out_shape=jax.ShapeDtypeStruct((b, s, width), BF16),
        scratch_shapes=[pltpu.VMEM((s, 2 * PAIR_W), BF16),
                        pltpu.VMEM((ATTN_SLOTS, 2 * GRID_W, NA_ROWS * GRID_W), F32),
                        pltpu.VMEM((ATTN_SLOTS, 2 * GRID_W, NA_ROWS * GRID_W), BF16)],
        compiler_params=pltpu.CompilerParams(dimension_semantics=("parallel", "parallel"),
                                             vmem_limit_bytes=VMEM_LIMIT),
        name="na_attn",
    )(q, k, v, bias)


def _dil_kernel(q_ref, k_ref, v_ref, mwide_ref, mfull_ref, o_ref,
                f32a_ref, f32b_ref, qc_ref, kc_ref, vc_ref, acc_ref, den_ref, max_ref, s_ref, p_ref):
    s = q_ref.shape[0]
    for i, ref in enumerate((q_ref, k_ref, v_ref)):
        f32a_ref[i] = ref[...].astype(F32)
    first = _lane_first_half((DIL_BLK, PAIR_W))
    prev_ref, next_ref, prev_dil = f32a_ref, f32b_ref, 1

    for p, (window, dil) in enumerate(DIL_PAIRS):
        radius = window // (2 * dil)
        cls_len = s // dil
        nblk = cls_len // DIL_BLK
        wide = cls_len >= 2 * DIL_BLK
        win = 2 * DIL_BLK if wide else cls_len
        assert radius == DIL_BLK // 2 and cls_len % DIL_BLK == 0 and dil % prev_dil == 0
        assert dil == 1 or dil % SLAB_DIL == 0

        ratio = dil // prev_dil
        keep = ratio > 1 and p + 1 < len(DIL_PAIRS)
        for c in range(dil):
            src = pl.ds((c % prev_dil) * (s // prev_dil) + c // prev_dil, cls_len, stride=ratio)
            dst = pl.ds(c * cls_len, cls_len)
            vals = [prev_ref[i, src, :] for i in range(3)]
            if keep:
                for i in range(3):
                    next_ref[i, dst, :] = vals[i]
            qc_ref[dst, :] = vals[0].astype(BF16)
            kc_ref[dst, :] = vals[1].astype(BF16)
            vc_ref[dst, :] = _with_ones(vals[2].astype(BF16))
        if keep:
            prev_ref, next_ref, prev_dil = next_ref, prev_ref, dil

        def slices(n):
            c, i = divmod(n, nblk)
            base = c * cls_len
            ws = _clip(i * DIL_BLK - radius, 0, cls_len - win)
            kind = 0 if i == 0 else (2 if i == nblk - 1 else 1)
            if dil == 1:
                res = pl.ds(i * DIL_BLK, DIL_BLK)
            else:
                sub = dil // SLAB_DIL
                res = pl.ds((c % SLAB_DIL) * (s // SLAB_DIL) + c // SLAB_DIL + sub * DIL_BLK * i, DIL_BLK, stride=sub)
            return pl.ds(base + i * DIL_BLK, DIL_BLK), pl.ds(base + ws, win), kind, res

        def qk(n, slot):
            qs, ks, kind, _ = slices(n)
            mask = mwide_ref[kind] if wide else mfull_ref[...]
            _qk_stage(qc_ref[qs, :], kc_ref[ks, :], mask, mask, s_ref, slot, first)

        def softmax(n, slot):
            m = _softmax_stage(s_ref, p_ref, slot, win)
            max_ref[p, slices(n)[3], :] = jnp.where(first, m[:DIL_BLK], m[DIL_BLK:])

        def pv(n, slot):
            _, ks, _, tok = slices(n)
            num, den = _pv_stage(p_ref, slot, vc_ref[ks, :], first)
            acc_ref[p, tok, :] = num
            den_ref[p, tok, :] = den

        _software_pipeline(dil * nblk, (qk, softmax, pv))

    out_f32 = f32a_ref.at[0]
    quarter = s // SLAB_DIL
    for c in range(SLAB_DIL):
        rows = [pl.ds(c, quarter, stride=SLAB_DIL) if dil == 1 else pl.ds(c * quarter, quarter)
                for _, dil in DIL_PAIRS]
        maxima = [max_ref[p, rows[p], :] for p in range(len(DIL_PAIRS))]
        m = functools.reduce(jnp.maximum, maxima)
        num = jnp.zeros_like(m)
        den = jnp.zeros_like(m)
        for p in range(len(DIL_PAIRS)):
            w = jnp.exp2(maxima[p] - m)
            num = num + w * acc_ref[p, rows[p], :]
            den = den + w * den_ref[p, rows[p], :]
        out_f32[pl.ds(c, quarter, stride=SLAB_DIL), :] = num / den
    o_ref[...] = out_f32[...].astype(o_ref.dtype)


def _band_mask(kind):
    radius = DIL_BLK // 2
    qq = np.arange(DIL_BLK)[:, None]
    if kind == "full":
        kk = np.arange(DIL_BLK)[None, :]
        shift = 0
    else:
        kk = np.arange(2 * DIL_BLK)[None, :]
        shift = {"first": 0, "inner": radius, "last": DIL_BLK}[kind]
    return np.where(np.abs(kk - qq - shift) <= radius, 0.0, NEG).astype(np.float32)


def _dil_attention(q, k, v):
    b, s, width = q.shape
    pairs = width // PAIR_W
    for window, dil in DIL_PAIRS:
        assert s % (window // 2) == 0 and (s // dil) % DIL_BLK == 0
    mwide = jnp.asarray(np.stack([_band_mask("first"), _band_mask("inner"), _band_mask("last")]))
    mfull = jnp.asarray(_band_mask("full"))
    qkv_spec = pl.BlockSpec((None, s, PAIR_W), lambda bi, j: (bi, 0, j))
    npat = len(DIL_PAIRS)
    return pl.pallas_call(
        _dil_kernel,
        grid=(b, pairs),
        in_specs=[qkv_spec, qkv_spec, qkv_spec,
                  pl.BlockSpec(mwide.shape, lambda bi, j: (0, 0, 0)),
                  pl.BlockSpec(mfull.shape, lambda bi, j: (0, 0))],
        out_specs=pl.BlockSpec((None, s, PAIR_W), lambda bi, j: (bi, 0, j)),
        out_shape=jax.ShapeDtypeStruct((b, s, width), BF16),
        scratch_shapes=[pltpu.VMEM((3, s, PAIR_W), F32)] * 2 + [pltpu.VMEM((s, PAIR_W), BF16)] * 2
        + [pltpu.VMEM((s, 2 * PAIR_W), BF16)] + [pltpu.VMEM((npat, s, PAIR_W), F32)] * 3
        + [pltpu.VMEM((ATTN_SLOTS, 2 * DIL_BLK, 2 * DIL_BLK), F32),
           pltpu.VMEM((ATTN_SLOTS, 2 * DIL_BLK, 2 * DIL_BLK), BF16)],
        compiler_params=pltpu.CompilerParams(dimension_semantics=("parallel", "parallel"),
                                             vmem_limit_bytes=VMEM_LIMIT),
        name="dil_attn",
    )(q, k, v, mwide, mfull)


def _split_bf16(x):
    hi = x.astype(BF16)
    return hi, (x - hi.astype(F32)).astype(BF16)


def _route(logits):
    lane = lax.broadcasted_iota(jnp.int32, logits.shape, 1)
    ninf = jnp.float32(-jnp.inf)

    def first_argmax(vals, vmax):
        return jnp.min(jnp.where(vals == vmax, lane, LANES), axis=-1, keepdims=True)

    gl = jnp.where(lane < N_GROUPS, logits, ninf)
    gmax = jnp.max(gl, axis=-1, keepdims=True)
    gsel = first_argmax(gl, gmax)
    gw = 1.0 / jnp.sum(jnp.exp(gl - gmax), axis=-1, keepdims=True)
    lo = N_GROUPS + EXPERTS_PER_GROUP * gsel
    el = jnp.where((lane >= lo) & (lane < lo + EXPERTS_PER_GROUP), logits, ninf)
    v0 = jnp.max(el, axis=-1, keepdims=True)
    i0 = first_argmax(el, v0)
    el = jnp.where(lane == i0, ninf, el)
    v1 = jnp.max(el, axis=-1, keepdims=True)
    i1 = first_argmax(el, v1)
    t = jnp.exp(v1 - v0)
    w0 = gw / (1.0 + t)
    w1 = gw * t / (1.0 + t)
    e0 = (i0 - N_GROUPS).astype(F32)
    e1 = (i1 - N_GROUPS).astype(F32)
    return jnp.where(lane == 0, e0, jnp.where(lane == 1, e1, jnp.where(lane == 2, w0, jnp.where(lane == 3, w1, 0.0))))


def _out_router_kernel(oa_ref, ob_ref, x_ref, ga_ref, gb_ref, wo_ref, gf_ref, wr_ref, br_ref,
                       x1_ref, h_ref, rt_ref, rtt_ref, cnt_ref):
    half = oa_ref.shape[1]

    @pl.when(pl.program_id(0) == 0)
    def _():
        cnt_ref[...] = jnp.zeros_like(cnt_ref)

    for r0 in range(0, oa_ref.shape[0], ROUTER_SUB):
        rows = pl.ds(r0, ROUTER_SUB)
        ya = _rms(oa_ref[rows, :].astype(F32), ga_ref[...]).astype(BF16)
        yb = _rms(ob_ref[rows, :].astype(F32), gb_ref[...]).astype(BF16)
        x1 = x_ref[rows, :] + _dot(ya, wo_ref[:half, :]) + _dot(yb, wo_ref[half:, :])
        x1_ref[rows, :] = x1
        h = _rms(x1, gf_ref[...])
        h_ref[rows, :] = _pack_bf16_pairs(h)
        h_hi, h_lo = _split_bf16(h)
        both = _dot(h_hi, wr_ref[...])
        logits = both[:, :LANES] + both[:, LANES:] + _dot(h_lo, wr_ref[:, :LANES]) + br_ref[...]
        rt = _route(logits)
        rt_ref[rows, :] = rt
        rtt = rt.T[:rtt_ref.shape[0], :]
        rtt_ref[:, rows] = rtt
        oh0, oh1 = _slot_one_hots(rtt)
        cnt_ref[...] += jnp.sum(oh0 + oh1, axis=1, keepdims=True)


def _out_router(oa, ob, x2, g_na, g_dil, w_o, g_ffn, w_rg, b_rg, w_re, b_re, tm, row0, n):
    d = x2.shape[1]
    half = d // 2
    blk0 = row0 // tm
    wr = jnp.concatenate([w_rg.astype(F32), w_re.astype(F32).transpose(1, 0, 2).reshape(d, N_EXPERTS)], axis=1)
    wr = jnp.pad(wr, ((0, 0), (0, LANES - wr.shape[1])))
    wr_hi = wr.astype(BF16)
    wr_cat = jnp.concatenate([wr_hi, (wr - wr_hi.astype(F32)).astype(BF16)], axis=1)
    br =jnp.pad(jnp.concatenate([b_rg.astype(F32), b_re.astype(F32).reshape(-1)]), (0, LANES - N_GROUPS - N_EXPERTS))
    full = lambda shape: pl.BlockSpec(shape, lambda i: (0,) * len(shape))
    row = lambda w: pl.BlockSpec((tm, w), lambda i: (i, 0))
    row_in = lambda w: pl.BlockSpec((tm, w), lambda i: (i + blk0, 0))
    return pl.pallas_call(
        _out_router_kernel,
        grid=(n // tm,),
        in_specs=[row_in(half), row_in(half), row_in(d), full((1, half)), full((1, half)), full((d, d)),
                  full((1, d)), full((d, 2 * LANES)), full((1, LANES))],
        out_specs=[row(d), row(half), row(LANES), pl.BlockSpec((8, tm), lambda i: (0, i)), full((LANES, LANES))],
        out_shape=[jax.ShapeDtypeStruct((n, d), F32), jax.ShapeDtypeStruct((n, half), jnp.uint32),
                   jax.ShapeDtypeStruct((n, LANES), F32), jax.ShapeDtypeStruct((8, n), F32),
                   jax.ShapeDtypeStruct((LANES, LANES), F32)],
        compiler_params=pltpu.CompilerParams(dimension_semantics=("arbitrary",), vmem_limit_bytes=VMEM_LIMIT),
        name="out_router",
    )(oa, ob, x2, g_na[None, :].astype(F32), g_dil[None, :].astype(F32), w_o.astype(BF16),
      g_ffn[None, :].astype(F32), wr_cat, br[None, :])


def _pack_bf16_pairs(x):
    w = x.shape[1] // 2
    bits = lax.bitcast_convert_type(x.astype(BF16).astype(F32), jnp.uint32)
    return bits[:, :w] | (bits[:, w:] >> 16)


def _unpack_bf16_pairs(u):
    hi = lax.bitcast_convert_type(u & jnp.uint32(0xFFFF0000), F32)
    lo = lax.bitcast_convert_type(u << 16, F32)
    return hi, lo


def _slot_one_hots(rtt):
    sub = lax.broadcasted_iota(jnp.int32, (LANES, rtt.shape[1]), 0).astype(F32)
    return (sub == rtt[0:1, :]).astype(F32), (sub == rtt[1:2, :]).astype(F32)


def _position_kernel(rtt_ref, start_ref, pos_ref, base_ref, earlier_ref):
    tm = rtt_ref.shape[1]

    @pl.when(pl.program_id(0) == 0)
    def _():
        base_ref[...] = start_ref[...]
        earlier = lax.broadcasted_iota(jnp.int32, (tm, tm), 0) < lax.broadcasted_iota(jnp.int32, (tm, tm), 1)
        earlier_ref[...] = earlier.astype(BF16)

    oh0, oh1 = _slot_one_hots(rtt_ref[...])
    oh = oh0 + oh1
    before = _dot(oh.astype(BF16), earlier_ref[...]) + base_ref[:, 0:1]
    p0 = jnp.sum(before * oh0, axis=0, keepdims=True)
    p1 = jnp.sum(before * oh1, axis=0, keepdims=True)
    row = lax.broadcasted_iota(jnp.int32, pos_ref.shape, 0)
    pos_ref[...] = jnp.where(row == 0, p0, jnp.where(row == 1, p1, 0.0)).astype(jnp.int32)
    base_ref[...] += jnp.sum(oh, axis=1, keepdims=True)


def _expert_positions(rtt, starts, tm):
    n = rtt.shape[1]
    return pl.pallas_call(
        _position_kernel,
        grid=(n // tm,),
        in_specs=[pl.BlockSpec((8, tm), lambda i: (0, i)), pl.BlockSpec((LANES, LANES), lambda i: (0, 0))],
        out_specs=pl.BlockSpec((8, tm), lambda i: (0, i)),
        out_shape=jax.ShapeDtypeStruct((8, n), jnp.int32),
        scratch_shapes=[pltpu.VMEM((LANES, LANES), F32), pltpu.VMEM((tm, tm), BF16)],
        compiler_params=pltpu.CompilerParams(dimension_semantics=("arbitrary",)),
        name="expert_positions",
    )(rtt, starts)


def _sc_mesh():
    return plsc.VectorSubcoreMesh(core_axis_name="c", subcore_axis_name="s",
                                  num_cores=SC_CORES, num_subcores=SC_SUBCORES)


def _sc_dispatch(hp, pos0, pos1, n_out):
    n, w = hp.shape
    workers = SC_CORES * SC_SUBCORES
    per = n // workers
    chunks = per // SC_CHUNK
    assert n % (workers * SC_CHUNK) == 0

    @functools.partial(
        pl.kernel, out_type=jax.ShapeDtypeStruct((n_out, w), hp.dtype), mesh=_sc_mesh(),
        scratch_types=[pltpu.VMEM((chunks, SC_CHUNK), jnp.int32), pltpu.VMEM((chunks, SC_CHUNK), jnp.int32),
                       pltpu.VMEM((SC_CHUNK, w), hp.dtype)],
        name="moe_dispatch")
    def body(h_hbm, p0_hbm, p1_hbm, xs_hbm, i0_v, i1_v, rows_v):
        wid = lax.axis_index("s") * SC_CORES + lax.axis_index("c")
        pltpu.sync_copy(p0_hbm.at[wid], i0_v)
        pltpu.sync_copy(p1_hbm.at[wid], i1_v)

        @pl.loop(0, chunks)
        def _(j):
            pltpu.sync_copy(h_hbm.at[pl.ds(wid * per + j * SC_CHUNK, SC_CHUNK)], rows_v)
            pltpu.sync_copy(rows_v, xs_hbm.at[i0_v.at[j]])
            pltpu.sync_copy(rows_v, xs_hbm.at[i1_v.at[j]])

    return body(hp, pos0.reshape(workers, chunks, SC_CHUNK), pos1.reshape(workers, chunks, SC_CHUNK))


def _sc_collect(ys, pos0, pos1):
    n = pos0.shape[0]
    w = ys.shape[1]
    workers = SC_CORES * SC_SUBCORES
    per = n // workers
    chunks = per // SC_CHUNK
    out = jax.ShapeDtypeStruct((n, w), ys.dtype)

    @functools.partial(
        pl.kernel, out_type=(out, out), mesh=_sc_mesh(),
        scratch_types=[pltpu.VMEM((chunks, SC_CHUNK), jnp.int32), pltpu.VMEM((chunks, SC_CHUNK), jnp.int32),
                       pltpu.VMEM((SC_CHUNK, w), ys.dtype)],
        name="moe_collect")
    def body(ys_hbm, p0_hbm, p1_hbm, y0_hbm, y1_hbm, i0_v, i1_v, rows_v):
        wid = lax.axis_index("s") * SC_CORES + lax.axis_index("c")
        pltpu.sync_copy(p0_hbm.at[wid], i0_v)
        pltpu.sync_copy(p1_hbm.at[wid], i1_v)

        @pl.loop(0, chunks)
        def _(j):
            dst = pl.ds(wid * per + j * SC_CHUNK, SC_CHUNK)
            pltpu.sync_copy(ys_hbm.at[i0_v.at[j]], rows_v)
            pltpu.sync_copy(rows_v, y0_hbm.at[dst])
            pltpu.sync_copy(ys_hbm.at[i1_v.at[j]], rows_v)
            pltpu.sync_copy(rows_v, y1_hbm.at[dst])

    return body(ys, pos0.reshape(workers, chunks, SC_CHUNK), pos1.reshape(workers, chunks, SC_CHUNK))


def _experts_kernel(plan_ref, nt_ref, xs_ref, wg_ref, wu_ref, wd_ref, ys_ref, wg_bf, wu_bf, wd_bf):
    g = pl.program_id(0)
    prev = jnp.maximum(g - 1, 0)

    for row, (src, dst) in enumerate(((wg_ref, wg_bf), (wu_ref, wu_bf), (wd_ref, wd_bf))):
        @pl.when((g == 0) | (plan_ref[row, g] != plan_ref[row, prev]))
        def _(row=row, src=src, dst=dst):
            dst[plan_ref[row + 3, g]] = src[...].astype(BF16)

    @pl.when((g >= EXPERT_LEAD) & (g < nt_ref[0] + EXPERT_LEAD))
    def _():
        slot = plan_ref[6, g]
        half = wg_bf.shape[1] // 2
        hi, lo = _unpack_bf16_pairs(xs_ref[...])
        hi = hi.astype(BF16)
        lo = lo.astype(BF16)
        a = _dot(hi, wg_bf[slot, :half, :]) + _dot(lo, wg_bf[slot, half:, :])
        u = _dot(hi, wu_bf[slot, :half, :]) + _dot(lo, wu_bf[slot, half:, :])
        act = (a * jax.nn.sigmoid(a) * u).astype(BF16)
        ys_ref[...] = _pack_bf16_pairs(_dot(act, wd_bf[slot]))


def _experts(xs, tile_expert, n_tiles, w_gate, w_up, w_down, tmg):
    rows, w = xs.shape
    ne, d, de = w_gate.shape
    steps = rows // tmg + EXPERT_LEAD
    run = jnp.concatenate([jnp.zeros((1,), jnp.int32),
                           jnp.cumsum((tile_expert[1:] != tile_expert[:-1]).astype(jnp.int32))])
    tile = jnp.clip(jnp.arange(-EXPERT_LEAD, steps, dtype=jnp.int32), 0, n_tiles[0] - 1)
    expert, slot = tile_expert[tile], run[tile] % EXPERT_SLOTS
    at = lambda a, lag: a[EXPERT_LEAD - lag:EXPERT_LEAD - lag + steps]
    plan = jnp.stack([at(expert, lag) for lag in range(EXPERT_LEAD)]
                     + [at(slot, lag) for lag in range(EXPERT_LEAD + 1)]
                     + [at(tile, EXPERT_LEAD)]).astype(jnp.int32)
    return pl.pallas_call(
        _experts_kernel,
        grid_spec=pltpu.PrefetchScalarGridSpec(
            num_scalar_prefetch=2,
            grid=(steps,),
            in_specs=[pl.BlockSpec((tmg, w), lambda s, plan, nt: (plan[2 * EXPERT_LEAD + 1, s], 0)),
                      pl.BlockSpec((None, d, de), lambda s, plan, nt: (plan[0, s], 0, 0)),
                      pl.BlockSpec((None, d, de), lambda s, plan, nt: (plan[1, s], 0, 0)),
                      pl.BlockSpec((None, de, d), lambda s, plan, nt: (plan[2, s], 0, 0))],
            out_specs=pl.BlockSpec((tmg, w), lambda s, plan, nt: (plan[2 * EXPERT_LEAD + 1, s], 0)),
            scratch_shapes=[pltpu.VMEM((EXPERT_SLOTS, d, de), BF16), pltpu.VMEM((EXPERT_SLOTS, d, de), BF16),
                            pltpu.VMEM((EXPERT_SLOTS, de, d), BF16)],
        ),
        out_shape=jax.ShapeDtypeStruct((rows, w), jnp.uint32),
        compiler_params=pltpu.CompilerParams(dimension_semantics=("arbitrary",), vmem_limit_bytes=VMEM_LIMIT),
        name="experts",
    )(plan, n_tiles, xs, w_gate, w_up, w_down)


def _moe(hp, rtt, cnt, w_gate, w_up, w_down, tm, tmg):
    n = hp.shape[0]
    ne = w_gate.shape[0]
    counts = cnt[:ne, 0].astype(jnp.int32)
    padded = (counts + tmg - 1) // tmg * tmg
    ends = jnp.cumsum(padded)
    starts = jnp.pad((ends - padded).astype(F32), (0, LANES - ne))
    pos = _expert_positions(rtt, jnp.broadcast_to(starts[:, None], (LANES, LANES)), tm)
    pos0, pos1 = pos[0], pos[1]
    rows = 2 * n + ne * tmg
    tile_start = jnp.arange(rows // tmg, dtype=jnp.int32) * tmg
    tile_expert = jnp.minimum(jnp.sum(tile_start[:, None] >= ends[None, :], axis=1), ne - 1).astype(jnp.int32)
    n_tiles = (ends[-1:] // tmg).astype(jnp.int32)
    xs = _sc_dispatch(hp, pos0, pos1, rows)
    ys = _experts(xs, tile_expert, n_tiles, w_gate, w_up, w_down, tmg)
    return _sc_collect(ys, pos0, pos1)


def _ple_kernel(x1_ref, y0_ref, y1_ref, rt_ref, p_ref, g_ref, wg_ref, wp_ref, o_ref):
    for r0 in range(0, x1_ref.shape[0], PLE_SUB):
        rows = pl.ds(r0, PLE_SUB)
        rt = rt_ref[rows, :]
        y0 = jnp.concatenate(_unpack_bf16_pairs(y0_ref[rows, :]), axis=1)
        y1 = jnp.concatenate(_unpack_bf16_pairs(y1_ref[rows, :]), axis=1)
        x2 = x1_ref[rows, :] + rt[:, 2:3] * y0 + rt[:, 3:4] * y1
        gate = jax.nn.sigmoid(_dot(_rms(x2, g_ref[...]).astype(BF16), wg_ref[...]))
        o_ref[rows, :] = x2 + gate * _dot(p_ref[rows, :].astype(BF16), wp_ref[...])


def _ple_kernel_into(prev_ref, *refs):
    del prev_ref
    _ple_kernel(*refs)


def _ple(x1, y0, y1, rt, p2, g_ple, w_gate, w_proj, tm, row0, out_prev):
    n, d = x1.shape
    n_all, dp = p2.shape
    blk0 = row0 // tm
    full = lambda shape: pl.BlockSpec(shape, lambda i: (0,) * len(shape))
    row = lambda w: pl.BlockSpec((tm, w), lambda i: (i, 0))
    row_all = lambda w: pl.BlockSpec((tm, w), lambda i: (i + blk0, 0))
    in_specs = [row(d), row(d // 2), row(d // 2), row(LANES), row_all(dp), full((1, d)), full((d, d)), full((dp, d))]
    args = (x1, y0, y1, rt, p2, g_ple[None, :].astype(F32), w_gate.astype(BF16), w_proj.astype(BF16))
    if out_prev is not None:
        in_specs = [pl.BlockSpec(memory_space=pl.ANY)] + in_specs
        args = (out_prev,) + args
    return pl.pallas_call(
        _ple_kernel if out_prev is None else _ple_kernel_into,
        grid=(n // tm,),
        in_specs=in_specs,
        out_specs=row_all(d),
        out_shape=jax.ShapeDtypeStruct((n_all, d), F32),
        input_output_aliases={} if out_prev is None else {0: 0},
        compiler_params=pltpu.CompilerParams(dimension_semantics=("parallel",), vmem_limit_bytes=VMEM_LIMIT),
        name="ple",
    )(*args)


def _layer(x, p_l, g_attn, w_qkv, q_norm_na, k_norm_na, rpb_na, q_norm_dil, k_norm_dil, g_out_na, g_out_dil,
           w_o, g_ffn, w_rg, b_rg, w_re, b_re, w_exp_gate, w_exp_up, w_exp_down, g_ple, w_ple_gate, w_ple_proj):
    b, s, d = x.shape
    n = b * s
    half = d // 2
    assert d == N_HEADS * HEAD_DIM and half == N_HEADS_NA * HEAD_DIM
    tm = 512
    assert s % tm == 0 and s % QKV_TILE == 0
    x2 = x.reshape(n, d)
    qa, ka, va, qb, kb, vb = _qkv_proj(x2, g_attn, w_qkv, q_norm_na, k_norm_na, q_norm_dil, k_norm_dil, s, QKV_TILE)
    seq = lambda t: t.reshape(b, s, half)
    oa = _na_attention(seq(qa), seq(ka), seq(va), rpb_na).reshape(n, half)
    ob = _dil_attention(seq(qb), seq(kb), seq(vb)).reshape(n, half)
    p2 = p_l.reshape(n, -1)
    chunk_unit = SC_CORES * SC_SUBCORES * SC_CHUNK
    n_chunks = TOKEN_CHUNKS if n % (TOKEN_CHUNKS * chunk_unit) == 0 else 1
    chunk = n // n_chunks
    assert chunk % ROW_TILE == 0
    out = None
    for c in range(n_chunks):
        x1, h, rt, rtt, cnt = _out_router(oa, ob, x2, g_out_na, g_out_dil, w_o, g_ffn, w_rg, b_rg, w_re, b_re,
                                          ROW_TILE, c * chunk, chunk)
        y0, y1 = _moe(h, rtt, cnt, w_exp_gate, w_exp_up, w_exp_down, 2 * tm, tm)
        out = _ple(x1, y0, y1, rt, p2, g_ple, w_ple_gate, w_ple_proj, ROW_TILE, c * chunk, out)
    return out.reshape(b, s, d)


def kernel(x, p, g_attn, w_qkv, q_norm_na, k_norm_na, rpb_na, q_norm_dil, k_norm_dil, g_out_na, g_out_dil, w_o,
           g_ffn, w_router_group, b_router_group, w_router_expert, b_router_expert, w_exp_gate, w_exp_up,
           w_exp_down, g_ple, w_ple_gate, w_ple_proj):
    for i in range(p.shape[0]):
        x = _layer(x, p[i], g_attn[i], w_qkv[i], q_norm_na[i], k_norm_na[i], rpb_na[i], q_norm_dil[i],
                   k_norm_dil[i], g_out_na[i], g_out_dil[i], w_o[i], g_ffn[i], w_router_group[i],
                   b_router_group[i], w_router_expert[i], b_router_expert[i], w_exp_gate[i], w_exp_up[i],
                   w_exp_down[i], g_ple[i], w_ple_gate[i], w_ple_proj[i])
    return x
```

```python
import functools

import numpy as np
import jax
import jax.numpy as jnp
from jax import lax
from jax.experimental import pallas as pl
from jax.experimental.pallas import tpu as pltpu
from jax.experimental.pallas import tpu_sc as plsc

HEAD_DIM = 64
N_HEADS = 16
N_HEADS_NA = 8
GRID_W = 64
NA_ROWS = 8
NA_COLS = 16
DIL_PAIRS = ((128, 1), (512, 4), (2048, 16))
ROPE_THETA = 10000.0
N_GROUPS = 4
EXPERTS_PER_GROUP = 8
N_EXPERTS = N_GROUPS * EXPERTS_PER_GROUP
EPS = 1e-6
NEG = -1e30
LOG2_E = 1.4426950408889634

LANES = 128
PAIR_W = 2 * HEAD_DIM
DIL_BLK = 128
SLAB_DIL = 4
ATTN_SLOTS = 4
ATTN_STAGE_LAG = 2
ROUTER_SUB = 512
PLE_SUB = 256
ROW_TILE = 1024
QKV_TILE = 1024
VMEM_LIMIT = 56 * 1024 * 1024
SC_CORES = 2
SC_SUBCORES = 16
SC_CHUNK = 128
EXPERT_LEAD = 3
EXPERT_SLOTS = 4
TOKEN_CHUNKS = 2

F32 = jnp.float32
BF16 = jnp.bfloat16


def _dot(a, b):
    return jnp.dot(a, b, preferred_element_type=F32)


def _dot_nt(a, b):
    return lax.dot_general(a, b, (((1,), (1,)), ((), ())), preferred_element_type=F32)


def _rms(x, gain):
    return x * lax.rsqrt(jnp.mean(x * x, axis=-1, keepdims=True) + EPS) * gain


def _lane_first_half(shape):
    return lax.broadcasted_iota(jnp.int32, shape, len(shape) - 1) < HEAD_DIM


def _qkv_kernel(x_ref, g_ref, w_ref, gq_na_ref, gk_na_ref, gq_dil_ref, gk_dil_ref, cos_ref, sin_ref,
                hsum_ref, qa_ref, ka_ref, va_ref, qb_ref, kb_ref, vb_ref, wb_ref):
    d = x_ref.shape[1]
    half = d // 2
    scale = HEAD_DIM ** -0.5 * LOG2_E

    @pl.when(pl.program_id(0) == 0)
    def _():
        wb_ref[...] = w_ref[...].astype(BF16)

    h = _rms(x_ref[...], g_ref[...]).astype(BF16)

    def proj(col):
        return _dot(h, wb_ref[:, col:col + half])

    def head_norm(y, gain):
        sq = (y * y).astype(BF16)
        w = hsum_ref.shape[0]
        ms = jnp.concatenate([_dot(sq[:, c:c + w], hsum_ref[...]) for c in range(0, half, w)], axis=1)
        return y * lax.rsqrt(ms + EPS) * gain

    def rope(y):
        lane = lax.broadcasted_iota(jnp.int32, (y.shape[0], LANES), 1)
        lower = (lane % HEAD_DIM) < HEAD_DIM // 2
        cos = cos_ref[...]
        sin = sin_ref[...]
        outs = []
        for c in range(0, half, LANES):
            yc = y[:, c:c + LANES]
            up = pltpu.roll(yc, LANES - HEAD_DIM // 2, axis=1)
            down = pltpu.roll(yc, HEAD_DIM // 2, axis=1)
            outs.append(yc * cos + jnp.where(lower, up, down) * sin)
        return jnp.concatenate(outs, axis=1)

    qa_ref[...] = (head_norm(proj(0), gq_na_ref[...]) * scale).astype(BF16)
    qb_ref[...] = (rope(head_norm(proj(half), gq_dil_ref[...])) * scale).astype(BF16)
    ka_ref[...] = head_norm(proj(d), gk_na_ref[...]).astype(BF16)
    kb_ref[...] = rope(head_norm(proj(d + half), gk_dil_ref[...])).astype(BF16)
    va_ref[...] = proj(2 * d).astype(BF16)
    vb_ref[...] = proj(2 * d + half).astype(BF16)


def _qkv_proj(x2, g_attn, w_qkv, gq_na, gk_na, gq_dil, gk_dil, seq, tm):
    n, d = x2.shape
    half = d // 2
    inv = ROPE_THETA ** (-np.arange(HEAD_DIM // 2, dtype=np.float64) / (HEAD_DIM // 2))
    ang = np.arange(seq, dtype=np.float64)[:, None] * inv[None, :]
    cos = jnp.asarray(np.tile(np.cos(ang), (1, LANES // (HEAD_DIM // 2))), F32)
    sin = jnp.asarray(np.tile(np.concatenate([-np.sin(ang), np.sin(ang)], axis=1), (1, LANES // HEAD_DIM)), F32)
    hs_w = 2 * LANES
    blk = np.arange(hs_w) // HEAD_DIM
    hsum = jnp.asarray((blk[:, None] == blk[None, :]).astype(np.float32) / HEAD_DIM, BF16)
    tile_gain = lambda g: jnp.tile(g.astype(F32), half // HEAD_DIM)[None, :]
    steps_per_seq = seq // tm
    full = lambda shape: pl.BlockSpec(shape, lambda i: (0,) * len(shape))
    out = jax.ShapeDtypeStruct((n, half), BF16)
    return pl.pallas_call(
        _qkv_kernel,
        grid=(n // tm,),
        in_specs=[
            pl.BlockSpec((tm, d), lambda i: (i, 0)),
            full((1, d)),
            pl.BlockSpec((d, 3 * d), lambda i: (0, 0), pipeline_mode=pl.Buffered(1)),
            full((1, half)), full((1, half)), full((1, half)), full((1, half)),
            pl.BlockSpec((tm, LANES), lambda i: (i % steps_per_seq, 0)),
            pl.BlockSpec((tm, LANES), lambda i: (i % steps_per_seq, 0)),
            full((hs_w, hs_w)),
        ],
        out_specs=[pl.BlockSpec((tm, half), lambda i: (i, 0))] * 6,
        out_shape=[out] * 6,
        scratch_shapes=[pltpu.VMEM((d, 3 * d), BF16)],
        compiler_params=pltpu.CompilerParams(dimension_semantics=("arbitrary",), vmem_limit_bytes=VMEM_LIMIT),
        name="qkv_proj",
    )(x2, g_attn[None, :].astype(F32), w_qkv, tile_gain(gq_na), tile_gain(gk_na),
      tile_gain(gq_dil), tile_gain(gk_dil), cos, sin, hsum)


def _clip(x, lo, hi):
    return min(max(x, lo), hi)


def _software_pipeline(n_items, stages):
    last_lag = (len(stages) - 1) * ATTN_STAGE_LAG
    assert last_lag <= ATTN_SLOTS
    for t in range(n_items + last_lag):
        for k in reversed(range(len(stages))):
            item = t - k * ATTN_STAGE_LAG
            if 0 <= item < n_items:
                stages[k](item, item % ATTN_SLOTS)


def _qk_stage(q, kwin, bias_a, bias_b, s_ref, slot, first):
    m = q.shape[0]
    w = kwin.shape[0]
    zero = jnp.zeros_like(q)
    s = _dot_nt(jnp.concatenate([jnp.where(first, q, zero), jnp.where(first, zero, q)], axis=0), kwin)
    s_ref[slot, :m, :w] = s[:m] + bias_a
    s_ref[slot, m:, :w] = s[m:] + bias_b


def _softmax_stage(s_ref, p_ref, slot, w):
    m = jnp.max(s_ref[slot, :, :w], axis=-1, keepdims=True)
    p_ref[slot, :, :w] = jnp.exp2(s_ref[slot, :, :w] - m).astype(BF16)
    return m


def _pv_stage(p_ref, slot, v_win, first):
    w = v_win.shape[0]
    r = _dot(p_ref[slot, :, :w], v_win)
    m = r.shape[0] // 2
    return jnp.where(first, r[:m, :PAIR_W], r[m:, :PAIR_W]), jnp.where(first, r[:m, PAIR_W:], r[m:, PAIR_W:])


def _with_ones(v):
    return jnp.concatenate([v, jnp.ones_like(v)], axis=1)


def _na_kernel(q_ref, k_ref, v_ref, bias_ref, o_ref, v1_ref, s_ref, p_ref):
    rows = q_ref.shape[0] // GRID_W
    win = NA_ROWS * GRID_W
    v1_ref[...] = _with_ones(v_ref[...])
    first = _lane_first_half((GRID_W, PAIR_W))

    def slices(r):
        rs = _clip(r - NA_ROWS // 2, 0, rows - NA_ROWS)
        return pl.ds(r * GRID_W, GRID_W), pl.ds(rs * GRID_W, win), r - rs

    def bias(head, delta):
        return jnp.concatenate([bias_ref[head, j - delta + NA_ROWS - 1] for j in range(0, NA_ROWS, 2)], axis=1)

    def qk(r, slot):
        qs, ks, delta = slices(r)
        _qk_stage(q_ref[qs, :], k_ref[ks, :], bias(0, delta), bias(1, delta), s_ref, slot, first)

    def softmax(r, slot):
        _softmax_stage(s_ref, p_ref, slot, win)

    def pv(r, slot):
        qs, ks, _ = slices(r)
        num, den = _pv_stage(p_ref, slot, v1_ref[ks, :], first)
        o_ref[qs, :] = (num / den).astype(o_ref.dtype)

    _software_pipeline(rows, (qk, softmax, pv))


def _na_bias_table(rpb):
    w = np.arange(GRID_W)
    cs = np.clip(w - NA_COLS // 2, 0, GRID_W - NA_COLS)
    kc = np.arange(GRID_W)
    valid = (kc[None, :] >= cs[:, None]) & (kc[None, :] < cs[:, None] + NA_COLS)
    coff = np.clip(kc[None, :] - w[:, None] + NA_COLS - 1, 0, 2 * NA_COLS - 2)
    pick = np.zeros((2 * NA_COLS - 1, GRID_W * GRID_W), np.float32)
    pick[coff.reshape(-1), np.arange(GRID_W * GRID_W)] = 1.0
    tab = jnp.einsum("hrc,cx->hrx", rpb.astype(F32), jnp.asarray(pick), precision=lax.Precision.HIGHEST)
    tab = jnp.where(valid[None, None], tab.reshape(tab.shape[:2] + valid.shape) * LOG2_E, NEG)
    return jnp.concatenate([tab[:, :-1], tab[:, 1:]], axis=-1)


def _na_attention(q, k, v, rpb):
    b, s, width = q.shape
    pairs = width // PAIR_W
    assert s % GRID_W == 0 and s // GRID_W >= NA_ROWS
    bias = _na_bias_table(rpb).reshape(pairs, 2, 2 * NA_ROWS - 2, GRID_W, 2 * GRID_W)
    qkv_spec = pl.BlockSpec((None, s, PAIR_W), lambda bi, j: (bi, 0, j))
    return pl.pallas_call(
        _na_kernel,
        grid=(b, pairs),
        in_specs=[qkv_spec, qkv_spec, qkv_spec,
                  pl.BlockSpec((None, 2, 2 * NA_ROWS - 2, GRID_W, 2 * GRID_W), lambda bi, j: (j, 0, 0, 0, 0))],
        out_specs=pl.BlockSpec((None, s, PAIR_W), lambda bi, j: (bi, 0, j)),
        out_shape=jax.ShapeDtypeStruct((b, s, width), BF16),
        scratch_shapes=[pltpu.VMEM((s, 2 * PAIR_W), BF16),
                        pltpu.VMEM((ATTN_SLOTS, 2 * GRID_W, NA_ROWS * GRID_W), F32),
                        pltpu.VMEM((ATTN_SLOTS, 2 * GRID_W, NA_ROWS * GRID_W), BF16)],
        compiler_params=pltpu.CompilerParams(dimension_semantics=("parallel", "parallel"),
                                             vmem_limit_bytes=VMEM_LIMIT),
        name="na_attn",
    )(q, k, v, bias)


def _dil_kernel(q_ref, k_ref, v_ref, mwide_ref, mfull_ref, o_ref,
                f32a_ref, f32b_ref, qc_ref, kc_ref, vc_ref, acc_ref, den_ref, max_ref, s_ref, p_ref):
    s = q_ref.shape[0]
    for i, ref in enumerate((q_ref, k_ref, v_ref)):
        f32a_ref[i] = ref[...].astype(F32)
    first = _lane_first_half((DIL_BLK, PAIR_W))
    prev_ref, next_ref, prev_dil = f32a_ref, f32b_ref, 1

    for p, (window, dil) in enumerate(DIL_PAIRS):
        radius = window // (2 * dil)
        cls_len = s // dil
        nblk = cls_len // DIL_BLK
        wide = cls_len >= 2 * DIL_BLK
        win = 2 * DIL_BLK if wide else cls_len
        assert radius == DIL_BLK // 2 and cls_len % DIL_BLK == 0 and dil % prev_dil == 0
        assert dil == 1 or dil % SLAB_DIL == 0

        ratio = dil // prev_dil
        keep = ratio > 1 and p + 1 < len(DIL_PAIRS)
        if dil == 1:
            vc_ref[...] = _with_ones(v_ref[...])
        for c in range(dil if dil > 1 else 0):
            src = pl.ds((c % prev_dil) * (s // prev_dil) + c // prev_dil, cls_len, stride=ratio)
            dst = pl.ds(c * cls_len, cls_len)
            vals = [prev_ref[i, src, :] for i in range(3)]
            if keep:
                for i in range(3):
                    next_ref[i, dst, :] = vals[i]
            qc_ref[dst, :] = vals[0].astype(BF16)
            kc_ref[dst, :] = vals[1].astype(BF16)
            vc_ref[dst, :] = _with_ones(vals[2].astype(BF16))
        if keep:
            prev_ref, next_ref, prev_dil = next_ref, prev_ref, dil

        def slices(n):
            c, i = divmod(n, nblk)
            base = c * cls_len
            ws = _clip(i * DIL_BLK - radius, 0, cls_len - win)
            kind = 0 if i == 0 else (2 if i == nblk - 1 else 1)
            if dil == 1:
                res = pl.ds(i * DIL_BLK, DIL_BLK)
            else:
                sub = dil // SLAB_DIL
                res = pl.ds((c % SLAB_DIL) * (s // SLAB_DIL) + c // SLAB_DIL + sub * DIL_BLK * i, DIL_BLK, stride=sub)
            return pl.ds(base + i * DIL_BLK, DIL_BLK), pl.ds(base + ws, win), kind, res

        def qk(n, slot):
            qs, ks, kind, _ = slices(n)
            mask = mwide_ref[kind] if wide else mfull_ref[...]
            q_src, k_src = (q_ref, k_ref) if dil == 1 else (qc_ref, kc_ref)
            _qk_stage(q_src[qs, :], k_src[ks, :], mask, mask, s_ref, slot, first)

        def softmax(n, slot):
            m = _softmax_stage(s_ref, p_ref, slot, win)
            max_ref[p, slices(n)[3], :] = jnp.where(first, m[:DIL_BLK], m[DIL_BLK:])

        def pv(n, slot):
            _, ks, _, tok = slices(n)
            num, den = _pv_stage(p_ref, slot, vc_ref[ks, :], first)
            acc_ref[p, tok, :] = num
            den_ref[p, tok, :] = den

        _software_pipeline(dil * nblk, (qk, softmax, pv))

    out_f32 = f32a_ref.at[0]
    quarter = s // SLAB_DIL
    for c in range(SLAB_DIL):
        rows = [pl.ds(c, quarter, stride=SLAB_DIL) if dil == 1 else pl.ds(c * quarter, quarter)
                for _, dil in DIL_PAIRS]
        maxima = [max_ref[p, rows[p], :] for p in range(len(DIL_PAIRS))]
        m = functools.reduce(jnp.maximum, maxima)
        num = jnp.zeros_like(m)
        den = jnp.zeros_like(m)
        for p in range(len(DIL_PAIRS)):
            w = jnp.exp2(maxima[p] - m)
            num = num + w * acc_ref[p, rows[p], :]
            den = den + w * den_ref[p, rows[p], :]
        out_f32[pl.ds(c, quarter, stride=SLAB_DIL), :] = num / den
    o_ref[...] = out_f32[...].astype(o_ref.dtype)


def _band_mask(kind):
    radius = DIL_BLK // 2
    qq = np.arange(DIL_BLK)[:, None]
    if kind == "full":
        kk = np.arange(DIL_BLK)[None, :]
        shift = 0
    else:
        kk = np.arange(2 * DIL_BLK)[None, :]
        shift = {"first": 0, "inner": radius, "last": DIL_BLK}[kind]
    return np.where(np.abs(kk - qq - shift) <= radius, 0.0, NEG).astype(np.float32)


def _dil_attention(q, k, v):
    b, s, width = q.shape
    pairs = width // PAIR_W
    for window, dil in DIL_PAIRS:
        assert s % (window // 2) == 0 and (s // dil) % DIL_BLK == 0
    mwide = jnp.asarray(np.stack([_band_mask("first"), _band_mask("inner"), _band_mask("last")]))
    mfull = jnp.asarray(_band_mask("full"))
    qkv_spec = pl.BlockSpec((None, s, PAIR_W), lambda bi, j: (bi, 0, j))
    npat = len(DIL_PAIRS)
    return pl.pallas_call(
        _dil_kernel,
        grid=(b, pairs),
        in_specs=[qkv_spec, qkv_spec, qkv_spec,
                  pl.BlockSpec(mwide.shape, lambda bi, j: (0, 0, 0)),
                  pl.BlockSpec(mfull.shape, lambda bi, j: (0, 0))],
        out_specs=pl.BlockSpec((None, s, PAIR_W), lambda bi, j: (bi, 0, j)),
        out_shape=jax.ShapeDtypeStruct((b, s, width), BF16),
        scratch_shapes=[pltpu.VMEM((3, s, PAIR_W), F32)] * 2 + [pltpu.VMEM((s, PAIR_W), BF16)] * 2
        + [pltpu.VMEM((s, 2 * PAIR_W), BF16)] + [pltpu.VMEM((npat, s, PAIR_W), F32)] * 3
        + [pltpu.VMEM((ATTN_SLOTS, 2 * DIL_BLK, 2 * DIL_BLK), F32),
           pltpu.VMEM((ATTN_SLOTS, 2 * DIL_BLK, 2 * DIL_BLK), BF16)],
        compiler_params=pltpu.CompilerParams(dimension_semantics=("parallel", "parallel"),
                                             vmem_limit_bytes=VMEM_LIMIT),
        name="dil_attn",
    )(q, k, v, mwide, mfull)


def _split_bf16(x):
    hi = x.astype(BF16)
    return hi, (x - hi.astype(F32)).astype(BF16)


def _route(logits):
    lane = lax.broadcasted_iota(jnp.int32, logits.shape, 1)
    ninf = jnp.float32(-jnp.inf)

    def first_argmax(vals, vmax):
        return jnp.min(jnp.where(vals == vmax, lane, LANES), axis=-1, keepdims=True)

    gl = jnp.where(lane < N_GROUPS, logits, ninf)
    gmax = jnp.max(gl, axis=-1, keepdims=True)
    gsel = first_argmax(gl, gmax)
    gw = 1.0 / jnp.sum(jnp.exp(gl - gmax), axis=-1, keepdims=True)
    lo = N_GROUPS + EXPERTS_PER_GROUP * gsel
    el = jnp.where((lane >= lo) & (lane < lo + EXPERTS_PER_GROUP), logits, ninf)
    v0 = jnp.max(el, axis=-1, keepdims=True)
    i0 = first_argmax(el, v0)
    el = jnp.where(lane == i0, ninf, el)
    v1 = jnp.max(el, axis=-1, keepdims=True)
    i1 = first_argmax(el, v1)
    t = jnp.exp(v1 - v0)
    w0 = gw / (1.0 + t)
    w1 = gw * t / (1.0 + t)
    e0 = (i0 - N_GROUPS).astype(F32)
    e1 = (i1 - N_GROUPS).astype(F32)
    return jnp.where(lane == 0, e0, jnp.where(lane == 1, e1, jnp.where(lane == 2, w0, jnp.where(lane == 3, w1, 0.0))))


def _out_router_kernel(oa_ref, ob_ref, x_ref, ga_ref, gb_ref, wo_ref, gf_ref, wr_ref, br_ref,
                       x1_ref, h_ref, rt_ref, rtt_ref, cnt_ref):
    half = oa_ref.shape[1]

    @pl.when(pl.program_id(0) == 0)
    def _():
        cnt_ref[...] = jnp.zeros_like(cnt_ref)

    for r0 in range(0, oa_ref.shape[0], ROUTER_SUB):
        rows = pl.ds(r0, ROUTER_SUB)
        ya = _rms(oa_ref[rows, :].astype(F32), ga_ref[...]).astype(BF16)
        yb = _rms(ob_ref[rows, :].astype(F32), gb_ref[...]).astype(BF16)
        x1 = x_ref[rows, :] + _dot(ya, wo_ref[:half, :]) + _dot(yb, wo_ref[half:, :])
        x1_ref[rows, :] = x1
        h = _rms(x1, gf_ref[...])
        h_ref[rows, :] = _pack_bf16_pairs(h)
        h_hi, h_lo = _split_bf16(h)
        both = _dot(h_hi, wr_ref[...])
        logits = both[:, :LANES] + both[:, LANES:] + _dot(h_lo, wr_ref[:, :LANES]) + br_ref[...]
        rt = _route(logits)
        rt_ref[rows, :] = rt
        rtt = rt.T[:rtt_ref.shape[0], :]
        rtt_ref[:, rows] = rtt
        oh0, oh1 = _slot_one_hots(rtt)
        cnt_ref[...] += jnp.sum(oh0 + oh1, axis=1, keepdims=True)


def _out_router(oa, ob, x2, g_na, g_dil, w_o, g_ffn, w_rg, b_rg, w_re, b_re, tm, row0, n):
    d = x2.shape[1]
    half = d // 2
    blk0 = row0 // tm
    wr = jnp.concatenate([w_rg.astype(F32), w_re.astype(F32).transpose(1, 0, 2).reshape(d, N_EXPERTS)], axis=1)
    wr = jnp.pad(wr, ((0, 0), (0, LANES - wr.shape[1])))
    wr_hi = wr.astype(BF16)
    wr_cat = jnp.concatenate([wr_hi, (wr - wr_hi.astype(F32)).astype(BF16)], axis=1)
    br =jnp.pad(jnp.concatenate([b_rg.astype(F32), b_re.astype(F32).reshape(-1)]), (0, LANES - N_GROUPS - N_EXPERTS))
    full = lambda shape: pl.BlockSpec(shape, lambda i: (0,) * len(shape))
    row = lambda w: pl.BlockSpec((tm, w), lambda i: (i, 0))
    row_in = lambda w: pl.BlockSpec((tm, w), lambda i: (i + blk0, 0))
    return pl.pallas_call(
        _out_router_kernel,
        grid=(n // tm,),
        in_specs=[row_in(half), row_in(half), row_in(d), full((1, half)), full((1, half)), full((d, d)),
                  full((1, d)), full((d, 2 * LANES)), full((1, LANES))],
        out_specs=[row(d), row(half), row(LANES), pl.BlockSpec((8, tm), lambda i: (0, i)), full((LANES, LANES))],
        out_shape=[jax.ShapeDtypeStruct((n, d), F32), jax.ShapeDtypeStruct((n, half), jnp.uint32),
                   jax.ShapeDtypeStruct((n, LANES), F32), jax.ShapeDtypeStruct((8, n), F32),
                   jax.ShapeDtypeStruct((LANES, LANES), F32)],
        compiler_params=pltpu.CompilerParams(dimension_semantics=("arbitrary",), vmem_limit_bytes=VMEM_LIMIT),
        name="out_router",
    )(oa, ob, x2, g_na[None, :].astype(F32), g_dil[None, :].astype(F32), w_o.astype(BF16),
      g_ffn[None, :].astype(F32), wr_cat, br[None, :])


def _pack_bf16_pairs(x):
    w = x.shape[1] // 2
    bits = lax.bitcast_convert_type(x.astype(BF16).astype(F32), jnp.uint32)
    return bits[:, :w] | (bits[:, w:] >> 16)


def _unpack_bf16_pairs(u):
    hi = lax.bitcast_convert_type(u & jnp.uint32(0xFFFF0000), F32)
    lo = lax.bitcast_convert_type(u << 16, F32)
    return hi, lo


def _slot_one_hots(rtt):
    sub = lax.broadcasted_iota(jnp.int32, (LANES, rtt.shape[1]), 0).astype(F32)
    return (sub == rtt[0:1, :]).astype(F32), (sub == rtt[1:2, :]).astype(F32)


def _position_kernel(rtt_ref, start_ref, pos_ref, base_ref):
    tm = rtt_ref.shape[1]

    @pl.when(pl.program_id(0) == 0)
    def _():
        base_ref[...] = start_ref[...]

    oh0, oh1 = _slot_one_hots(rtt_ref[...])
    oh = oh0 + oh1
    earlier = lax.broadcasted_iota(jnp.int32, (tm, tm), 0) < lax.broadcasted_iota(jnp.int32, (tm, tm), 1)
    before = _dot(oh.astype(BF16), earlier.astype(BF16)) + base_ref[:, 0:1]
    p0 = jnp.sum(before * oh0, axis=0, keepdims=True)
    p1 = jnp.sum(before * oh1, axis=0, keepdims=True)
    row = lax.broadcasted_iota(jnp.int32, pos_ref.shape, 0)
    pos_ref[...] = jnp.where(row == 0, p0, jnp.where(row == 1, p1, 0.0)).astype(jnp.int32)
    base_ref[...] += jnp.sum(oh, axis=1, keepdims=True)


def _expert_positions(rtt, starts, tm):
    n = rtt.shape[1]
    return pl.pallas_call(
        _position_kernel,
        grid=(n // tm,),
        in_specs=[pl.BlockSpec((8, tm), lambda i: (0, i)), pl.BlockSpec((LANES, LANES), lambda i: (0, 0))],
        out_specs=pl.BlockSpec((8, tm), lambda i: (0, i)),
        out_shape=jax.ShapeDtypeStruct((8, n), jnp.int32),
        scratch_shapes=[pltpu.VMEM((LANES, LANES), F32)],
        compiler_params=pltpu.CompilerParams(dimension_semantics=("arbitrary",)),
        name="expert_positions",
    )(rtt, starts)


def _sc_mesh():
    return plsc.VectorSubcoreMesh(core_axis_name="c", subcore_axis_name="s",
                                  num_cores=SC_CORES, num_subcores=SC_SUBCORES)


def _sc_dispatch(hp, pos0, pos1, n_out):
    n, w = hp.shape
    workers = SC_CORES * SC_SUBCORES
    per = n // workers
    chunks = per // SC_CHUNK
    assert n % (workers * SC_CHUNK) == 0

    @functools.partial(
        pl.kernel, out_type=jax.ShapeDtypeStruct((n_out, w), hp.dtype), mesh=_sc_mesh(),
        scratch_types=[pltpu.VMEM((chunks, SC_CHUNK), jnp.int32), pltpu.VMEM((chunks, SC_CHUNK), jnp.int32),
                       pltpu.VMEM((SC_CHUNK, w), hp.dtype)],
        name="moe_dispatch")
    def body(h_hbm, p0_hbm, p1_hbm, xs_hbm, i0_v, i1_v, rows_v):
        wid = lax.axis_index("s") * SC_CORES + lax.axis_index("c")
        pltpu.sync_copy(p0_hbm.at[wid], i0_v)
        pltpu.sync_copy(p1_hbm.at[wid], i1_v)

        @pl.loop(0, chunks)
        def _(j):
            pltpu.sync_copy(h_hbm.at[pl.ds(wid * per + j * SC_CHUNK, SC_CHUNK)], rows_v)
            pltpu.sync_copy(rows_v, xs_hbm.at[i0_v.at[j]])
            pltpu.sync_copy(rows_v, xs_hbm.at[i1_v.at[j]])

    return body(hp, pos0.reshape(workers, chunks, SC_CHUNK), pos1.reshape(workers, chunks, SC_CHUNK))


def _sc_collect(ys, pos0, pos1):
    n = pos0.shape[0]
    w = ys.shape[1]
    workers = SC_CORES * SC_SUBCORES
    per = n // workers
    chunks = per // SC_CHUNK
    out = jax.ShapeDtypeStruct((n, w), ys.dtype)

    @functools.partial(
        pl.kernel, out_type=(out, out), mesh=_sc_mesh(),
        scratch_types=[pltpu.VMEM((chunks, SC_CHUNK), jnp.int32), pltpu.VMEM((chunks, SC_CHUNK), jnp.int32),
                       pltpu.VMEM((SC_CHUNK, w), ys.dtype)],
        name="moe_collect")
    def body(ys_hbm, p0_hbm, p1_hbm, y0_hbm, y1_hbm, i0_v, i1_v, rows_v):
        wid = lax.axis_index("s") * SC_CORES + lax.axis_index("c")
        pltpu.sync_copy(p0_hbm.at[wid], i0_v)
        pltpu.sync_copy(p1_hbm.at[wid], i1_v)

        @pl.loop(0, chunks)
        def _(j):
            dst = pl.ds(wid * per + j * SC_CHUNK, SC_CHUNK)
            pltpu.sync_copy(ys_hbm.at[i0_v.at[j]], rows_v)
            pltpu.sync_copy(rows_v, y0_hbm.at[dst])
            pltpu.sync_copy(ys_hbm.at[i1_v.at[j]], rows_v)
            pltpu.sync_copy(rows_v, y1_hbm.at[dst])

    return body(ys, pos0.reshape(workers, chunks, SC_CHUNK), pos1.reshape(workers, chunks, SC_CHUNK))


def _experts_kernel(plan_ref, nt_ref, xs_ref, wg_ref, wu_ref, wd_ref, ys_ref, wg_bf, wu_bf, wd_bf):
    g = pl.program_id(0)
    prev = jnp.maximum(g - 1, 0)

    for row, (src, dst) in enumerate(((wg_ref, wg_bf), (wu_ref, wu_bf), (wd_ref, wd_bf))):
        @pl.when((g == 0) | (plan_ref[row, g] != plan_ref[row, prev]))
        def _(row=row, src=src, dst=dst):
            dst[plan_ref[row + 3, g]] = src[...].astype(BF16)

    @pl.when((g >= EXPERT_LEAD) & (g < nt_ref[0] + EXPERT_LEAD))
    def _():
        slot = plan_ref[6, g]
        half = wg_bf.shape[1] // 2
        hi, lo = _unpack_bf16_pairs(xs_ref[...])
        hi = hi.astype(BF16)
        lo = lo.astype(BF16)
        a = _dot(hi, wg_bf[slot, :half, :]) + _dot(lo, wg_bf[slot, half:, :])
        u = _dot(hi, wu_bf[slot, :half, :]) + _dot(lo, wu_bf[slot, half:, :])
        act = (a * jax.nn.sigmoid(a) * u).astype(BF16)
        ys_ref[...] = _pack_bf16_pairs(_dot(act, wd_bf[slot]))


def _experts(xs, tile_expert, n_tiles, w_gate, w_up, w_down, tmg):
    rows, w = xs.shape
    ne, d, de = w_gate.shape
    steps = rows // tmg + EXPERT_LEAD
    run = jnp.concatenate([jnp.zeros((1,), jnp.int32),
                           jnp.cumsum((tile_expert[1:] != tile_expert[:-1]).astype(jnp.int32))])
    tile = jnp.clip(jnp.arange(-EXPERT_LEAD, steps, dtype=jnp.int32), 0, n_tiles[0] - 1)
    expert, slot = tile_expert[tile], run[tile] % EXPERT_SLOTS
    at = lambda a, lag: a[EXPERT_LEAD - lag:EXPERT_LEAD - lag + steps]
    plan = jnp.stack([at(expert, lag) for lag in range(EXPERT_LEAD)]
                     + [at(slot, lag) for lag in range(EXPERT_LEAD + 1)]
                     + [at(tile, EXPERT_LEAD)]).astype(jnp.int32)
    return pl.pallas_call(
        _experts_kernel,
        grid_spec=pltpu.PrefetchScalarGridSpec(
            num_scalar_prefetch=2,
            grid=(steps,),
            in_specs=[pl.BlockSpec((tmg, w), lambda s, plan, nt: (plan[2 * EXPERT_LEAD + 1, s], 0)),
                      pl.BlockSpec((None, d, de), lambda s, plan, nt: (plan[0, s], 0, 0)),
                      pl.BlockSpec((None, d, de), lambda s, plan, nt: (plan[1, s], 0, 0)),
                      pl.BlockSpec((None, de, d), lambda s, plan, nt: (plan[2, s], 0, 0))],
            out_specs=pl.BlockSpec((tmg, w), lambda s, plan, nt: (plan[2 * EXPERT_LEAD + 1, s], 0)),
            scratch_shapes=[pltpu.VMEM((EXPERT_SLOTS, d, de), BF16), pltpu.VMEM((EXPERT_SLOTS, d, de), BF16),
                            pltpu.VMEM((EXPERT_SLOTS, de, d), BF16)],
        ),
        out_shape=jax.ShapeDtypeStruct((rows, w), jnp.uint32),
        compiler_params=pltpu.CompilerParams(dimension_semantics=("arbitrary",), vmem_limit_bytes=VMEM_LIMIT),
        name="experts",
    )(plan, n_tiles, xs, w_gate, w_up, w_down)


def _moe(hp, rtt, cnt, w_gate, w_up, w_down, tm, tmg):
    n = hp.shape[0]
    ne = w_gate.shape[0]
    counts = cnt[:ne, 0].astype(jnp.int32)
    padded = (counts + tmg - 1) // tmg * tmg
    ends = jnp.cumsum(padded)
    starts = jnp.pad((ends - padded).astype(F32), (0, LANES - ne))
    pos = _expert_positions(rtt, jnp.broadcast_to(starts[:, None], (LANES, LANES)), tm)
    pos0, pos1 = pos[0], pos[1]
    rows = 2 * n + ne * tmg
    tile_start = jnp.arange(rows // tmg, dtype=jnp.int32) * tmg
    tile_expert = jnp.minimum(jnp.sum(tile_start[:, None] >= ends[None, :], axis=1), ne - 1).astype(jnp.int32)
    n_tiles = (ends[-1:] // tmg).astype(jnp.int32)
    xs = _sc_dispatch(hp, pos0, pos1, rows)
    ys = _experts(xs, tile_expert, n_tiles, w_gate, w_up, w_down, tmg)
    return _sc_collect(ys, pos0, pos1)


def _ple_kernel(x1_ref, y0_ref, y1_ref, rt_ref, p_ref, g_ref, wg_ref, wp_ref, o_ref):
    for r0 in range(0, x1_ref.shape[0], PLE_SUB):
        rows = pl.ds(r0, PLE_SUB)
        rt = rt_ref[rows, :]
        y0 = jnp.concatenate(_unpack_bf16_pairs(y0_ref[rows, :]), axis=1)
        y1 = jnp.concatenate(_unpack_bf16_pairs(y1_ref[rows, :]), axis=1)
        x2 = x1_ref[rows, :] + rt[:, 2:3] * y0 + rt[:, 3:4] * y1
        gate = jax.nn.sigmoid(_dot(_rms(x2, g_ref[...]).astype(BF16), wg_ref[...]))
        o_ref[rows, :] = x2 + gate * _dot(p_ref[rows, :].astype(BF16), wp_ref[...])


def _ple_kernel_into(prev_ref, *refs):
    del prev_ref
    _ple_kernel(*refs)


def _ple(x1, y0, y1, rt, p2, g_ple, w_gate, w_proj, tm, row0, out_prev):
    n, d = x1.shape
    n_all, dp = p2.shape
    blk0 = row0 // tm
    full = lambda shape: pl.BlockSpec(shape, lambda i: (0,) * len(shape))
    row = lambda w: pl.BlockSpec((tm, w), lambda i: (i, 0))
    row_all = lambda w: pl.BlockSpec((tm, w), lambda i: (i + blk0, 0))
    in_specs = [row(d), row(d // 2), row(d // 2), row(LANES), row_all(dp), full((1, d)), full((d, d)), full((dp, d))]
    args = (x1, y0, y1, rt, p2, g_ple[None, :].astype(F32), w_gate.astype(BF16), w_proj.astype(BF16))
    if out_prev is not None:
        in_specs = [pl.BlockSpec(memory_space=pl.ANY)] + in_specs
        args = (out_prev,) + args
    return pl.pallas_call(
        _ple_kernel if out_prev is None else _ple_kernel_into,
        grid=(n // tm,),
        in_specs=in_specs,
        out_specs=row_all(d),
        out_shape=jax.ShapeDtypeStruct((n_all, d), F32),
        input_output_aliases={} if out_prev is None else {0: 0},
        compiler_params=pltpu.CompilerParams(dimension_semantics=("parallel",), vmem_limit_bytes=VMEM_LIMIT),
        name="ple",
    )(*args)


def _layer(x, p_l, g_attn, w_qkv, q_norm_na, k_norm_na, rpb_na, q_norm_dil, k_norm_dil, g_out_na, g_out_dil,
           w_o, g_ffn, w_rg, b_rg, w_re, b_re, w_exp_gate, w_exp_up, w_exp_down, g_ple, w_ple_gate, w_ple_proj):
    b, s, d = x.shape
    n = b * s
    half = d // 2
    assert d == N_HEADS * HEAD_DIM and half == N_HEADS_NA * HEAD_DIM
    tm = 512
    assert s % tm == 0 and s % QKV_TILE == 0
    x2 = x.reshape(n, d)
    qa, ka, va, qb, kb, vb = _qkv_proj(x2, g_attn, w_qkv, q_norm_na, k_norm_na, q_norm_dil, k_norm_dil, s, QKV_TILE)
    seq = lambda t: t.reshape(b, s, half)
    oa = _na_attention(seq(qa), seq(ka), seq(va), rpb_na).reshape(n, half)
    ob = _dil_attention(seq(qb), seq(kb), seq(vb)).reshape(n, half)
    p2 = p_l.reshape(n, -1)
    chunk_unit = SC_CORES * SC_SUBCORES * SC_CHUNK
    n_chunks = TOKEN_CHUNKS if n % (TOKEN_CHUNKS * chunk_unit) == 0 else 1
    chunk = n // n_chunks
    assert chunk % ROW_TILE == 0
    out = None
    for c in range(n_chunks):
        x1, h, rt, rtt, cnt = _out_router(oa, ob, x2, g_out_na, g_out_dil, w_o, g_ffn, w_rg, b_rg, w_re, b_re,
                                          ROW_TILE, c * chunk, chunk)
        y0, y1 = _moe(h, rtt, cnt, w_exp_gate, w_exp_up, w_exp_down, 2 * tm, tm)
        out = _ple(x1, y0, y1, rt, p2, g_ple, w_ple_gate, w_ple_proj, ROW_TILE, c * chunk, out)
    return out.reshape(b, s, d)


def kernel(x, p, g_attn, w_qkv, q_norm_na, k_norm_na, rpb_na, q_norm_dil, k_norm_dil, g_out_na, g_out_dil, w_o,
           g_ffn, w_router_group, b_router_group, w_router_expert, b_router_expert, w_exp_gate, w_exp_up,
           w_exp_down, g_ple, w_ple_gate, w_ple_proj):
    for i in range(p.shape[0]):
        x = _layer(x, p[i], g_attn[i], w_qkv[i], q_norm_na[i], k_norm_na[i], rpb_na[i], q_norm_dil[i],
                   k_norm_dil[i], g_out_na[i], g_out_dil[i], w_o[i], g_ffn[i], w_router_group[i],
                   b_router_group[i], w_router_expert[i], b_router_expert[i], w_exp_gate[i], w_exp_up[i],
                   w_exp_down[i], g_ple[i], w_ple_gate[i], w_ple_proj[i])
    return x
```

```python
import functools

import numpy as np
import jax
import jax.numpy as jnp
from jax import lax
from jax.experimental import pallas as pl
from jax.experimental.pallas import tpu as pltpu
from jax.experimental.pallas import tpu_sc as plsc

HEAD_DIM = 64
N_HEADS = 16
N_HEADS_NA = 8
GRID_W = 64
NA_ROWS = 8
NA_COLS = 16
DIL_PAIRS = ((128, 1), (512, 4), (2048, 16))
ROPE_THETA = 10000.0
N_GROUPS = 4
EXPERTS_PER_GROUP = 8
N_EXPERTS = N_GROUPS * EXPERTS_PER_GROUP
EPS = 1e-6
NEG = -1e30
LOG2_E = 1.4426950408889634

LANES = 128
PAIR_W = 2 * HEAD_DIM
DIL_BLK = 128
SLAB_DIL = 4
ATTN_SLOTS = 4
ATTN_STAGE_LAG = 2
ROUTER_SUB = 512
PLE_SUB = 256
ROW_TILE = 1024
QKV_TILE = 1024
VMEM_LIMIT = 56 * 1024 * 1024
SC_CORES = 2
SC_SUBCORES = 16
SC_CHUNK = 128
EXPERT_LEAD = 3
EXPERT_SLOTS = 4
TOKEN_CHUNKS = 2

F32 = jnp.float32
BF16 = jnp.bfloat16


def _dot(a, b):
    return jnp.dot(a, b, preferred_element_type=F32)


def _dot_nt(a, b):
    return lax.dot_general(a, b, (((1,), (1,)), ((), ())), preferred_element_type=F32)


def _rms(x, gain):
    return x * lax.rsqrt(jnp.mean(x * x, axis=-1, keepdims=True) + EPS) * gain


def _lane_first_half(shape):
    return lax.broadcasted_iota(jnp.int32, shape, len(shape) - 1) < HEAD_DIM


def _qkv_kernel(x_ref, g_ref, w_ref, gq_na_ref, gk_na_ref, gq_dil_ref, gk_dil_ref, cos_ref, sin_ref,
                hsum_ref, qa_ref, ka_ref, va_ref, qb_ref, kb_ref, vb_ref, wb_ref):
    d = x_ref.shape[1]
    half = d // 2
    scale = HEAD_DIM ** -0.5 * LOG2_E

    @pl.when(pl.program_id(0) == 0)
    def _():
        wb_ref[...] = w_ref[...].astype(BF16)

    h = _rms(x_ref[...], g_ref[...]).astype(BF16)

    def proj(col):
        return _dot(h, wb_ref[:, col:col + half])

    def head_norm(y, gain):
        sq = (y * y).astype(BF16)
        w = hsum_ref.shape[0]
        ms = jnp.concatenate([_dot(sq[:, c:c + w], hsum_ref[...]) for c in range(0, half, w)], axis=1)
        return y * lax.rsqrt(ms + EPS) * gain

    def rope(y):
        lane = lax.broadcasted_iota(jnp.int32, (y.shape[0], LANES), 1)
        lower = (lane % HEAD_DIM) < HEAD_DIM // 2
        cos = cos_ref[...]
        sin = sin_ref[...]
        outs = []
        for c in range(0, half, LANES):
            yc = y[:, c:c + LANES]
            up = pltpu.roll(yc, LANES - HEAD_DIM // 2, axis=1)
            down = pltpu.roll(yc, HEAD_DIM // 2, axis=1)
            outs.append(yc * cos + jnp.where(lower, up, down) * sin)
        return jnp.concatenate(outs, axis=1)

    qa_ref[...] = (head_norm(proj(0), gq_na_ref[...]) * scale).astype(BF16)
    qb_ref[...] = (rope(head_norm(proj(half), gq_dil_ref[...])) * scale).astype(BF16)
    ka_ref[...] = head_norm(proj(d), gk_na_ref[...]).astype(BF16)
    kb_ref[...] = rope(head_norm(proj(d + half), gk_dil_ref[...])).astype(BF16)
    va_ref[...] = proj(2 * d).astype(BF16)
    vb_ref[...] = proj(2 * d + half).astype(BF16)


def _qkv_proj(x2, g_attn, w_qkv, gq_na, gk_na, gq_dil, gk_dil, seq, tm):
    n, d = x2.shape
    half = d // 2
    inv = ROPE_THETA ** (-np.arange(HEAD_DIM // 2, dtype=np.float64) / (HEAD_DIM // 2))
    ang = np.arange(seq, dtype=np.float64)[:, None] * inv[None, :]
    cos = jnp.asarray(np.tile(np.cos(ang), (1, LANES // (HEAD_DIM // 2))), F32)
    sin = jnp.asarray(np.tile(np.concatenate([-np.sin(ang), np.sin(ang)], axis=1), (1, LANES // HEAD_DIM)), F32)
    hs_w = 2 * LANES
    blk = np.arange(hs_w) // HEAD_DIM
    hsum = jnp.asarray((blk[:, None] == blk[None, :]).astype(np.float32) / HEAD_DIM, BF16)
    tile_gain = lambda g: jnp.tile(g.astype(F32), half // HEAD_DIM)[None, :]
    steps_per_seq = seq // tm
    full = lambda shape: pl.BlockSpec(shape, lambda i: (0,) * len(shape))
    out = jax.ShapeDtypeStruct((n, half), BF16)
    return pl.pallas_call(
        _qkv_kernel,
        grid=(n // tm,),
        in_specs=[
            pl.BlockSpec((tm, d), lambda i: (i, 0)),
            full((1, d)),
            pl.BlockSpec((d, 3 * d), lambda i: (0, 0), pipeline_mode=pl.Buffered(1)),
            full((1, half)), full((1, half)), full((1, half)), full((1, half)),
            pl.BlockSpec((tm, LANES), lambda i: (i % steps_per_seq, 0)),
            pl.BlockSpec((tm, LANES), lambda i: (i % steps_per_seq, 0)),
            full((hs_w, hs_w)),
        ],
        out_specs=[pl.BlockSpec((tm, half), lambda i: (i, 0))] * 6,
        out_shape=[out] * 6,
        scratch_shapes=[pltpu.VMEM((d, 3 * d), BF16)],
        compiler_params=pltpu.CompilerParams(dimension_semantics=("arbitrary",), vmem_limit_bytes=VMEM_LIMIT),
        name="qkv_proj",
    )(x2, g_attn[None, :].astype(F32), w_qkv, tile_gain(gq_na), tile_gain(gk_na),
      tile_gain(gq_dil), tile_gain(gk_dil), cos, sin, hsum)


def _clip(x, lo, hi):
    return min(max(x, lo), hi)


def _software_pipeline(n_items, stages):
    last_lag = (len(stages) - 1) * ATTN_STAGE_LAG
    assert last_lag <= ATTN_SLOTS
    for t in range(n_items + last_lag):
        for k in reversed(range(len(stages))):
            item = t - k * ATTN_STAGE_LAG
            if 0 <= item < n_items:
                stages[k](item, item % ATTN_SLOTS)


def _qk_stage(q, kwin, bias_a, bias_b, s_ref, slot, first):
    m = q.shape[0]
    w = kwin.shape[0]
    zero = jnp.zeros_like(q)
    s = _dot_nt(jnp.concatenate([jnp.where(first, q, zero), jnp.where(first, zero, q)], axis=0), kwin)
    s_ref[slot, :m, :w] = s[:m] + bias_a
    s_ref[slot, m:, :w] = s[m:] + bias_b


def _softmax_stage(s_ref, p_ref, slot, w):
    m = jnp.max(s_ref[slot, :, :w], axis=-1, keepdims=True)
    p_ref[slot, :, :w] = jnp.exp2(s_ref[slot, :, :w] - m).astype(BF16)
    return m


def _pv_stage(p_ref, slot, v_win, first):
    w = v_win.shape[0]
    r = _dot(p_ref[slot, :, :w], v_win)
    m = r.shape[0] // 2
    return jnp.where(first, r[:m, :PAIR_W], r[m:, :PAIR_W]), jnp.where(first, r[:m, PAIR_W:], r[m:, PAIR_W:])


def _with_ones(v):
    return jnp.concatenate([v, jnp.ones_like(v)], axis=1)


def _na_kernel(q_ref, k_ref, v_ref, bias_ref, o_ref, v1_ref, s_ref, p_ref):
    _software_pipeline(*_na_stages(q_ref, k_ref, v_ref, bias_ref, o_ref, v1_ref, s_ref, p_ref))


def _na_stages(q_ref, k_ref, v_ref, bias_ref, o_ref, v1_ref, s_ref, p_ref):
    rows = q_ref.shape[0] // GRID_W
    win = NA_ROWS * GRID_W
    v1_ref[...] = _with_ones(v_ref[...])
    first = _lane_first_half((GRID_W, PAIR_W))

    def slices(r):
        rs = _clip(r - NA_ROWS // 2, 0, rows - NA_ROWS)
        return pl.ds(r * GRID_W, GRID_W), pl.ds(rs * GRID_W, win), r - rs

    def bias(head, delta):
        return jnp.concatenate([bias_ref[head, j - delta + NA_ROWS - 1] for j in range(0, NA_ROWS, 2)], axis=1)

    def qk(r, slot):
        qs, ks, delta = slices(r)
        _qk_stage(q_ref[qs, :], k_ref[ks, :], bias(0, delta), bias(1, delta), s_ref, slot, first)

    def softmax(r, slot):
        _softmax_stage(s_ref, p_ref, slot, win)

    def pv(r, slot):
        qs, ks, _ = slices(r)
        num, den = _pv_stage(p_ref, slot, v1_ref[ks, :], first)
        o_ref[qs, :] = (num / den).astype(o_ref.dtype)

    return rows, (qk, softmax, pv)


def _interleave(a, b):
    out, i, j = [], 0, 0
    while i < len(a) or j < len(b):
        if j >= len(b) or (i < len(a) and i * len(b) <= j * len(a)):
            out.append(a[i])
            i += 1
        else:
            out.append(b[j])
            j += 1
    return out


def _attn_kernel(qa_ref, ka_ref, va_ref, bias_ref, q_ref, k_ref, v_ref, mwide_ref, mfull_ref, oa_ref, o_ref,
                 v1_ref, sna_ref, pna_ref, *dil_scratch):
    na = _na_stages(qa_ref, ka_ref, va_ref, bias_ref, oa_ref, v1_ref, sna_ref, pna_ref)
    _dil_kernel(q_ref, k_ref, v_ref, mwide_ref, mfull_ref, o_ref, *dil_scratch, na=na)


def _na_bias_table(rpb):
    w = np.arange(GRID_W)
    cs = np.clip(w - NA_COLS // 2, 0, GRID_W - NA_COLS)
    kc = np.arange(GRID_W)
    valid = (kc[None, :] >= cs[:, None]) & (kc[None, :] < cs[:, None] + NA_COLS)
    coff = np.clip(kc[None, :] - w[:, None] + NA_COLS - 1, 0, 2 * NA_COLS - 2)
    pick = np.zeros((2 * NA_COLS - 1, GRID_W * GRID_W), np.float32)
    pick[coff.reshape(-1), np.arange(GRID_W * GRID_W)] = 1.0
    tab = jnp.einsum("hrc,cx->hrx", rpb.astype(F32), jnp.asarray(pick), precision=lax.Precision.HIGHEST)
    tab = jnp.where(valid[None, None], tab.reshape(tab.shape[:2] + valid.shape) * LOG2_E, NEG)
    return jnp.concatenate([tab[:, :-1], tab[:, 1:]], axis=-1)


def _na_attention(q, k, v, rpb):
    b, s, width = q.shape
    pairs = width // PAIR_W
    assert s % GRID_W == 0 and s // GRID_W >= NA_ROWS
    bias = _na_bias_table(rpb).reshape(pairs, 2, 2 * NA_ROWS - 2, GRID_W, 2 * GRID_W)
    qkv_spec = pl.BlockSpec((None, s, PAIR_W), lambda bi, j: (bi, 0, j))
    return pl.pallas_call(
        _na_kernel,
        grid=(b, pairs),
        in_specs=[qkv_spec, qkv_spec, qkv_spec,
                  pl.BlockSpec((None, 2, 2 * NA_ROWS - 2, GRID_W, 2 * GRID_W), lambda bi, j: (j, 0, 0, 0, 0))],
        out_specs=pl.BlockSpec((None, s, PAIR_W), lambda bi, j: (bi, 0, j)),
        out_shape=jax.ShapeDtypeStruct((b, s, width), BF16),
        scratch_shapes=[pltpu.VMEM((s, 2 * PAIR_W), BF16),
                        pltpu.VMEM((ATTN_SLOTS, 2 * GRID_W, NA_ROWS * GRID_W), F32),
                        pltpu.VMEM((ATTN_SLOTS, 2 * GRID_W, NA_ROWS * GRID_W), BF16)],
        compiler_params=pltpu.CompilerParams(dimension_semantics=("parallel", "parallel"),
                                             vmem_limit_bytes=VMEM_LIMIT),
        name="na_attn",
    )(q, k, v, bias)


def _dil_kernel(q_ref, k_ref, v_ref, mwide_ref, mfull_ref, o_ref,
                f32a_ref, f32b_ref, qc_ref, kc_ref, vc_ref, acc_ref, den_ref, max_ref, s_ref, p_ref, na=None):
    s = q_ref.shape[0]
    for i, ref in enumerate((q_ref, k_ref, v_ref)):
        f32a_ref[i] = ref[...].astype(F32)
    first = _lane_first_half((DIL_BLK, PAIR_W))
    prev_ref, next_ref, prev_dil = f32a_ref, f32b_ref, 1

    for p, (window, dil) in enumerate(DIL_PAIRS):
        radius = window // (2 * dil)
        cls_len = s // dil
        nblk = cls_len // DIL_BLK
        wide = cls_len >= 2 * DIL_BLK
        win = 2 * DIL_BLK if wide else cls_len
        assert radius == DIL_BLK // 2 and cls_len % DIL_BLK == 0 and dil % prev_dil == 0
        assert dil == 1 or dil % SLAB_DIL == 0

        ratio = dil // prev_dil
        keep = ratio > 1 and p + 1 < len(DIL_PAIRS)
        for c in range(dil):
            src = pl.ds((c % prev_dil) * (s // prev_dil) + c // prev_dil, cls_len, stride=ratio)
            dst = pl.ds(c * cls_len, cls_len)
            vals = [prev_ref[i, src, :] for i in range(3)]
            if keep:
                for i in range(3):
                    next_ref[i, dst, :] = vals[i]
            qc_ref[dst, :] = vals[0].astype(BF16)
            kc_ref[dst, :] = vals[1].astype(BF16)
            vc_ref[dst, :] = _with_ones(vals[2].astype(BF16))
        if keep:
            prev_ref, next_ref, prev_dil = next_ref, prev_ref, dil

        def slices(n):
            c, i = divmod(n, nblk)
            base = c * cls_len
            ws = _clip(i * DIL_BLK - radius, 0, cls_len - win)
            kind = 0 if i == 0 else (2 if i == nblk - 1 else 1)
            if dil == 1:
                res = pl.ds(i * DIL_BLK, DIL_BLK)
            else:
                sub = dil // SLAB_DIL
                res = pl.ds((c % SLAB_DIL) * (s // SLAB_DIL) + c // SLAB_DIL + sub * DIL_BLK * i, DIL_BLK, stride=sub)
            return pl.ds(base + i * DIL_BLK, DIL_BLK), pl.ds(base + ws, win), kind, res

        def qk(n, slot):
            qs, ks, kind, _ = slices(n)
            mask = mwide_ref[kind] if wide else mfull_ref[...]
            _qk_stage(qc_ref[qs, :], kc_ref[ks, :], mask, mask, s_ref, slot, first)

        def softmax(n, slot):
            m = _softmax_stage(s_ref, p_ref, slot, win)
            max_ref[p, slices(n)[3], :] = jnp.where(first, m[:DIL_BLK], m[DIL_BLK:])

        def pv(n, slot):
            _, ks, _, tok = slices(n)
            num, den = _pv_stage(p_ref, slot, vc_ref[ks, :], first)
            acc_ref[p, tok, :] = num
            den_ref[p, tok, :] = den

        if na is None:
            _software_pipeline(dil * nblk, (qk, softmax, pv))
        else:
            na_rows, na_stages = na
            share = range(na_rows * p // len(DIL_PAIRS), na_rows * (p + 1) // len(DIL_PAIRS))
            order = _interleave([(0, n) for n in range(dil * nblk)], [(1, r) for r in share])
            both = ((qk, softmax, pv), na_stages)
            _software_pipeline(len(order), tuple(
                (lambda t, slot, k=k, order=order, both=both: both[order[t][0]][k](order[t][1], slot))
                for k in range(3)))

    out_f32 = f32a_ref.at[0]
    quarter = s // SLAB_DIL
    for c in range(SLAB_DIL):
        rows = [pl.ds(c, quarter, stride=SLAB_DIL) if dil == 1 else pl.ds(c * quarter, quarter)
                for _, dil in DIL_PAIRS]
        maxima = [max_ref[p, rows[p], :] for p in range(len(DIL_PAIRS))]
        m = functools.reduce(jnp.maximum, maxima)
        num = jnp.zeros_like(m)
        den = jnp.zeros_like(m)
        for p in range(len(DIL_PAIRS)):
            w = jnp.exp2(maxima[p] - m)
            num = num + w * acc_ref[p, rows[p], :]
            den = den + w * den_ref[p, rows[p], :]
        out_f32[pl.ds(c, quarter, stride=SLAB_DIL), :] = num / den
    o_ref[...] = out_f32[...].astype(o_ref.dtype)


def _band_mask(kind):
    radius = DIL_BLK // 2
    qq = np.arange(DIL_BLK)[:, None]
    if kind == "full":
        kk = np.arange(DIL_BLK)[None, :]
        shift = 0
    else:
        kk = np.arange(2 * DIL_BLK)[None, :]
        shift = {"first": 0, "inner": radius, "last": DIL_BLK}[kind]
    return np.where(np.abs(kk - qq - shift) <= radius, 0.0, NEG).astype(np.float32)


def _dil_attention(q, k, v):
    b, s, width = q.shape
    pairs = width // PAIR_W
    for window, dil in DIL_PAIRS:
        assert s % (window // 2) == 0 and (s // dil) % DIL_BLK == 0
    mwide = jnp.asarray(np.stack([_band_mask("first"), _band_mask("inner"), _band_mask("last")]))
    mfull = jnp.asarray(_band_mask("full"))
    qkv_spec = pl.BlockSpec((None, s, PAIR_W), lambda bi, j: (bi, 0, j))
    npat = len(DIL_PAIRS)
    return pl.pallas_call(
        _dil_kernel,
        grid=(b, pairs),
        in_specs=[qkv_spec, qkv_spec, qkv_spec,
                  pl.BlockSpec(mwide.shape, lambda bi, j: (0, 0, 0)),
                  pl.BlockSpec(mfull.shape, lambda bi, j: (0, 0))],
        out_specs=pl.BlockSpec((None, s, PAIR_W), lambda bi, j: (bi, 0, j)),
        out_shape=jax.ShapeDtypeStruct((b, s, width), BF16),
        scratch_shapes=[pltpu.VMEM((3, s, PAIR_W), F32)] * 2 + [pltpu.VMEM((s, PAIR_W), BF16)] * 2
        + [pltpu.VMEM((s, 2 * PAIR_W), BF16)] + [pltpu.VMEM((npat, s, PAIR_W), F32)] * 3
        + [pltpu.VMEM((ATTN_SLOTS, 2 * DIL_BLK, 2 * DIL_BLK), F32),
           pltpu.VMEM((ATTN_SLOTS, 2 * DIL_BLK, 2 * DIL_BLK), BF16)],
        compiler_params=pltpu.CompilerParams(dimension_semantics=("parallel", "parallel"),
                                             vmem_limit_bytes=VMEM_LIMIT),
        name="dil_attn",
    )(q, k, v, mwide, mfull)


def _attention(qa, ka, va, rpb, qb, kb, vb):
    b, s, width = qa.shape
    pairs = width // PAIR_W
    assert s % GRID_W == 0 and s // GRID_W >= NA_ROWS
    for window, dil in DIL_PAIRS:
        assert s % (window // 2) == 0 and (s // dil) % DIL_BLK == 0
    bias = _na_bias_table(rpb).reshape(pairs, 2, 2 * NA_ROWS - 2, GRID_W, 2 * GRID_W)
    mwide = jnp.asarray(np.stack([_band_mask("first"), _band_mask("inner"), _band_mask("last")]))
    mfull = jnp.asarray(_band_mask("full"))
    qkv_spec = pl.BlockSpec((None, s, PAIR_W), lambda bi, j: (bi, 0, j))
    npat = len(DIL_PAIRS)
    out = jax.ShapeDtypeStruct((b, s, width), BF16)
    return pl.pallas_call(
        _attn_kernel,
        grid=(b, pairs),
        in_specs=[qkv_spec, qkv_spec, qkv_spec,
                  pl.BlockSpec((None, 2, 2 * NA_ROWS - 2, GRID_W, 2 * GRID_W), lambda bi, j: (j, 0, 0, 0, 0)),
                  qkv_spec, qkv_spec, qkv_spec,
                  pl.BlockSpec(mwide.shape, lambda bi, j: (0, 0, 0)),
                  pl.BlockSpec(mfull.shape, lambda bi, j: (0, 0))],
        out_specs=[qkv_spec, qkv_spec],
        out_shape=[out, out],
        scratch_shapes=[pltpu.VMEM((s, 2 * PAIR_W), BF16),
                        pltpu.VMEM((ATTN_SLOTS, 2 * GRID_W, NA_ROWS * GRID_W), F32),
                        pltpu.VMEM((ATTN_SLOTS, 2 * GRID_W, NA_ROWS * GRID_W), BF16)]
        + [pltpu.VMEM((3, s, PAIR_W), F32)] * 2 + [pltpu.VMEM((s, PAIR_W), BF16)] * 2
        + [pltpu.VMEM((s, 2 * PAIR_W), BF16)] + [pltpu.VMEM((npat, s, PAIR_W), F32)] * 3
        + [pltpu.VMEM((ATTN_SLOTS, 2 * DIL_BLK, 2 * DIL_BLK), F32),
           pltpu.VMEM((ATTN_SLOTS, 2 * DIL_BLK, 2 * DIL_BLK), BF16)],
        compiler_params=pltpu.CompilerParams(dimension_semantics=("parallel", "parallel"),
                                             vmem_limit_bytes=VMEM_LIMIT),
        name="attention",
    )(qa, ka, va, bias, qb, kb, vb, mwide, mfull)


def _split_bf16(x):
    hi = x.astype(BF16)
    return hi, (x - hi.astype(F32)).astype(BF16)


def _route(logits):
    lane = lax.broadcasted_iota(jnp.int32, logits.shape, 1)
    ninf = jnp.float32(-jnp.inf)

    def first_argmax(vals, vmax):
        return jnp.min(jnp.where(vals == vmax, lane, LANES), axis=-1, keepdims=True)

    gl = jnp.where(lane < N_GROUPS, logits, ninf)
    gmax = jnp.max(gl, axis=-1, keepdims=True)
    gsel = first_argmax(gl, gmax)
    gw = 1.0 / jnp.sum(jnp.exp(gl - gmax), axis=-1, keepdims=True)
    lo = N_GROUPS + EXPERTS_PER_GROUP * gsel
    el = jnp.where((lane >= lo) & (lane < lo + EXPERTS_PER_GROUP), logits, ninf)
    v0 = jnp.max(el, axis=-1, keepdims=True)
    i0 = first_argmax(el, v0)
    el = jnp.where(lane == i0, ninf, el)
    v1 = jnp.max(el, axis=-1, keepdims=True)
    i1 = first_argmax(el, v1)
    t = jnp.exp(v1 - v0)
    w0 = gw / (1.0 + t)
    w1 = gw * t / (1.0 + t)
    e0 = (i0 - N_GROUPS).astype(F32)
    e1 = (i1 - N_GROUPS).astype(F32)
    return jnp.where(lane == 0, e0, jnp.where(lane == 1, e1, jnp.where(lane == 2, w0, jnp.where(lane == 3, w1, 0.0))))


def _out_router_kernel(oa_ref, ob_ref, x_ref, ga_ref, gb_ref, wo_ref, gf_ref, wr_ref, br_ref,
                       x1_ref, h_ref, rt_ref, rtt_ref, cnt_ref):
    half = oa_ref.shape[1]

    @pl.when(pl.program_id(0) == 0)
    def _():
        cnt_ref[...] = jnp.zeros_like(cnt_ref)

    for r0 in range(0, oa_ref.shape[0], ROUTER_SUB):
        rows = pl.ds(r0, ROUTER_SUB)
        ya = _rms(oa_ref[rows, :].astype(F32), ga_ref[...]).astype(BF16)
        yb = _rms(ob_ref[rows, :].astype(F32), gb_ref[...]).astype(BF16)
        x1 = x_ref[rows, :] + _dot(ya, wo_ref[:half, :]) + _dot(yb, wo_ref[half:, :])
        x1_ref[rows, :] = x1
        h = _rms(x1, gf_ref[...])
        h_ref[rows, :] = _pack_bf16_pairs(h)
        h_hi, h_lo = _split_bf16(h)
        both = _dot(h_hi, wr_ref[...])
        logits = both[:, :LANES] + both[:, LANES:] + _dot(h_lo, wr_ref[:, :LANES]) + br_ref[...]
        rt = _route(logits)
        rt_ref[rows, :] = rt
        rtt = rt.T[:rtt_ref.shape[0], :]
        rtt_ref[:, rows] = rtt
        oh0, oh1 = _slot_one_hots(rtt)
        cnt_ref[...] += jnp.sum(oh0 + oh1, axis=1, keepdims=True)


def _out_router(oa, ob, x2, g_na, g_dil, w_o, g_ffn, w_rg, b_rg, w_re, b_re, tm, row0, n):
    d = x2.shape[1]
    half = d // 2
    blk0 = row0 // tm
    wr = jnp.concatenate([w_rg.astype(F32), w_re.astype(F32).transpose(1, 0, 2).reshape(d, N_EXPERTS)], axis=1)
    wr = jnp.pad(wr, ((0, 0), (0, LANES - wr.shape[1])))
    wr_hi = wr.astype(BF16)
    wr_cat = jnp.concatenate([wr_hi, (wr - wr_hi.astype(F32)).astype(BF16)], axis=1)
    br =jnp.pad(jnp.concatenate([b_rg.astype(F32), b_re.astype(F32).reshape(-1)]), (0, LANES - N_GROUPS - N_EXPERTS))
    full = lambda shape: pl.BlockSpec(shape, lambda i: (0,) * len(shape))
    row = lambda w: pl.BlockSpec((tm, w), lambda i: (i, 0))
    row_in = lambda w: pl.BlockSpec((tm, w), lambda i: (i + blk0, 0))
    return pl.pallas_call(
        _out_router_kernel,
        grid=(n // tm,),
        in_specs=[row_in(half), row_in(half), row_in(d), full((1, half)), full((1, half)), full((d, d)),
                  full((1, d)), full((d, 2 * LANES)), full((1, LANES))],
        out_specs=[row(d), row(half), row(LANES), pl.BlockSpec((8, tm), lambda i: (0, i)), full((LANES, LANES))],
        out_shape=[jax.ShapeDtypeStruct((n, d), F32), jax.ShapeDtypeStruct((n, half), jnp.uint32),
                   jax.ShapeDtypeStruct((n, LANES), F32), jax.ShapeDtypeStruct((8, n), F32),
                   jax.ShapeDtypeStruct((LANES, LANES), F32)],
        compiler_params=pltpu.CompilerParams(dimension_semantics=("arbitrary",), vmem_limit_bytes=VMEM_LIMIT),
        name="out_router",
    )(oa, ob, x2, g_na[None, :].astype(F32), g_dil[None, :].astype(F32), w_o.astype(BF16),
      g_ffn[None, :].astype(F32), wr_cat, br[None, :])


def _pack_bf16_pairs(x):
    w = x.shape[1] // 2
    bits = lax.bitcast_convert_type(x.astype(BF16).astype(F32), jnp.uint32)
    return bits[:, :w] | (bits[:, w:] >> 16)


def _unpack_bf16_pairs(u):
    hi = lax.bitcast_convert_type(u & jnp.uint32(0xFFFF0000), F32)
    lo = lax.bitcast_convert_type(u << 16, F32)
    return hi, lo


def _slot_one_hots(rtt):
    sub = lax.broadcasted_iota(jnp.int32, (LANES, rtt.shape[1]), 0).astype(F32)
    return (sub == rtt[0:1, :]).astype(F32), (sub == rtt[1:2, :]).astype(F32)


def _position_kernel(rtt_ref, start_ref, pos_ref, base_ref):
    tm = rtt_ref.shape[1]

    @pl.when(pl.program_id(0) == 0)
    def _():
        base_ref[...] = start_ref[...]

    oh0, oh1 = _slot_one_hots(rtt_ref[...])
    oh = oh0 + oh1
    earlier = lax.broadcasted_iota(jnp.int32, (tm, tm), 0) < lax.broadcasted_iota(jnp.int32, (tm, tm), 1)
    before = _dot(oh.astype(BF16), earlier.astype(BF16)) + base_ref[:, 0:1]
    p0 = jnp.sum(before * oh0, axis=0, keepdims=True)
    p1 = jnp.sum(before * oh1, axis=0, keepdims=True)
    row = lax.broadcasted_iota(jnp.int32, pos_ref.shape, 0)
    pos_ref[...] = jnp.where(row == 0, p0, jnp.where(row == 1, p1, 0.0)).astype(jnp.int32)
    base_ref[...] += jnp.sum(oh, axis=1, keepdims=True)


def _expert_positions(rtt, starts, tm):
    n = rtt.shape[1]
    return pl.pallas_call(
        _position_kernel,
        grid=(n // tm,),
        in_specs=[pl.BlockSpec((8, tm), lambda i: (0, i)), pl.BlockSpec((LANES, LANES), lambda i: (0, 0))],
        out_specs=pl.BlockSpec((8, tm), lambda i: (0, i)),
        out_shape=jax.ShapeDtypeStruct((8, n), jnp.int32),
        scratch_shapes=[pltpu.VMEM((LANES, LANES), F32)],
        compiler_params=pltpu.CompilerParams(dimension_semantics=("arbitrary",)),
        name="expert_positions",
    )(rtt, starts)


def _sc_mesh():
    return plsc.VectorSubcoreMesh(core_axis_name="c", subcore_axis_name="s",
                                  num_cores=SC_CORES, num_subcores=SC_SUBCORES)


def _sc_dispatch(hp, pos0, pos1, n_out):
    n, w = hp.shape
    workers = SC_CORES * SC_SUBCORES
    per = n // workers
    chunks = per // SC_CHUNK
    assert n % (workers * SC_CHUNK) == 0

    @functools.partial(
        pl.kernel, out_type=jax.ShapeDtypeStruct((n_out, w), hp.dtype), mesh=_sc_mesh(),
        scratch_types=[pltpu.VMEM((chunks, SC_CHUNK), jnp.int32), pltpu.VMEM((chunks, SC_CHUNK), jnp.int32),
                       pltpu.VMEM((SC_CHUNK, w), hp.dtype)],
        name="moe_dispatch")
    def body(h_hbm, p0_hbm, p1_hbm, xs_hbm, i0_v, i1_v, rows_v):
        wid = lax.axis_index("s") * SC_CORES + lax.axis_index("c")
        pltpu.sync_copy(p0_hbm.at[wid], i0_v)
        pltpu.sync_copy(p1_hbm.at[wid], i1_v)

        @pl.loop(0, chunks)
        def _(j):
            pltpu.sync_copy(h_hbm.at[pl.ds(wid * per + j * SC_CHUNK, SC_CHUNK)], rows_v)
            pltpu.sync_copy(rows_v, xs_hbm.at[i0_v.at[j]])
            pltpu.sync_copy(rows_v, xs_hbm.at[i1_v.at[j]])

    return body(hp, pos0.reshape(workers, chunks, SC_CHUNK), pos1.reshape(workers, chunks, SC_CHUNK))


def _sc_collect(ys, pos0, pos1):
    n = pos0.shape[0]
    w = ys.shape[1]
    workers = SC_CORES * SC_SUBCORES
    per = n // workers
    chunks = per // SC_CHUNK
    out = jax.ShapeDtypeStruct((n, w), ys.dtype)

    @functools.partial(
        pl.kernel, out_type=(out, out), mesh=_sc_mesh(),
        scratch_types=[pltpu.VMEM((chunks, SC_CHUNK), jnp.int32), pltpu.VMEM((chunks, SC_CHUNK), jnp.int32),
                       pltpu.VMEM((SC_CHUNK, w), ys.dtype)],
        name="moe_collect")
    def body(ys_hbm, p0_hbm, p1_hbm, y0_hbm, y1_hbm, i0_v, i1_v, rows_v):
        wid = lax.axis_index("s") * SC_CORES + lax.axis_index("c")
        pltpu.sync_copy(p0_hbm.at[wid], i0_v)
        pltpu.sync_copy(p1_hbm.at[wid], i1_v)

        @pl.loop(0, chunks)
        def _(j):
            dst = pl.ds(wid * per + j * SC_CHUNK, SC_CHUNK)
            pltpu.sync_copy(ys_hbm.at[i0_v.at[j]], rows_v)
            pltpu.sync_copy(rows_v, y0_hbm.at[dst])
            pltpu.sync_copy(ys_hbm.at[i1_v.at[j]], rows_v)
            pltpu.sync_copy(rows_v, y1_hbm.at[dst])

    return body(ys, pos0.reshape(workers, chunks, SC_CHUNK), pos1.reshape(workers, chunks, SC_CHUNK))


def _experts_kernel(plan_ref, nt_ref, xs_ref, wg_ref, wu_ref, wd_ref, ys_ref, wg_bf, wu_bf, wd_bf):
    g = pl.program_id(0)
    prev = jnp.maximum(g - 1, 0)

    for row, (src, dst) in enumerate(((wg_ref, wg_bf), (wu_ref, wu_bf), (wd_ref, wd_bf))):
        @pl.when((g == 0) | (plan_ref[row, g] != plan_ref[row, prev]))
        def _(row=row, src=src, dst=dst):
            dst[plan_ref[row + 3, g]] = src[...].astype(BF16)

    @pl.when((g >= EXPERT_LEAD) & (g < nt_ref[0] + EXPERT_LEAD))
    def _():
        slot = plan_ref[6, g]
        half = wg_bf.shape[1] // 2
        hi, lo = _unpack_bf16_pairs(xs_ref[...])
        hi = hi.astype(BF16)
        lo = lo.astype(BF16)
        a = _dot(hi, wg_bf[slot, :half, :]) + _dot(lo, wg_bf[slot, half:, :])
        u = _dot(hi, wu_bf[slot, :half, :]) + _dot(lo, wu_bf[slot, half:, :])
        act = (a * jax.nn.sigmoid(a) * u).astype(BF16)
        ys_ref[...] = _pack_bf16_pairs(_dot(act, wd_bf[slot]))


def _experts(xs, tile_expert, n_tiles, w_gate, w_up, w_down, tmg):
    rows, w = xs.shape
    ne, d, de = w_gate.shape
    steps = rows // tmg + EXPERT_LEAD
    run = jnp.concatenate([jnp.zeros((1,), jnp.int32),
                           jnp.cumsum((tile_expert[1:] != tile_expert[:-1]).astype(jnp.int32))])
    tile = jnp.clip(jnp.arange(-EXPERT_LEAD, steps, dtype=jnp.int32), 0, n_tiles[0] - 1)
    expert, slot = tile_expert[tile], run[tile] % EXPERT_SLOTS
    at = lambda a, lag: a[EXPERT_LEAD - lag:EXPERT_LEAD - lag + steps]
    plan = jnp.stack([at(expert, lag) for lag in range(EXPERT_LEAD)]
                     + [at(slot, lag) for lag in range(EXPERT_LEAD + 1)]
                     + [at(tile, EXPERT_LEAD)]).astype(jnp.int32)
    return pl.pallas_call(
        _experts_kernel,
        grid_spec=pltpu.PrefetchScalarGridSpec(
            num_scalar_prefetch=2,
            grid=(steps,),
            in_specs=[pl.BlockSpec((tmg, w), lambda s, plan, nt: (plan[2 * EXPERT_LEAD + 1, s], 0)),
                      pl.BlockSpec((None, d, de), lambda s, plan, nt: (plan[0, s], 0, 0)),
                      pl.BlockSpec((None, d, de), lambda s, plan, nt: (plan[1, s], 0, 0)),
                      pl.BlockSpec((None, de, d), lambda s, plan, nt: (plan[2, s], 0, 0))],
            out_specs=pl.BlockSpec((tmg, w), lambda s, plan, nt: (plan[2 * EXPERT_LEAD + 1, s], 0)),
            scratch_shapes=[pltpu.VMEM((EXPERT_SLOTS, d, de), BF16), pltpu.VMEM((EXPERT_SLOTS, d, de), BF16),
                            pltpu.VMEM((EXPERT_SLOTS, de, d), BF16)],
        ),
        out_shape=jax.ShapeDtypeStruct((rows, w), jnp.uint32),
        compiler_params=pltpu.CompilerParams(dimension_semantics=("arbitrary",), vmem_limit_bytes=VMEM_LIMIT),
        name="experts",
    )(plan, n_tiles, xs, w_gate, w_up, w_down)


def _moe(hp, rtt, cnt, w_gate, w_up, w_down, tm, tmg):
    n = hp.shape[0]
    ne = w_gate.shape[0]
    counts = cnt[:ne, 0].astype(jnp.int32)
    padded = (counts + tmg - 1) // tmg * tmg
    ends = jnp.cumsum(padded)
    starts = jnp.pad((ends - padded).astype(F32), (0, LANES - ne))
    pos = _expert_positions(rtt, jnp.broadcast_to(starts[:, None], (LANES, LANES)), tm)
    pos0, pos1 = pos[0], pos[1]
    rows = 2 * n + ne * tmg
    tile_start = jnp.arange(rows // tmg, dtype=jnp.int32) * tmg
    tile_expert = jnp.minimum(jnp.sum(tile_start[:, None] >= ends[None, :], axis=1), ne - 1).astype(jnp.int32)
    n_tiles = (ends[-1:] // tmg).astype(jnp.int32)
    xs = _sc_dispatch(hp, pos0, pos1, rows)
    ys = _experts(xs, tile_expert, n_tiles, w_gate, w_up, w_down, tmg)
    return _sc_collect(ys, pos0, pos1)


def _ple_kernel(x1_ref, y0_ref, y1_ref, rt_ref, p_ref, g_ref, wg_ref, wp_ref, o_ref):
    for r0 in range(0, x1_ref.shape[0], PLE_SUB):
        rows = pl.ds(r0, PLE_SUB)
        rt = rt_ref[rows, :]
        y0 = jnp.concatenate(_unpack_bf16_pairs(y0_ref[rows, :]), axis=1)
        y1 = jnp.concatenate(_unpack_bf16_pairs(y1_ref[rows, :]), axis=1)
        x2 = x1_ref[rows, :] + rt[:, 2:3] * y0 + rt[:, 3:4] * y1
        gate = jax.nn.sigmoid(_dot(_rms(x2, g_ref[...]).astype(BF16), wg_ref[...]))
        o_ref[rows, :] = x2 + gate * _dot(p_ref[rows, :].astype(BF16), wp_ref[...])


def _ple_kernel_into(prev_ref, *refs):
    del prev_ref
    _ple_kernel(*refs)


def _ple(x1, y0, y1, rt, p2, g_ple, w_gate, w_proj, tm, row0, out_prev):
    n, d = x1.shape
    n_all, dp = p2.shape
    blk0 = row0 // tm
    full = lambda shape: pl.BlockSpec(shape, lambda i: (0,) * len(shape))
    row = lambda w: pl.BlockSpec((tm, w), lambda i: (i, 0))
    row_all = lambda w: pl.BlockSpec((tm, w), lambda i: (i + blk0, 0))
    in_specs = [row(d), row(d // 2), row(d // 2), row(LANES), row_all(dp), full((1, d)), full((d, d)), full((dp, d))]
    args = (x1, y0, y1, rt, p2, g_ple[None, :].astype(F32), w_gate.astype(BF16), w_proj.astype(BF16))
    if out_prev is not None:
        in_specs = [pl.BlockSpec(memory_space=pl.ANY)] + in_specs
        args = (out_prev,) + args
    return pl.pallas_call(
        _ple_kernel if out_prev is None else _ple_kernel_into,
        grid=(n // tm,),
        in_specs=in_specs,
        out_specs=row_all(d),
        out_shape=jax.ShapeDtypeStruct((n_all, d), F32),
        input_output_aliases={} if out_prev is None else {0: 0},
        compiler_params=pltpu.CompilerParams(dimension_semantics=("parallel",), vmem_limit_bytes=VMEM_LIMIT),
        name="ple",
    )(*args)


def _layer(x, p_l, g_attn, w_qkv, q_norm_na, k_norm_na, rpb_na, q_norm_dil, k_norm_dil, g_out_na, g_out_dil,
           w_o, g_ffn, w_rg, b_rg, w_re, b_re, w_exp_gate, w_exp_up, w_exp_down, g_ple, w_ple_gate, w_ple_proj):
    b, s, d = x.shape
    n = b * s
    half = d // 2
    assert d == N_HEADS * HEAD_DIM and half == N_HEADS_NA * HEAD_DIM
    tm = 512
    assert s % tm == 0 and s % QKV_TILE == 0
    x2 = x.reshape(n, d)
    qa, ka, va, qb, kb, vb = _qkv_proj(x2, g_attn, w_qkv, q_norm_na, k_norm_na, q_norm_dil, k_norm_dil, s, QKV_TILE)
    seq = lambda t: t.reshape(b, s, half)
    oa, ob = _attention(seq(qa), seq(ka), seq(va), rpb_na, seq(qb), seq(kb), seq(vb))
    oa, ob = oa.reshape(n, half), ob.reshape(n, half)
    p2 = p_l.reshape(n, -1)
    chunk_unit = SC_CORES * SC_SUBCORES * SC_CHUNK
    n_chunks = TOKEN_CHUNKS if n % (TOKEN_CHUNKS * chunk_unit) == 0 else 1
    chunk = n // n_chunks
    assert chunk % ROW_TILE == 0
    out = None
    for c in range(n_chunks):
        x1, h, rt, rtt, cnt = _out_router(oa, ob, x2, g_out_na, g_out_dil, w_o, g_ffn, w_rg, b_rg, w_re, b_re,
                                          ROW_TILE, c * chunk, chunk)
        y0, y1 = _moe(h, rtt, cnt, w_exp_gate, w_exp_up, w_exp_down, 2 * tm, tm)
        out = _ple(x1, y0, y1, rt, p2, g_ple, w_ple_gate, w_ple_proj, ROW_TILE, c * chunk, out)
    return out.reshape(b, s, d)


def kernel(x, p, g_attn, w_qkv, q_norm_na, k_norm_na, rpb_na, q_norm_dil, k_norm_dil, g_out_na, g_out_dil, w_o,
           g_ffn, w_router_group, b_router_group, w_router_expert, b_router_expert, w_exp_gate, w_exp_up,
           w_exp_down, g_ple, w_ple_gate, w_ple_proj):
    for i in range(p.shape[0]):
        x = _layer(x, p[i], g_attn[i], w_qkv[i], q_norm_na[i], k_norm_na[i], rpb_na[i], q_norm_dil[i],
                   k_norm_dil[i], g_out_na[i], g_out_dil[i], w_o[i], g_ffn[i], w_router_group[i],
                   b_router_group[i], w_router_expert[i], b_router_expert[i], w_exp_gate[i], w_exp_up[i],
                   w_exp_down[i], g_ple[i], w_ple_gate[i], w_ple_proj[i])
    return x
```
